```python
import jax, jax.numpy as jnp
from jax import lax
import numpy as np

D_MODEL = 2048
BATCH = 8
SEQ = 8192
DEPTH = 4

N_MIXERS = 2
FOX_HEADS = 16
FOX_HEAD_DIM = D_MODEL // FOX_HEADS
FOX_WIDTH = FOX_HEADS * FOX_HEAD_DIM
Q_BLOCK = 128
CONV_CHANNELS = D_MODEL
CONV_KERNEL = 31
RMS_EPS = 1e-6
LN_EPS = 1e-5
N_FOX = (DEPTH + 1) // 2
N_CONV = DEPTH // 2

kernel_name = 'hybrid_fox_conformer_conv_trunk'


def rmsnorm(x, g):
    xf = x.astype(jnp.float32)
    y = xf * lax.rsqrt(jnp.mean(xf * xf, axis=-1, keepdims=True) + RMS_EPS) * g.astype(jnp.float32)
    return y.astype(x.dtype)


def layernorm(x, g, b):
    xf = x.astype(jnp.float32)
    mu = jnp.mean(xf, axis=-1, keepdims=True)
    var = jnp.mean(jnp.square(xf - mu), axis=-1, keepdims=True)
    y = (xf - mu) * lax.rsqrt(var + LN_EPS) * g.astype(jnp.float32) + b.astype(jnp.float32)
    return y.astype(x.dtype)


def fox_mixer(h, w_in, b_f, w_out):
    B, S, _ = h.shape
    W, H, Dh = FOX_WIDTH, FOX_HEADS, FOX_HEAD_DIM
    proj = h @ w_in
    q, k, v, gate, f_logit = jnp.split(proj, [W, 2 * W, 3 * W, 4 * W], axis=-1)
    q = q.reshape(B, S, H, Dh).transpose(0, 2, 1, 3)
    kf = k.reshape(B, S, H, Dh).transpose(0, 2, 1, 3).astype(jnp.float32)
    v = v.reshape(B, S, H, Dh).transpose(0, 2, 1, 3)
    log_f = jax.nn.log_sigmoid((f_logit + b_f).astype(jnp.float32))
    c = jnp.cumsum(log_f, axis=1).transpose(0, 2, 1)
    nb = S // Q_BLOCK
    q_blocks = q.reshape(B, H, nb, Q_BLOCK, Dh).transpose(2, 0, 1, 3, 4)
    c_blocks = c.reshape(B, H, nb, Q_BLOCK).transpose(2, 0, 1, 3)
    starts = jnp.arange(nb, dtype=jnp.int32) * Q_BLOCK
    k_pos = jnp.arange(S, dtype=jnp.int32)
    scale = FOX_HEAD_DIM ** -0.5

    def attend(args):
        q_blk, c_blk, start = args
        s = jnp.einsum('bhqd,bhkd->bhqk', q_blk.astype(jnp.float32), kf) * scale
        s = s + c_blk[..., :, None] - c[:, :, None, :]
        q_pos = start + jnp.arange(Q_BLOCK, dtype=jnp.int32)
        s = jnp.where(q_pos[:, None] >= k_pos[None, :], s, -jnp.inf)
        p = jax.nn.softmax(s, axis=-1)
        return jnp.einsum('bhqk,bhkd->bhqd', p.astype(v.dtype), v)

    o = lax.map(attend, (q_blocks, c_blocks, starts))
    o = o.transpose(1, 0, 3, 2, 4).reshape(B, S, W)
    y = o * jax.nn.silu(gate)
    return y @ w_out


def conv_mixer(h, w_in, b_in, dw, dw_b, ln_g, ln_b, w_out):
    C = CONV_CHANNELS
    proj = h @ w_in + b_in
    a, b, gate = jnp.split(proj, [C, 2 * C], axis=-1)
    u = a * jax.nn.sigmoid(b)
    u = lax.conv_general_dilated(
        u, dw[:, None, :].astype(u.dtype), window_strides=(1,),
        padding=[(CONV_KERNEL - 1, 0)],
        dimension_numbers=('NWC', 'WIO', 'NWC'),
        feature_group_count=C) + dw_b
    u = jax.nn.silu(layernorm(u, ln_g, ln_b))
    y = u * jax.nn.silu(gate)
    return y @ w_out


def _fwd_setup_inputs(seed: int = 0) -> dict:
    key = jax.random.key(seed)
    ks = jax.random.split(key, 16)
    D, W, H, C, K = D_MODEL, FOX_WIDTH, FOX_HEADS, CONV_CHANNELS, CONV_KERNEL
    nrm = jax.random.normal
    return {
        'x': nrm(ks[0], (BATCH, SEQ, D), jnp.float32),
        'norm_g': 1.0 + 0.02 * nrm(ks[1], (DEPTH, D), jnp.float32),
        'fox_w_in': nrm(ks[2], (N_FOX, D, 4 * W + H), jnp.float32) * D ** -0.5,
        'fox_b_f': 2.0 + 0.5 * nrm(ks[3], (N_FOX, H), jnp.float32),
        'fox_w_out': nrm(ks[4], (N_FOX, W, D), jnp.float32) * W ** -0.5,
        'conv_w_in': nrm(ks[5], (N_CONV, D, 3 * C), jnp.float32) * D ** -0.5,
        'conv_b_in': 0.02 * nrm(ks[6], (N_CONV, 3 * C), jnp.float32),
        'conv_dw': nrm(ks[7], (N_CONV, K, C), jnp.float32) * K ** -0.5,
        'conv_dw_b': 0.02 * nrm(ks[8], (N_CONV, C), jnp.float32),
        'conv_ln_g': 1.0 + 0.02 * nrm(ks[9], (N_CONV, C), jnp.float32),
        'conv_ln_b': 0.02 * nrm(ks[10], (N_CONV, C), jnp.float32),
        'conv_w_out': nrm(ks[11], (N_CONV, C, D), jnp.float32) * C ** -0.5,
        'final_norm_g': 1.0 + 0.02 * nrm(ks[12], (D,), jnp.float32),
    }


def _fwd_reference(x, norm_g, fox_w_in, fox_b_f, fox_w_out, conv_w_in, conv_b_in, conv_dw,
              conv_dw_b, conv_ln_g, conv_ln_b, conv_w_out, final_norm_g):
    h = x
    for i in range(DEPTH):
        hn = rmsnorm(h, norm_g[i])
        j = i // N_MIXERS
        if i % N_MIXERS == 0:
            h = h + fox_mixer(hn, fox_w_in[j], fox_b_f[j], fox_w_out[j])
        else:
            h = h + conv_mixer(hn, conv_w_in[j], conv_b_in[j], conv_dw[j], conv_dw_b[j],
                               conv_ln_g[j], conv_ln_b[j], conv_w_out[j])
    return rmsnorm(h, final_norm_g)


import jax as _jax
import jax.numpy as _jnp

TWIN_FORMAT = 'train_step'
FWD_PARAMS = ['x', 'norm_g', 'fox_w_in', 'fox_b_f', 'fox_w_out', 'conv_w_in', 'conv_b_in', 'conv_dw', 'conv_dw_b', 'conv_ln_g', 'conv_ln_b', 'conv_w_out', 'final_norm_g']
TWIN_WEIGHTS = ['norm_g', 'fox_w_in', 'fox_b_f', 'fox_w_out', 'conv_w_in', 'conv_b_in', 'conv_dw', 'conv_dw_b', 'conv_ln_g', 'conv_ln_b', 'conv_w_out', 'final_norm_g']
TWIN_DIFF_INPUT = 'x'
TWIN_INPUTS = ['x', 'norm_g', 'fox_w_in', 'fox_b_f', 'fox_w_out', 'conv_w_in', 'conv_b_in', 'conv_dw', 'conv_dw_b', 'conv_ln_g', 'conv_ln_b', 'conv_w_out', 'final_norm_g', 'loss_target', 'm_norm_g', 'm_fox_w_in', 'm_fox_b_f', 'm_fox_w_out', 'm_conv_w_in', 'm_conv_b_in', 'm_conv_dw', 'm_conv_dw_b', 'm_conv_ln_g', 'm_conv_ln_b', 'm_conv_w_out', 'm_final_norm_g', 'v_norm_g', 'v_fox_w_in', 'v_fox_b_f', 'v_fox_w_out', 'v_conv_w_in', 'v_conv_b_in', 'v_conv_dw', 'v_conv_dw_b', 'v_conv_ln_g', 'v_conv_ln_b', 'v_conv_w_out', 'v_final_norm_g']
TWIN_OUTPUTS = ['loss', 'grad_x', 'grad_norm_g', 'grad_fox_w_in', 'grad_fox_b_f', 'grad_fox_w_out', 'grad_conv_w_in', 'grad_conv_b_in', 'grad_conv_dw', 'grad_conv_dw_b', 'grad_conv_ln_g', 'grad_conv_ln_b', 'grad_conv_w_out', 'grad_final_norm_g', 'delta_norm_g', 'delta_fox_w_in', 'delta_fox_b_f', 'delta_fox_w_out', 'delta_conv_w_in', 'delta_conv_b_in', 'delta_conv_dw', 'delta_conv_dw_b', 'delta_conv_ln_g', 'delta_conv_ln_b', 'delta_conv_w_out', 'delta_final_norm_g', 'new_m_norm_g', 'new_m_fox_w_in', 'new_m_fox_b_f', 'new_m_fox_w_out', 'new_m_conv_w_in', 'new_m_conv_b_in', 'new_m_conv_dw', 'new_m_conv_dw_b', 'new_m_conv_ln_g', 'new_m_conv_ln_b', 'new_m_conv_w_out', 'new_m_final_norm_g', 'new_v_norm_g', 'new_v_fox_w_in', 'new_v_fox_b_f', 'new_v_fox_w_out', 'new_v_conv_w_in', 'new_v_conv_b_in', 'new_v_conv_dw', 'new_v_conv_dw_b', 'new_v_conv_ln_g', 'new_v_conv_ln_b', 'new_v_conv_w_out', 'new_v_final_norm_g']
TWIN_LEAF_KINDS = {'loss': 'loss', 'grad_x': 'grad_x', 'grad_norm_g': 'grad_w', 'grad_fox_w_in': 'grad_w', 'grad_fox_b_f': 'grad_w', 'grad_fox_w_out': 'grad_w', 'grad_conv_w_in': 'grad_w', 'grad_conv_b_in': 'grad_w', 'grad_conv_dw': 'grad_w', 'grad_conv_dw_b': 'grad_w', 'grad_conv_ln_g': 'grad_w', 'grad_conv_ln_b': 'grad_w', 'grad_conv_w_out': 'grad_w', 'grad_final_norm_g': 'grad_w', 'delta_norm_g': 'delta_w', 'delta_fox_w_in': 'delta_w', 'delta_fox_b_f': 'delta_w', 'delta_fox_w_out': 'delta_w', 'delta_conv_w_in': 'delta_w', 'delta_conv_b_in': 'delta_w', 'delta_conv_dw': 'delta_w', 'delta_conv_dw_b': 'delta_w', 'delta_conv_ln_g': 'delta_w', 'delta_conv_ln_b': 'delta_w', 'delta_conv_w_out': 'delta_w', 'delta_final_norm_g': 'delta_w', 'new_m_norm_g': 'new_m', 'new_m_fox_w_in': 'new_m', 'new_m_fox_b_f': 'new_m', 'new_m_fox_w_out': 'new_m', 'new_m_conv_w_in': 'new_m', 'new_m_conv_b_in': 'new_m', 'new_m_conv_dw': 'new_m', 'new_m_conv_dw_b': 'new_m', 'new_m_conv_ln_g': 'new_m', 'new_m_conv_ln_b': 'new_m', 'new_m_conv_w_out': 'new_m', 'new_m_final_norm_g': 'new_m', 'new_v_norm_g': 'new_v', 'new_v_fox_w_in': 'new_v', 'new_v_fox_b_f': 'new_v', 'new_v_fox_w_out': 'new_v', 'new_v_conv_w_in': 'new_v', 'new_v_conv_b_in': 'new_v', 'new_v_conv_dw': 'new_v', 'new_v_conv_dw_b': 'new_v', 'new_v_conv_ln_g': 'new_v', 'new_v_conv_ln_b': 'new_v', 'new_v_conv_w_out': 'new_v', 'new_v_final_norm_g': 'new_v'}


def _forward(args):
    return _fwd_reference(*[args[k] for k in FWD_PARAMS])


def _output_shape():
    def fwd():
        inp = _fwd_setup_inputs(0)
        return _fwd_reference(*[inp[k] for k in FWD_PARAMS])
    out = _jax.eval_shape(fwd)
    return out.shape, out.dtype

N_MICROBATCH = 1
ADAM_LR = 0.001
ADAM_B1 = 0.9
ADAM_B2 = 0.999
ADAM_EPS = 1e-08
ADAM_WD = 0.01
ADAM_STEP = 10
PER_EXAMPLE_BATCH_AXIS = {'x': 0, 'loss_target': 0}
SHARED_INPUTS = []
_WEIGHT_DTYPES = {'norm_g': _jnp.float32, 'fox_w_in': _jnp.float32, 'fox_b_f': _jnp.float32, 'fox_w_out': _jnp.float32, 'conv_w_in': _jnp.float32, 'conv_b_in': _jnp.float32, 'conv_dw': _jnp.float32, 'conv_dw_b': _jnp.float32, 'conv_ln_g': _jnp.float32, 'conv_ln_b': _jnp.float32, 'conv_w_out': _jnp.float32, 'final_norm_g': _jnp.float32}
MOMENT_SCALE = {'norm_g': 6.325650e-02, 'fox_w_in': 3.043892e-02, 'fox_b_f': 2.857754e-01, 'fox_w_out': 3.433144e-02, 'conv_w_in': 3.696499e-02, 'conv_b_in': 3.931084e-02, 'conv_dw': 4.372073e-02, 'conv_dw_b': 9.225928e-02, 'conv_ln_g': 5.344502e-02, 'conv_ln_b': 4.548417e-02, 'conv_w_out': 4.244255e-02, 'final_norm_g': 3.198048e+01}


def _to_microbatches(a, axis):
    t = _jnp.moveaxis(a, axis, 0)
    t = t.reshape((N_MICROBATCH, t.shape[0] // N_MICROBATCH) + t.shape[1:])
    return _jnp.moveaxis(t, 1, axis + 1)


def setup_inputs(seed: int = 0) -> dict:
    inp = _fwd_setup_inputs(seed)
    key = _jax.random.fold_in(_jax.random.key(seed), 7919)
    shape, _ = _output_shape()
    out = dict(inp)
    out["loss_target"] = _jax.random.normal(_jax.random.fold_in(key, 0), shape, _jnp.float32)
    for i, name in enumerate(TWIN_WEIGHTS):
        w = inp[name].astype(_jnp.float32)
        if MOMENT_SCALE is None:
            s = _jnp.sqrt(_jnp.mean(_jnp.square(w)) + 1e-30)
        else:
            s = MOMENT_SCALE[name]
        km, kv = _jax.random.split(_jax.random.fold_in(key, i + 1))
        out[name] = w
        out["m_" + name] = s * _jax.random.normal(km, w.shape, _jnp.float32)
        out["v_" + name] = (s * s) * _jax.random.uniform(kv, w.shape, _jnp.float32, 0.5, 1.5)
    if N_MICROBATCH > 1:
        for name, axis in PER_EXAMPLE_BATCH_AXIS.items():
            out[name] = _to_microbatches(out[name], axis)
    return {'x': out['x'], 'norm_g': out['norm_g'], 'fox_w_in': out['fox_w_in'], 'fox_b_f': out['fox_b_f'], 'fox_w_out': out['fox_w_out'], 'conv_w_in': out['conv_w_in'], 'conv_b_in': out['conv_b_in'], 'conv_dw': out['conv_dw'], 'conv_dw_b': out['conv_dw_b'], 'conv_ln_g': out['conv_ln_g'], 'conv_ln_b': out['conv_ln_b'], 'conv_w_out': out['conv_w_out'], 'final_norm_g': out['final_norm_g'], 'loss_target': out['loss_target'], 'm_norm_g': out['m_norm_g'], 'm_fox_w_in': out['m_fox_w_in'], 'm_fox_b_f': out['m_fox_b_f'], 'm_fox_w_out': out['m_fox_w_out'], 'm_conv_w_in': out['m_conv_w_in'], 'm_conv_b_in': out['m_conv_b_in'], 'm_conv_dw': out['m_conv_dw'], 'm_conv_dw_b': out['m_conv_dw_b'], 'm_conv_ln_g': out['m_conv_ln_g'], 'm_conv_ln_b': out['m_conv_ln_b'], 'm_conv_w_out': out['m_conv_w_out'], 'm_final_norm_g': out['m_final_norm_g'], 'v_norm_g': out['v_norm_g'], 'v_fox_w_in': out['v_fox_w_in'], 'v_fox_b_f': out['v_fox_b_f'], 'v_fox_w_out': out['v_fox_w_out'], 'v_conv_w_in': out['v_conv_w_in'], 'v_conv_b_in': out['v_conv_b_in'], 'v_conv_dw': out['v_conv_dw'], 'v_conv_dw_b': out['v_conv_dw_b'], 'v_conv_ln_g': out['v_conv_ln_g'], 'v_conv_ln_b': out['v_conv_ln_b'], 'v_conv_w_out': out['v_conv_w_out'], 'v_final_norm_g': out['v_final_norm_g']}


def _loss(weights, diff, rest, loss_target):
    with _jax.named_scope("forward"):
        args = {**rest, TWIN_DIFF_INPUT: diff, **{k: w.astype(_WEIGHT_DTYPES[k]) for k, w in weights.items()}}
        y = _forward(args)
    with _jax.named_scope("loss_head"):
        err = _jnp.square(y.astype(_jnp.float32) - loss_target)
        return 0.5 * _jnp.sum(_jnp.mean(err, axis=-1)) if err.ndim else 0.5 * err


def _adamw(w, g, m, v):
    m = ADAM_B1 * m + (1.0 - ADAM_B1) * g
    v = ADAM_B2 * v + (1.0 - ADAM_B2) * _jnp.square(g)
    m_hat = m / (1.0 - ADAM_B1 ** ADAM_STEP)
    v_hat = v / (1.0 - ADAM_B2 ** ADAM_STEP)
    delta = -ADAM_LR * (m_hat / (_jnp.sqrt(v_hat) + ADAM_EPS) + ADAM_WD * w)
    return delta, m, v


def reference(x, norm_g, fox_w_in, fox_b_f, fox_w_out, conv_w_in, conv_b_in, conv_dw, conv_dw_b, conv_ln_g, conv_ln_b, conv_w_out, final_norm_g, loss_target, m_norm_g, m_fox_w_in, m_fox_b_f, m_fox_w_out, m_conv_w_in, m_conv_b_in, m_conv_dw, m_conv_dw_b, m_conv_ln_g, m_conv_ln_b, m_conv_w_out, m_final_norm_g, v_norm_g, v_fox_w_in, v_fox_b_f, v_fox_w_out, v_conv_w_in, v_conv_b_in, v_conv_dw, v_conv_dw_b, v_conv_ln_g, v_conv_ln_b, v_conv_w_out, v_final_norm_g):
    given = dict(x=x, norm_g=norm_g, fox_w_in=fox_w_in, fox_b_f=fox_b_f, fox_w_out=fox_w_out, conv_w_in=conv_w_in, conv_b_in=conv_b_in, conv_dw=conv_dw, conv_dw_b=conv_dw_b, conv_ln_g=conv_ln_g, conv_ln_b=conv_ln_b, conv_w_out=conv_w_out, final_norm_g=final_norm_g, loss_target=loss_target, m_norm_g=m_norm_g, m_fox_w_in=m_fox_w_in, m_fox_b_f=m_fox_b_f, m_fox_w_out=m_fox_w_out, m_conv_w_in=m_conv_w_in, m_conv_b_in=m_conv_b_in, m_conv_dw=m_conv_dw, m_conv_dw_b=m_conv_dw_b, m_conv_ln_g=m_conv_ln_g, m_conv_ln_b=m_conv_ln_b, m_conv_w_out=m_conv_w_out, m_final_norm_g=m_final_norm_g, v_norm_g=v_norm_g, v_fox_w_in=v_fox_w_in, v_fox_b_f=v_fox_b_f, v_fox_w_out=v_fox_w_out, v_conv_w_in=v_conv_w_in, v_conv_b_in=v_conv_b_in, v_conv_dw=v_conv_dw, v_conv_dw_b=v_conv_dw_b, v_conv_ln_g=v_conv_ln_g, v_conv_ln_b=v_conv_ln_b, v_conv_w_out=v_conv_w_out, v_final_norm_g=v_final_norm_g)
    weights = {n: given[n] for n in TWIN_WEIGHTS}
    shared = {n: given[n] for n in SHARED_INPUTS}
    per_example = {n: given[n] for n in ['x']}
    grad_fn = _jax.value_and_grad(_loss, argnums=(0, 1))

    def one_microbatch(ex, loss_target):
        ex = dict(ex)
        diff = ex.pop(TWIN_DIFF_INPUT)
        return grad_fn(weights, diff, {**shared, **ex}, loss_target)

    if N_MICROBATCH == 1:
        loss, (grad_w, grad_x) = one_microbatch(per_example, given["loss_target"])
    else:
        def body(carry, xs):
            loss_sum, grad_sum = carry
            l_k, (gw_k, gx_k) = one_microbatch(xs[0], xs[1])
            with _jax.named_scope("update"):
                return (loss_sum + l_k, _jax.tree.map(_jnp.add, grad_sum, gw_k)), gx_k

        init = (_jnp.zeros((), _jnp.float32), _jax.tree.map(_jnp.zeros_like, weights))
        (loss, grad_w), grad_x = _jax.lax.scan(body, init, (per_example, given["loss_target"]))
    with _jax.named_scope("update"):
        delta_w, new_m, new_v = {}, {}, {}
        for n in TWIN_WEIGHTS:
            delta_w[n], new_m[n], new_v[n] = _adamw(weights[n], grad_w[n], given["m_" + n], given["v_" + n])
    return (loss, grad_x, *[grad_w[n] for n in TWIN_WEIGHTS], *[delta_w[n] for n in TWIN_WEIGHTS],
            *[new_m[n] for n in TWIN_WEIGHTS], *[new_v[n] for n in TWIN_WEIGHTS])
```

```python
import functools
import math

import jax
import jax.numpy as jnp
from jax import lax
from jax.experimental import pallas as pl
from jax.experimental.pallas import tpu as pltpu

F32 = jnp.float32
BF16 = jnp.bfloat16
MESH = pl.DeviceIdType.MESH

RMS_EPS = 1e-6
LN_EPS = 1e-5
CONV_K = 31
HALO = 32
HEAD_DIM = 128
ADAM_LR = 0.001
ADAM_B1 = 0.9
ADAM_B2 = 0.999
ADAM_EPS = 1e-08
ADAM_WD = 0.01
ADAM_STEP = 10
N_CHIPS = 4
N_DEV = 8
VMEM_LIMIT = 56 * 1024 * 1024
NEG_BIG = -1e30
SMALL_ROWS = 88
PACK_ROWS = 80


def _pcall(body, **kw):
    return pl.pallas_call(body, **kw)


def _cparams(sem=None):
    return pltpu.CompilerParams(dimension_semantics=sem, vmem_limit_bytes=VMEM_LIMIT)


def _tile(n, cap, mult=128):
    if n <= cap:
        return n
    t = (cap // mult) * mult
    while t >= mult:
        if n % t == 0:
            return t
        t -= mult
    raise ValueError(f"no tile for {n} under {cap}")


def _sigmoid(x):
    return 1.0 / (1.0 + jnp.exp(-x))


def _split3(x):
    hi = x.astype(BF16).astype(F32)
    r = x - hi
    mid = r.astype(BF16).astype(F32)
    lo = (r - mid).astype(BF16).astype(F32)
    return hi, mid, lo


_DN = {
    "nn": (((1,), (0,)), ((), ())),
    "nt": (((1,), (1,)), ((), ())),
    "tn": (((0,), (0,)), ((), ())),
}


def _matmul(a, b, *, contract, grid, a_spec, b_spec, o_spec, out_shape, acc_shape, name,
            bias=None, bias_spec=None, res=None, res_spec=None, alias_res=False):
    nk = grid[2]
    has_bias = bias is not None
    has_res = res is not None

    def body(*refs):
        a_ref, b_ref = refs[0], refs[1]
        pos = 2
        bias_ref = res_ref = None
        if has_bias:
            bias_ref = refs[pos]
            pos += 1
        if has_res:
            res_ref = refs[pos]
            pos += 1
        o_ref = refs[pos]
        acc_ref = refs[pos + 1] if nk > 1 else None
        p = lax.dot_general(a_ref[...].astype(BF16), b_ref[...].astype(BF16), _DN[contract],
                            preferred_element_type=F32)

        def finish(v):
            if has_bias:
                v = v + bias_ref[...]
            if has_res:
                v = res_ref[...] + v
            o_ref[...] = v.astype(o_ref.dtype)

        if nk == 1:
            finish(p)
        else:
            k = pl.program_id(2)

            @pl.when(k == 0)
            def _():
                acc_ref[...] = p

            @pl.when(k > 0)
            def _():
                acc_ref[...] += p

            @pl.when(k == nk - 1)
            def _():
                finish(acc_ref[...])

    ins = [a, b]
    specs = [a_spec, b_spec]
    if has_bias:
        ins.append(bias)
        specs.append(bias_spec)
    if has_res:
        ins.append(res)
        specs.append(res_spec)
    aliases = {len(ins) - 1: 0} if (has_res and alias_res) else {}
    return _pcall(
        body, name=name, grid=grid, in_specs=specs, out_specs=o_spec, out_shape=out_shape,
        scratch_shapes=[pltpu.VMEM(acc_shape, F32)] if nk > 1 else [],
        input_output_aliases=aliases,
        compiler_params=_cparams(("parallel", "parallel", "arbitrary")),
    )(*ins)


def mm_nn(a, w, lidx, n_slots, *, out_dtype, name, bias=None):
    M, K = a.shape
    Ns = w.shape[-1]
    tm = _tile(M, 1024)
    tn = _tile(Ns, 1024)
    per = Ns // tn
    grid = (M // tm, n_slots * per, 1)
    return _matmul(
        a, w, contract="nn", grid=grid, name=name,
        a_spec=pl.BlockSpec((tm, K), lambda i, j, k: (i, 0)),
        b_spec=pl.BlockSpec((None, None, K, tn), lambda i, j, k: (lidx, j // per, 0, j % per)),
        o_spec=pl.BlockSpec((tm, tn), lambda i, j, k: (i, j)),
        out_shape=jax.ShapeDtypeStruct((M, n_slots * Ns), out_dtype), acc_shape=(tm, tn),
        bias=bias, bias_spec=None if bias is None else pl.BlockSpec((1, tn), lambda i, j, k: (0, j)),
    )


def mm_nt(a, w, lidx, n_slots, *, name, res=None):
    M = a.shape[0]
    N, Ns = w.shape[-2], w.shape[-1]
    tm = _tile(M, 1024)
    tn = _tile(N, 1024)
    tk = _tile(Ns, 2048)
    per = Ns // tk
    grid = (M // tm, N // tn, n_slots * per)
    return _matmul(
        a, w, contract="nt", grid=grid, name=name,
        a_spec=pl.BlockSpec((tm, tk), lambda i, j, k: (i, k)),
        b_spec=pl.BlockSpec((None, None, tn, tk), lambda i, j, k: (lidx, k // per, j, k % per)),
        o_spec=pl.BlockSpec((tm, tn), lambda i, j, k: (i, j)),
        out_shape=jax.ShapeDtypeStruct((M, N), F32), acc_shape=(tm, tn),
        res=res, res_spec=None if res is None else pl.BlockSpec((tm, tn), lambda i, j, k: (i, j)),
        alias_res=res is not None,
    )


def mm_tn(a, b, n_slots, *, name):
    S, M = a.shape
    Ns = b.shape[1] // n_slots
    tm = _tile(M, 1024)
    tn = _tile(Ns, 1024)
    tk = _tile(S, 2048)
    per = Ns // tn
    grid = (M // tm, n_slots * per, S // tk)
    return _matmul(
        a, b, contract="tn", grid=grid, name=name,
        a_spec=pl.BlockSpec((tk, tm), lambda i, j, k: (k, i)),
        b_spec=pl.BlockSpec((tk, tn), lambda i, j, k: (k, j)),
        o_spec=pl.BlockSpec((None, tm, tn), lambda i, j, k: (j // per, i, j % per)),
        out_shape=jax.ShapeDtypeStruct((n_slots, M, Ns), F32), acc_shape=(tm, tn),
    )


def rms_fwd(h, g, name):
    S, D = h.shape
    tm = _tile(S, 256, 8)

    def body(h_ref, g_ref, o_ref):
        x = h_ref[...]
        r = lax.rsqrt(jnp.mean(x * x, axis=-1, keepdims=True) + RMS_EPS)
        o_ref[...] = (x * r * g_ref[...]).astype(BF16)

    return _pcall(
        body, name=name, grid=(S // tm,),
        in_specs=[pl.BlockSpec((tm, D), lambda i: (i, 0)), pl.BlockSpec((1, D), lambda i: (0, 0))],
        out_specs=pl.BlockSpec((tm, D), lambda i: (i, 0)),
        out_shape=jax.ShapeDtypeStruct((S, D), BF16),
        compiler_params=_cparams(("parallel",)),
    )(h, g)


def _rms_bwd_rows(x, g, dy):
    d = x.shape[-1]
    r = lax.rsqrt(jnp.mean(x * x, axis=-1, keepdims=True) + RMS_EPS)
    gd = dy * g
    dx = r * gd - x * ((r * r * r) * (jnp.sum(x * gd, axis=-1, keepdims=True) / d))
    return dx, dy * x * r


def rms_bwd(dhn, h, g, dres, name):
    S, D = h.shape
    tm = _tile(S, 256, 8)

    def body(dhn_ref, h_ref, g_ref, dres_ref, dh_ref, dg_ref):
        dx, dgr = _rms_bwd_rows(h_ref[...], g_ref[...], dhn_ref[...])
        dh_ref[...] = dres_ref[...] + dx

        @pl.when(pl.program_id(0) == 0)
        def _():
            dg_ref[...] = jnp.zeros_like(dg_ref)

        dg_ref[...] += jnp.sum(dgr, axis=0, keepdims=True)

    row = pl.BlockSpec((tm, D), lambda i: (i, 0))
    vec = pl.BlockSpec((1, D), lambda i: (0, 0))
    return _pcall(
        body, name=name, grid=(S // tm,),
        in_specs=[row, row, vec, row], out_specs=[row, vec],
        out_shape=[jax.ShapeDtypeStruct((S, D), F32), jax.ShapeDtypeStruct((1, D), F32)],
        input_output_aliases={3: 0},
        compiler_params=_cparams(("arbitrary",)),
    )(dhn, h, g, dres)


def loss_head(h, g, target, name):
    S, D = h.shape
    tm = _tile(S, 256, 8)

    def body(h_ref, g_ref, t_ref, dh_ref, dg_ref, loss_ref):
        x = h_ref[...]
        gg = g_ref[...]
        r = lax.rsqrt(jnp.mean(x * x, axis=-1, keepdims=True) + RMS_EPS)
        y = x * r * gg
        e = y - t_ref[...]
        part = 0.5 * jnp.sum(jnp.mean(e * e, axis=-1, keepdims=True), axis=0, keepdims=True)
        dy = e * (1.0 / D)
        dx, dgr = _rms_bwd_rows(x, gg, dy)
        dh_ref[...] = dx

        @pl.when(pl.program_id(0) == 0)
        def _():
            dg_ref[...] = jnp.zeros_like(dg_ref)
            loss_ref[...] = jnp.zeros_like(loss_ref)

        dg_ref[...] += jnp.sum(dgr, axis=0, keepdims=True)
        loss_ref[...] += jnp.broadcast_to(part, loss_ref.shape)

    row = pl.BlockSpec((tm, D), lambda i: (i, 0))
    vec = pl.BlockSpec((1, D), lambda i: (0, 0))
    return _pcall(
        body, name=name, grid=(S // tm,),
        in_specs=[row, vec, row],
        out_specs=[row, vec, pl.BlockSpec((1, 128), lambda i: (0, 0))],
        out_shape=[jax.ShapeDtypeStruct((S, D), F32), jax.ShapeDtypeStruct((1, D), F32),
                   jax.ShapeDtypeStruct((1, 128), F32)],
        compiler_params=_cparams(("arbitrary",)),
    )(h, g, target)


def fox_decay(pf, bf, n_heads, scale, name):
    S = pf.shape[0]
    tm = _tile(S, 256, 8)
    inv_scale = 1.0 / scale

    def body(pf_ref, bf_ref, qa_ref, ka_ref, carry_ref):
        @pl.when(pl.program_id(0) == 0)
        def _():
            carry_ref[...] = jnp.zeros_like(carry_ref)

        z = pf_ref[...] + bf_ref[...]
        logf = jnp.minimum(z, 0.0) - jnp.log(1.0 + jnp.exp(-jnp.abs(z)))
        row = lax.broadcasted_iota(jnp.int32, (tm, tm), 0)
        col = lax.broadcasted_iota(jnp.int32, (tm, tm), 1)
        tri = (row >= col).astype(F32)
        c = jnp.dot(tri, logf, precision=lax.Precision.HIGHEST, preferred_element_type=F32) + carry_ref[...]
        carry_ref[...] = c[tm - 1:tm, :]
        lane = lax.broadcasted_iota(jnp.int32, (tm, HEAD_DIM), 1)
        for hh in range(n_heads):
            hi, mid, lo = _split3(c[:, hh:hh + 1] * inv_scale)
            qa = jnp.where(lane == 0, hi, jnp.where(lane == 1, mid, jnp.where(lane == 2, lo,
                 jnp.where(lane < 6, 1.0, 0.0))))
            ka = jnp.where(lane < 3, 1.0, jnp.where(lane == 3, -hi, jnp.where(lane == 4, -mid,
                 jnp.where(lane == 5, -lo, jnp.where(lane < 9, 1.0, 0.0)))))
            qa_ref[:, hh * HEAD_DIM:(hh + 1) * HEAD_DIM] = qa.astype(BF16)
            ka_ref[:, hh * HEAD_DIM:(hh + 1) * HEAD_DIM] = ka.astype(BF16)

    wide = pl.BlockSpec((tm, n_heads * HEAD_DIM), lambda i: (i, 0))
    return _pcall(
        body, name=name, grid=(S // tm,),
        in_specs=[pl.BlockSpec((tm, 128), lambda i: (i, 0)), pl.BlockSpec((1, 128), lambda i: (0, 0))],
        out_specs=[wide, wide],
        out_shape=[jax.ShapeDtypeStruct((S, n_heads * HEAD_DIM), BF16)] * 2,
        scratch_shapes=[pltpu.VMEM((1, 128), F32)],
        compiler_params=_cparams(("arbitrary",)),
    )(pf, bf)


def _attn_tile(S):
    return 512 if S % 512 == 0 and S >= 2048 else 128


def attn_fwd(p1, p2, qa, ka, n_heads, scale, name):
    S = p1.shape[0]
    W = n_heads * HEAD_DIM
    t = _attn_tile(S)
    nq = S // t

    def body(q_ref, g_ref, qa_ref, k_ref, v_ref, ka_ref, o_ref, y_ref, qb_ref, m_ref, l_ref, acc_ref):
        i = pl.program_id(1)
        qq = jnp.concatenate([q_ref[...], qa_ref[...]], axis=1)
        m_ref[...] = jnp.full_like(m_ref, NEG_BIG)
        l_ref[...] = jnp.zeros_like(l_ref)
        acc_ref[...] = jnp.zeros_like(acc_ref)

        def block(j, masked):
            rows = pl.ds(pl.multiple_of(j * t, t), t)
            kk = jnp.concatenate([k_ref[rows, :], ka_ref[rows, :]], axis=1)
            s = lax.dot_general(qq, kk, _DN["nt"], preferred_element_type=F32)
            if masked:
                r = lax.broadcasted_iota(jnp.int32, (t, t), 0)
                c = lax.broadcasted_iota(jnp.int32, (t, t), 1)
                s = jnp.where(r >= c, s, NEG_BIG)
            m_old = m_ref[...]
            m_new = jnp.maximum(m_old, jnp.max(s, axis=1, keepdims=True))
            p = jnp.exp(scale * (s - m_new))
            alpha = jnp.exp(scale * (m_old - m_new))
            l_ref[...] = alpha * l_ref[...] + jnp.sum(p, axis=1, keepdims=True)
            acc_ref[...] = alpha * acc_ref[...] + jnp.dot(p.astype(BF16), v_ref[rows, :],
                                                          preferred_element_type=F32)
            m_ref[...] = m_new

        def loop_body(j, carry):
            block(j, False)
            return carry

        lax.fori_loop(0, i, loop_body, 0)
        block(i, True)

        l = l_ref[...]
        o = acc_ref[...] / l
        gate = g_ref[...].astype(F32)
        o_ref[...] = o.astype(BF16)
        y_ref[...] = (o * (gate * _sigmoid(gate))).astype(BF16)
        nlse = -(m_ref[...] + jnp.log(l) * (1.0 / scale))
        hi, mid, lo = _split3(nlse)
        lane = lax.broadcasted_iota(jnp.int32, (t, HEAD_DIM), 1)
        qb = jnp.where(lane == 6, hi, jnp.where(lane == 7, mid, jnp.where(lane == 8, lo, qa_ref[...].astype(F32))))
        qb_ref[...] = qb.astype(BF16)

    H = n_heads
    qtile = lambda off: pl.BlockSpec((t, HEAD_DIM), lambda h, i: (i, off + h))
    full = lambda fn: pl.BlockSpec((S, HEAD_DIM), fn)
    return _pcall(
        body, name=name, grid=(H, nq),
        in_specs=[qtile(0), qtile(H), qtile(0),
                  full(lambda h, i: (0, 2 * h)), full(lambda h, i: (0, 2 * h + 1)), full(lambda h, i: (0, h))],
        out_specs=[qtile(0), qtile(0), qtile(0)],
        out_shape=[jax.ShapeDtypeStruct((S, W), BF16)] * 3,
        scratch_shapes=[pltpu.VMEM((t, 1), F32), pltpu.VMEM((t, 1), F32), pltpu.VMEM((t, HEAD_DIM), F32)],
        compiler_params=_cparams(("parallel", "arbitrary")),
    )(p1, p1, qa, p2, p2, ka)


def attn_bwd_prep(dy, o, p1, n_heads, name):
    S, W = dy.shape
    tm = _tile(S, 256, 8)
    H = n_heads

    def body(dy_ref, o_ref, g_ref, do_ref, dg_ref, da_ref):
        lane = lax.broadcasted_iota(jnp.int32, (tm, HEAD_DIM), 1)
        for hh in range(H):
            cs = slice(hh * HEAD_DIM, (hh + 1) * HEAD_DIM)
            g = g_ref[:, cs].astype(F32)
            oo = o_ref[:, cs].astype(F32)
            dyv = dy_ref[:, cs]
            sg = _sigmoid(g)
            do = dyv * (g * sg)
            do_ref[:, cs] = do.astype(BF16)
            dg_ref[:, cs] = (dyv * oo * (sg * (1.0 + g * (1.0 - sg)))).astype(BF16)
            hi, mid, lo = _split3(-jnp.sum(do * oo, axis=1, keepdims=True))
            da = jnp.where(lane == 0, hi, jnp.where(lane == 1, mid, jnp.where(lane == 2, lo, 0.0)))
            da_ref[:, cs] = da.astype(BF16)

    row = lambda blk: pl.BlockSpec((tm, W), lambda i: (i, blk))
    return _pcall(
        body, name=name, grid=(S // tm,),
        in_specs=[row(0), row(0), row(1)],
        out_specs=[row(0), row(1), row(0)],
        out_shape=[jax.ShapeDtypeStruct((S, W), BF16), jax.ShapeDtypeStruct((S, 2 * W), BF16),
                   jax.ShapeDtypeStruct((S, W), BF16)],
        compiler_params=_cparams(("parallel",)),
    )(dy, o, p1)


def attn_bwd(p1, qb, do, da, p2, ka, dp1, n_heads, scale, name):
    S = p1.shape[0]
    W = n_heads * HEAD_DIM
    t = _attn_tile(S)
    nb = S // t
    H = n_heads

    def body(q_ref, qb_ref, do_ref, da_ref, k_ref, v_ref, ka_ref, dp1_in, dq_ref, dkv_ref, dc_ref,
             dq_acc, dk_acc, dv_acc):
        del dp1_in
        h = pl.program_id(0)
        j = pl.program_id(1)

        @pl.when(j == 0)
        def _():
            dq_acc[...] = jnp.zeros_like(dq_acc)

        @pl.when((j == 0) & (h == 0))
        def _():
            dc_ref[...] = jnp.zeros_like(dc_ref)

        lane = lax.broadcasted_iota(jnp.int32, (t, HEAD_DIM), 1)
        ones3 = jnp.where(lane < 3, 1.0, 0.0).astype(BF16)
        kk = jnp.concatenate([k_ref[...], ka_ref[...]], axis=1)
        vv = jnp.concatenate([v_ref[...], ones3], axis=1)
        dk_acc[...] = jnp.zeros_like(dk_acc)
        dv_acc[...] = jnp.zeros_like(dv_acc)

        def block(i, masked):
            rows = pl.ds(pl.multiple_of(i * t, t), t)
            qq = jnp.concatenate([q_ref[rows, :], qb_ref[rows, :]], axis=1)
            dd = jnp.concatenate([do_ref[rows, :], da_ref[rows, :]], axis=1)
            a = lax.dot_general(qq, kk, _DN["nt"], preferred_element_type=F32)
            p = jnp.exp(scale * a)
            if masked:
                r = lax.broadcasted_iota(jnp.int32, (t, t), 0)
                c = lax.broadcasted_iota(jnp.int32, (t, t), 1)
                p = jnp.where(r >= c, p, 0.0)
            dpd = lax.dot_general(dd, vv, _DN["nt"], preferred_element_type=F32)
            ds = (p * dpd).astype(BF16)
            pb = p.astype(BF16)
            dv_acc[...] += lax.dot_general(pb, dd, _DN["tn"], preferred_element_type=F32)
            dk_acc[...] += lax.dot_general(ds, qq, _DN["tn"], preferred_element_type=F32)
            dq_acc[rows, :] += jnp.dot(ds, kk, preferred_element_type=F32)

        block(j, True)

        def loop_body(i, carry):
            block(i, False)
            return carry

        lax.fori_loop(j + 1, nb, loop_body, 0)

        dkv_ref[...] = jnp.concatenate([dk_acc[:, :HEAD_DIM] * scale, dv_acc[:, :HEAD_DIM]], axis=1).astype(BF16)
        colsum = dk_acc[:, HEAD_DIM + 3:HEAD_DIM + 4]
        krows = pl.ds(pl.multiple_of(j * t, t), t)
        dc_ref[krows, :] += jnp.where(lane == h, -colsum, 0.0)

        @pl.when(j == nb - 1)
        def _():
            dq_ref[...] = (dq_acc[:, :HEAD_DIM] * scale).astype(BF16)
            lane_s = lax.broadcasted_iota(jnp.int32, (S, HEAD_DIM), 1)
            dc_ref[...] += jnp.where(lane_s == h, dq_acc[:, HEAD_DIM:HEAD_DIM + 1], 0.0)

    full = lambda fn: pl.BlockSpec((S, HEAD_DIM), fn)
    ktile = lambda fn: pl.BlockSpec((t, HEAD_DIM), fn)
    return _pcall(
        body, name=name, grid=(H, nb),
        in_specs=[full(lambda h, j: (0, h)), full(lambda h, j: (0, h)), full(lambda h, j: (0, h)),
                  full(lambda h, j: (0, h)),
                  ktile(lambda h, j: (j, 2 * h)), ktile(lambda h, j: (j, 2 * h + 1)), ktile(lambda h, j: (j, h)),
                  pl.BlockSpec(memory_space=pl.ANY)],
        out_specs=[full(lambda h, j: (0, h)),
                   pl.BlockSpec((t, 2 * HEAD_DIM), lambda h, j: (j, h)),
                   pl.BlockSpec((S, 128), lambda h, j: (0, 0))],
        out_shape=[jax.ShapeDtypeStruct((S, 2 * W), BF16), jax.ShapeDtypeStruct((S, 2 * W), BF16),
                   jax.ShapeDtypeStruct((S, 128), F32)],
        scratch_shapes=[pltpu.VMEM((S, 2 * HEAD_DIM), F32), pltpu.VMEM((t, 2 * HEAD_DIM), F32),
                        pltpu.VMEM((t, 2 * HEAD_DIM), F32)],
        input_output_aliases={7: 0},
        compiler_params=_cparams(("arbitrary", "arbitrary")),
    )(p1, qb, do, da, p2, p2, ka, dp1)


def fox_decay_bwd(dc, pf, bf, n_heads, name):
    S = dc.shape[0]
    tm = _tile(S, 256, 8)
    nb = S // tm

    def body(dc_ref, pf_ref, bf_ref, dz_ref, db_ref, carry_ref):
        @pl.when(pl.program_id(0) == 0)
        def _():
            carry_ref[...] = jnp.zeros_like(carry_ref)
            db_ref[...] = jnp.zeros_like(db_ref)

        row = lax.broadcasted_iota(jnp.int32, (tm, tm), 0)
        col = lax.broadcasted_iota(jnp.int32, (tm, tm), 1)
        tri = (row <= col).astype(F32)
        dlogf = jnp.dot(tri, dc_ref[...], precision=lax.Precision.HIGHEST, preferred_element_type=F32) + carry_ref[...]
        carry_ref[...] = dlogf[0:1, :]
        z = pf_ref[...] + bf_ref[...]
        lane = lax.broadcasted_iota(jnp.int32, (tm, 128), 1)
        dz = jnp.where(lane < n_heads, dlogf * _sigmoid(-z), 0.0)
        dz_ref[...] = dz.astype(BF16)
        db_ref[...] += jnp.sum(dz, axis=0, keepdims=True)

    rev = pl.BlockSpec((tm, 128), lambda i: (nb - 1 - i, 0))
    vec = pl.BlockSpec((1, 128), lambda i: (0, 0))
    return _pcall(
        body, name=name, grid=(nb,),
        in_specs=[rev, rev, vec], out_specs=[rev, vec],
        out_shape=[jax.ShapeDtypeStruct((S, 128), BF16), jax.ShapeDtypeStruct((1, 128), F32)],
        scratch_shapes=[pltpu.VMEM((1, 128), F32)],
        compiler_params=_cparams(("arbitrary",)),
    )(dc, pf, bf)


def _conv_tile(S):
    return _tile(S, 256, HALO)


def _fill_glu(ubuf, a_ref, b_ref, ah_ref, bh_ref, first, tm):
    uh = ah_ref[...].astype(F32) * _sigmoid(bh_ref[...].astype(F32))
    ubuf[0:HALO, :] = jnp.where(first, 0.0, uh)
    ubuf[HALO:HALO + tm, :] = a_ref[...].astype(F32) * _sigmoid(b_ref[...].astype(F32))


def conv_fwd(proj, dw, dwb, lng, lnb, name):
    S = proj.shape[0]
    C = proj.shape[1] // 3
    tm = _conv_tile(S)
    hb = tm // HALO
    nch = C // 128

    def body(a_ref, b_ref, g_ref, ah_ref, bh_ref, dw_ref, dwb_ref, lng_ref, lnb_ref, u2_ref, y_ref, ubuf):
        i = pl.program_id(0)
        _fill_glu(ubuf, a_ref, b_ref, ah_ref, bh_ref, i == 0, tm)

        def chunk(cc, carry):
            cols = pl.ds(pl.multiple_of(cc * 128, 128), 128)
            acc = jnp.broadcast_to(dwb_ref[:, cols], (tm, 128))
            for k in range(CONV_K):
                acc = acc + dw_ref[k:k + 1, cols] * ubuf[pl.ds(HALO - (CONV_K - 1) + k, tm), cols]
            u2_ref[:, cols] = acc
            return carry

        lax.fori_loop(0, nch, chunk, 0)
        x = u2_ref[...]
        mu = jnp.mean(x, axis=-1, keepdims=True)
        xc = x - mu
        var = jnp.mean(xc * xc, axis=-1, keepdims=True)
        ln = xc * lax.rsqrt(var + LN_EPS) * lng_ref[...] + lnb_ref[...]
        gate = g_ref[...].astype(F32)
        y_ref[...] = ((ln * _sigmoid(ln)) * (gate * _sigmoid(gate))).astype(BF16)

    blk = lambda cb: pl.BlockSpec((tm, C), lambda i: (i, cb))
    halo = lambda cb: pl.BlockSpec((HALO, C), lambda i: (jnp.maximum(i * hb - 1, 0), cb))
    vec = pl.BlockSpec((1, C), lambda i: (0, 0))
    return _pcall(
        body, name=name, grid=(S // tm,),
        in_specs=[blk(0), blk(1), blk(2), halo(0), halo(1),
                  pl.BlockSpec((CONV_K, C), lambda i: (0, 0)), vec, vec, vec],
        out_specs=[blk(0), blk(0)],
        out_shape=[jax.ShapeDtypeStruct((S, C), F32), jax.ShapeDtypeStruct((S, C), BF16)],
        scratch_shapes=[pltpu.VMEM((HALO + tm, C), F32)],
        compiler_params=_cparams(("parallel",)),
    )(proj, proj, proj, proj, proj, dw, dwb, lng, lnb)


def conv_bwd_norm(dy, proj, u2, lng, lnb, name):
    S, C = dy.shape
    tm = _tile(S, 256, 8)

    def body(dy_ref, g_ref, u2_ref, lng_ref, lnb_ref, du2_ref, dg_ref, sm_ref):
        x = u2_ref[...]
        mu = jnp.mean(x, axis=-1, keepdims=True)
        xc = x - mu
        var = jnp.mean(xc * xc, axis=-1, keepdims=True)
        rs = lax.rsqrt(var + LN_EPS)
        xhat = xc * rs
        gam = lng_ref[...]
        ln = xhat * gam + lnb_ref[...]
        sl = _sigmoid(ln)
        u3 = ln * sl
        gate = g_ref[...].astype(F32)
        sg = _sigmoid(gate)
        dyv = dy_ref[...]
        dgate = dyv * u3 * (sg * (1.0 + gate * (1.0 - sg)))
        dln = (dyv * (gate * sg)) * (sl * (1.0 + ln * (1.0 - sl)))
        dxh = dln * gam
        du2 = rs * (dxh - jnp.mean(dxh, axis=-1, keepdims=True)
                    - xhat * jnp.mean(dxh * xhat, axis=-1, keepdims=True))
        du2_ref[...] = du2
        dg_ref[...] = dgate.astype(BF16)

        @pl.when(pl.program_id(0) == 0)
        def _():
            sm_ref[...] = jnp.zeros_like(sm_ref)

        sm_ref[0:1, :] += jnp.sum(dln * xhat, axis=0, keepdims=True)
        sm_ref[1:2, :] += jnp.sum(dln, axis=0, keepdims=True)
        sm_ref[2:3, :] += jnp.sum(du2, axis=0, keepdims=True)
        sm_ref[3:4, :] += jnp.sum(dgate, axis=0, keepdims=True)

    blk = lambda cb: pl.BlockSpec((tm, C), lambda i: (i, cb))
    vec = pl.BlockSpec((1, C), lambda i: (0, 0))
    return _pcall(
        body, name=name, grid=(S // tm,),
        in_specs=[blk(0), blk(2), blk(0), vec, vec],
        out_specs=[blk(0), blk(2), pl.BlockSpec((8, C), lambda i: (0, 0))],
        out_shape=[jax.ShapeDtypeStruct((S, C), F32), jax.ShapeDtypeStruct((S, 3 * C), BF16),
                   jax.ShapeDtypeStruct((8, C), F32)],
        compiler_params=_cparams(("arbitrary",)),
    )(dy, proj, u2, lng, lnb)


def conv_bwd_taps(du2, proj, dw, dproj, name):
    S, C = du2.shape
    tm = _conv_tile(S)
    hb = tm // HALO
    nb = S // tm
    nch = C // 128

    def body(d_ref, dh_ref, a_ref, b_ref, ah_ref, bh_ref, dw_ref, dp_in, dab_ref, sm_ref, ubuf, dbuf):
        del dp_in
        i = pl.program_id(0)
        _fill_glu(ubuf, a_ref, b_ref, ah_ref, bh_ref, i == 0, tm)
        dbuf[0:tm, :] = d_ref[...]
        dbuf[tm:tm + HALO, :] = jnp.where(i == nb - 1, 0.0, dh_ref[...])

        @pl.when(i == 0)
        def _():
            sm_ref[...] = jnp.zeros_like(sm_ref)

        def chunk(cc, carry):
            cols = pl.ds(pl.multiple_of(cc * 128, 128), 128)
            d0 = dbuf[0:tm, cols]
            du = jnp.zeros((tm, 128), F32)
            for k in range(CONV_K):
                du = du + dw_ref[k:k + 1, cols] * dbuf[pl.ds(CONV_K - 1 - k, tm), cols]
                sm_ref[k:k + 1, cols] += jnp.sum(d0 * ubuf[pl.ds(HALO - (CONV_K - 1) + k, tm), cols],
                                                 axis=0, keepdims=True)
            a = a_ref[:, cols].astype(F32)
            sb = _sigmoid(b_ref[:, cols].astype(F32))
            da = du * sb
            db = du * a * sb * (1.0 - sb)
            dab_ref[:, cols] = da.astype(BF16)
            dab_ref[:, pl.ds(pl.multiple_of(C + cc * 128, 128), 128)] = db.astype(BF16)
            sm_ref[32:33, cols] += jnp.sum(da, axis=0, keepdims=True)
            sm_ref[33:34, cols] += jnp.sum(db, axis=0, keepdims=True)
            return carry

        lax.fori_loop(0, nch, chunk, 0)

    blk = lambda cb: pl.BlockSpec((tm, C), lambda i: (i, cb))
    halo = lambda cb: pl.BlockSpec((HALO, C), lambda i: (jnp.maximum(i * hb - 1, 0), cb))
    nxt = pl.BlockSpec((HALO, C), lambda i: (jnp.minimum((i + 1) * hb, nb * hb - 1), 0))
    return _pcall(
        body, name=name, grid=(nb,),
        in_specs=[blk(0), nxt, blk(0), blk(1), halo(0), halo(1),
                  pl.BlockSpec((CONV_K, C), lambda i: (0, 0)), pl.BlockSpec(memory_space=pl.ANY)],
        out_specs=[pl.BlockSpec((tm, 2 * C), lambda i: (i, 0)), pl.BlockSpec((40, C), lambda i: (0, 0))],
        out_shape=[jax.ShapeDtypeStruct((S, 3 * C), BF16), jax.ShapeDtypeStruct((40, C), F32)],
        scratch_shapes=[pltpu.VMEM((HALO + tm, C), F32), pltpu.VMEM((tm + HALO, C), F32)],
        input_output_aliases={7: 0},
        compiler_params=_cparams(("arbitrary",)),
    )(du2, du2, proj, proj, proj, proj, dw, dproj)


def _rows_tile(R, Cc, budget=1 << 18):
    cap = max(8, budget // max(Cc, 1))
    if R <= cap:
        return R
    t = (cap // 8) * 8
    while t >= 8:
        if R % t == 0:
            return t
        t -= 8
    return R


def cast_bf16(w, name):
    shape = w.shape
    R, Cc = shape[-2], shape[-1]
    w3 = w.reshape((-1, R, Cc))
    tr = _rows_tile(R, Cc)

    def body(w_ref, o_ref):
        o_ref[...] = w_ref[...].astype(BF16)

    spec = pl.BlockSpec((None, tr, Cc), lambda l, r: (l, r, 0))
    out = _pcall(
        body, name=name, grid=(w3.shape[0], R // tr), in_specs=[spec], out_specs=spec,
        out_shape=jax.ShapeDtypeStruct(w3.shape, BF16), compiler_params=_cparams(("parallel", "parallel")),
    )(w3)
    return out.reshape(shape)


def pair_sum(g, rcv, cidx, name):
    L, K, R, Cc = g.shape
    half = R // 2
    g5 = g.reshape((L, K, 2, half, Cc))
    tr = _rows_tile(half, Cc)

    def body(c_ref, g_ref, r_ref, o_ref):
        del c_ref
        o_ref[...] = (g_ref[...] + r_ref[...]).astype(BF16)

    grid_spec = pltpu.PrefetchScalarGridSpec(
        num_scalar_prefetch=1, grid=(L, K, half // tr),
        in_specs=[pl.BlockSpec((None, None, None, tr, Cc), lambda l, k, r, c: (l, k, c[0], r, 0)),
                  pl.BlockSpec((None, None, tr, Cc), lambda l, k, r, c: (l, k, r, 0))],
        out_specs=pl.BlockSpec((None, None, tr, Cc), lambda l, k, r, c: (l, k, r, 0)),
    )
    return _pcall(
        body, name=name, grid_spec=grid_spec, out_shape=jax.ShapeDtypeStruct((L, K, half, Cc), BF16),
        compiler_params=_cparams(("parallel", "parallel", "parallel")),
    )(cidx, g5, rcv)


def chip_sum(parts, name):
    L, K, R, Cc = parts.shape
    tr = _rows_tile(R, Cc)

    def body(p_ref, o_ref):
        acc = p_ref[0].astype(F32)
        for k in range(1, K):
            acc = acc + p_ref[k].astype(F32)
        o_ref[...] = acc

    return _pcall(
        body, name=name, grid=(L, R // tr),
        in_specs=[pl.BlockSpec((None, K, tr, Cc), lambda l, r: (l, 0, r, 0))],
        out_specs=pl.BlockSpec((None, tr, Cc), lambda l, r: (l, r, 0)),
        out_shape=jax.ShapeDtypeStruct((L, R, Cc), F32), compiler_params=_cparams(("parallel", "parallel")),
    )(parts)


def dev_sum(parts, name):
    K, R, Cc = parts.shape

    def body(p_ref, o_ref):
        acc = p_ref[0]
        for k in range(1, K):
            acc = acc + p_ref[k]
        o_ref[...] = acc

    return _pcall(
        body, name=name, grid=(R // 8,),
        in_specs=[pl.BlockSpec((K, 8, Cc), lambda r: (0, r, 0))],
        out_specs=pl.BlockSpec((8, Cc), lambda r: (r, 0)),
        out_shape=jax.ShapeDtypeStruct((R, Cc), F32), compiler_params=_cparams(("parallel",)),
    )(parts)


def adamw(w, g, m, v, name):
    shape = w.shape
    if w.ndim == 1:
        R, Cc = 1, shape[0]
    else:
        R, Cc = math.prod(shape[:-1]), shape[-1]
    flat = lambda t: t.reshape((R, Cc))
    tr = _rows_tile(R, Cc, budget=1 << 17)
    c1 = 1.0 - ADAM_B1 ** ADAM_STEP
    c2 = 1.0 - ADAM_B2 ** ADAM_STEP

    def body(w_ref, g_ref, m_ref, v_ref, d_ref, nm_ref, nv_ref):
        gg = g_ref[...]
        nm = ADAM_B1 * m_ref[...] + (1.0 - ADAM_B1) * gg
        nv = ADAM_B2 * v_ref[...] + (1.0 - ADAM_B2) * (gg * gg)
        d_ref[...] = -ADAM_LR * ((nm / c1) / (jnp.sqrt(nv / c2) + ADAM_EPS) + ADAM_WD * w_ref[...])
        nm_ref[...] = nm
        nv_ref[...] = nv

    spec = pl.BlockSpec((tr, Cc), lambda r: (r, 0))
    outs = _pcall(
        body, name=name, grid=(R // tr,), in_specs=[spec] * 4, out_specs=[spec] * 3,
        out_shape=[jax.ShapeDtypeStruct((R, Cc), F32)] * 3, compiler_params=_cparams(("parallel",)),
    )(flat(w), flat(g), flat(m), flat(v))
    return tuple(o.reshape(shape) for o in outs)


def _place():
    x, y, c = lax.axis_index("x"), lax.axis_index("y"), lax.axis_index("c")
    other_chips = [(1 - x, y), (x, 1 - y), (1 - x, 1 - y)]
    return x, y, c, other_chips


def _rcopy(src, dst, ssem, rsem, k, to):
    return pltpu.make_async_remote_copy(src_ref=src, dst_ref=dst, send_sem=ssem.at[k], recv_sem=rsem.at[k],
                                        device_id=to, device_id_type=MESH)


def gather_weights(shards, small):
    nt = len(shards)

    def body(*refs):
        ins, small_in = refs[:nt], refs[nt]
        outs, small_out = refs[nt + 1:2 * nt + 1], refs[2 * nt + 1]
        ssem, rsem, lsem = refs[2 * nt + 2:]
        x, y, c, chips = _place()
        me = 2 * x + y
        sib = (x, y, 1 - c)

        def rows(t, half_of):
            half = ins[t].shape[1] // 2
            return pl.ds(half_of * half, half)

        local = [pltpu.make_async_copy(ins[t], outs[t].at[:, me], lsem.at[t]) for t in range(nt)]
        local.append(pltpu.make_async_copy(small_in, small_out.at[me], lsem.at[nt]))
        for cp in local:
            cp.start()
        first, passed = [], []
        for t in range(nt):
            for j, chip in enumerate(chips):
                first.append(_rcopy(ins[t].at[:, rows(t, c), :], outs[t].at[:, me, rows(t, c), :],
                                    ssem, rsem, 6 * t + j, (*chip, c)))
        for j, chip in enumerate(chips):
            first.append(_rcopy(small_in, small_out.at[me], ssem, rsem, 6 * nt + j, (*chip, c)))
        for cp in first:
            cp.start()
        for t in range(nt):
            for j, (px, py) in enumerate(chips):
                land = outs[t].at[:, 2 * px + py, rows(t, c), :]
                _rcopy(land, land, ssem, rsem, 6 * t + j, (x, y, c)).wait_recv()
                fw = _rcopy(land, land, ssem, rsem, 6 * t + 3 + j, sib)
                fw.start()
                passed.append(fw)
        for j, (px, py) in enumerate(chips):
            land = small_out.at[2 * px + py]
            _rcopy(land, land, ssem, rsem, 6 * nt + j, (x, y, c)).wait_recv()
        for t in range(nt):
            for j, (px, py) in enumerate(chips):
                land = outs[t].at[:, 2 * px + py, rows(t, 1 - c), :]
                _rcopy(land, land, ssem, rsem, 6 * t + 3 + j, (x, y, c)).wait_recv()
        for cp in first + passed:
            cp.wait_send()
        for cp in local:
            cp.wait()

    out_shape = [jax.ShapeDtypeStruct((s.shape[0], N_CHIPS) + s.shape[1:], s.dtype) for s in shards]
    out_shape.append(jax.ShapeDtypeStruct((N_CHIPS,) + small.shape, small.dtype))
    nsem = 6 * nt + 3
    return _pcall(
        body, name="gather_weights", out_shape=out_shape,
        in_specs=[pl.BlockSpec(memory_space=pl.ANY)] * (nt + 1),
        out_specs=[pl.BlockSpec(memory_space=pl.ANY)] * (nt + 1),
        scratch_shapes=[pltpu.SemaphoreType.DMA((nsem,)), pltpu.SemaphoreType.DMA((nsem,)),
                        pltpu.SemaphoreType.DMA((nt + 1,))],
    )(*shards, small)


def swap_halves(grads):
    nt = len(grads)

    def body(*refs):
        ins, outs = refs[:nt], refs[nt:2 * nt]
        ssem, rsem = refs[2 * nt:]
        x, y, c, _ = _place()
        sib = (x, y, 1 - c)
        cps = []
        for t in range(nt):
            half = ins[t].shape[2] // 2
            cps.append(_rcopy(ins[t].at[:, :, pl.ds((1 - c) * half, half), :], outs[t], ssem, rsem, t, sib))
        for cp in cps:
            cp.start()
        for cp in cps:
            cp.wait()

    out_shape = [jax.ShapeDtypeStruct(g.shape[:2] + (g.shape[2] // 2, g.shape[3]), g.dtype) for g in grads]
    return _pcall(
        body, name="swap_halves", out_shape=out_shape,
        in_specs=[pl.BlockSpec(memory_space=pl.ANY)] * nt, out_specs=[pl.BlockSpec(memory_space=pl.ANY)] * nt,
        scratch_shapes=[pltpu.SemaphoreType.DMA((nt,)), pltpu.SemaphoreType.DMA((nt,))],
    )(*grads)


def exchange_partials(parts, small):
    nt = len(parts)

    def body(*refs):
        ins, small_in = refs[:nt], refs[nt]
        outs, small_out = refs[nt + 1:2 * nt + 1], refs[2 * nt + 1]
        ssem, rsem, lsem = refs[2 * nt + 2:]
        x, y, c, chips = _place()
        me = 2 * x + y
        dev = 4 * x + 2 * y + c
        local = [pltpu.make_async_copy(ins[t].at[:, me], outs[t].at[:, me], lsem.at[t]) for t in range(nt)]
        local.append(pltpu.make_async_copy(small_in, small_out.at[dev], lsem.at[nt]))
        for cp in local:
            cp.start()
        sends, lands = [], []
        for t in range(nt):
            for j, (px, py) in enumerate(chips):
                k = 3 * t + j
                sends.append(_rcopy(ins[t].at[:, 2 * px + py], outs[t].at[:, me], ssem, rsem, k, (px, py, c)))
                land = outs[t].at[:, 2 * px + py]
                lands.append(_rcopy(land, land, ssem, rsem, k, (x, y, c)))
        peers = [(px, py, pc) for pc in (c, 1 - c) for (px, py) in [(x, y)] + chips][1:]
        for j, (px, py, pc) in enumerate(peers):
            k = 3 * nt + j
            sends.append(_rcopy(small_in, small_out.at[dev], ssem, rsem, k, (px, py, pc)))
            land = small_out.at[4 * px + 2 * py + pc]
            lands.append(_rcopy(land, land, ssem, rsem, k, (x, y, c)))
        for cp in sends:
            cp.start()
        for cp in lands:
            cp.wait_recv()
        for cp in sends:
            cp.wait_send()
        for cp in local:
            cp.wait()

    out_shape = [jax.ShapeDtypeStruct(p.shape, p.dtype) for p in parts]
    out_shape.append(jax.ShapeDtypeStruct((N_DEV,) + small.shape, small.dtype))
    nsem = 3 * nt + 7
    return _pcall(
        body, name="exchange_partials", out_shape=out_shape,
        in_specs=[pl.BlockSpec(memory_space=pl.ANY)] * (nt + 1),
        out_specs=[pl.BlockSpec(memory_space=pl.ANY)] * (nt + 1),
        scratch_shapes=[pltpu.SemaphoreType.DMA((nsem,)), pltpu.SemaphoreType.DMA((nsem,)),
                        pltpu.SemaphoreType.DMA((nt + 1,))],
    )(*parts, small)


def join_halves(halves):
    nt = len(halves)

    def body(*refs):
        ins, outs = refs[:nt], refs[nt:2 * nt]
        ssem, rsem, lsem = refs[2 * nt:]
        x, y, c, _ = _place()
        sib = (x, y, 1 - c)
        local = [pltpu.make_async_copy(ins[t], outs[t].at[:, c], lsem.at[t]) for t in range(nt)]
        sends = [_rcopy(ins[t], outs[t].at[:, c], ssem, rsem, t, sib) for t in range(nt)]
        for cp in local + sends:
            cp.start()
        for t in range(nt):
            land = outs[t].at[:, 1 - c]
            _rcopy(land, land, ssem, rsem, t, (x, y, c)).wait_recv()
        for cp in sends:
            cp.wait_send()
        for cp in local:
            cp.wait()

    out_shape = [jax.ShapeDtypeStruct((h.shape[0], 2) + h.shape[1:], h.dtype) for h in halves]
    return _pcall(
        body, name="join_halves", out_shape=out_shape,
        in_specs=[pl.BlockSpec(memory_space=pl.ANY)] * nt, out_specs=[pl.BlockSpec(memory_space=pl.ANY)] * nt,
        scratch_shapes=[pltpu.SemaphoreType.DMA((nt,)), pltpu.SemaphoreType.DMA((nt,)),
                        pltpu.SemaphoreType.DMA((nt,))],
    )(*halves)


def _pad_cols(a, n):
    return jnp.pad(a, [(0, 0)] * (a.ndim - 1) + [(0, n - a.shape[-1])])


def kernel(x, norm_g, fox_w_in, fox_b_f, fox_w_out, conv_w_in, conv_b_in, conv_dw, conv_dw_b, conv_ln_g, conv_ln_b, conv_w_out, final_norm_g, loss_target, m_norm_g, m_fox_w_in, m_fox_b_f, m_fox_w_out, m_conv_w_in, m_conv_b_in, m_conv_dw, m_conv_dw_b, m_conv_ln_g, m_conv_ln_b, m_conv_w_out, m_final_norm_g, v_norm_g, v_fox_w_in, v_fox_b_f, v_fox_w_out, v_conv_w_in, v_conv_b_in, v_conv_dw, v_conv_dw_b, v_conv_ln_g, v_conv_ln_b, v_conv_w_out, v_final_norm_g):
    S, D = x.shape[1], x.shape[2]
    H = fox_b_f.shape[1]
    assert D == H * HEAD_DIM, "one head must be one lane tile"
    W = C = D
    NL = fox_w_in.shape[0]
    Dq = D // N_CHIPS
    NA = fox_w_in.shape[2]
    scale = HEAD_DIM ** -0.5
    chip = 2 * lax.axis_index("x") + lax.axis_index("y")
    cidx = lax.axis_index("c").astype(jnp.int32).reshape((1,))

    small_pack = jnp.concatenate([
        conv_b_in.reshape((NL * 3, Dq)), conv_dw.reshape((NL * CONV_K, Dq)), conv_dw_b, conv_ln_g, conv_ln_b,
        jnp.zeros((PACK_ROWS - NL * (3 + CONV_K + 3), Dq), F32)], axis=0)
    ga, gb, gc, gd, gsmall = gather_weights(
        [cast_bf16(fox_w_in, "cast_fox_w_in"), cast_bf16(fox_w_out, "cast_fox_w_out"),
         cast_bf16(conv_w_in, "cast_conv_w_in"), cast_bf16(conv_w_out, "cast_conv_w_out")], small_pack)
    wfull = jnp.transpose(ga, (0, 2, 1, 3)).reshape((NL, D, N_CHIPS * NA))
    wq, wk, wv, wg, wf = (wfull[:, :, 0:W], wfull[:, :, W:2 * W], wfull[:, :, 2 * W:3 * W],
                          wfull[:, :, 3 * W:4 * W], wfull[:, :, 4 * W:])
    w1 = jnp.concatenate([wq, wg], axis=-1).reshape((NL, 1, D, 2 * W))
    w2 = jnp.stack([wk.reshape((NL, D, H, HEAD_DIM)), wv.reshape((NL, D, H, HEAD_DIM))], axis=3)
    w2 = w2.reshape((NL, 1, D, 2 * W))
    wfp = _pad_cols(wf, 128).reshape((NL, 1, D, 128))
    wo_fox = gb.reshape((NL, 1, W, D))
    wo_conv = gd.reshape((NL, 1, C, D))
    wc = gc
    b_in = gsmall[:, 0:3 * NL, :].reshape((N_CHIPS, NL, 3 * Dq)).transpose((1, 0, 2)).reshape((NL, 3 * C))
    dwt = gsmall[:, 3 * NL:3 * NL + CONV_K * NL, :].reshape((N_CHIPS, NL, CONV_K, Dq))
    dwt = dwt.transpose((1, 2, 0, 3)).reshape((NL, CONV_K, C))
    r0 = (3 + CONV_K) * NL
    vecs = gsmall[:, r0:r0 + 3 * NL, :].reshape((N_CHIPS, 3, NL, Dq)).transpose((1, 2, 0, 3)).reshape((3, NL, C))
    dwb, lng, lnb = vecs[0], vecs[1], vecs[2]
    bfp = _pad_cols(fox_b_f, 128)

    h = x.reshape((S, D))
    tgt = loss_target.reshape((S, D))
    saved = []
    n_layers = norm_g.shape[0]
    for i in range(n_layers):
        j = i // 2
        g_i = norm_g[i:i + 1]
        hn = rms_fwd(h, g_i, f"rms_fwd_{i}")
        if i % 2 == 0:
            p1 = mm_nn(hn, w1, j, 1, out_dtype=BF16, name=f"fox_proj_qg_{i}")
            p2 = mm_nn(hn, w2, j, 1, out_dtype=BF16, name=f"fox_proj_kv_{i}")
            pf = mm_nn(hn, wfp, j, 1, out_dtype=F32, name=f"fox_proj_f_{i}")
            qa, ka = fox_decay(pf, bfp[j:j + 1], H, scale, f"fox_decay_{i}")
            o, yv, qb = attn_fwd(p1, p2, qa, ka, H, scale, f"attn_fwd_{i}")
            h_new = mm_nt_res(yv, wo_fox, j, h, f"fox_out_{i}")
            saved.append((h, hn, p1, p2, pf, ka, o, yv, qb))
        else:
            proj = mm_nn(hn, wc, j, N_CHIPS, out_dtype=BF16, name=f"conv_proj_{i}", bias=b_in[j:j + 1])
            u2, yv = conv_fwd(proj, dwt[j], dwb[j:j + 1], lng[j:j + 1], lnb[j:j + 1], f"conv_fwd_{i}")
            h_new = mm_nt_res(yv, wo_conv, j, h, f"conv_out_{i}")
            saved.append((h, hn, proj, u2, yv))
        h = h_new

    dh, d_gf, loss_part = loss_head(h, final_norm_g.reshape((1, D)), tgt, "loss_head")

    d_norm = [None] * n_layers
    d_fox_in = [None] * NL
    d_fox_out = [None] * NL
    d_fox_b = [None] * NL
    d_conv_in = [None] * NL
    d_conv_out = [None] * NL
    d_conv_small = [None] * NL
    for i in reversed(range(n_layers)):
        j = i // 2
        g_i = norm_g[i:i + 1]
        if i % 2 == 0:
            h_in, hn, p1, p2, pf, ka, o, yv, qb = saved[i]
            d_fox_out[j] = mm_tn(yv, dh, 1, name=f"fox_out_dw_{i}")[0]
            dy = mm_nn_t(dh, wo_fox, j, f"fox_out_dx_{i}")
            do, dp1, da = attn_bwd_prep(dy, o, p1, H, f"attn_bwd_prep_{i}")
            dp1, dp2, dc = attn_bwd(p1, qb, do, da, p2, ka, dp1, H, scale, f"attn_bwd_{i}")
            dz, dbf = fox_decay_bwd(dc, pf, bfp[j:j + 1], H, f"fox_decay_bwd_{i}")
            d_fox_b[j] = dbf
            dw1 = mm_tn(hn, dp1, 1, name=f"fox_dw_qg_{i}")[0]
            dw2 = mm_tn(hn, dp2, 1, name=f"fox_dw_kv_{i}")[0]
            dwf = mm_tn(hn, dz, 1, name=f"fox_dw_f_{i}")[0]
            dw2 = dw2.reshape((D, H, 2, HEAD_DIM))
            d_fox_in[j] = jnp.concatenate([dw1[:, :W], dw2[:, :, 0].reshape((D, W)), dw2[:, :, 1].reshape((D, W)),
                                           dw1[:, W:], dwf[:, :H]], axis=-1)
            dhn = mm_nt(dp1, w1, j, 1, name=f"fox_dx_qg_{i}")
            dhn = mm_nt(dp2, w2, j, 1, name=f"fox_dx_kv_{i}", res=dhn)
            dhn = mm_nt(dz, wfp, j, 1, name=f"fox_dx_f_{i}", res=dhn)
        else:
            h_in, hn, proj, u2, yv = saved[i]
            d_conv_out[j] = mm_tn(yv, dh, 1, name=f"conv_out_dw_{i}")[0]
            dy = mm_nn_t(dh, wo_conv, j, f"conv_out_dx_{i}")
            du2, dproj, sm1 = conv_bwd_norm(dy, proj, u2, lng[j:j + 1], lnb[j:j + 1], f"conv_bwd_norm_{i}")
            dproj, sm2 = conv_bwd_taps(du2, proj, dwt[j], dproj, f"conv_bwd_taps_{i}")
            d_conv_small[j] = (sm1, sm2)
            d_conv_in[j] = mm_tn(hn, dproj, N_CHIPS, name=f"conv_dw_in_{i}")
            dhn = mm_nt(dproj, wc, j, N_CHIPS, name=f"conv_dx_{i}")
        dh, d_norm[i] = rms_bwd(dhn, h_in, g_i, dh, f"rms_bwd_{i}")

    g_a = jnp.stack([d.reshape((D, N_CHIPS, NA)).transpose((1, 0, 2)) for d in d_fox_in])
    g_b = jnp.stack([d.reshape((N_CHIPS, Dq, D)) for d in d_fox_out])
    g_c = jnp.stack(d_conv_in)
    g_d = jnp.stack([d.reshape((N_CHIPS, Dq, D)) for d in d_conv_out])
    big = [g_a, g_b, g_c, g_d]
    rcv = swap_halves(big)
    parts = [pair_sum(g, r, cidx, f"pair_sum_{n}") for n, (g, r) in enumerate(zip(big, rcv))]

    zrow = jnp.zeros((1, D), F32)
    rows = list(d_norm) + [d_gf]
    rows += [_pad_cols(d_fox_b[l][:, :H], D) for l in range(NL)]
    rows += [_pad_cols(loss_part[:, :1], D)]
    for l in range(NL):
        sm1, sm2 = d_conv_small[l]
        rows += [sm2[32:33], sm2[33:34], sm1[3:4]]
    for l in range(NL):
        rows += [d_conv_small[l][1][0:CONV_K]]
    rows += [d_conv_small[l][0][2:3] for l in range(NL)]
    rows += [d_conv_small[l][0][0:1] for l in range(NL)]
    rows += [d_conv_small[l][0][1:2] for l in range(NL)]
    n_rows = sum(r.shape[0] for r in rows)
    rows += [zrow] * (SMALL_ROWS - n_rows)
    small = jnp.concatenate(rows, axis=0)

    got = exchange_partials(parts, small)
    halves = [chip_sum(p, f"chip_sum_{n}") for n, p in enumerate(got[:4])]
    tot = dev_sum(got[4], "dev_sum")
    full = join_halves(halves)
    grad_fox_w_in = full[0].reshape(fox_w_in.shape)
    grad_fox_w_out = full[1].reshape(fox_w_out.shape)
    grad_conv_w_in = full[2].reshape(conv_w_in.shape)
    grad_conv_w_out = full[3].reshape(conv_w_out.shape)

    def mine(v):
        return lax.dynamic_slice_in_dim(v, chip * Dq, Dq, axis=v.ndim - 1)

    r = n_layers
    grad_norm_g = tot[0:r]
    grad_final = tot[r]
    grad_fox_b_f = tot[r + 1:r + 1 + NL, :H]
    loss = tot[r + 1 + NL, 0]
    r = r + 2 + NL
    gb_full = tot[r:r + 3 * NL].reshape((NL, 3 * C))
    grad_conv_b_in = lax.dynamic_slice_in_dim(gb_full, chip * 3 * Dq, 3 * Dq, axis=1)
    r += 3 * NL
    grad_conv_dw = mine(tot[r:r + CONV_K * NL].reshape((NL, CONV_K, C)))
    r += CONV_K * NL
    grad_conv_dw_b = mine(tot[r:r + NL])
    grad_conv_ln_g = mine(tot[r + NL:r + 2 * NL])
    grad_conv_ln_b = mine(tot[r + 2 * NL:r + 3 * NL])

    grads = [grad_norm_g, grad_fox_w_in, grad_fox_b_f, grad_fox_w_out, grad_conv_w_in, grad_conv_b_in,
             grad_conv_dw, grad_conv_dw_b, grad_conv_ln_g, grad_conv_ln_b, grad_conv_w_out, grad_final]
    ws = [norm_g, fox_w_in, fox_b_f, fox_w_out, conv_w_in, conv_b_in, conv_dw, conv_dw_b, conv_ln_g, conv_ln_b,
          conv_w_out, final_norm_g]
    ms = [m_norm_g, m_fox_w_in, m_fox_b_f, m_fox_w_out, m_conv_w_in, m_conv_b_in, m_conv_dw, m_conv_dw_b,
          m_conv_ln_g, m_conv_ln_b, m_conv_w_out, m_final_norm_g]
    vs = [v_norm_g, v_fox_w_in, v_fox_b_f, v_fox_w_out, v_conv_w_in, v_conv_b_in, v_conv_dw, v_conv_dw_b,
          v_conv_ln_g, v_conv_ln_b, v_conv_w_out, v_final_norm_g]
    deltas, new_ms, new_vs = [], [], []
    for n, (w_, g_, m_, v_) in enumerate(zip(ws, grads, ms, vs)):
        d_, nm_, nv_ = adamw(w_, g_, m_, v_, f"adamw_{n}")
        deltas.append(d_)
        new_ms.append(nm_)
        new_vs.append(nv_)
    grad_x = dh.reshape(x.shape)
    return (loss, grad_x, *grads, *deltas, *new_ms, *new_vs)


def mm_nt_res(y, wo, lidx, h, name):
    M, K = y.shape
    N = wo.shape[-1]
    tm = _tile(M, 1024)
    tn = _tile(N, 1024)
    grid = (M // tm, N // tn, 1)
    return _matmul(
        y, wo, contract="nn", grid=grid, name=name,
        a_spec=pl.BlockSpec((tm, K), lambda i, j, k: (i, 0)),
        b_spec=pl.BlockSpec((None, None, K, tn), lambda i, j, k: (lidx, 0, 0, j)),
        o_spec=pl.BlockSpec((tm, tn), lambda i, j, k: (i, j)),
        out_shape=jax.ShapeDtypeStruct((M, N), F32), acc_shape=(tm, tn),
        res=h, res_spec=pl.BlockSpec((tm, tn), lambda i, j, k: (i, j)),
    )


def mm_nn_t(dh, wo, lidx, name):
    M, K = dh.shape
    N = wo.shape[-2]
    tm = _tile(M, 512)
    tn = _tile(N, 1024)
    grid = (M // tm, N // tn, 1)
    return _matmul(
        dh, wo, contract="nt", grid=grid, name=name,
        a_spec=pl.BlockSpec((tm, K), lambda i, j, k: (i, 0)),
        b_spec=pl.BlockSpec((None, None, tn, K), lambda i, j, k: (lidx, 0, j, 0)),
        o_spec=pl.BlockSpec((tm, tn), lambda i, j, k: (i, j)),
        out_shape=jax.ShapeDtypeStruct((M, N), F32), acc_shape=(tm, tn),
    )
```

```python
import jax
import jax.numpy as jnp
from jax import lax
from jax.experimental import pallas as pl
from jax.experimental.pallas import tpu as pltpu

F32 = jnp.float32
BF16 = jnp.bfloat16
MESH = pl.DeviceIdType.MESH

RMS_EPS = 1e-6
LN_EPS = 1e-5
CONV_K = 31
HALO = 32
HEAD_DIM = 128
ADAM_LR = 0.001
ADAM_B1 = 0.9
ADAM_B2 = 0.999
ADAM_EPS = 1e-08
ADAM_WD = 0.01
ADAM_STEP = 10
N_CHIPS = 4
N_DEV = 8
VMEM_LIMIT = 56 * 1024 * 1024
NEG_BIG = -1e30
SMALL_ROWS = 88
PACK_ROWS = 80


def _pcall(body, **kw):
    return pl.pallas_call(body, **kw)


def _cparams(sem=None):
    return pltpu.CompilerParams(dimension_semantics=sem, vmem_limit_bytes=VMEM_LIMIT)


def _tile(n, cap, mult=128):
    if n <= cap:
        return n
    t = (cap // mult) * mult
    while t >= mult:
        if n % t == 0:
            return t
        t -= mult
    raise ValueError(f"no tile for {n} under {cap}")


def _sigmoid(x):
    return 1.0 / (1.0 + jnp.exp(-x))


def _split3(x):
    hi = x.astype(BF16).astype(F32)
    r = x - hi
    mid = r.astype(BF16).astype(F32)
    lo = (r - mid).astype(BF16).astype(F32)
    return hi, mid, lo


_DN = {
    "nn": (((1,), (0,)), ((), ())),
    "nt": (((1,), (1,)), ((), ())),
    "tn": (((0,), (0,)), ((), ())),
}


def _matmul(a, b, *, contract, grid, a_spec, b_spec, o_spec, out_shape, acc_shape, name,
            bias=None, bias_spec=None, res=None, res_spec=None, alias_res=False):
    nk = grid[2]
    has_bias = bias is not None
    has_res = res is not None

    def body(*refs):
        a_ref, b_ref = refs[0], refs[1]
        pos = 2
        bias_ref = res_ref = None
        if has_bias:
            bias_ref = refs[pos]
            pos += 1
        if has_res:
            res_ref = refs[pos]
            pos += 1
        o_ref = refs[pos]
        acc_ref = refs[pos + 1] if nk > 1 else None
        p = lax.dot_general(a_ref[...].astype(BF16), b_ref[...].astype(BF16), _DN[contract],
                            preferred_element_type=F32)

        def finish(v):
            if has_bias:
                v = v + bias_ref[...]
            if has_res:
                v = res_ref[...] + v
            o_ref[...] = v.astype(o_ref.dtype)

        if nk == 1:
            finish(p)
        else:
            k = pl.program_id(2)

            @pl.when(k == 0)
            def _():
                acc_ref[...] = p

            @pl.when(k > 0)
            def _():
                acc_ref[...] += p

            @pl.when(k == nk - 1)
            def _():
                finish(acc_ref[...])

    ins = [a, b]
    specs = [a_spec, b_spec]
    if has_bias:
        ins.append(bias)
        specs.append(bias_spec)
    if has_res:
        ins.append(res)
        specs.append(res_spec)
    aliases = {len(ins) - 1: 0} if (has_res and alias_res) else {}
    return _pcall(
        body, name=name, grid=grid, in_specs=specs, out_specs=o_spec, out_shape=out_shape,
        scratch_shapes=[pltpu.VMEM(acc_shape, F32)] if nk > 1 else [],
        input_output_aliases=aliases,
        compiler_params=_cparams(("parallel", "parallel", "arbitrary")),
    )(*ins)


def mm_nn(a, w, lidx, n_slots, *, out_dtype, name, bias=None):
    M, K = a.shape
    Ns = w.shape[-1]
    tm = _tile(M, 1024)
    tn = _tile(Ns, 1024)
    per = Ns // tn
    grid = (M // tm, n_slots * per, 1)
    return _matmul(
        a, w, contract="nn", grid=grid, name=name,
        a_spec=pl.BlockSpec((tm, K), lambda i, j, k: (i, 0)),
        b_spec=pl.BlockSpec((None, None, K, tn), lambda i, j, k: (lidx, j // per, 0, j % per)),
        o_spec=pl.BlockSpec((tm, tn), lambda i, j, k: (i, j)),
        out_shape=jax.ShapeDtypeStruct((M, n_slots * Ns), out_dtype), acc_shape=(tm, tn),
        bias=bias, bias_spec=None if bias is None else pl.BlockSpec((1, tn), lambda i, j, k: (0, j)),
    )


def mm_nt(a, w, lidx, n_slots, *, name, res=None):
    M = a.shape[0]
    N, Ns = w.shape[-2], w.shape[-1]
    tm = _tile(M, 1024)
    tn = _tile(N, 1024)
    tk = _tile(Ns, 2048)
    per = Ns // tk
    grid = (M // tm, N // tn, n_slots * per)
    return _matmul(
        a, w, contract="nt", grid=grid, name=name,
        a_spec=pl.BlockSpec((tm, tk), lambda i, j, k: (i, k)),
        b_spec=pl.BlockSpec((None, None, tn, tk), lambda i, j, k: (lidx, k // per, j, k % per)),
        o_spec=pl.BlockSpec((tm, tn), lambda i, j, k: (i, j)),
        out_shape=jax.ShapeDtypeStruct((M, N), F32), acc_shape=(tm, tn),
        res=res, res_spec=None if res is None else pl.BlockSpec((tm, tn), lambda i, j, k: (i, j)),
        alias_res=res is not None,
    )


def mm_tn(a, b, n_slots, *, name):
    S, M = a.shape
    Ns = b.shape[1] // n_slots
    tm = _tile(M, 1024)
    tn = _tile(Ns, 1024)
    tk = _tile(S, 2048)
    per = Ns // tn
    grid = (M // tm, n_slots * per, S // tk)
    return _matmul(
        a, b, contract="tn", grid=grid, name=name,
        a_spec=pl.BlockSpec((tk, tm), lambda i, j, k: (k, i)),
        b_spec=pl.BlockSpec((tk, tn), lambda i, j, k: (k, j)),
        o_spec=pl.BlockSpec((None, tm, tn), lambda i, j, k: (j // per, i, j % per)),
        out_shape=jax.ShapeDtypeStruct((n_slots, M, Ns), F32), acc_shape=(tm, tn),
    )


def rms_fwd(h, g, name):
    S, D = h.shape
    tm = _tile(S, 256, 8)

    def body(h_ref, g_ref, o_ref):
        x = h_ref[...]
        r = lax.rsqrt(jnp.mean(x * x, axis=-1, keepdims=True) + RMS_EPS)
        o_ref[...] = (x * r * g_ref[...]).astype(BF16)

    return _pcall(
        body, name=name, grid=(S // tm,),
        in_specs=[pl.BlockSpec((tm, D), lambda i: (i, 0)), pl.BlockSpec((1, D), lambda i: (0, 0))],
        out_specs=pl.BlockSpec((tm, D), lambda i: (i, 0)),
        out_shape=jax.ShapeDtypeStruct((S, D), BF16),
        compiler_params=_cparams(("parallel",)),
    )(h, g)


def _rms_bwd_rows(x, g, dy):
    d = x.shape[-1]
    r = lax.rsqrt(jnp.mean(x * x, axis=-1, keepdims=True) + RMS_EPS)
    gd = dy * g
    dx = r * gd - x * ((r * r * r) * (jnp.sum(x * gd, axis=-1, keepdims=True) / d))
    return dx, dy * x * r


def rms_bwd(dhn, h, g, dres, name):
    S, D = h.shape
    tm = _tile(S, 256, 8)

    def body(dhn_ref, h_ref, g_ref, dres_ref, dh_ref, dg_ref):
        dx, dgr = _rms_bwd_rows(h_ref[...], g_ref[...], dhn_ref[...])
        dh_ref[...] = dres_ref[...] + dx

        @pl.when(pl.program_id(0) == 0)
        def _():
            dg_ref[...] = jnp.zeros_like(dg_ref)

        dg_ref[...] += jnp.sum(dgr, axis=0, keepdims=True)

    row = pl.BlockSpec((tm, D), lambda i: (i, 0))
    vec = pl.BlockSpec((1, D), lambda i: (0, 0))
    return _pcall(
        body, name=name, grid=(S // tm,),
        in_specs=[row, row, vec, row], out_specs=[row, vec],
        out_shape=[jax.ShapeDtypeStruct((S, D), F32), jax.ShapeDtypeStruct((1, D), F32)],
        input_output_aliases={3: 0},
        compiler_params=_cparams(("arbitrary",)),
    )(dhn, h, g, dres)


def loss_head(h, g, target, name):
    S, D = h.shape
    tm = _tile(S, 256, 8)

    def body(h_ref, g_ref, t_ref, dh_ref, dg_ref, loss_ref):
        x = h_ref[...]
        gg = g_ref[...]
        r = lax.rsqrt(jnp.mean(x * x, axis=-1, keepdims=True) + RMS_EPS)
        y = x * r * gg
        e = y - t_ref[...]
        part = 0.5 * jnp.sum(jnp.mean(e * e, axis=-1, keepdims=True), axis=0, keepdims=True)
        dy = e * (1.0 / D)
        dx, dgr = _rms_bwd_rows(x, gg, dy)
        dh_ref[...] = dx

        @pl.when(pl.program_id(0) == 0)
        def _():
            dg_ref[...] = jnp.zeros_like(dg_ref)
            loss_ref[...] = jnp.zeros_like(loss_ref)

        dg_ref[...] += jnp.sum(dgr, axis=0, keepdims=True)
        loss_ref[...] += jnp.broadcast_to(part, loss_ref.shape)

    row = pl.BlockSpec((tm, D), lambda i: (i, 0))
    vec = pl.BlockSpec((1, D), lambda i: (0, 0))
    return _pcall(
        body, name=name, grid=(S // tm,),
        in_specs=[row, vec, row],
        out_specs=[row, vec, pl.BlockSpec((1, 128), lambda i: (0, 0))],
        out_shape=[jax.ShapeDtypeStruct((S, D), F32), jax.ShapeDtypeStruct((1, D), F32),
                   jax.ShapeDtypeStruct((1, 128), F32)],
        compiler_params=_cparams(("arbitrary",)),
    )(h, g, target)


def fox_decay(pf, bf, n_heads, scale, name):
    S = pf.shape[0]
    tm = _tile(S, 256, 8)
    inv_scale = 1.0 / scale

    def body(pf_ref, bf_ref, qa_ref, ka_ref, carry_ref):
        @pl.when(pl.program_id(0) == 0)
        def _():
            carry_ref[...] = jnp.zeros_like(carry_ref)

        z = pf_ref[...] + bf_ref[...]
        logf = jnp.minimum(z, 0.0) - jnp.log(1.0 + jnp.exp(-jnp.abs(z)))
        row = lax.broadcasted_iota(jnp.int32, (tm, tm), 0)
        col = lax.broadcasted_iota(jnp.int32, (tm, tm), 1)
        tri = (row >= col).astype(F32)
        c = jnp.dot(tri, logf, precision=lax.Precision.HIGHEST, preferred_element_type=F32) + carry_ref[...]
        carry_ref[...] = c[tm - 1:tm, :]
        lane = lax.broadcasted_iota(jnp.int32, (tm, HEAD_DIM), 1)
        for hh in range(n_heads):
            hi, mid, lo = _split3(c[:, hh:hh + 1] * inv_scale)
            qa = jnp.where(lane == 0, hi, jnp.where(lane == 1, mid, jnp.where(lane == 2, lo,
                 jnp.where(lane < 6, 1.0, 0.0))))
            ka = jnp.where(lane < 3, 1.0, jnp.where(lane == 3, -hi, jnp.where(lane == 4, -mid,
                 jnp.where(lane == 5, -lo, jnp.where(lane < 9, 1.0, 0.0)))))
            qa_ref[:, hh * HEAD_DIM:(hh + 1) * HEAD_DIM] = qa.astype(BF16)
            ka_ref[:, hh * HEAD_DIM:(hh + 1) * HEAD_DIM] = ka.astype(BF16)

    wide = pl.BlockSpec((tm, n_heads * HEAD_DIM), lambda i: (i, 0))
    return _pcall(
        body, name=name, grid=(S // tm,),
        in_specs=[pl.BlockSpec((tm, 128), lambda i: (i, 0)), pl.BlockSpec((1, 128), lambda i: (0, 0))],
        out_specs=[wide, wide],
        out_shape=[jax.ShapeDtypeStruct((S, n_heads * HEAD_DIM), BF16)] * 2,
        scratch_shapes=[pltpu.VMEM((1, 128), F32)],
        compiler_params=_cparams(("arbitrary",)),
    )(pf, bf)


def _attn_tile(S):
    return 512 if S % 512 == 0 and S >= 2048 else 128


def attn_fwd(p1, p2, qa, ka, n_heads, scale, name):
    S = p1.shape[0]
    W = n_heads * HEAD_DIM
    t = _attn_tile(S)
    nq = S // t

    def body(q_ref, g_ref, qa_ref, k_ref, v_ref, ka_ref, o_ref, y_ref, qb_ref, mp_ref, qq_ref, acc_ref):
        i = pl.program_id(1)
        lane = lax.broadcasted_iota(jnp.int32, (t, HEAD_DIM), 1)
        qa = qa_ref[...].astype(F32)

        def tile_with(neg_stat):
            hi, mid, lo = _split3(neg_stat)
            return jnp.where(lane == 6, hi, jnp.where(lane == 7, mid, jnp.where(lane == 8, lo, qa))).astype(BF16)

        def keys(start, width):
            rows = pl.ds(pl.multiple_of(start, t), width)
            return rows, jnp.concatenate([k_ref[rows, :], ka_ref[rows, :]], axis=1)

        def causal():
            r = lax.broadcasted_iota(jnp.int32, (t, t), 0)
            c = lax.broadcasted_iota(jnp.int32, (t, t), 1)
            return r >= c

        def over_keys(block):
            def wide(jj, carry):
                block(jj * (4 * t), 4 * t, False)
                return carry

            lax.fori_loop(0, i // 4, wide, 0)
            done = (i // 4) * 4

            @pl.when((i & 2) != 0)
            def _():
                block(done * t, 2 * t, False)

            @pl.when((i & 1) != 0)
            def _():
                block((done + (i & 2)) * t, t, False)

            block(i * t, t, True)

        qq_ref[:, :HEAD_DIM] = q_ref[...]
        qq_ref[:, HEAD_DIM:] = qa_ref[...]
        mp_ref[...] = jnp.full(mp_ref.shape, NEG_BIG, F32)

        def max_block(start, width, masked):
            _, kk = keys(start, width)
            s = lax.dot_general(qq_ref[...], kk, _DN["nt"], preferred_element_type=F32)
            if masked:
                s = jnp.where(causal(), s, NEG_BIG)
            part = s[:, 0:HEAD_DIM]
            for a in range(1, width // HEAD_DIM):
                part = jnp.maximum(part, s[:, a * HEAD_DIM:(a + 1) * HEAD_DIM])
            mp_ref[...] = jnp.maximum(mp_ref[...], part)

        over_keys(max_block)
        m = jnp.max(mp_ref[...], axis=1, keepdims=True)
        qq_ref[:, HEAD_DIM:] = tile_with(-m)
        acc_ref[...] = jnp.zeros_like(acc_ref)

        def sum_block(start, width, masked):
            rows, kk = keys(start, width)
            a = lax.dot_general(qq_ref[...], kk, _DN["nt"], preferred_element_type=F32)
            p = jnp.exp(scale * a)
            if masked:
                p = jnp.where(causal(), p, 0.0)
            ones0 = jnp.where(lax.broadcasted_iota(jnp.int32, (width, HEAD_DIM), 1) == 0, 1.0, 0.0).astype(BF16)
            vv = jnp.concatenate([v_ref[rows, :], ones0], axis=1)
            acc_ref[...] += jnp.dot(p.astype(BF16), vv, preferred_element_type=F32)

        over_keys(sum_block)

        l = acc_ref[:, HEAD_DIM:HEAD_DIM + 1]
        o = acc_ref[:, :HEAD_DIM] / l
        gate = g_ref[...].astype(F32)
        o_ref[...] = o.astype(BF16)
        y_ref[...] = (o * (gate * _sigmoid(gate))).astype(BF16)
        qb_ref[...] = tile_with(-(m + jnp.log(l) * (1.0 / scale)))

    H = n_heads
    qtile = lambda off: pl.BlockSpec((t, HEAD_DIM), lambda h, i: (i, off + h))
    full = lambda fn: pl.BlockSpec((S, HEAD_DIM), fn)
    return _pcall(
        body, name=name, grid=(H, nq),
        in_specs=[qtile(0), qtile(H), qtile(0),
                  full(lambda h, i: (0, 2 * h)), full(lambda h, i: (0, 2 * h + 1)), full(lambda h, i: (0, h))],
        out_specs=[qtile(0), qtile(0), qtile(0)],
        out_shape=[jax.ShapeDtypeStruct((S, W), BF16)] * 3,
        scratch_shapes=[pltpu.VMEM((t, HEAD_DIM), F32), pltpu.VMEM((t, 2 * HEAD_DIM), BF16),
                        pltpu.VMEM((t, 2 * HEAD_DIM), F32)],
        compiler_params=_cparams(("parallel", "arbitrary")),
    )(p1, p1, qa, p2, p2, ka)


def attn_bwd_prep(dy, o, p1, n_heads, name):
    S, W = dy.shape
    tm = _tile(S, 256, 8)
    H = n_heads

    def body(dy_ref, o_ref, g_ref, do_ref, dg_ref, da_ref):
        lane = lax.broadcasted_iota(jnp.int32, (tm, HEAD_DIM), 1)
        for hh in range(H):
            cs = slice(hh * HEAD_DIM, (hh + 1) * HEAD_DIM)
            g = g_ref[:, cs].astype(F32)
            oo = o_ref[:, cs].astype(F32)
            dyv = dy_ref[:, cs]
            sg = _sigmoid(g)
            do = dyv * (g * sg)
            do_ref[:, cs] = do.astype(BF16)
            dg_ref[:, cs] = (dyv * oo * (sg * (1.0 + g * (1.0 - sg)))).astype(BF16)
            hi, mid, lo = _split3(-jnp.sum(do * oo, axis=1, keepdims=True))
            da = jnp.where(lane == 0, hi, jnp.where(lane == 1, mid, jnp.where(lane == 2, lo, 0.0)))
            da_ref[:, cs] = da.astype(BF16)

    row = lambda blk: pl.BlockSpec((tm, W), lambda i: (i, blk))
    return _pcall(
        body, name=name, grid=(S // tm,),
        in_specs=[row(0), row(0), row(1)],
        out_specs=[row(0), row(1), row(0)],
        out_shape=[jax.ShapeDtypeStruct((S, W), BF16), jax.ShapeDtypeStruct((S, 2 * W), BF16),
                   jax.ShapeDtypeStruct((S, W), BF16)],
        compiler_params=_cparams(("parallel",)),
    )(dy, o, p1)


def attn_bwd(p1, qb, do, da, p2, ka, dp1, n_heads, scale, name):
    S = p1.shape[0]
    W = n_heads * HEAD_DIM
    t = _attn_tile(S)
    nb = S // t
    H = n_heads

    def body(q_ref, qb_ref, do_ref, da_ref, k_ref, v_ref, ka_ref, dp1_in, dq_ref, dkv_ref, dc_ref,
             dq_acc, dk_acc, dv_acc):
        del dp1_in
        h = pl.program_id(0)
        j = pl.program_id(1)

        @pl.when(j == 0)
        def _():
            dq_acc[...] = jnp.zeros_like(dq_acc)

        @pl.when((j == 0) & (h == 0))
        def _():
            dc_ref[...] = jnp.zeros_like(dc_ref)

        lane = lax.broadcasted_iota(jnp.int32, (t, HEAD_DIM), 1)
        ones3 = jnp.where(lane < 3, 1.0, 0.0).astype(BF16)
        kk = jnp.concatenate([k_ref[...], ka_ref[...]], axis=1)
        vv = jnp.concatenate([v_ref[...], ones3], axis=1)
        dk_acc[...] = jnp.zeros_like(dk_acc)
        dv_acc[...] = jnp.zeros_like(dv_acc)

        def block(start, width, masked):
            rows = pl.ds(pl.multiple_of(start, t), width)
            qq = jnp.concatenate([q_ref[rows, :], qb_ref[rows, :]], axis=1)
            dd = jnp.concatenate([do_ref[rows, :], da_ref[rows, :]], axis=1)
            a = lax.dot_general(qq, kk, _DN["nt"], preferred_element_type=F32)
            p = jnp.exp(scale * a)
            if masked:
                r = lax.broadcasted_iota(jnp.int32, (t, t), 0)
                c = lax.broadcasted_iota(jnp.int32, (t, t), 1)
                p = jnp.where(r >= c, p, 0.0)
            dpd = lax.dot_general(dd, vv, _DN["nt"], preferred_element_type=F32)
            ds = (p * dpd).astype(BF16)
            pb = p.astype(BF16)
            dv_acc[...] += lax.dot_general(pb, dd, _DN["tn"], preferred_element_type=F32)
            dk_acc[...] += lax.dot_general(ds, qq, _DN["tn"], preferred_element_type=F32)
            dq_acc[rows, :] += jnp.dot(ds, kk, preferred_element_type=F32)

        block(j * t, t, True)
        n_after = nb - 1 - j

        @pl.when((n_after & 1) != 0)
        def _():
            block((j + 1) * t, t, False)

        first = j + 1 + (n_after & 1)

        def loop_body(ii, carry):
            block((first + 2 * ii) * t, 2 * t, False)
            return carry

        lax.fori_loop(0, n_after // 2, loop_body, 0)

        dkv_ref[...] = jnp.concatenate([dk_acc[:, :HEAD_DIM] * scale, dv_acc[:, :HEAD_DIM]], axis=1).astype(BF16)
        colsum = dk_acc[:, HEAD_DIM + 3:HEAD_DIM + 4]
        krows = pl.ds(pl.multiple_of(j * t, t), t)
        dc_ref[krows, :] += jnp.where(lane == h, -colsum, 0.0)

        @pl.when(j == nb - 1)
        def _():
            dq_ref[...] = (dq_acc[:, :HEAD_DIM] * scale).astype(BF16)
            lane_s = lax.broadcasted_iota(jnp.int32, (S, HEAD_DIM), 1)
            dc_ref[...] += jnp.where(lane_s == h, dq_acc[:, HEAD_DIM:HEAD_DIM + 1], 0.0)

    full = lambda fn: pl.BlockSpec((S, HEAD_DIM), fn)
    ktile = lambda fn: pl.BlockSpec((t, HEAD_DIM), fn)
    return _pcall(
        body, name=name, grid=(H, nb),
        in_specs=[full(lambda h, j: (0, h)), full(lambda h, j: (0, h)), full(lambda h, j: (0, h)),
                  full(lambda h, j: (0, h)),
                  ktile(lambda h, j: (j, 2 * h)), ktile(lambda h, j: (j, 2 * h + 1)), ktile(lambda h, j: (j, h)),
                  pl.BlockSpec(memory_space=pl.ANY)],
        out_specs=[full(lambda h, j: (0, h)),
                   pl.BlockSpec((t, 2 * HEAD_DIM), lambda h, j: (j, h)),
                   pl.BlockSpec((S, 128), lambda h, j: (0, 0))],
        out_shape=[jax.ShapeDtypeStruct((S, 2 * W), BF16), jax.ShapeDtypeStruct((S, 2 * W), BF16),
                   jax.ShapeDtypeStruct((S, 128), F32)],
        scratch_shapes=[pltpu.VMEM((S, 2 * HEAD_DIM), F32), pltpu.VMEM((t, 2 * HEAD_DIM), F32),
                        pltpu.VMEM((t, 2 * HEAD_DIM), F32)],
        input_output_aliases={7: 0},
        compiler_params=_cparams(("arbitrary", "arbitrary")),
    )(p1, qb, do, da, p2, p2, ka, dp1)


def fox_decay_bwd(dc, pf, bf, n_heads, name):
    S = dc.shape[0]
    tm = _tile(S, 256, 8)
    nb = S // tm

    def body(dc_ref, pf_ref, bf_ref, dz_ref, db_ref, carry_ref):
        @pl.when(pl.program_id(0) == 0)
        def _():
            carry_ref[...] = jnp.zeros_like(carry_ref)
            db_ref[...] = jnp.zeros_like(db_ref)

        row = lax.broadcasted_iota(jnp.int32, (tm, tm), 0)
        col = lax.broadcasted_iota(jnp.int32, (tm, tm), 1)
        tri = (row <= col).astype(F32)
        dlogf = jnp.dot(tri, dc_ref[...], precision=lax.Precision.HIGHEST, preferred_element_type=F32) + carry_ref[...]
        carry_ref[...] = dlogf[0:1, :]
        z = pf_ref[...] + bf_ref[...]
        lane = lax.broadcasted_iota(jnp.int32, (tm, 128), 1)
        dz = jnp.where(lane < n_heads, dlogf * _sigmoid(-z), 0.0)
        dz_ref[...] = dz.astype(BF16)
        db_ref[...] += jnp.sum(dz, axis=0, keepdims=True)

    rev = pl.BlockSpec((tm, 128), lambda i: (nb - 1 - i, 0))
    vec = pl.BlockSpec((1, 128), lambda i: (0, 0))
    return _pcall(
        body, name=name, grid=(nb,),
        in_specs=[rev, rev, vec], out_specs=[rev, vec],
        out_shape=[jax.ShapeDtypeStruct((S, 128), BF16), jax.ShapeDtypeStruct((1, 128), F32)],
        scratch_shapes=[pltpu.VMEM((1, 128), F32)],
        compiler_params=_cparams(("arbitrary",)),
    )(dc, pf, bf)


def _conv_tile(S):
    return _tile(S, 256, HALO)


def _fill_glu(ubuf, a_ref, b_ref, ah_ref, bh_ref, first, tm):
    uh = ah_ref[...].astype(F32) * _sigmoid(bh_ref[...].astype(F32))
    ubuf[0:HALO, :] = jnp.where(first, 0.0, uh)
    ubuf[HALO:HALO + tm, :] = a_ref[...].astype(F32) * _sigmoid(b_ref[...].astype(F32))


def conv_fwd(proj, dw, dwb, lng, lnb, name):
    S = proj.shape[0]
    C = proj.shape[1] // 3
    tm = _conv_tile(S)
    hb = tm // HALO
    nch = C // 128
    rb = _tile(tm, 128, 8)

    def body(a_ref, b_ref, g_ref, ah_ref, bh_ref, dw_ref, dwb_ref, lng_ref, lnb_ref, u2_ref, y_ref, ubuf):
        i = pl.program_id(0)
        _fill_glu(ubuf, a_ref, b_ref, ah_ref, bh_ref, i == 0, tm)

        def chunk(cc, carry):
            cols = pl.ds(pl.multiple_of(cc * 128, 128), 128)
            for r0 in range(0, tm, rb):
                acc = jnp.broadcast_to(dwb_ref[:, cols], (rb, 128))
                for b in range(8):
                    taps = list(range(b, CONV_K, 8))
                    shifted = ubuf[pl.ds(HALO - (CONV_K - 1) + b + r0, rb + 8 * (len(taps) - 1)), cols]
                    for a, k in enumerate(taps):
                        acc = acc + dw_ref[k:k + 1, cols] * shifted[8 * a:8 * a + rb]
                u2_ref[pl.ds(r0, rb), cols] = acc
            return carry

        lax.fori_loop(0, nch, chunk, 0)
        x = u2_ref[...]
        mu = jnp.mean(x, axis=-1, keepdims=True)
        xc = x - mu
        var = jnp.mean(xc * xc, axis=-1, keepdims=True)
        ln = xc * lax.rsqrt(var + LN_EPS) * lng_ref[...] + lnb_ref[...]
        gate = g_ref[...].astype(F32)
        y_ref[...] = ((ln * _sigmoid(ln)) * (gate * _sigmoid(gate))).astype(BF16)

    blk = lambda cb: pl.BlockSpec((tm, C), lambda i: (i, cb))
    halo = lambda cb: pl.BlockSpec((HALO, C), lambda i: (jnp.maximum(i * hb - 1, 0), cb))
    vec = pl.BlockSpec((1, C), lambda i: (0, 0))
    return _pcall(
        body, name=name, grid=(S // tm,),
        in_specs=[blk(0), blk(1), blk(2), halo(0), halo(1),
                  pl.BlockSpec((CONV_K, C), lambda i: (0, 0)), vec, vec, vec],
        out_specs=[blk(0), blk(0)],
        out_shape=[jax.ShapeDtypeStruct((S, C), F32), jax.ShapeDtypeStruct((S, C), BF16)],
        scratch_shapes=[pltpu.VMEM((HALO + tm, C), F32)],
        compiler_params=_cparams(("parallel",)),
    )(proj, proj, proj, proj, proj, dw, dwb, lng, lnb)


def conv_bwd_norm(dy, proj, u2, lng, lnb, name):
    S, C = dy.shape
    tm = _tile(S, 256, 8)

    def body(dy_ref, g_ref, u2_ref, lng_ref, lnb_ref, du2_ref, dg_ref, sm_ref):
        x = u2_ref[...]
        mu = jnp.mean(x, axis=-1, keepdims=True)
        xc = x - mu
        var = jnp.mean(xc * xc, axis=-1, keepdims=True)
        rs = lax.rsqrt(var + LN_EPS)
        xhat = xc * rs
        gam = lng_ref[...]
        ln = xhat * gam + lnb_ref[...]
        sl = _sigmoid(ln)
        u3 = ln * sl
        gate = g_ref[...].astype(F32)
        sg = _sigmoid(gate)
        dyv = dy_ref[...]
        dgate = dyv * u3 * (sg * (1.0 + gate * (1.0 - sg)))
        dln = (dyv * (gate * sg)) * (sl * (1.0 + ln * (1.0 - sl)))
        dxh = dln * gam
        du2 = rs * (dxh - jnp.mean(dxh, axis=-1, keepdims=True)
                    - xhat * jnp.mean(dxh * xhat, axis=-1, keepdims=True))
        du2_ref[...] = du2
        dg_ref[...] = dgate.astype(BF16)

        @pl.when(pl.program_id(0) == 0)
        def _():
            sm_ref[...] = jnp.zeros_like(sm_ref)

        sm_ref[0:1, :] += jnp.sum(dln * xhat, axis=0, keepdims=True)
        sm_ref[1:2, :] += jnp.sum(dln, axis=0, keepdims=True)
        sm_ref[2:3, :] += jnp.sum(du2, axis=0, keepdims=True)
        sm_ref[3:4, :] += jnp.sum(dgate, axis=0, keepdims=True)

    blk = lambda cb: pl.BlockSpec((tm, C), lambda i: (i, cb))
    vec = pl.BlockSpec((1, C), lambda i: (0, 0))
    return _pcall(
        body, name=name, grid=(S // tm,),
        in_specs=[blk(0), blk(2), blk(0), vec, vec],
        out_specs=[blk(0), blk(2), pl.BlockSpec((8, C), lambda i: (0, 0))],
        out_shape=[jax.ShapeDtypeStruct((S, C), F32), jax.ShapeDtypeStruct((S, 3 * C), BF16),
                   jax.ShapeDtypeStruct((8, C), F32)],
        compiler_params=_cparams(("arbitrary",)),
    )(dy, proj, u2, lng, lnb)


def conv_bwd_taps(du2, proj, dw, dproj, name):
    S, C = du2.shape
    tm = _conv_tile(S)
    hb = tm // HALO
    nb = S // tm
    nch = C // 128
    rb = _tile(tm, 128, 8)

    def body(d_ref, dh_ref, a_ref, b_ref, ah_ref, bh_ref, dw_ref, dp_in, dab_ref, sm_ref, ubuf, dbuf):
        del dp_in
        i = pl.program_id(0)
        _fill_glu(ubuf, a_ref, b_ref, ah_ref, bh_ref, i == 0, tm)
        dbuf[0:tm, :] = d_ref[...]
        dbuf[tm:tm + HALO, :] = jnp.where(i == nb - 1, 0.0, dh_ref[...])

        @pl.when(i == 0)
        def _():
            sm_ref[...] = jnp.zeros_like(sm_ref)

        def chunk(cc, carry):
            cols = pl.ds(pl.multiple_of(cc * 128, 128), 128)
            cols_b = pl.ds(pl.multiple_of(C + cc * 128, 128), 128)
            for r0 in range(0, tm, rb):
                d0 = dbuf[r0:r0 + rb, cols]
                du = jnp.zeros((rb, 128), F32)
                for b in range(8):
                    offs = list(range(b, CONV_K, 8))
                    shifted = dbuf[pl.ds(r0 + b, rb + 8 * (len(offs) - 1)), cols]
                    for a, o in enumerate(offs):
                        k = CONV_K - 1 - o
                        du = du + dw_ref[k:k + 1, cols] * shifted[8 * a:8 * a + rb]
                    shifted = ubuf[pl.ds(HALO - (CONV_K - 1) + b + r0, rb + 8 * (len(offs) - 1)), cols]
                    for a, k in enumerate(offs):
                        sm_ref[k:k + 1, cols] += jnp.sum(d0 * shifted[8 * a:8 * a + rb], axis=0, keepdims=True)
                rows = pl.ds(r0, rb)
                av = a_ref[rows, cols].astype(F32)
                sb = _sigmoid(b_ref[rows, cols].astype(F32))
                da = du * sb
                db = du * av * sb * (1.0 - sb)
                dab_ref[rows, cols] = da.astype(BF16)
                dab_ref[rows, cols_b] = db.astype(BF16)
                sm_ref[32:33, cols] += jnp.sum(da, axis=0, keepdims=True)
                sm_ref[33:34, cols] += jnp.sum(db, axis=0, keepdims=True)
            return carry

        lax.fori_loop(0, nch, chunk, 0)

    blk = lambda cb: pl.BlockSpec((tm, C), lambda i: (i, cb))
    halo = lambda cb: pl.BlockSpec((HALO, C), lambda i: (jnp.maximum(i * hb - 1, 0), cb))
    nxt = pl.BlockSpec((HALO, C), lambda i: (jnp.minimum((i + 1) * hb, nb * hb - 1), 0))
    return _pcall(
        body, name=name, grid=(nb,),
        in_specs=[blk(0), nxt, blk(0), blk(1), halo(0), halo(1),
                  pl.BlockSpec((CONV_K, C), lambda i: (0, 0)), pl.BlockSpec(memory_space=pl.ANY)],
        out_specs=[pl.BlockSpec((tm, 2 * C), lambda i: (i, 0)), pl.BlockSpec((40, C), lambda i: (0, 0))],
        out_shape=[jax.ShapeDtypeStruct((S, 3 * C), BF16), jax.ShapeDtypeStruct((40, C), F32)],
        scratch_shapes=[pltpu.VMEM((HALO + tm, C), F32), pltpu.VMEM((tm + HALO, C), F32)],
        input_output_aliases={7: 0},
        compiler_params=_cparams(("arbitrary",)),
    )(du2, du2, proj, proj, proj, proj, dw, dproj)


def _rows_tile(R, Cc, budget=1 << 18):
    cap = max(8, budget // max(Cc, 1))
    if R <= cap:
        return R
    t = (cap // 8) * 8
    while t >= 8:
        if R % t == 0:
            return t
        t -= 8
    return R


def cast_into_slot(w, chip, name):
    L, R, Cc = w.shape
    tr = _rows_tile(R, Cc)

    def body(s_ref, w_ref, o_ref):
        del s_ref
        o_ref[...] = w_ref[...].astype(BF16)

    grid_spec = pltpu.PrefetchScalarGridSpec(
        num_scalar_prefetch=1, grid=(L, R // tr),
        in_specs=[pl.BlockSpec((None, tr, Cc), lambda l, r, s: (l, r, 0))],
        out_specs=pl.BlockSpec((None, None, tr, Cc), lambda l, r, s: (l, s[0], r, 0)),
    )
    return _pcall(
        body, name=name, grid_spec=grid_spec, out_shape=jax.ShapeDtypeStruct((L, N_CHIPS, R, Cc), BF16),
        compiler_params=_cparams(("parallel", "parallel")),
    )(chip, w)


def pair_sum(g, rcv, cidx, name):
    L, K, R, Cc = g.shape
    half = R // 2
    g5 = g.reshape((L, K, 2, half, Cc))
    tr = _rows_tile(half, Cc)

    def body(c_ref, g_ref, r_ref, o_ref):
        del c_ref
        o_ref[...] = (g_ref[...] + r_ref[...]).astype(BF16)

    grid_spec = pltpu.PrefetchScalarGridSpec(
        num_scalar_prefetch=1, grid=(L, K, half // tr),
        in_specs=[pl.BlockSpec((None, None, None, tr, Cc), lambda l, k, r, c: (l, k, c[0], r, 0)),
                  pl.BlockSpec((None, None, tr, Cc), lambda l, k, r, c: (l, k, r, 0))],
        out_specs=pl.BlockSpec((None, None, tr, Cc), lambda l, k, r, c: (l, k, r, 0)),
    )
    return _pcall(
        body, name=name, grid_spec=grid_spec, out_shape=jax.ShapeDtypeStruct((L, K, half, Cc), BF16),
        compiler_params=_cparams(("parallel", "parallel", "parallel")),
    )(cidx, g5, rcv)


def chip_sum(parts, got, sel, name):
    L, _, R, Cc = parts.shape
    n_got = got.shape[1]
    tr = _rows_tile(R, Cc)

    def body(s_ref, p_ref, g_ref, o_ref):
        del s_ref
        acc = p_ref[...].astype(F32)
        for k in range(n_got):
            acc = acc + g_ref[k].astype(F32)
        o_ref[...] = acc

    grid_spec = pltpu.PrefetchScalarGridSpec(
        num_scalar_prefetch=1, grid=(L, R // tr),
        in_specs=[pl.BlockSpec((None, None, tr, Cc), lambda l, r, s: (l, s[0], r, 0)),
                  pl.BlockSpec((None, n_got, tr, Cc), lambda l, r, s: (l, 0, r, 0))],
        out_specs=pl.BlockSpec((None, None, tr, Cc), lambda l, r, s: (l, s[1], r, 0)),
    )
    return _pcall(
        body, name=name, grid_spec=grid_spec, out_shape=jax.ShapeDtypeStruct((L, 2, R, Cc), F32),
        compiler_params=_cparams(("parallel", "parallel")),
    )(sel, parts, got)


def dev_sum(parts, name):
    K, R, Cc = parts.shape

    def body(p_ref, o_ref):
        acc = p_ref[0]
        for k in range(1, K):
            acc = acc + p_ref[k]
        o_ref[...] = acc

    return _pcall(
        body, name=name, grid=(R // 8,),
        in_specs=[pl.BlockSpec((K, 8, Cc), lambda r: (0, r, 0))],
        out_specs=pl.BlockSpec((8, Cc), lambda r: (r, 0)),
        out_shape=jax.ShapeDtypeStruct((R, Cc), F32), compiler_params=_cparams(("parallel",)),
    )(parts)


def adamw(w, g, m, v, name):
    shape = w.shape
    if w.ndim == 3:
        L, R, Cc = shape
    else:
        L, R, Cc = 1, (1 if w.ndim == 1 else shape[0]), shape[-1]
    view = lambda t: t.reshape((L, R, Cc))
    tr = _rows_tile(R, Cc, budget=1 << 17)
    c1 = 1.0 - ADAM_B1 ** ADAM_STEP
    c2 = 1.0 - ADAM_B2 ** ADAM_STEP

    def body(w_ref, g_ref, m_ref, v_ref, d_ref, nm_ref, nv_ref):
        gg = g_ref[...]
        nm = ADAM_B1 * m_ref[...] + (1.0 - ADAM_B1) * gg
        nv = ADAM_B2 * v_ref[...] + (1.0 - ADAM_B2) * (gg * gg)
        d_ref[...] = -ADAM_LR * ((nm / c1) / (jnp.sqrt(nv / c2) + ADAM_EPS) + ADAM_WD * w_ref[...])
        nm_ref[...] = nm
        nv_ref[...] = nv

    spec = pl.BlockSpec((None, tr, Cc), lambda l, r: (l, r, 0))
    outs = _pcall(
        body, name=name, grid=(L, R // tr), in_specs=[spec] * 4, out_specs=[spec] * 3,
        out_shape=[jax.ShapeDtypeStruct((L, R, Cc), F32)] * 3, compiler_params=_cparams(("parallel", "parallel")),
    )(view(w), view(g), view(m), view(v))
    return tuple(o.reshape(shape) for o in outs)


def _place():
    x, y, c = lax.axis_index("x"), lax.axis_index("y"), lax.axis_index("c")
    other_chips = [(1 - x, y), (x, 1 - y), (1 - x, 1 - y)]
    return x, y, c, other_chips


def _rcopy(src, dst, ssem, rsem, k, to):
    return pltpu.make_async_remote_copy(src_ref=src, dst_ref=dst, send_sem=ssem.at[k], recv_sem=rsem.at[k],
                                        device_id=to, device_id_type=MESH)


def gather_weights(slots, small):
    nt = len(slots)

    def body(*refs):
        outs, small_out = refs[nt + 1:2 * nt + 1], refs[2 * nt + 1]
        ssem, rsem = refs[2 * nt + 2:]
        x, y, c, chips = _place()
        me = 2 * x + y
        sib = (x, y, 1 - c)

        def rows(t, half_of):
            half = outs[t].shape[2] // 2
            return pl.ds(half_of * half, half)

        first, passed = [], []
        for t in range(nt):
            mine = outs[t].at[:, me, rows(t, c), :]
            for j, chip in enumerate(chips):
                first.append(_rcopy(mine, mine, ssem, rsem, 6 * t + j, (*chip, c)))
        for j, chip in enumerate(chips):
            first.append(_rcopy(small_out.at[me], small_out.at[me], ssem, rsem, 6 * nt + j, (*chip, c)))
        for cp in first:
            cp.start()
        for t in range(nt):
            for j, (px, py) in enumerate(chips):
                land = outs[t].at[:, 2 * px + py, rows(t, c), :]
                _rcopy(land, land, ssem, rsem, 6 * t + j, (x, y, c)).wait_recv()
                fw = _rcopy(land, land, ssem, rsem, 6 * t + 3 + j, sib)
                fw.start()
                passed.append(fw)
        for j, (px, py) in enumerate(chips):
            land = small_out.at[2 * px + py]
            _rcopy(land, land, ssem, rsem, 6 * nt + j, (x, y, c)).wait_recv()
        for t in range(nt):
            for j, (px, py) in enumerate(chips):
                land = outs[t].at[:, 2 * px + py, rows(t, 1 - c), :]
                _rcopy(land, land, ssem, rsem, 6 * t + 3 + j, (x, y, c)).wait_recv()
        for cp in first + passed:
            cp.wait_send()

    ops = list(slots) + [small]
    nsem = 6 * nt + 3
    return _pcall(
        body, name="gather_weights", out_shape=[jax.ShapeDtypeStruct(s.shape, s.dtype) for s in ops],
        in_specs=[pl.BlockSpec(memory_space=pl.ANY)] * (nt + 1),
        out_specs=[pl.BlockSpec(memory_space=pl.ANY)] * (nt + 1),
        input_output_aliases={n: n for n in range(nt + 1)},
        scratch_shapes=[pltpu.SemaphoreType.DMA((nsem,)), pltpu.SemaphoreType.DMA((nsem,))],
    )(*ops)


def swap_halves(grads):
    nt = len(grads)

    def body(*refs):
        ins, outs = refs[:nt], refs[nt:2 * nt]
        ssem, rsem = refs[2 * nt:]
        x, y, c, _ = _place()
        sib = (x, y, 1 - c)
        cps = []
        for t in range(nt):
            half = ins[t].shape[2] // 2
            cps.append(_rcopy(ins[t].at[:, :, pl.ds((1 - c) * half, half), :], outs[t], ssem, rsem, t, sib))
        for cp in cps:
            cp.start()
        for cp in cps:
            cp.wait()

    out_shape = [jax.ShapeDtypeStruct(g.shape[:2] + (g.shape[2] // 2, g.shape[3]), g.dtype) for g in grads]
    return _pcall(
        body, name="swap_halves", out_shape=out_shape,
        in_specs=[pl.BlockSpec(memory_space=pl.ANY)] * nt, out_specs=[pl.BlockSpec(memory_space=pl.ANY)] * nt,
        scratch_shapes=[pltpu.SemaphoreType.DMA((nt,)), pltpu.SemaphoreType.DMA((nt,))],
    )(*grads)


def exchange_partials(parts, small):
    nt = len(parts)

    def body(*refs):
        ins = refs[:nt]
        outs, small_out = refs[nt + 1:2 * nt + 1], refs[2 * nt + 1]
        ssem, rsem = refs[2 * nt + 2:]
        x, y, c, chips = _place()
        dev = 4 * x + 2 * y + c
        sends, lands = [], []
        for t in range(nt):
            for j, (px, py) in enumerate(chips):
                k = 3 * t + j
                sends.append(_rcopy(ins[t].at[:, 2 * px + py], outs[t].at[:, j], ssem, rsem, k, (px, py, c)))
                land = outs[t].at[:, j]
                lands.append(_rcopy(land, land, ssem, rsem, k, (x, y, c)))
        peers = [(px, py, pc) for pc in (c, 1 - c) for (px, py) in [(x, y)] + chips][1:]
        for j, (px, py, pc) in enumerate(peers):
            k = 3 * nt + j
            sends.append(_rcopy(small_out.at[dev], small_out.at[dev], ssem, rsem, k, (px, py, pc)))
            land = small_out.at[4 * px + 2 * py + pc]
            lands.append(_rcopy(land, land, ssem, rsem, k, (x, y, c)))
        for cp in sends:
            cp.start()
        for cp in lands:
            cp.wait_recv()
        for cp in sends:
            cp.wait_send()

    out_shape = [jax.ShapeDtypeStruct((p.shape[0], N_CHIPS - 1) + p.shape[2:], p.dtype) for p in parts]
    out_shape.append(jax.ShapeDtypeStruct(small.shape, small.dtype))
    nsem = 3 * nt + 7
    return _pcall(
        body, name="exchange_partials", out_shape=out_shape,
        in_specs=[pl.BlockSpec(memory_space=pl.ANY)] * (nt + 1),
        out_specs=[pl.BlockSpec(memory_space=pl.ANY)] * (nt + 1),
        input_output_aliases={nt: nt},
        scratch_shapes=[pltpu.SemaphoreType.DMA((nsem,)), pltpu.SemaphoreType.DMA((nsem,))],
    )(*parts, small)


def join_halves(halves):
    nt = len(halves)

    def body(*refs):
        outs = refs[nt:2 * nt]
        ssem, rsem = refs[2 * nt:]
        x, y, c, _ = _place()
        sib = (x, y, 1 - c)
        sends = [_rcopy(outs[t].at[:, c], outs[t].at[:, c], ssem, rsem, t, sib) for t in range(nt)]
        for cp in sends:
            cp.start()
        for t in range(nt):
            land = outs[t].at[:, 1 - c]
            _rcopy(land, land, ssem, rsem, t, (x, y, c)).wait_recv()
        for cp in sends:
            cp.wait_send()

    return _pcall(
        body, name="join_halves", out_shape=[jax.ShapeDtypeStruct(h.shape, h.dtype) for h in halves],
        in_specs=[pl.BlockSpec(memory_space=pl.ANY)] * nt, out_specs=[pl.BlockSpec(memory_space=pl.ANY)] * nt,
        input_output_aliases={n: n for n in range(nt)},
        scratch_shapes=[pltpu.SemaphoreType.DMA((nt,)), pltpu.SemaphoreType.DMA((nt,))],
    )(*halves)


def _pad_cols(a, n):
    return jnp.pad(a, [(0, 0)] * (a.ndim - 1) + [(0, n - a.shape[-1])])


def kernel(x, norm_g, fox_w_in, fox_b_f, fox_w_out, conv_w_in, conv_b_in, conv_dw, conv_dw_b, conv_ln_g, conv_ln_b, conv_w_out, final_norm_g, loss_target, m_norm_g, m_fox_w_in, m_fox_b_f, m_fox_w_out, m_conv_w_in, m_conv_b_in, m_conv_dw, m_conv_dw_b, m_conv_ln_g, m_conv_ln_b, m_conv_w_out, m_final_norm_g, v_norm_g, v_fox_w_in, v_fox_b_f, v_fox_w_out, v_conv_w_in, v_conv_b_in, v_conv_dw, v_conv_dw_b, v_conv_ln_g, v_conv_ln_b, v_conv_w_out, v_final_norm_g):
    S, D = x.shape[1], x.shape[2]
    H = fox_b_f.shape[1]
    assert D == H * HEAD_DIM, "one head must be one lane tile"
    W = C = D
    NL = fox_w_in.shape[0]
    Dq = D // N_CHIPS
    NA = fox_w_in.shape[2]
    scale = HEAD_DIM ** -0.5
    chip = 2 * lax.axis_index("x") + lax.axis_index("y")
    core = lax.axis_index("c")
    cidx = core.astype(jnp.int32).reshape((1,))
    chip1 = chip.astype(jnp.int32).reshape((1,))
    sel = jnp.stack([chip, core]).astype(jnp.int32)

    small_pack = jnp.concatenate([
        conv_b_in.reshape((NL * 3, Dq)), conv_dw.reshape((NL * CONV_K, Dq)), conv_dw_b, conv_ln_g, conv_ln_b,
        jnp.zeros((PACK_ROWS - NL * (3 + CONV_K + 3), Dq), F32)], axis=0)
    small_slots = lax.dynamic_update_slice(jnp.zeros((N_CHIPS, PACK_ROWS, Dq), F32), small_pack[None], (chip, 0, 0))
    ga, gb, gc, gd, gsmall = gather_weights(
        [cast_into_slot(fox_w_in, chip1, "cast_fox_w_in"), cast_into_slot(fox_w_out, chip1, "cast_fox_w_out"),
         cast_into_slot(conv_w_in, chip1, "cast_conv_w_in"), cast_into_slot(conv_w_out, chip1, "cast_conv_w_out")],
        small_slots)
    wfull = jnp.transpose(ga, (0, 2, 1, 3)).reshape((NL, D, N_CHIPS * NA))
    wq, wk, wv, wg, wf = (wfull[:, :, 0:W], wfull[:, :, W:2 * W], wfull[:, :, 2 * W:3 * W],
                          wfull[:, :, 3 * W:4 * W], wfull[:, :, 4 * W:])
    w1 = jnp.concatenate([wq, wg], axis=-1).reshape((NL, 1, D, 2 * W))
    w2 = jnp.stack([wk.reshape((NL, D, H, HEAD_DIM)), wv.reshape((NL, D, H, HEAD_DIM))], axis=3)
    w2 = w2.reshape((NL, 1, D, 2 * W))
    wfp = _pad_cols(wf, 128).reshape((NL, 1, D, 128))
    wo_fox = gb.reshape((NL, 1, W, D))
    wo_conv = gd.reshape((NL, 1, C, D))
    wc = gc
    b_in = gsmall[:, 0:3 * NL, :].reshape((N_CHIPS, NL, 3 * Dq)).transpose((1, 0, 2)).reshape((NL, 3 * C))
    dwt = gsmall[:, 3 * NL:3 * NL + CONV_K * NL, :].reshape((N_CHIPS, NL, CONV_K, Dq))
    dwt = dwt.transpose((1, 2, 0, 3)).reshape((NL, CONV_K, C))
    r0 = (3 + CONV_K) * NL
    vecs = gsmall[:, r0:r0 + 3 * NL, :].reshape((N_CHIPS, 3, NL, Dq)).transpose((1, 2, 0, 3)).reshape((3, NL, C))
    dwb, lng, lnb = vecs[0], vecs[1], vecs[2]
    bfp = _pad_cols(fox_b_f, 128)

    h = x.reshape((S, D))
    tgt = loss_target.reshape((S, D))
    saved = []
    n_layers = norm_g.shape[0]
    for i in range(n_layers):
        j = i // 2
        g_i = norm_g[i:i + 1]
        hn = rms_fwd(h, g_i, f"rms_fwd_{i}")
        if i % 2 == 0:
            p1 = mm_nn(hn, w1, j, 1, out_dtype=BF16, name=f"fox_proj_qg_{i}")
            p2 = mm_nn(hn, w2, j, 1, out_dtype=BF16, name=f"fox_proj_kv_{i}")
            pf = mm_nn(hn, wfp, j, 1, out_dtype=F32, name=f"fox_proj_f_{i}")
            qa, ka = fox_decay(pf, bfp[j:j + 1], H, scale, f"fox_decay_{i}")
            o, yv, qb = attn_fwd(p1, p2, qa, ka, H, scale, f"attn_fwd_{i}")
            h_new = mm_nt_res(yv, wo_fox, j, h, f"fox_out_{i}")
            saved.append((h, hn, p1, p2, pf, ka, o, yv, qb))
        else:
            proj = mm_nn(hn, wc, j, N_CHIPS, out_dtype=BF16, name=f"conv_proj_{i}", bias=b_in[j:j + 1])
            u2, yv = conv_fwd(proj, dwt[j], dwb[j:j + 1], lng[j:j + 1], lnb[j:j + 1], f"conv_fwd_{i}")
            h_new = mm_nt_res(yv, wo_conv, j, h, f"conv_out_{i}")
            saved.append((h, hn, proj, u2, yv))
        h = h_new

    dh, d_gf, loss_part = loss_head(h, final_norm_g.reshape((1, D)), tgt, "loss_head")

    d_norm = [None] * n_layers
    d_fox_in = [None] * NL
    d_fox_out = [None] * NL
    d_fox_b = [None] * NL
    d_conv_in = [None] * NL
    d_conv_out = [None] * NL
    d_conv_small = [None] * NL
    for i in reversed(range(n_layers)):
        j = i // 2
        g_i = norm_g[i:i + 1]
        if i % 2 == 0:
            h_in, hn, p1, p2, pf, ka, o, yv, qb = saved[i]
            d_fox_out[j] = mm_tn(yv, dh, 1, name=f"fox_out_dw_{i}")[0]
            dy = mm_nn_t(dh, wo_fox, j, f"fox_out_dx_{i}")
            do, dp1, da = attn_bwd_prep(dy, o, p1, H, f"attn_bwd_prep_{i}")
            dp1, dp2, dc = attn_bwd(p1, qb, do, da, p2, ka, dp1, H, scale, f"attn_bwd_{i}")
            dz, dbf = fox_decay_bwd(dc, pf, bfp[j:j + 1], H, f"fox_decay_bwd_{i}")
            d_fox_b[j] = dbf
            dw1 = mm_tn(hn, dp1, 1, name=f"fox_dw_qg_{i}")[0]
            dw2 = mm_tn(hn, dp2, 1, name=f"fox_dw_kv_{i}")[0]
            dwf = mm_tn(hn, dz, 1, name=f"fox_dw_f_{i}")[0]
            dw2 = dw2.reshape((D, H, 2, HEAD_DIM))
            d_fox_in[j] = jnp.concatenate([dw1[:, :W], dw2[:, :, 0].reshape((D, W)), dw2[:, :, 1].reshape((D, W)),
                                           dw1[:, W:], dwf[:, :H]], axis=-1)
            dhn = mm_nt(dp1, w1, j, 1, name=f"fox_dx_qg_{i}")
            dhn = mm_nt(dp2, w2, j, 1, name=f"fox_dx_kv_{i}", res=dhn)
            dhn = mm_nt(dz, wfp, j, 1, name=f"fox_dx_f_{i}", res=dhn)
        else:
            h_in, hn, proj, u2, yv = saved[i]
            d_conv_out[j] = mm_tn(yv, dh, 1, name=f"conv_out_dw_{i}")[0]
            dy = mm_nn_t(dh, wo_conv, j, f"conv_out_dx_{i}")
            du2, dproj, sm1 = conv_bwd_norm(dy, proj, u2, lng[j:j + 1], lnb[j:j + 1], f"conv_bwd_norm_{i}")
            dproj, sm2 = conv_bwd_taps(du2, proj, dwt[j], dproj, f"conv_bwd_taps_{i}")
            d_conv_small[j] = (sm1, sm2)
            d_conv_in[j] = mm_tn(hn, dproj, N_CHIPS, name=f"conv_dw_in_{i}")
            dhn = mm_nt(dproj, wc, j, N_CHIPS, name=f"conv_dx_{i}")
        dh, d_norm[i] = rms_bwd(dhn, h_in, g_i, dh, f"rms_bwd_{i}")

    g_a = jnp.stack([d.reshape((D, N_CHIPS, NA)).transpose((1, 0, 2)) for d in d_fox_in])
    g_b = jnp.stack([d.reshape((N_CHIPS, Dq, D)) for d in d_fox_out])
    g_c = jnp.stack(d_conv_in)
    g_d = jnp.stack([d.reshape((N_CHIPS, Dq, D)) for d in d_conv_out])
    big = [g_a, g_b, g_c, g_d]
    rcv = swap_halves(big)
    parts = [pair_sum(g, r, cidx, f"pair_sum_{n}") for n, (g, r) in enumerate(zip(big, rcv))]

    zrow = jnp.zeros((1, D), F32)
    rows = list(d_norm) + [d_gf]
    rows += [_pad_cols(d_fox_b[l][:, :H], D) for l in range(NL)]
    rows += [_pad_cols(loss_part[:, :1], D)]
    for l in range(NL):
        sm1, sm2 = d_conv_small[l]
        rows += [sm2[32:33], sm2[33:34], sm1[3:4]]
    for l in range(NL):
        rows += [d_conv_small[l][1][0:CONV_K]]
    rows += [d_conv_small[l][0][2:3] for l in range(NL)]
    rows += [d_conv_small[l][0][0:1] for l in range(NL)]
    rows += [d_conv_small[l][0][1:2] for l in range(NL)]
    n_rows = sum(r.shape[0] for r in rows)
    rows += [zrow] * (SMALL_ROWS - n_rows)
    small = jnp.concatenate(rows, axis=0)

    dev = 4 * lax.axis_index("x") + 2 * lax.axis_index("y") + core
    small_slots = lax.dynamic_update_slice(jnp.zeros((N_DEV, SMALL_ROWS, D), F32), small[None], (dev, 0, 0))
    got = exchange_partials(parts, small_slots)
    halves = [chip_sum(p, r, sel, f"chip_sum_{n}") for n, (p, r) in enumerate(zip(parts, got[:4]))]
    tot = dev_sum(got[4], "dev_sum")
    full = join_halves(halves)
    grad_fox_w_in = full[0].reshape(fox_w_in.shape)
    grad_fox_w_out = full[1].reshape(fox_w_out.shape)
    grad_conv_w_in = full[2].reshape(conv_w_in.shape)
    grad_conv_w_out = full[3].reshape(conv_w_out.shape)

    def mine(v):
        return lax.dynamic_slice_in_dim(v, chip * Dq, Dq, axis=v.ndim - 1)

    r = n_layers
    grad_norm_g = tot[0:r]
    grad_final = tot[r]
    grad_fox_b_f = tot[r + 1:r + 1 + NL, :H]
    loss = tot[r + 1 + NL, 0]
    r = r + 2 + NL
    gb_full = tot[r:r + 3 * NL].reshape((NL, 3 * C))
    grad_conv_b_in = lax.dynamic_slice_in_dim(gb_full, chip * 3 * Dq, 3 * Dq, axis=1)
    r += 3 * NL
    grad_conv_dw = mine(tot[r:r + CONV_K * NL].reshape((NL, CONV_K, C)))
    r += CONV_K * NL
    grad_conv_dw_b = mine(tot[r:r + NL])
    grad_conv_ln_g = mine(tot[r + NL:r + 2 * NL])
    grad_conv_ln_b = mine(tot[r + 2 * NL:r + 3 * NL])

    grads = [grad_norm_g, grad_fox_w_in, grad_fox_b_f, grad_fox_w_out, grad_conv_w_in, grad_conv_b_in,
             grad_conv_dw, grad_conv_dw_b, grad_conv_ln_g, grad_conv_ln_b, grad_conv_w_out, grad_final]
    ws = [norm_g, fox_w_in, fox_b_f, fox_w_out, conv_w_in, conv_b_in, conv_dw, conv_dw_b, conv_ln_g, conv_ln_b,
          conv_w_out, final_norm_g]
    ms = [m_norm_g, m_fox_w_in, m_fox_b_f, m_fox_w_out, m_conv_w_in, m_conv_b_in, m_conv_dw, m_conv_dw_b,
          m_conv_ln_g, m_conv_ln_b, m_conv_w_out, m_final_norm_g]
    vs = [v_norm_g, v_fox_w_in, v_fox_b_f, v_fox_w_out, v_conv_w_in, v_conv_b_in, v_conv_dw, v_conv_dw_b,
          v_conv_ln_g, v_conv_ln_b, v_conv_w_out, v_final_norm_g]
    deltas, new_ms, new_vs = [], [], []
    for n, (w_, g_, m_, v_) in enumerate(zip(ws, grads, ms, vs)):
        d_, nm_, nv_ = adamw(w_, g_, m_, v_, f"adamw_{n}")
        deltas.append(d_)
        new_ms.append(nm_)
        new_vs.append(nv_)
    grad_x = dh.reshape(x.shape)
    return (loss, grad_x, *grads, *deltas, *new_ms, *new_vs)


def mm_nt_res(y, wo, lidx, h, name):
    M, K = y.shape
    N = wo.shape[-1]
    tm = _tile(M, 1024)
    tn = _tile(N, 1024)
    grid = (M // tm, N // tn, 1)
    return _matmul(
        y, wo, contract="nn", grid=grid, name=name,
        a_spec=pl.BlockSpec((tm, K), lambda i, j, k: (i, 0)),
        b_spec=pl.BlockSpec((None, None, K, tn), lambda i, j, k: (lidx, 0, 0, j)),
        o_spec=pl.BlockSpec((tm, tn), lambda i, j, k: (i, j)),
        out_shape=jax.ShapeDtypeStruct((M, N), F32), acc_shape=(tm, tn),
        res=h, res_spec=pl.BlockSpec((tm, tn), lambda i, j, k: (i, j)),
    )


def mm_nn_t(dh, wo, lidx, name):
    M, K = dh.shape
    N = wo.shape[-2]
    tm = _tile(M, 512)
    tn = _tile(N, 1024)
    grid = (M // tm, N // tn, 1)
    return _matmul(
        dh, wo, contract="nt", grid=grid, name=name,
        a_spec=pl.BlockSpec((tm, K), lambda i, j, k: (i, 0)),
        b_spec=pl.BlockSpec((None, None, tn, K), lambda i, j, k: (lidx, 0, j, 0)),
        o_spec=pl.BlockSpec((tm, tn), lambda i, j, k: (i, j)),
        out_shape=jax.ShapeDtypeStruct((M, N), F32), acc_shape=(tm, tn),
    )
```

```python
import jax
import jax.numpy as jnp
from jax import lax
from jax.experimental import pallas as pl
from jax.experimental.pallas import tpu as pltpu

F32 = jnp.float32
BF16 = jnp.bfloat16
MESH = pl.DeviceIdType.MESH

RMS_EPS = 1e-6
LN_EPS = 1e-5
CONV_K = 31
HALO = 32
HEAD_DIM = 128
ADAM_LR = 0.001
ADAM_B1 = 0.9
ADAM_B2 = 0.999
ADAM_EPS = 1e-08
ADAM_WD = 0.01
ADAM_STEP = 10
N_CHIPS = 4
N_DEV = 8
VMEM_LIMIT = 56 * 1024 * 1024
NEG_BIG = -1e30
SMALL_ROWS = 88
PACK_ROWS = 80


def _pcall(body, **kw):
    return pl.pallas_call(body, **kw)


def _cparams(sem=None):
    return pltpu.CompilerParams(dimension_semantics=sem, vmem_limit_bytes=VMEM_LIMIT)


def _tile(n, cap, mult=128):
    if n <= cap:
        return n
    t = (cap // mult) * mult
    while t >= mult:
        if n % t == 0:
            return t
        t -= mult
    raise ValueError(f"no tile for {n} under {cap}")


def _sigmoid(x):
    return 1.0 / (1.0 + jnp.exp(-x))


def _split3(x):
    hi = x.astype(BF16).astype(F32)
    r = x - hi
    mid = r.astype(BF16).astype(F32)
    lo = (r - mid).astype(BF16).astype(F32)
    return hi, mid, lo


_DN = {
    "nn": (((1,), (0,)), ((), ())),
    "nt": (((1,), (1,)), ((), ())),
    "tn": (((0,), (0,)), ((), ())),
}


def _matmul(a, b, *, contract, grid, a_spec, b_spec, o_spec, out_shape, acc_shape, name,
            bias=None, bias_spec=None, res=None, res_spec=None, alias_res=False, prev=None):
    nk = grid[2]
    has_bias = bias is not None
    has_res = res is not None
    assert not (alias_res and prev is not None)

    def body(*refs):
        a_ref, b_ref = refs[0], refs[1]
        pos = 2
        bias_ref = res_ref = None
        if has_bias:
            bias_ref = refs[pos]
            pos += 1
        if has_res:
            res_ref = refs[pos]
            pos += 1
        if prev is not None:
            pos += 1
        o_ref = refs[pos]
        acc_ref = refs[pos + 1] if nk > 1 else None
        p = lax.dot_general(a_ref[...].astype(BF16), b_ref[...].astype(BF16), _DN[contract],
                            preferred_element_type=F32)

        def finish(v):
            if has_bias:
                v = v + bias_ref[...]
            if has_res:
                v = res_ref[...] + v
            o_ref[...] = v.astype(o_ref.dtype)

        if nk == 1:
            finish(p)
        else:
            k = pl.program_id(2)

            @pl.when(k == 0)
            def _():
                acc_ref[...] = p

            @pl.when(k > 0)
            def _():
                acc_ref[...] += p

            @pl.when(k == nk - 1)
            def _():
                finish(acc_ref[...])

    ins = [a, b]
    specs = [a_spec, b_spec]
    if has_bias:
        ins.append(bias)
        specs.append(bias_spec)
    if has_res:
        ins.append(res)
        specs.append(res_spec)
    aliases = {len(ins) - 1: 0} if (has_res and alias_res) else {}
    if prev is not None:
        ins.append(prev)
        specs.append(pl.BlockSpec(memory_space=pl.ANY))
        aliases = {len(ins) - 1: 0}
    return _pcall(
        body, name=name, grid=grid, in_specs=specs, out_specs=o_spec, out_shape=out_shape,
        scratch_shapes=[pltpu.VMEM(acc_shape, F32)] if nk > 1 else [],
        input_output_aliases=aliases,
        compiler_params=_cparams(("parallel", "parallel", "arbitrary")),
    )(*ins)


def mm_nn(a, w, lidx, n_slots, *, out_dtype, name, bias=None):
    M, K = a.shape
    Ns = w.shape[-1]
    tm = _tile(M, 1024)
    tn = _tile(Ns, 1024)
    per = Ns // tn
    grid = (M // tm, n_slots * per, 1)
    return _matmul(
        a, w, contract="nn", grid=grid, name=name,
        a_spec=pl.BlockSpec((tm, K), lambda i, j, k: (i, 0)),
        b_spec=pl.BlockSpec((None, None, K, tn), lambda i, j, k: (lidx, j // per, 0, j % per)),
        o_spec=pl.BlockSpec((tm, tn), lambda i, j, k: (i, j)),
        out_shape=jax.ShapeDtypeStruct((M, n_slots * Ns), out_dtype), acc_shape=(tm, tn),
        bias=bias, bias_spec=None if bias is None else pl.BlockSpec((1, tn), lambda i, j, k: (0, j)),
    )


def mm_nt(a, w, lidx, n_slots, *, name, res=None):
    M = a.shape[0]
    N, Ns = w.shape[-2], w.shape[-1]
    tm = _tile(M, 1024)
    tn = _tile(N, 1024)
    tk = _tile(Ns, 2048)
    per = Ns // tk
    grid = (M // tm, N // tn, n_slots * per)
    return _matmul(
        a, w, contract="nt", grid=grid, name=name,
        a_spec=pl.BlockSpec((tm, tk), lambda i, j, k: (i, k)),
        b_spec=pl.BlockSpec((None, None, tn, tk), lambda i, j, k: (lidx, k // per, j, k % per)),
        o_spec=pl.BlockSpec((tm, tn), lambda i, j, k: (i, j)),
        out_shape=jax.ShapeDtypeStruct((M, N), F32), acc_shape=(tm, tn),
        res=res, res_spec=None if res is None else pl.BlockSpec((tm, tn), lambda i, j, k: (i, j)),
        alias_res=res is not None,
    )


def mm_tn(a, b, n_slots, *, name, layers=1, lidx=0, prev=None):
    S, M = a.shape
    Ns = b.shape[1] // n_slots
    tm = _tile(M, 1024)
    tn = _tile(Ns, 1024)
    tk = _tile(S, 2048)
    per = Ns // tn
    grid = (M // tm, n_slots * per, S // tk)
    return _matmul(
        a, b, contract="tn", grid=grid, name=name,
        a_spec=pl.BlockSpec((tk, tm), lambda i, j, k: (k, i)),
        b_spec=pl.BlockSpec((tk, tn), lambda i, j, k: (k, j)),
        o_spec=pl.BlockSpec((None, None, tm, tn), lambda i, j, k: (lidx, j // per, i, j % per)),
        out_shape=jax.ShapeDtypeStruct((layers, n_slots, M, Ns), F32), acc_shape=(tm, tn), prev=prev,
    )


def rms_fwd(h, g, name):
    S, D = h.shape
    tm = _tile(S, 256, 8)

    def body(h_ref, g_ref, o_ref):
        x = h_ref[...]
        r = lax.rsqrt(jnp.mean(x * x, axis=-1, keepdims=True) + RMS_EPS)
        o_ref[...] = (x * r * g_ref[...]).astype(BF16)

    return _pcall(
        body, name=name, grid=(S // tm,),
        in_specs=[pl.BlockSpec((tm, D), lambda i: (i, 0)), pl.BlockSpec((1, D), lambda i: (0, 0))],
        out_specs=pl.BlockSpec((tm, D), lambda i: (i, 0)),
        out_shape=jax.ShapeDtypeStruct((S, D), BF16),
        compiler_params=_cparams(("parallel",)),
    )(h, g)


def _rms_bwd_rows(x, g, dy):
    d = x.shape[-1]
    r = lax.rsqrt(jnp.mean(x * x, axis=-1, keepdims=True) + RMS_EPS)
    gd = dy * g
    dx = r * gd - x * ((r * r * r) * (jnp.sum(x * gd, axis=-1, keepdims=True) / d))
    return dx, dy * x * r


def rms_bwd(dhn, h, g, dres, name):
    S, D = h.shape
    tm = _tile(S, 256, 8)

    def body(dhn_ref, h_ref, g_ref, dres_ref, dh_ref, dg_ref):
        dx, dgr = _rms_bwd_rows(h_ref[...], g_ref[...], dhn_ref[...])
        dh_ref[...] = dres_ref[...] + dx

        @pl.when(pl.program_id(0) == 0)
        def _():
            dg_ref[...] = jnp.zeros_like(dg_ref)

        dg_ref[...] += jnp.sum(dgr, axis=0, keepdims=True)

    row = pl.BlockSpec((tm, D), lambda i: (i, 0))
    vec = pl.BlockSpec((1, D), lambda i: (0, 0))
    return _pcall(
        body, name=name, grid=(S // tm,),
        in_specs=[row, row, vec, row], out_specs=[row, vec],
        out_shape=[jax.ShapeDtypeStruct((S, D), F32), jax.ShapeDtypeStruct((1, D), F32)],
        input_output_aliases={3: 0},
        compiler_params=_cparams(("arbitrary",)),
    )(dhn, h, g, dres)


def loss_head(h, g, target, name):
    S, D = h.shape
    tm = _tile(S, 256, 8)

    def body(h_ref, g_ref, t_ref, dh_ref, dg_ref, loss_ref):
        x = h_ref[...]
        gg = g_ref[...]
        r = lax.rsqrt(jnp.mean(x * x, axis=-1, keepdims=True) + RMS_EPS)
        y = x * r * gg
        e = y - t_ref[...]
        part = 0.5 * jnp.sum(jnp.mean(e * e, axis=-1, keepdims=True), axis=0, keepdims=True)
        dy = e * (1.0 / D)
        dx, dgr = _rms_bwd_rows(x, gg, dy)
        dh_ref[...] = dx

        @pl.when(pl.program_id(0) == 0)
        def _():
            dg_ref[...] = jnp.zeros_like(dg_ref)
            loss_ref[...] = jnp.zeros_like(loss_ref)

        dg_ref[...] += jnp.sum(dgr, axis=0, keepdims=True)
        loss_ref[...] += jnp.broadcast_to(part, loss_ref.shape)

    row = pl.BlockSpec((tm, D), lambda i: (i, 0))
    vec = pl.BlockSpec((1, D), lambda i: (0, 0))
    return _pcall(
        body, name=name, grid=(S // tm,),
        in_specs=[row, vec, row],
        out_specs=[row, vec, pl.BlockSpec((1, 128), lambda i: (0, 0))],
        out_shape=[jax.ShapeDtypeStruct((S, D), F32), jax.ShapeDtypeStruct((1, D), F32),
                   jax.ShapeDtypeStruct((1, 128), F32)],
        compiler_params=_cparams(("arbitrary",)),
    )(h, g, target)


def fox_decay(pf, bf, n_heads, scale, name):
    S = pf.shape[0]
    tm = _tile(S, 256, 8)
    inv_scale = 1.0 / scale

    def body(pf_ref, bf_ref, qa_ref, ka_ref, carry_ref):
        @pl.when(pl.program_id(0) == 0)
        def _():
            carry_ref[...] = jnp.zeros_like(carry_ref)

        z = pf_ref[...] + bf_ref[...]
        logf = jnp.minimum(z, 0.0) - jnp.log(1.0 + jnp.exp(-jnp.abs(z)))
        row = lax.broadcasted_iota(jnp.int32, (tm, tm), 0)
        col = lax.broadcasted_iota(jnp.int32, (tm, tm), 1)
        tri = (row >= col).astype(F32)
        c = jnp.dot(tri, logf, precision=lax.Precision.HIGHEST, preferred_element_type=F32) + carry_ref[...]
        carry_ref[...] = c[tm - 1:tm, :]
        lane = lax.broadcasted_iota(jnp.int32, (tm, HEAD_DIM), 1)
        for hh in range(n_heads):
            hi, mid, lo = _split3(c[:, hh:hh + 1] * inv_scale)
            qa = jnp.where(lane == 0, hi, jnp.where(lane == 1, mid, jnp.where(lane == 2, lo,
                 jnp.where(lane < 6, 1.0, 0.0))))
            ka = jnp.where(lane < 3, 1.0, jnp.where(lane == 3, -hi, jnp.where(lane == 4, -mid,
                 jnp.where(lane == 5, -lo, jnp.where(lane < 9, 1.0, 0.0)))))
            qa_ref[:, hh * HEAD_DIM:(hh + 1) * HEAD_DIM] = qa.astype(BF16)
            ka_ref[:, hh * HEAD_DIM:(hh + 1) * HEAD_DIM] = ka.astype(BF16)

    wide = pl.BlockSpec((tm, n_heads * HEAD_DIM), lambda i: (i, 0))
    return _pcall(
        body, name=name, grid=(S // tm,),
        in_specs=[pl.BlockSpec((tm, 128), lambda i: (i, 0)), pl.BlockSpec((1, 128), lambda i: (0, 0))],
        out_specs=[wide, wide],
        out_shape=[jax.ShapeDtypeStruct((S, n_heads * HEAD_DIM), BF16)] * 2,
        scratch_shapes=[pltpu.VMEM((1, 128), F32)],
        compiler_params=_cparams(("arbitrary",)),
    )(pf, bf)


def _attn_tile(S):
    return 512 if S % 512 == 0 and S >= 2048 else 128


def attn_fwd(p1, p2, qa, ka, n_heads, scale, name):
    S = p1.shape[0]
    W = n_heads * HEAD_DIM
    t = _attn_tile(S)
    nq = S // t

    def body(q_ref, g_ref, qa_ref, k_ref, v_ref, ka_ref, o_ref, y_ref, qb_ref, mp_ref, qq_ref, acc_ref):
        i = pl.program_id(1)
        lane = lax.broadcasted_iota(jnp.int32, (t, HEAD_DIM), 1)
        qa = qa_ref[...].astype(F32)

        def tile_with(neg_stat):
            hi, mid, lo = _split3(neg_stat)
            return jnp.where(lane == 6, hi, jnp.where(lane == 7, mid, jnp.where(lane == 8, lo, qa))).astype(BF16)

        def keys(start, width):
            rows = pl.ds(pl.multiple_of(start, t), width)
            return rows, jnp.concatenate([k_ref[rows, :], ka_ref[rows, :]], axis=1)

        def causal():
            r = lax.broadcasted_iota(jnp.int32, (t, t), 0)
            c = lax.broadcasted_iota(jnp.int32, (t, t), 1)
            return r >= c

        def over_keys(block):
            def wide(jj, carry):
                block(jj * (4 * t), 4 * t, False)
                return carry

            lax.fori_loop(0, i // 4, wide, 0)
            done = (i // 4) * 4

            @pl.when((i & 2) != 0)
            def _():
                block(done * t, 2 * t, False)

            @pl.when((i & 1) != 0)
            def _():
                block((done + (i & 2)) * t, t, False)

            block(i * t, t, True)

        qq_ref[:, :HEAD_DIM] = q_ref[...]
        qq_ref[:, HEAD_DIM:] = qa_ref[...]
        mp_ref[...] = jnp.full(mp_ref.shape, NEG_BIG, F32)

        def max_block(start, width, masked):
            _, kk = keys(start, width)
            s = lax.dot_general(qq_ref[...], kk, _DN["nt"], preferred_element_type=F32)
            if masked:
                s = jnp.where(causal(), s, NEG_BIG)
            part = s[:, 0:HEAD_DIM]
            for a in range(1, width // HEAD_DIM):
                part = jnp.maximum(part, s[:, a * HEAD_DIM:(a + 1) * HEAD_DIM])
            mp_ref[...] = jnp.maximum(mp_ref[...], part)

        over_keys(max_block)
        m = jnp.max(mp_ref[...], axis=1, keepdims=True)
        qq_ref[:, HEAD_DIM:] = tile_with(-m)
        acc_ref[...] = jnp.zeros_like(acc_ref)

        def sum_block(start, width, masked):
            rows, kk = keys(start, width)
            a = lax.dot_general(qq_ref[...], kk, _DN["nt"], preferred_element_type=F32)
            p = jnp.exp(scale * a)
            if masked:
                p = jnp.where(causal(), p, 0.0)
            ones0 = jnp.where(lax.broadcasted_iota(jnp.int32, (width, HEAD_DIM), 1) == 0, 1.0, 0.0).astype(BF16)
            vv = jnp.concatenate([v_ref[rows, :], ones0], axis=1)
            acc_ref[...] += jnp.dot(p.astype(BF16), vv, preferred_element_type=F32)

        over_keys(sum_block)

        l = acc_ref[:, HEAD_DIM:HEAD_DIM + 1]
        o = acc_ref[:, :HEAD_DIM] / l
        gate = g_ref[...].astype(F32)
        o_ref[...] = o.astype(BF16)
        y_ref[...] = (o * (gate * _sigmoid(gate))).astype(BF16)
        qb_ref[...] = tile_with(-(m + jnp.log(l) * (1.0 / scale)))

    H = n_heads
    qtile = lambda off: pl.BlockSpec((t, HEAD_DIM), lambda h, i: (i, off + h))
    full = lambda fn: pl.BlockSpec((S, HEAD_DIM), fn)
    return _pcall(
        body, name=name, grid=(H, nq),
        in_specs=[qtile(0), qtile(H), qtile(0),
                  full(lambda h, i: (0, 2 * h)), full(lambda h, i: (0, 2 * h + 1)), full(lambda h, i: (0, h))],
        out_specs=[qtile(0), qtile(0), qtile(0)],
        out_shape=[jax.ShapeDtypeStruct((S, W), BF16)] * 3,
        scratch_shapes=[pltpu.VMEM((t, HEAD_DIM), F32), pltpu.VMEM((t, 2 * HEAD_DIM), BF16),
                        pltpu.VMEM((t, 2 * HEAD_DIM), F32)],
        compiler_params=_cparams(("parallel", "arbitrary")),
    )(p1, p1, qa, p2, p2, ka)


def attn_bwd_prep(dy, o, p1, n_heads, name):
    S, W = dy.shape
    tm = _tile(S, 256, 8)
    H = n_heads

    def body(dy_ref, o_ref, g_ref, do_ref, dg_ref, da_ref):
        lane = lax.broadcasted_iota(jnp.int32, (tm, HEAD_DIM), 1)
        for hh in range(H):
            cs = slice(hh * HEAD_DIM, (hh + 1) * HEAD_DIM)
            g = g_ref[:, cs].astype(F32)
            oo = o_ref[:, cs].astype(F32)
            dyv = dy_ref[:, cs]
            sg = _sigmoid(g)
            do = dyv * (g * sg)
            do_ref[:, cs] = do.astype(BF16)
            dg_ref[:, cs] = (dyv * oo * (sg * (1.0 + g * (1.0 - sg)))).astype(BF16)
            hi, mid, lo = _split3(-jnp.sum(do * oo, axis=1, keepdims=True))
            da = jnp.where(lane == 0, hi, jnp.where(lane == 1, mid, jnp.where(lane == 2, lo, 0.0)))
            da_ref[:, cs] = da.astype(BF16)

    row = lambda blk: pl.BlockSpec((tm, W), lambda i: (i, blk))
    return _pcall(
        body, name=name, grid=(S // tm,),
        in_specs=[row(0), row(0), row(1)],
        out_specs=[row(0), row(1), row(0)],
        out_shape=[jax.ShapeDtypeStruct((S, W), BF16), jax.ShapeDtypeStruct((S, 2 * W), BF16),
                   jax.ShapeDtypeStruct((S, W), BF16)],
        compiler_params=_cparams(("parallel",)),
    )(dy, o, p1)


def attn_bwd(p1, qb, do, da, p2, ka, dp1, n_heads, scale, name):
    S = p1.shape[0]
    W = n_heads * HEAD_DIM
    t = _attn_tile(S)
    nb = S // t
    H = n_heads

    def body(q_ref, qb_ref, do_ref, da_ref, k_ref, v_ref, ka_ref, dp1_in, dq_ref, dkv_ref, dc_ref,
             dq_acc, dk_acc, dv_acc):
        del dp1_in
        h = pl.program_id(0)
        j = pl.program_id(1)

        @pl.when(j == 0)
        def _():
            dq_acc[...] = jnp.zeros_like(dq_acc)

        @pl.when((j == 0) & (h == 0))
        def _():
            dc_ref[...] = jnp.zeros_like(dc_ref)

        lane = lax.broadcasted_iota(jnp.int32, (t, HEAD_DIM), 1)
        ones3 = jnp.where(lane < 3, 1.0, 0.0).astype(BF16)
        kk = jnp.concatenate([k_ref[...], ka_ref[...]], axis=1)
        vv = jnp.concatenate([v_ref[...], ones3], axis=1)
        dk_acc[...] = jnp.zeros_like(dk_acc)
        dv_acc[...] = jnp.zeros_like(dv_acc)

        def block(start, width, masked):
            rows = pl.ds(pl.multiple_of(start, t), width)
            qq = jnp.concatenate([q_ref[rows, :], qb_ref[rows, :]], axis=1)
            dd = jnp.concatenate([do_ref[rows, :], da_ref[rows, :]], axis=1)
            a = lax.dot_general(qq, kk, _DN["nt"], preferred_element_type=F32)
            p = jnp.exp(scale * a)
            if masked:
                r = lax.broadcasted_iota(jnp.int32, (t, t), 0)
                c = lax.broadcasted_iota(jnp.int32, (t, t), 1)
                p = jnp.where(r >= c, p, 0.0)
            dpd = lax.dot_general(dd, vv, _DN["nt"], preferred_element_type=F32)
            ds = (p * dpd).astype(BF16)
            pb = p.astype(BF16)
            dv_acc[...] += lax.dot_general(pb, dd, _DN["tn"], preferred_element_type=F32)
            dk_acc[...] += lax.dot_general(ds, qq, _DN["tn"], preferred_element_type=F32)
            dq_acc[rows, :] += jnp.dot(ds, kk, preferred_element_type=F32)

        block(j * t, t, True)
        n_after = nb - 1 - j

        @pl.when((n_after & 1) != 0)
        def _():
            block((j + 1) * t, t, False)

        first = j + 1 + (n_after & 1)

        def loop_body(ii, carry):
            block((first + 2 * ii) * t, 2 * t, False)
            return carry

        lax.fori_loop(0, n_after // 2, loop_body, 0)

        dkv_ref[...] = jnp.concatenate([dk_acc[:, :HEAD_DIM] * scale, dv_acc[:, :HEAD_DIM]], axis=1).astype(BF16)
        colsum = dk_acc[:, HEAD_DIM + 3:HEAD_DIM + 4]
        krows = pl.ds(pl.multiple_of(j * t, t), t)
        dc_ref[krows, :] += jnp.where(lane == h, -colsum, 0.0)

        @pl.when(j == nb - 1)
        def _():
            dq_ref[...] = (dq_acc[:, :HEAD_DIM] * scale).astype(BF16)
            lane_s = lax.broadcasted_iota(jnp.int32, (S, HEAD_DIM), 1)
            dc_ref[...] += jnp.where(lane_s == h, dq_acc[:, HEAD_DIM:HEAD_DIM + 1], 0.0)

    full = lambda fn: pl.BlockSpec((S, HEAD_DIM), fn)
    ktile = lambda fn: pl.BlockSpec((t, HEAD_DIM), fn)
    return _pcall(
        body, name=name, grid=(H, nb),
        in_specs=[full(lambda h, j: (0, h)), full(lambda h, j: (0, h)), full(lambda h, j: (0, h)),
                  full(lambda h, j: (0, h)),
                  ktile(lambda h, j: (j, 2 * h)), ktile(lambda h, j: (j, 2 * h + 1)), ktile(lambda h, j: (j, h)),
                  pl.BlockSpec(memory_space=pl.ANY)],
        out_specs=[full(lambda h, j: (0, h)),
                   pl.BlockSpec((t, 2 * HEAD_DIM), lambda h, j: (j, h)),
                   pl.BlockSpec((S, 128), lambda h, j: (0, 0))],
        out_shape=[jax.ShapeDtypeStruct((S, 2 * W), BF16), jax.ShapeDtypeStruct((S, 2 * W), BF16),
                   jax.ShapeDtypeStruct((S, 128), F32)],
        scratch_shapes=[pltpu.VMEM((S, 2 * HEAD_DIM), F32), pltpu.VMEM((t, 2 * HEAD_DIM), F32),
                        pltpu.VMEM((t, 2 * HEAD_DIM), F32)],
        input_output_aliases={7: 0},
        compiler_params=_cparams(("arbitrary", "arbitrary")),
    )(p1, qb, do, da, p2, p2, ka, dp1)


def fox_decay_bwd(dc, pf, bf, n_heads, name):
    S = dc.shape[0]
    tm = _tile(S, 256, 8)
    nb = S // tm

    def body(dc_ref, pf_ref, bf_ref, dz_ref, db_ref, carry_ref):
        @pl.when(pl.program_id(0) == 0)
        def _():
            carry_ref[...] = jnp.zeros_like(carry_ref)
            db_ref[...] = jnp.zeros_like(db_ref)

        row = lax.broadcasted_iota(jnp.int32, (tm, tm), 0)
        col = lax.broadcasted_iota(jnp.int32, (tm, tm), 1)
        tri = (row <= col).astype(F32)
        dlogf = jnp.dot(tri, dc_ref[...], precision=lax.Precision.HIGHEST, preferred_element_type=F32) + carry_ref[...]
        carry_ref[...] = dlogf[0:1, :]
        z = pf_ref[...] + bf_ref[...]
        lane = lax.broadcasted_iota(jnp.int32, (tm, 128), 1)
        dz = jnp.where(lane < n_heads, dlogf * _sigmoid(-z), 0.0)
        dz_ref[...] = dz.astype(BF16)
        db_ref[...] += jnp.sum(dz, axis=0, keepdims=True)

    rev = pl.BlockSpec((tm, 128), lambda i: (nb - 1 - i, 0))
    vec = pl.BlockSpec((1, 128), lambda i: (0, 0))
    return _pcall(
        body, name=name, grid=(nb,),
        in_specs=[rev, rev, vec], out_specs=[rev, vec],
        out_shape=[jax.ShapeDtypeStruct((S, 128), BF16), jax.ShapeDtypeStruct((1, 128), F32)],
        scratch_shapes=[pltpu.VMEM((1, 128), F32)],
        compiler_params=_cparams(("arbitrary",)),
    )(dc, pf, bf)


def _conv_tile(S):
    return _tile(S, 256, HALO)


def _fill_glu(ubuf, a_ref, b_ref, ah_ref, bh_ref, first, tm):
    uh = ah_ref[...].astype(F32) * _sigmoid(bh_ref[...].astype(F32))
    ubuf[0:HALO, :] = jnp.where(first, 0.0, uh)
    ubuf[HALO:HALO + tm, :] = a_ref[...].astype(F32) * _sigmoid(b_ref[...].astype(F32))


def conv_fwd(proj, dw, dwb, lng, lnb, name):
    S = proj.shape[0]
    C = proj.shape[1] // 3
    tm = _conv_tile(S)
    hb = tm // HALO
    nch = C // 128
    rb = _tile(tm, 128, 8)

    def body(a_ref, b_ref, g_ref, ah_ref, bh_ref, dw_ref, dwb_ref, lng_ref, lnb_ref, u2_ref, y_ref, ubuf, sh):
        i = pl.program_id(0)
        _fill_glu(ubuf, a_ref, b_ref, ah_ref, bh_ref, i == 0, tm)

        def chunk(cc, carry):
            cols = pl.ds(pl.multiple_of(cc * 128, 128), 128)
            for r0 in range(0, tm, rb):
                acc = jnp.broadcast_to(dwb_ref[:, cols], (rb, 128))
                for b in range(8):
                    taps = list(range(b, CONV_K, 8))
                    n = rb + 8 * (len(taps) - 1)
                    sh[0:n, :] = ubuf[pl.ds(HALO - (CONV_K - 1) + b + r0, n), cols]
                    for a, k in enumerate(taps):
                        acc = acc + dw_ref[k:k + 1, cols] * sh[8 * a:8 * a + rb, :]
                u2_ref[pl.ds(r0, rb), cols] = acc
            return carry

        lax.fori_loop(0, nch, chunk, 0)
        x = u2_ref[...]
        mu = jnp.mean(x, axis=-1, keepdims=True)
        xc = x - mu
        var = jnp.mean(xc * xc, axis=-1, keepdims=True)
        ln = xc * lax.rsqrt(var + LN_EPS) * lng_ref[...] + lnb_ref[...]
        gate = g_ref[...].astype(F32)
        y_ref[...] = ((ln * _sigmoid(ln)) * (gate * _sigmoid(gate))).astype(BF16)

    blk = lambda cb: pl.BlockSpec((tm, C), lambda i: (i, cb))
    halo = lambda cb: pl.BlockSpec((HALO, C), lambda i: (jnp.maximum(i * hb - 1, 0), cb))
    vec = pl.BlockSpec((1, C), lambda i: (0, 0))
    return _pcall(
        body, name=name, grid=(S // tm,),
        in_specs=[blk(0), blk(1), blk(2), halo(0), halo(1),
                  pl.BlockSpec((CONV_K, C), lambda i: (0, 0)), vec, vec, vec],
        out_specs=[blk(0), blk(0)],
        out_shape=[jax.ShapeDtypeStruct((S, C), F32), jax.ShapeDtypeStruct((S, C), BF16)],
        scratch_shapes=[pltpu.VMEM((HALO + tm, C), F32), pltpu.VMEM((rb + HALO, 128), F32)],
        compiler_params=_cparams(("parallel",)),
    )(proj, proj, proj, proj, proj, dw, dwb, lng, lnb)


def conv_bwd_norm(dy, proj, u2, lng, lnb, name):
    S, C = dy.shape
    tm = _tile(S, 256, 8)

    def body(dy_ref, g_ref, u2_ref, lng_ref, lnb_ref, du2_ref, dg_ref, sm_ref):
        x = u2_ref[...]
        mu = jnp.mean(x, axis=-1, keepdims=True)
        xc = x - mu
        var = jnp.mean(xc * xc, axis=-1, keepdims=True)
        rs = lax.rsqrt(var + LN_EPS)
        xhat = xc * rs
        gam = lng_ref[...]
        ln = xhat * gam + lnb_ref[...]
        sl = _sigmoid(ln)
        u3 = ln * sl
        gate = g_ref[...].astype(F32)
        sg = _sigmoid(gate)
        dyv = dy_ref[...]
        dgate = dyv * u3 * (sg * (1.0 + gate * (1.0 - sg)))
        dln = (dyv * (gate * sg)) * (sl * (1.0 + ln * (1.0 - sl)))
        dxh = dln * gam
        du2 = rs * (dxh - jnp.mean(dxh, axis=-1, keepdims=True)
                    - xhat * jnp.mean(dxh * xhat, axis=-1, keepdims=True))
        du2_ref[...] = du2
        dg_ref[...] = dgate.astype(BF16)

        @pl.when(pl.program_id(0) == 0)
        def _():
            sm_ref[...] = jnp.zeros_like(sm_ref)

        sm_ref[0:1, :] += jnp.sum(dln * xhat, axis=0, keepdims=True)
        sm_ref[1:2, :] += jnp.sum(dln, axis=0, keepdims=True)
        sm_ref[2:3, :] += jnp.sum(du2, axis=0, keepdims=True)
        sm_ref[3:4, :] += jnp.sum(dgate, axis=0, keepdims=True)

    blk = lambda cb: pl.BlockSpec((tm, C), lambda i: (i, cb))
    vec = pl.BlockSpec((1, C), lambda i: (0, 0))
    return _pcall(
        body, name=name, grid=(S // tm,),
        in_specs=[blk(0), blk(2), blk(0), vec, vec],
        out_specs=[blk(0), blk(2), pl.BlockSpec((8, C), lambda i: (0, 0))],
        out_shape=[jax.ShapeDtypeStruct((S, C), F32), jax.ShapeDtypeStruct((S, 3 * C), BF16),
                   jax.ShapeDtypeStruct((8, C), F32)],
        compiler_params=_cparams(("arbitrary",)),
    )(dy, proj, u2, lng, lnb)


def conv_bwd_taps(du2, proj, dw, dproj, name):
    S, C = du2.shape
    tm = _conv_tile(S)
    hb = tm // HALO
    nb = S // tm
    nch = C // 128
    rb = _tile(tm, 128, 8)

    def body(d_ref, dh_ref, a_ref, b_ref, ah_ref, bh_ref, dw_ref, dp_in, dab_ref, sm_ref, ubuf, dbuf, sh, sh2):
        del dp_in
        i = pl.program_id(0)
        _fill_glu(ubuf, a_ref, b_ref, ah_ref, bh_ref, i == 0, tm)
        dbuf[0:tm, :] = d_ref[...]
        dbuf[tm:tm + HALO, :] = jnp.where(i == nb - 1, 0.0, dh_ref[...])

        @pl.when(i == 0)
        def _():
            sm_ref[...] = jnp.zeros_like(sm_ref)

        def chunk(cc, carry):
            cols = pl.ds(pl.multiple_of(cc * 128, 128), 128)
            cols_b = pl.ds(pl.multiple_of(C + cc * 128, 128), 128)
            for r0 in range(0, tm, rb):
                d0 = dbuf[r0:r0 + rb, cols]
                du = jnp.zeros((rb, 128), F32)
                for b in range(8):
                    offs = list(range(b, CONV_K, 8))
                    n = rb + 8 * (len(offs) - 1)
                    sh[0:n, :] = dbuf[pl.ds(r0 + b, n), cols]
                    for a, o in enumerate(offs):
                        k = CONV_K - 1 - o
                        du = du + dw_ref[k:k + 1, cols] * sh[8 * a:8 * a + rb, :]
                    sh2[0:n, :] = ubuf[pl.ds(HALO - (CONV_K - 1) + b + r0, n), cols]
                    for a, k in enumerate(offs):
                        sm_ref[k:k + 1, cols] += jnp.sum(d0 * sh2[8 * a:8 * a + rb, :], axis=0, keepdims=True)
                rows = pl.ds(r0, rb)
                av = a_ref[rows, cols].astype(F32)
                sb = _sigmoid(b_ref[rows, cols].astype(F32))
                da = du * sb
                db = du * av * sb * (1.0 - sb)
                dab_ref[rows, cols] = da.astype(BF16)
                dab_ref[rows, cols_b] = db.astype(BF16)
                sm_ref[32:33, cols] += jnp.sum(da, axis=0, keepdims=True)
                sm_ref[33:34, cols] += jnp.sum(db, axis=0, keepdims=True)
            return carry

        lax.fori_loop(0, nch, chunk, 0)

    blk = lambda cb: pl.BlockSpec((tm, C), lambda i: (i, cb))
    halo = lambda cb: pl.BlockSpec((HALO, C), lambda i: (jnp.maximum(i * hb - 1, 0), cb))
    nxt = pl.BlockSpec((HALO, C), lambda i: (jnp.minimum((i + 1) * hb, nb * hb - 1), 0))
    return _pcall(
        body, name=name, grid=(nb,),
        in_specs=[blk(0), nxt, blk(0), blk(1), halo(0), halo(1),
                  pl.BlockSpec((CONV_K, C), lambda i: (0, 0)), pl.BlockSpec(memory_space=pl.ANY)],
        out_specs=[pl.BlockSpec((tm, 2 * C), lambda i: (i, 0)), pl.BlockSpec((40, C), lambda i: (0, 0))],
        out_shape=[jax.ShapeDtypeStruct((S, 3 * C), BF16), jax.ShapeDtypeStruct((40, C), F32)],
        scratch_shapes=[pltpu.VMEM((HALO + tm, C), F32), pltpu.VMEM((tm + HALO, C), F32),
                        pltpu.VMEM((rb + HALO, 128), F32), pltpu.VMEM((rb + HALO, 128), F32)],
        input_output_aliases={7: 0},
        compiler_params=_cparams(("arbitrary",)),
    )(du2, du2, proj, proj, proj, proj, dw, dproj)


def _rows_tile(R, Cc, budget=1 << 18):
    cap = max(8, budget // max(Cc, 1))
    if R <= cap:
        return R
    t = (cap // 8) * 8
    while t >= 8:
        if R % t == 0:
            return t
        t -= 8
    return R


def cast_into_slot(w, chip, name):
    L, R, Cc = w.shape
    tr = _rows_tile(R, Cc)

    def body(s_ref, w_ref, o_ref):
        del s_ref
        o_ref[...] = w_ref[...].astype(BF16)

    grid_spec = pltpu.PrefetchScalarGridSpec(
        num_scalar_prefetch=1, grid=(L, R // tr),
        in_specs=[pl.BlockSpec((None, tr, Cc), lambda l, r, s: (l, r, 0))],
        out_specs=pl.BlockSpec((None, None, tr, Cc), lambda l, r, s: (l, s[0], r, 0)),
    )
    return _pcall(
        body, name=name, grid_spec=grid_spec, out_shape=jax.ShapeDtypeStruct((L, N_CHIPS, R, Cc), BF16),
        compiler_params=_cparams(("parallel", "parallel")),
    )(chip, w)


def pair_sum(g, rcv, cidx, name):
    L, K, _, half, Cc = g.shape
    g5 = g
    tr = _rows_tile(half, Cc)

    def body(c_ref, g_ref, r_ref, o_ref):
        del c_ref
        o_ref[...] = (g_ref[...] + r_ref[...]).astype(BF16)

    grid_spec = pltpu.PrefetchScalarGridSpec(
        num_scalar_prefetch=1, grid=(L, K, half // tr),
        in_specs=[pl.BlockSpec((None, None, None, tr, Cc), lambda l, k, r, c: (l, k, c[0], r, 0)),
                  pl.BlockSpec((None, None, tr, Cc), lambda l, k, r, c: (l, k, r, 0))],
        out_specs=pl.BlockSpec((None, None, tr, Cc), lambda l, k, r, c: (l, k, r, 0)),
    )
    return _pcall(
        body, name=name, grid_spec=grid_spec, out_shape=jax.ShapeDtypeStruct((L, K, half, Cc), BF16),
        compiler_params=_cparams(("parallel", "parallel", "parallel")),
    )(cidx, g5, rcv)


def chip_sum(parts, got, sel, name):
    L, _, R, Cc = parts.shape
    n_got = got.shape[1]
    tr = _rows_tile(R, Cc)

    def body(s_ref, p_ref, g_ref, o_ref):
        del s_ref
        acc = p_ref[...].astype(F32)
        for k in range(n_got):
            acc = acc + g_ref[k].astype(F32)
        o_ref[...] = acc

    grid_spec = pltpu.PrefetchScalarGridSpec(
        num_scalar_prefetch=1, grid=(L, R // tr),
        in_specs=[pl.BlockSpec((None, None, tr, Cc), lambda l, r, s: (l, s[0], r, 0)),
                  pl.BlockSpec((None, n_got, tr, Cc), lambda l, r, s: (l, 0, r, 0))],
        out_specs=pl.BlockSpec((None, None, tr, Cc), lambda l, r, s: (l, s[1], r, 0)),
    )
    return _pcall(
        body, name=name, grid_spec=grid_spec, out_shape=jax.ShapeDtypeStruct((L, 2, R, Cc), F32),
        compiler_params=_cparams(("parallel", "parallel")),
    )(sel, parts, got)


def dev_sum(parts, name):
    K, R, Cc = parts.shape

    def body(p_ref, o_ref):
        acc = p_ref[0]
        for k in range(1, K):
            acc = acc + p_ref[k]
        o_ref[...] = acc

    return _pcall(
        body, name=name, grid=(R // 8,),
        in_specs=[pl.BlockSpec((K, 8, Cc), lambda r: (0, r, 0))],
        out_specs=pl.BlockSpec((8, Cc), lambda r: (r, 0)),
        out_shape=jax.ShapeDtypeStruct((R, Cc), F32), compiler_params=_cparams(("parallel",)),
    )(parts)


def adamw(w, g, m, v, name):
    shape = w.shape
    if w.ndim == 3:
        L, R, Cc = shape
    else:
        L, R, Cc = 1, (1 if w.ndim == 1 else shape[0]), shape[-1]
    view = lambda t: t.reshape((L, R, Cc))
    tr = _rows_tile(R, Cc, budget=1 << 17)
    c1 = 1.0 - ADAM_B1 ** ADAM_STEP
    c2 = 1.0 - ADAM_B2 ** ADAM_STEP

    def body(w_ref, g_ref, m_ref, v_ref, d_ref, nm_ref, nv_ref):
        gg = g_ref[...]
        nm = ADAM_B1 * m_ref[...] + (1.0 - ADAM_B1) * gg
        nv = ADAM_B2 * v_ref[...] + (1.0 - ADAM_B2) * (gg * gg)
        d_ref[...] = -ADAM_LR * ((nm / c1) / (jnp.sqrt(nv / c2) + ADAM_EPS) + ADAM_WD * w_ref[...])
        nm_ref[...] = nm
        nv_ref[...] = nv

    spec = pl.BlockSpec((None, tr, Cc), lambda l, r: (l, r, 0))
    outs = _pcall(
        body, name=name, grid=(L, R // tr), in_specs=[spec] * 4, out_specs=[spec] * 3,
        out_shape=[jax.ShapeDtypeStruct((L, R, Cc), F32)] * 3, compiler_params=_cparams(("parallel", "parallel")),
    )(view(w), view(g), view(m), view(v))
    return tuple(o.reshape(shape) for o in outs)


def _place():
    x, y, c = lax.axis_index("x"), lax.axis_index("y"), lax.axis_index("c")
    other_chips = [(1 - x, y), (x, 1 - y), (1 - x, 1 - y)]
    return x, y, c, other_chips


def _rcopy(src, dst, ssem, rsem, k, to):
    return pltpu.make_async_remote_copy(src_ref=src, dst_ref=dst, send_sem=ssem.at[k], recv_sem=rsem.at[k],
                                        device_id=to, device_id_type=MESH)


def gather_weights(slots, small):
    nt = len(slots)

    def body(*refs):
        outs, small_out = refs[nt + 1:2 * nt + 1], refs[2 * nt + 1]
        ssem, rsem = refs[2 * nt + 2:]
        x, y, c, chips = _place()
        me = 2 * x + y
        sib = (x, y, 1 - c)

        def rows(t, half_of):
            half = outs[t].shape[2] // 2
            return pl.ds(half_of * half, half)

        first, passed = [], []
        for t in range(nt):
            mine = outs[t].at[:, me, rows(t, c), :]
            for j, chip in enumerate(chips):
                first.append(_rcopy(mine, mine, ssem, rsem, 6 * t + j, (*chip, c)))
        for j, chip in enumerate(chips):
            first.append(_rcopy(small_out.at[me], small_out.at[me], ssem, rsem, 6 * nt + j, (*chip, c)))
        for cp in first:
            cp.start()
        for t in range(nt):
            for j, (px, py) in enumerate(chips):
                land = outs[t].at[:, 2 * px + py, rows(t, c), :]
                _rcopy(land, land, ssem, rsem, 6 * t + j, (x, y, c)).wait_recv()
                fw = _rcopy(land, land, ssem, rsem, 6 * t + 3 + j, sib)
                fw.start()
                passed.append(fw)
        for j, (px, py) in enumerate(chips):
            land = small_out.at[2 * px + py]
            _rcopy(land, land, ssem, rsem, 6 * nt + j, (x, y, c)).wait_recv()
        for t in range(nt):
            for j, (px, py) in enumerate(chips):
                land = outs[t].at[:, 2 * px + py, rows(t, 1 - c), :]
                _rcopy(land, land, ssem, rsem, 6 * t + 3 + j, (x, y, c)).wait_recv()
        for cp in first + passed:
            cp.wait_send()

    ops = list(slots) + [small]
    nsem = 6 * nt + 3
    return _pcall(
        body, name="gather_weights", out_shape=[jax.ShapeDtypeStruct(s.shape, s.dtype) for s in ops],
        in_specs=[pl.BlockSpec(memory_space=pl.ANY)] * (nt + 1),
        out_specs=[pl.BlockSpec(memory_space=pl.ANY)] * (nt + 1),
        input_output_aliases={n: n for n in range(nt + 1)},
        scratch_shapes=[pltpu.SemaphoreType.DMA((nsem,)), pltpu.SemaphoreType.DMA((nsem,))],
    )(*ops)


def swap_halves(grads):
    nt = len(grads)

    def body(*refs):
        ins, outs = refs[:nt], refs[nt:2 * nt]
        ssem, rsem = refs[2 * nt:]
        x, y, c, _ = _place()
        sib = (x, y, 1 - c)
        cps = [_rcopy(ins[t].at[:, :, 1 - c], outs[t], ssem, rsem, t, sib) for t in range(nt)]
        for cp in cps:
            cp.start()
        for cp in cps:
            cp.wait()

    out_shape = [jax.ShapeDtypeStruct(g.shape[:2] + g.shape[3:], g.dtype) for g in grads]
    return _pcall(
        body, name="swap_halves", out_shape=out_shape,
        in_specs=[pl.BlockSpec(memory_space=pl.ANY)] * nt, out_specs=[pl.BlockSpec(memory_space=pl.ANY)] * nt,
        scratch_shapes=[pltpu.SemaphoreType.DMA((nt,)), pltpu.SemaphoreType.DMA((nt,))],
    )(*grads)


def exchange_partials(parts, small):
    nt = len(parts)

    def body(*refs):
        ins = refs[:nt]
        outs, small_out = refs[nt + 1:2 * nt + 1], refs[2 * nt + 1]
        ssem, rsem = refs[2 * nt + 2:]
        x, y, c, chips = _place()
        dev = 4 * x + 2 * y + c
        sends, lands = [], []
        for t in range(nt):
            for j, (px, py) in enumerate(chips):
                k = 3 * t + j
                sends.append(_rcopy(ins[t].at[:, 2 * px + py], outs[t].at[:, j], ssem, rsem, k, (px, py, c)))
                land = outs[t].at[:, j]
                lands.append(_rcopy(land, land, ssem, rsem, k, (x, y, c)))
        peers = [(px, py, pc) for pc in (c, 1 - c) for (px, py) in [(x, y)] + chips][1:]
        for j, (px, py, pc) in enumerate(peers):
            k = 3 * nt + j
            sends.append(_rcopy(small_out.at[dev], small_out.at[dev], ssem, rsem, k, (px, py, pc)))
            land = small_out.at[4 * px + 2 * py + pc]
            lands.append(_rcopy(land, land, ssem, rsem, k, (x, y, c)))
        for cp in sends:
            cp.start()
        for cp in lands:
            cp.wait_recv()
        for cp in sends:
            cp.wait_send()

    out_shape = [jax.ShapeDtypeStruct((p.shape[0], N_CHIPS - 1) + p.shape[2:], p.dtype) for p in parts]
    out_shape.append(jax.ShapeDtypeStruct(small.shape, small.dtype))
    nsem = 3 * nt + 7
    return _pcall(
        body, name="exchange_partials", out_shape=out_shape,
        in_specs=[pl.BlockSpec(memory_space=pl.ANY)] * (nt + 1),
        out_specs=[pl.BlockSpec(memory_space=pl.ANY)] * (nt + 1),
        input_output_aliases={nt: nt},
        scratch_shapes=[pltpu.SemaphoreType.DMA((nsem,)), pltpu.SemaphoreType.DMA((nsem,))],
    )(*parts, small)


def join_halves(halves):
    nt = len(halves)

    def body(*refs):
        outs = refs[nt:2 * nt]
        ssem, rsem = refs[2 * nt:]
        x, y, c, _ = _place()
        sib = (x, y, 1 - c)
        sends = [_rcopy(outs[t].at[:, c], outs[t].at[:, c], ssem, rsem, t, sib) for t in range(nt)]
        for cp in sends:
            cp.start()
        for t in range(nt):
            land = outs[t].at[:, 1 - c]
            _rcopy(land, land, ssem, rsem, t, (x, y, c)).wait_recv()
        for cp in sends:
            cp.wait_send()

    return _pcall(
        body, name="join_halves", out_shape=[jax.ShapeDtypeStruct(h.shape, h.dtype) for h in halves],
        in_specs=[pl.BlockSpec(memory_space=pl.ANY)] * nt, out_specs=[pl.BlockSpec(memory_space=pl.ANY)] * nt,
        input_output_aliases={n: n for n in range(nt)},
        scratch_shapes=[pltpu.SemaphoreType.DMA((nt,)), pltpu.SemaphoreType.DMA((nt,))],
    )(*halves)


def _pad_cols(a, n):
    return jnp.pad(a, [(0, 0)] * (a.ndim - 1) + [(0, n - a.shape[-1])])


def kernel(x, norm_g, fox_w_in, fox_b_f, fox_w_out, conv_w_in, conv_b_in, conv_dw, conv_dw_b, conv_ln_g, conv_ln_b, conv_w_out, final_norm_g, loss_target, m_norm_g, m_fox_w_in, m_fox_b_f, m_fox_w_out, m_conv_w_in, m_conv_b_in, m_conv_dw, m_conv_dw_b, m_conv_ln_g, m_conv_ln_b, m_conv_w_out, m_final_norm_g, v_norm_g, v_fox_w_in, v_fox_b_f, v_fox_w_out, v_conv_w_in, v_conv_b_in, v_conv_dw, v_conv_dw_b, v_conv_ln_g, v_conv_ln_b, v_conv_w_out, v_final_norm_g):
    S, D = x.shape[1], x.shape[2]
    H = fox_b_f.shape[1]
    assert D == H * HEAD_DIM, "one head must be one lane tile"
    W = C = D
    NL = fox_w_in.shape[0]
    Dq = D // N_CHIPS
    NA = fox_w_in.shape[2]
    scale = HEAD_DIM ** -0.5
    chip = 2 * lax.axis_index("x") + lax.axis_index("y")
    core = lax.axis_index("c")
    cidx = core.astype(jnp.int32).reshape((1,))
    chip1 = chip.astype(jnp.int32).reshape((1,))
    sel = jnp.stack([chip, core]).astype(jnp.int32)

    small_pack = jnp.concatenate([
        conv_b_in.reshape((NL * 3, Dq)), conv_dw.reshape((NL * CONV_K, Dq)), conv_dw_b, conv_ln_g, conv_ln_b,
        jnp.zeros((PACK_ROWS - NL * (3 + CONV_K + 3), Dq), F32)], axis=0)
    small_slots = lax.dynamic_update_slice(jnp.zeros((N_CHIPS, PACK_ROWS, Dq), F32), small_pack[None], (chip, 0, 0))
    ga, gb, gc, gd, gsmall = gather_weights(
        [cast_into_slot(fox_w_in, chip1, "cast_fox_w_in"), cast_into_slot(fox_w_out, chip1, "cast_fox_w_out"),
         cast_into_slot(conv_w_in, chip1, "cast_conv_w_in"), cast_into_slot(conv_w_out, chip1, "cast_conv_w_out")],
        small_slots)
    wfull = jnp.transpose(ga, (0, 2, 1, 3)).reshape((NL, D, N_CHIPS * NA))
    wq, wk, wv, wg, wf = (wfull[:, :, 0:W], wfull[:, :, W:2 * W], wfull[:, :, 2 * W:3 * W],
                          wfull[:, :, 3 * W:4 * W], wfull[:, :, 4 * W:])
    w1 = jnp.concatenate([wq, wg], axis=-1).reshape((NL, 1, D, 2 * W))
    w2 = jnp.stack([wk.reshape((NL, D, H, HEAD_DIM)), wv.reshape((NL, D, H, HEAD_DIM))], axis=3)
    w2 = w2.reshape((NL, 1, D, 2 * W))
    wfp = _pad_cols(wf, 128).reshape((NL, 1, D, 128))
    wo_fox = gb.reshape((NL, 1, W, D))
    wo_conv = gd.reshape((NL, 1, C, D))
    wc = gc
    b_in = gsmall[:, 0:3 * NL, :].reshape((N_CHIPS, NL, 3 * Dq)).transpose((1, 0, 2)).reshape((NL, 3 * C))
    dwt = gsmall[:, 3 * NL:3 * NL + CONV_K * NL, :].reshape((N_CHIPS, NL, CONV_K, Dq))
    dwt = dwt.transpose((1, 2, 0, 3)).reshape((NL, CONV_K, C))
    r0 = (3 + CONV_K) * NL
    vecs = gsmall[:, r0:r0 + 3 * NL, :].reshape((N_CHIPS, 3, NL, Dq)).transpose((1, 2, 0, 3)).reshape((3, NL, C))
    dwb, lng, lnb = vecs[0], vecs[1], vecs[2]
    bfp = _pad_cols(fox_b_f, 128)

    h = x.reshape((S, D))
    tgt = loss_target.reshape((S, D))
    saved = []
    n_layers = norm_g.shape[0]
    for i in range(n_layers):
        j = i // 2
        g_i = norm_g[i:i + 1]
        hn = rms_fwd(h, g_i, f"rms_fwd_{i}")
        if i % 2 == 0:
            p1 = mm_nn(hn, w1, j, 1, out_dtype=BF16, name=f"fox_proj_qg_{i}")
            p2 = mm_nn(hn, w2, j, 1, out_dtype=BF16, name=f"fox_proj_kv_{i}")
            pf = mm_nn(hn, wfp, j, 1, out_dtype=F32, name=f"fox_proj_f_{i}")
            qa, ka = fox_decay(pf, bfp[j:j + 1], H, scale, f"fox_decay_{i}")
            o, yv, qb = attn_fwd(p1, p2, qa, ka, H, scale, f"attn_fwd_{i}")
            h_new = mm_nt_res(yv, wo_fox, j, h, f"fox_out_{i}")
            saved.append((h, hn, p1, p2, pf, ka, o, yv, qb))
        else:
            proj = mm_nn(hn, wc, j, N_CHIPS, out_dtype=BF16, name=f"conv_proj_{i}", bias=b_in[j:j + 1])
            u2, yv = conv_fwd(proj, dwt[j], dwb[j:j + 1], lng[j:j + 1], lnb[j:j + 1], f"conv_fwd_{i}")
            h_new = mm_nt_res(yv, wo_conv, j, h, f"conv_out_{i}")
            saved.append((h, hn, proj, u2, yv))
        h = h_new

    dh, d_gf, loss_part = loss_head(h, final_norm_g.reshape((1, D)), tgt, "loss_head")

    d_norm = [None] * n_layers
    d_fox_in = [None] * NL
    d_fox_b = [None] * NL
    d_conv_small = [None] * NL
    g_b = g_c = g_d = None
    for i in reversed(range(n_layers)):
        j = i // 2
        g_i = norm_g[i:i + 1]
        if i % 2 == 0:
            h_in, hn, p1, p2, pf, ka, o, yv, qb = saved[i]
            g_b = mm_tn(yv, dh, 1, name=f"fox_out_dw_{i}", layers=NL, lidx=j, prev=g_b)
            dy = mm_nn_t(dh, wo_fox, j, f"fox_out_dx_{i}")
            do, dp1, da = attn_bwd_prep(dy, o, p1, H, f"attn_bwd_prep_{i}")
            dp1, dp2, dc = attn_bwd(p1, qb, do, da, p2, ka, dp1, H, scale, f"attn_bwd_{i}")
            dz, dbf = fox_decay_bwd(dc, pf, bfp[j:j + 1], H, f"fox_decay_bwd_{i}")
            d_fox_b[j] = dbf
            dw1 = mm_tn(hn, dp1, 1, name=f"fox_dw_qg_{i}")[0, 0]
            dw2 = mm_tn(hn, dp2, 1, name=f"fox_dw_kv_{i}")[0, 0]
            dwf = mm_tn(hn, dz, 1, name=f"fox_dw_f_{i}")[0, 0]
            dw2 = dw2.reshape((D, H, 2, HEAD_DIM))
            d_fox_in[j] = jnp.concatenate([dw1[:, :W], dw2[:, :, 0].reshape((D, W)), dw2[:, :, 1].reshape((D, W)),
                                           dw1[:, W:], dwf[:, :H]], axis=-1)
            dhn = mm_nt(dp1, w1, j, 1, name=f"fox_dx_qg_{i}")
            dhn = mm_nt(dp2, w2, j, 1, name=f"fox_dx_kv_{i}", res=dhn)
            dhn = mm_nt(dz, wfp, j, 1, name=f"fox_dx_f_{i}", res=dhn)
        else:
            h_in, hn, proj, u2, yv = saved[i]
            g_d = mm_tn(yv, dh, 1, name=f"conv_out_dw_{i}", layers=NL, lidx=j, prev=g_d)
            dy = mm_nn_t(dh, wo_conv, j, f"conv_out_dx_{i}")
            du2, dproj, sm1 = conv_bwd_norm(dy, proj, u2, lng[j:j + 1], lnb[j:j + 1], f"conv_bwd_norm_{i}")
            dproj, sm2 = conv_bwd_taps(du2, proj, dwt[j], dproj, f"conv_bwd_taps_{i}")
            d_conv_small[j] = (sm1, sm2)
            g_c = mm_tn(hn, dproj, N_CHIPS, name=f"conv_dw_in_{i}", layers=NL, lidx=j, prev=g_c)
            dhn = mm_nt(dproj, wc, j, N_CHIPS, name=f"conv_dx_{i}")
        dh, d_norm[i] = rms_bwd(dhn, h_in, g_i, dh, f"rms_bwd_{i}")

    g_a = jnp.stack([d.reshape((D, N_CHIPS, NA)).transpose((1, 0, 2)) for d in d_fox_in])
    big = [g_a, g_b.reshape((NL, N_CHIPS, Dq, D)), g_c, g_d.reshape((NL, N_CHIPS, Dq, D))]
    big = [g.reshape(g.shape[:2] + (2, g.shape[2] // 2, g.shape[3])) for g in big]
    rcv = swap_halves(big)
    parts = [pair_sum(g, r, cidx, f"pair_sum_{n}") for n, (g, r) in enumerate(zip(big, rcv))]

    zrow = jnp.zeros((1, D), F32)
    rows = list(d_norm) + [d_gf]
    rows += [_pad_cols(d_fox_b[l][:, :H], D) for l in range(NL)]
    rows += [_pad_cols(loss_part[:, :1], D)]
    for l in range(NL):
        sm1, sm2 = d_conv_small[l]
        rows += [sm2[32:33], sm2[33:34], sm1[3:4]]
    for l in range(NL):
        rows += [d_conv_small[l][1][0:CONV_K]]
    rows += [d_conv_small[l][0][2:3] for l in range(NL)]
    rows += [d_conv_small[l][0][0:1] for l in range(NL)]
    rows += [d_conv_small[l][0][1:2] for l in range(NL)]
    n_rows = sum(r.shape[0] for r in rows)
    rows += [zrow] * (SMALL_ROWS - n_rows)
    small = jnp.concatenate(rows, axis=0)

    dev = 4 * lax.axis_index("x") + 2 * lax.axis_index("y") + core
    small_slots = lax.dynamic_update_slice(jnp.zeros((N_DEV, SMALL_ROWS, D), F32), small[None], (dev, 0, 0))
    got = exchange_partials(parts, small_slots)
    halves = [chip_sum(p, r, sel, f"chip_sum_{n}") for n, (p, r) in enumerate(zip(parts, got[:4]))]
    tot = dev_sum(got[4], "dev_sum")
    full = join_halves(halves)
    grad_fox_w_in = full[0].reshape(fox_w_in.shape)
    grad_fox_w_out = full[1].reshape(fox_w_out.shape)
    grad_conv_w_in = full[2].reshape(conv_w_in.shape)
    grad_conv_w_out = full[3].reshape(conv_w_out.shape)

    def mine(v):
        return lax.dynamic_slice_in_dim(v, chip * Dq, Dq, axis=v.ndim - 1)

    r = n_layers
    grad_norm_g = tot[0:r]
    grad_final = tot[r]
    grad_fox_b_f = tot[r + 1:r + 1 + NL, :H]
    loss = tot[r + 1 + NL, 0]
    r = r + 2 + NL
    gb_full = tot[r:r + 3 * NL].reshape((NL, 3 * C))
    grad_conv_b_in = lax.dynamic_slice_in_dim(gb_full, chip * 3 * Dq, 3 * Dq, axis=1)
    r += 3 * NL
    grad_conv_dw = mine(tot[r:r + CONV_K * NL].reshape((NL, CONV_K, C)))
    r += CONV_K * NL
    grad_conv_dw_b = mine(tot[r:r + NL])
    grad_conv_ln_g = mine(tot[r + NL:r + 2 * NL])
    grad_conv_ln_b = mine(tot[r + 2 * NL:r + 3 * NL])

    grads = [grad_norm_g, grad_fox_w_in, grad_fox_b_f, grad_fox_w_out, grad_conv_w_in, grad_conv_b_in,
             grad_conv_dw, grad_conv_dw_b, grad_conv_ln_g, grad_conv_ln_b, grad_conv_w_out, grad_final]
    ws = [norm_g, fox_w_in, fox_b_f, fox_w_out, conv_w_in, conv_b_in, conv_dw, conv_dw_b, conv_ln_g, conv_ln_b,
          conv_w_out, final_norm_g]
    ms = [m_norm_g, m_fox_w_in, m_fox_b_f, m_fox_w_out, m_conv_w_in, m_conv_b_in, m_conv_dw, m_conv_dw_b,
          m_conv_ln_g, m_conv_ln_b, m_conv_w_out, m_final_norm_g]
    vs = [v_norm_g, v_fox_w_in, v_fox_b_f, v_fox_w_out, v_conv_w_in, v_conv_b_in, v_conv_dw, v_conv_dw_b,
          v_conv_ln_g, v_conv_ln_b, v_conv_w_out, v_final_norm_g]
    deltas, new_ms, new_vs = [], [], []
    for n, (w_, g_, m_, v_) in enumerate(zip(ws, grads, ms, vs)):
        d_, nm_, nv_ = adamw(w_, g_, m_, v_, f"adamw_{n}")
        deltas.append(d_)
        new_ms.append(nm_)
        new_vs.append(nv_)
    grad_x = dh.reshape(x.shape)
    return (loss, grad_x, *grads, *deltas, *new_ms, *new_vs)


def mm_nt_res(y, wo, lidx, h, name):
    M, K = y.shape
    N = wo.shape[-1]
    tm = _tile(M, 1024)
    tn = _tile(N, 1024)
    grid = (M // tm, N // tn, 1)
    return _matmul(
        y, wo, contract="nn", grid=grid, name=name,
        a_spec=pl.BlockSpec((tm, K), lambda i, j, k: (i, 0)),
        b_spec=pl.BlockSpec((None, None, K, tn), lambda i, j, k: (lidx, 0, 0, j)),
        o_spec=pl.BlockSpec((tm, tn), lambda i, j, k: (i, j)),
        out_shape=jax.ShapeDtypeStruct((M, N), F32), acc_shape=(tm, tn),
        res=h, res_spec=pl.BlockSpec((tm, tn), lambda i, j, k: (i, j)),
    )


def mm_nn_t(dh, wo, lidx, name):
    M, K = dh.shape
    N = wo.shape[-2]
    tm = _tile(M, 512)
    tn = _tile(N, 1024)
    grid = (M // tm, N // tn, 1)
    return _matmul(
        dh, wo, contract="nt", grid=grid, name=name,
        a_spec=pl.BlockSpec((tm, K), lambda i, j, k: (i, 0)),
        b_spec=pl.BlockSpec((None, None, tn, K), lambda i, j, k: (lidx, 0, j, 0)),
        o_spec=pl.BlockSpec((tm, tn), lambda i, j, k: (i, j)),
        out_shape=jax.ShapeDtypeStruct((M, N), F32), acc_shape=(tm, tn),
    )
```

```python
import jax
import jax.numpy as jnp
from jax import lax
from jax.experimental import pallas as pl
from jax.experimental.pallas import tpu as pltpu

F32 = jnp.float32
BF16 = jnp.bfloat16
MESH = pl.DeviceIdType.MESH

RMS_EPS = 1e-6
LN_EPS = 1e-5
CONV_K = 31
HALO = 32
HEAD_DIM = 128
ADAM_LR = 0.001
ADAM_B1 = 0.9
ADAM_B2 = 0.999
ADAM_EPS = 1e-08
ADAM_WD = 0.01
ADAM_STEP = 10
N_CHIPS = 4
N_DEV = 8
VMEM_LIMIT = 56 * 1024 * 1024
NEG_BIG = -1e30
SMALL_ROWS = 88
PACK_ROWS = 80


def _pcall(body, **kw):
    return pl.pallas_call(body, **kw)


def _cparams(sem=None):
    return pltpu.CompilerParams(dimension_semantics=sem, vmem_limit_bytes=VMEM_LIMIT)


def _tile(n, cap, mult=128):
    if n <= cap:
        return n
    t = (cap // mult) * mult
    while t >= mult:
        if n % t == 0:
            return t
        t -= mult
    raise ValueError(f"no tile for {n} under {cap}")


def _sigmoid(x):
    return 1.0 / (1.0 + jnp.exp(-x))


def _split3(x):
    hi = x.astype(BF16).astype(F32)
    r = x - hi
    mid = r.astype(BF16).astype(F32)
    lo = (r - mid).astype(BF16).astype(F32)
    return hi, mid, lo


_DN = {
    "nn": (((1,), (0,)), ((), ())),
    "nt": (((1,), (1,)), ((), ())),
    "tn": (((0,), (0,)), ((), ())),
}


def _matmul(a, b, *, contract, grid, a_spec, b_spec, o_spec, out_shape, acc_shape, name,
            bias=None, bias_spec=None, res=None, res_spec=None, alias_res=False, prev=None):
    nk = grid[2]
    has_bias = bias is not None
    has_res = res is not None
    assert not (alias_res and prev is not None)

    def body(*refs):
        a_ref, b_ref = refs[0], refs[1]
        pos = 2
        bias_ref = res_ref = None
        if has_bias:
            bias_ref = refs[pos]
            pos += 1
        if has_res:
            res_ref = refs[pos]
            pos += 1
        if prev is not None:
            pos += 1
        o_ref = refs[pos]
        acc_ref = refs[pos + 1] if nk > 1 else None
        p = lax.dot_general(a_ref[...].astype(BF16), b_ref[...].astype(BF16), _DN[contract],
                            preferred_element_type=F32)

        def finish(v):
            if has_bias:
                v = v + bias_ref[...]
            if has_res:
                v = res_ref[...] + v
            o_ref[...] = v.astype(o_ref.dtype)

        if nk == 1:
            finish(p)
        else:
            k = pl.program_id(2)

            @pl.when(k == 0)
            def _():
                acc_ref[...] = p

            @pl.when(k > 0)
            def _():
                acc_ref[...] += p

            @pl.when(k == nk - 1)
            def _():
                finish(acc_ref[...])

    ins = [a, b]
    specs = [a_spec, b_spec]
    if has_bias:
        ins.append(bias)
        specs.append(bias_spec)
    if has_res:
        ins.append(res)
        specs.append(res_spec)
    aliases = {len(ins) - 1: 0} if (has_res and alias_res) else {}
    if prev is not None:
        ins.append(prev)
        specs.append(pl.BlockSpec(memory_space=pl.ANY))
        aliases = {len(ins) - 1: 0}
    return _pcall(
        body, name=name, grid=grid, in_specs=specs, out_specs=o_spec, out_shape=out_shape,
        scratch_shapes=[pltpu.VMEM(acc_shape, F32)] if nk > 1 else [],
        input_output_aliases=aliases,
        compiler_params=_cparams(("parallel", "parallel", "arbitrary")),
    )(*ins)


def mm_nn(a, w, lidx, n_slots, *, out_dtype, name, bias=None):
    M, K = a.shape
    Ns = w.shape[-1]
    tm = _tile(M, 1024)
    tn = _tile(Ns, 1024)
    per = Ns // tn
    grid = (M // tm, n_slots * per, 1)
    return _matmul(
        a, w, contract="nn", grid=grid, name=name,
        a_spec=pl.BlockSpec((tm, K), lambda i, j, k: (i, 0)),
        b_spec=pl.BlockSpec((None, None, K, tn), lambda i, j, k: (lidx, j // per, 0, j % per)),
        o_spec=pl.BlockSpec((tm, tn), lambda i, j, k: (i, j)),
        out_shape=jax.ShapeDtypeStruct((M, n_slots * Ns), out_dtype), acc_shape=(tm, tn),
        bias=bias, bias_spec=None if bias is None else pl.BlockSpec((1, tn), lambda i, j, k: (0, j)),
    )


def mm_nt(a, w, lidx, n_slots, *, name, res=None):
    M = a.shape[0]
    N, Ns = w.shape[-2], w.shape[-1]
    tm = _tile(M, 1024)
    tn = _tile(N, 1024)
    tk = _tile(Ns, 2048)
    per = Ns // tk
    grid = (M // tm, N // tn, n_slots * per)
    return _matmul(
        a, w, contract="nt", grid=grid, name=name,
        a_spec=pl.BlockSpec((tm, tk), lambda i, j, k: (i, k)),
        b_spec=pl.BlockSpec((None, None, tn, tk), lambda i, j, k: (lidx, k // per, j, k % per)),
        o_spec=pl.BlockSpec((tm, tn), lambda i, j, k: (i, j)),
        out_shape=jax.ShapeDtypeStruct((M, N), F32), acc_shape=(tm, tn),
        res=res, res_spec=None if res is None else pl.BlockSpec((tm, tn), lambda i, j, k: (i, j)),
        alias_res=res is not None,
    )


def mm_tn(a, b, n_slots, *, name, layers=1, lidx=0, prev=None):
    S, M = a.shape
    Ns = b.shape[1] // n_slots
    tm = _tile(M, 1024)
    tn = _tile(Ns, 1024)
    tk = _tile(S, 2048)
    per = Ns // tn
    grid = (M // tm, n_slots * per, S // tk)
    return _matmul(
        a, b, contract="tn", grid=grid, name=name,
        a_spec=pl.BlockSpec((tk, tm), lambda i, j, k: (k, i)),
        b_spec=pl.BlockSpec((tk, tn), lambda i, j, k: (k, j)),
        o_spec=pl.BlockSpec((None, None, tm, tn), lambda i, j, k: (lidx, j // per, i, j % per)),
        out_shape=jax.ShapeDtypeStruct((layers, n_slots, M, Ns), F32), acc_shape=(tm, tn), prev=prev,
    )


def rms_fwd(h, g, name):
    S, D = h.shape
    tm = _tile(S, 256, 8)

    def body(h_ref, g_ref, o_ref):
        x = h_ref[...]
        r = lax.rsqrt(jnp.mean(x * x, axis=-1, keepdims=True) + RMS_EPS)
        o_ref[...] = (x * r * g_ref[...]).astype(BF16)

    return _pcall(
        body, name=name, grid=(S // tm,),
        in_specs=[pl.BlockSpec((tm, D), lambda i: (i, 0)), pl.BlockSpec((1, D), lambda i: (0, 0))],
        out_specs=pl.BlockSpec((tm, D), lambda i: (i, 0)),
        out_shape=jax.ShapeDtypeStruct((S, D), BF16),
        compiler_params=_cparams(("parallel",)),
    )(h, g)


def _rms_bwd_rows(x, g, dy):
    d = x.shape[-1]
    r = lax.rsqrt(jnp.mean(x * x, axis=-1, keepdims=True) + RMS_EPS)
    gd = dy * g
    dx = r * gd - x * ((r * r * r) * (jnp.sum(x * gd, axis=-1, keepdims=True) / d))
    return dx, dy * x * r


def rms_bwd(dhn, h, g, dres, name):
    S, D = h.shape
    tm = _tile(S, 256, 8)

    def body(dhn_ref, h_ref, g_ref, dres_ref, dh_ref, dg_ref):
        dx, dgr = _rms_bwd_rows(h_ref[...], g_ref[...], dhn_ref[...])
        dh_ref[...] = dres_ref[...] + dx

        @pl.when(pl.program_id(0) == 0)
        def _():
            dg_ref[...] = jnp.zeros_like(dg_ref)

        dg_ref[...] += jnp.sum(dgr, axis=0, keepdims=True)

    row = pl.BlockSpec((tm, D), lambda i: (i, 0))
    vec = pl.BlockSpec((1, D), lambda i: (0, 0))
    return _pcall(
        body, name=name, grid=(S // tm,),
        in_specs=[row, row, vec, row], out_specs=[row, vec],
        out_shape=[jax.ShapeDtypeStruct((S, D), F32), jax.ShapeDtypeStruct((1, D), F32)],
        input_output_aliases={3: 0},
        compiler_params=_cparams(("arbitrary",)),
    )(dhn, h, g, dres)


def loss_head(h, g, target, name):
    S, D = h.shape
    tm = _tile(S, 256, 8)

    def body(h_ref, g_ref, t_ref, dh_ref, dg_ref, loss_ref):
        x = h_ref[...]
        gg = g_ref[...]
        r = lax.rsqrt(jnp.mean(x * x, axis=-1, keepdims=True) + RMS_EPS)
        y = x * r * gg
        e = y - t_ref[...]
        part = 0.5 * jnp.sum(jnp.mean(e * e, axis=-1, keepdims=True), axis=0, keepdims=True)
        dy = e * (1.0 / D)
        dx, dgr = _rms_bwd_rows(x, gg, dy)
        dh_ref[...] = dx

        @pl.when(pl.program_id(0) == 0)
        def _():
            dg_ref[...] = jnp.zeros_like(dg_ref)
            loss_ref[...] = jnp.zeros_like(loss_ref)

        dg_ref[...] += jnp.sum(dgr, axis=0, keepdims=True)
        loss_ref[...] += jnp.broadcast_to(part, loss_ref.shape)

    row = pl.BlockSpec((tm, D), lambda i: (i, 0))
    vec = pl.BlockSpec((1, D), lambda i: (0, 0))
    return _pcall(
        body, name=name, grid=(S // tm,),
        in_specs=[row, vec, row],
        out_specs=[row, vec, pl.BlockSpec((1, 128), lambda i: (0, 0))],
        out_shape=[jax.ShapeDtypeStruct((S, D), F32), jax.ShapeDtypeStruct((1, D), F32),
                   jax.ShapeDtypeStruct((1, 128), F32)],
        compiler_params=_cparams(("arbitrary",)),
    )(h, g, target)


def fox_decay(pf, bf, n_heads, scale, name):
    S = pf.shape[0]
    tm = _tile(S, 256, 8)
    inv_scale = 1.0 / scale

    def body(pf_ref, bf_ref, qa_ref, ka_ref, carry_ref):
        @pl.when(pl.program_id(0) == 0)
        def _():
            carry_ref[...] = jnp.zeros_like(carry_ref)

        z = pf_ref[...] + bf_ref[...]
        logf = jnp.minimum(z, 0.0) - jnp.log(1.0 + jnp.exp(-jnp.abs(z)))
        row = lax.broadcasted_iota(jnp.int32, (tm, tm), 0)
        col = lax.broadcasted_iota(jnp.int32, (tm, tm), 1)
        tri = (row >= col).astype(F32)
        c = jnp.dot(tri, logf, precision=lax.Precision.HIGHEST, preferred_element_type=F32) + carry_ref[...]
        carry_ref[...] = c[tm - 1:tm, :]
        lane = lax.broadcasted_iota(jnp.int32, (tm, HEAD_DIM), 1)
        for hh in range(n_heads):
            hi, mid, lo = _split3(c[:, hh:hh + 1] * inv_scale)
            qa = jnp.where(lane == 0, hi, jnp.where(lane == 1, mid, jnp.where(lane == 2, lo,
                 jnp.where(lane < 6, 1.0, 0.0))))
            ka = jnp.where(lane < 3, 1.0, jnp.where(lane == 3, -hi, jnp.where(lane == 4, -mid,
                 jnp.where(lane == 5, -lo, jnp.where(lane < 9, 1.0, 0.0)))))
            qa_ref[:, hh * HEAD_DIM:(hh + 1) * HEAD_DIM] = qa.astype(BF16)
            ka_ref[:, hh * HEAD_DIM:(hh + 1) * HEAD_DIM] = ka.astype(BF16)

    wide = pl.BlockSpec((tm, n_heads * HEAD_DIM), lambda i: (i, 0))
    return _pcall(
        body, name=name, grid=(S // tm,),
        in_specs=[pl.BlockSpec((tm, 128), lambda i: (i, 0)), pl.BlockSpec((1, 128), lambda i: (0, 0))],
        out_specs=[wide, wide],
        out_shape=[jax.ShapeDtypeStruct((S, n_heads * HEAD_DIM), BF16)] * 2,
        scratch_shapes=[pltpu.VMEM((1, 128), F32)],
        compiler_params=_cparams(("arbitrary",)),
    )(pf, bf)


def _attn_tile(S):
    return 512 if S % 512 == 0 and S >= 2048 else 128


def attn_fwd(p1, p2, qa, ka, n_heads, scale, name, gather=()):
    S = p1.shape[0]
    W = n_heads * HEAD_DIM
    t = _attn_tile(S)
    nq = S // t
    ng = len(gather)

    def body(q_ref, g_ref, qa_ref, k_ref, v_ref, ka_ref, *rest):
        o_ref, y_ref, qb_ref = rest[ng:ng + 3]
        bufs = rest[ng + 3:2 * ng + 3]
        mp_ref, qq_ref, acc_ref = rest[2 * ng + 3:2 * ng + 6]
        i = pl.program_id(1)
        if ng:
            ssem, rsem = rest[2 * ng + 6:]

            @pl.when((pl.program_id(0) == 0) & (i == 0))
            def _():
                for cp in _gather_direct_copies(bufs, ssem, rsem)[0]:
                    cp.start()
        lane = lax.broadcasted_iota(jnp.int32, (t, HEAD_DIM), 1)
        qa = qa_ref[...].astype(F32)

        def tile_with(neg_stat):
            hi, mid, lo = _split3(neg_stat)
            return jnp.where(lane == 6, hi, jnp.where(lane == 7, mid, jnp.where(lane == 8, lo, qa))).astype(BF16)

        def keys(start, width):
            rows = pl.ds(pl.multiple_of(start, t), width)
            return rows, jnp.concatenate([k_ref[rows, :], ka_ref[rows, :]], axis=1)

        def causal():
            r = lax.broadcasted_iota(jnp.int32, (t, t), 0)
            c = lax.broadcasted_iota(jnp.int32, (t, t), 1)
            return r >= c

        def over_keys(block):
            def wide(jj, carry):
                block(jj * (4 * t), 4 * t, False)
                return carry

            lax.fori_loop(0, i // 4, wide, 0)
            done = (i // 4) * 4

            @pl.when((i & 2) != 0)
            def _():
                block(done * t, 2 * t, False)

            @pl.when((i & 1) != 0)
            def _():
                block((done + (i & 2)) * t, t, False)

            block(i * t, t, True)

        qq_ref[:, :HEAD_DIM] = q_ref[...]
        qq_ref[:, HEAD_DIM:] = qa_ref[...]
        mp_ref[...] = jnp.full(mp_ref.shape, NEG_BIG, F32)

        def max_block(start, width, masked):
            _, kk = keys(start, width)
            s = lax.dot_general(qq_ref[...], kk, _DN["nt"], preferred_element_type=F32)
            if masked:
                s = jnp.where(causal(), s, NEG_BIG)
            part = s[:, 0:HEAD_DIM]
            for a in range(1, width // HEAD_DIM):
                part = jnp.maximum(part, s[:, a * HEAD_DIM:(a + 1) * HEAD_DIM])
            mp_ref[...] = jnp.maximum(mp_ref[...], part)

        over_keys(max_block)
        m = jnp.max(mp_ref[...], axis=1, keepdims=True)
        qq_ref[:, HEAD_DIM:] = tile_with(-m)
        acc_ref[...] = jnp.zeros_like(acc_ref)

        def sum_block(start, width, masked):
            rows, kk = keys(start, width)
            a = lax.dot_general(qq_ref[...], kk, _DN["nt"], preferred_element_type=F32)
            p = jnp.exp(scale * a)
            if masked:
                p = jnp.where(causal(), p, 0.0)
            ones0 = jnp.where(lax.broadcasted_iota(jnp.int32, (width, HEAD_DIM), 1) == 0, 1.0, 0.0).astype(BF16)
            vv = jnp.concatenate([v_ref[rows, :], ones0], axis=1)
            acc_ref[...] += jnp.dot(p.astype(BF16), vv, preferred_element_type=F32)

        over_keys(sum_block)

        l = acc_ref[:, HEAD_DIM:HEAD_DIM + 1]
        o = acc_ref[:, :HEAD_DIM] / l
        gate = g_ref[...].astype(F32)
        o_ref[...] = o.astype(BF16)
        y_ref[...] = (o * (gate * _sigmoid(gate))).astype(BF16)
        qb_ref[...] = tile_with(-(m + jnp.log(l) * (1.0 / scale)))

        if ng:
            @pl.when((pl.program_id(0) == n_heads - 1) & (i == nq - 1))
            def _():
                sends, lands = _gather_direct_copies(bufs, ssem, rsem)
                for cp in lands:
                    cp.wait_recv()
                for cp in sends:
                    cp.wait_send()

    H = n_heads
    qtile = lambda off: pl.BlockSpec((t, HEAD_DIM), lambda h, i: (i, off + h))
    full = lambda fn: pl.BlockSpec((S, HEAD_DIM), fn)
    hbm = pl.BlockSpec(memory_space=pl.ANY)
    sems = [pltpu.SemaphoreType.DMA((3 * ng,)), pltpu.SemaphoreType.DMA((3 * ng,))] if ng else []
    outs = _pcall(
        body, name=name, grid=(H, nq),
        in_specs=[qtile(0), qtile(H), qtile(0),
                  full(lambda h, i: (0, 2 * h)), full(lambda h, i: (0, 2 * h + 1)), full(lambda h, i: (0, h))]
                 + [hbm] * ng,
        out_specs=[qtile(0), qtile(0), qtile(0)] + [hbm] * ng,
        out_shape=[jax.ShapeDtypeStruct((S, W), BF16)] * 3 + [jax.ShapeDtypeStruct(b.shape, b.dtype) for b in gather],
        scratch_shapes=[pltpu.VMEM((t, HEAD_DIM), F32), pltpu.VMEM((t, 2 * HEAD_DIM), BF16),
                        pltpu.VMEM((t, 2 * HEAD_DIM), F32)] + sems,
        input_output_aliases={6 + n: 3 + n for n in range(ng)},
        compiler_params=_cparams(("arbitrary", "arbitrary") if ng else ("parallel", "arbitrary")),
    )(p1, p1, qa, p2, p2, ka, *gather)
    return outs[0], outs[1], outs[2], list(outs[3:])


def attn_bwd_prep(dy, o, p1, n_heads, name):
    S, W = dy.shape
    tm = _tile(S, 256, 8)
    H = n_heads

    def body(dy_ref, o_ref, g_ref, do_ref, dg_ref, da_ref):
        lane = lax.broadcasted_iota(jnp.int32, (tm, HEAD_DIM), 1)
        for hh in range(H):
            cs = slice(hh * HEAD_DIM, (hh + 1) * HEAD_DIM)
            g = g_ref[:, cs].astype(F32)
            oo = o_ref[:, cs].astype(F32)
            dyv = dy_ref[:, cs]
            sg = _sigmoid(g)
            do = dyv * (g * sg)
            do_ref[:, cs] = do.astype(BF16)
            dg_ref[:, cs] = (dyv * oo * (sg * (1.0 + g * (1.0 - sg)))).astype(BF16)
            hi, mid, lo = _split3(-jnp.sum(do * oo, axis=1, keepdims=True))
            da = jnp.where(lane == 0, hi, jnp.where(lane == 1, mid, jnp.where(lane == 2, lo, 0.0)))
            da_ref[:, cs] = da.astype(BF16)

    row = lambda blk: pl.BlockSpec((tm, W), lambda i: (i, blk))
    return _pcall(
        body, name=name, grid=(S // tm,),
        in_specs=[row(0), row(0), row(1)],
        out_specs=[row(0), row(1), row(0)],
        out_shape=[jax.ShapeDtypeStruct((S, W), BF16), jax.ShapeDtypeStruct((S, 2 * W), BF16),
                   jax.ShapeDtypeStruct((S, W), BF16)],
        compiler_params=_cparams(("parallel",)),
    )(dy, o, p1)


def attn_bwd(p1, qb, do, da, p2, ka, dp1, n_heads, scale, name, exchange=()):
    S = p1.shape[0]
    W = n_heads * HEAD_DIM
    t = _attn_tile(S)
    nb = S // t
    H = n_heads
    ne = len(exchange)

    def body(q_ref, qb_ref, do_ref, da_ref, k_ref, v_ref, ka_ref, dp1_in, *rest):
        del dp1_in
        parts = rest[:ne]
        dq_ref, dkv_ref, dc_ref = rest[ne:ne + 3]
        got = rest[ne + 3:2 * ne + 3]
        dq_acc, dk_acc, dv_acc = rest[2 * ne + 3:2 * ne + 6]
        h = pl.program_id(0)
        j = pl.program_id(1)
        if ne:
            ssem, rsem = rest[2 * ne + 6:]

            @pl.when((h == 0) & (j == 0))
            def _():
                for cp in _exchange_copies(parts, got, ssem, rsem)[0]:
                    cp.start()

        @pl.when(j == 0)
        def _():
            dq_acc[...] = jnp.zeros_like(dq_acc)

        @pl.when((j == 0) & (h == 0))
        def _():
            dc_ref[...] = jnp.zeros_like(dc_ref)

        lane = lax.broadcasted_iota(jnp.int32, (t, HEAD_DIM), 1)
        ones3 = jnp.where(lane < 3, 1.0, 0.0).astype(BF16)
        kk = jnp.concatenate([k_ref[...], ka_ref[...]], axis=1)
        vv = jnp.concatenate([v_ref[...], ones3], axis=1)
        dk_acc[...] = jnp.zeros_like(dk_acc)
        dv_acc[...] = jnp.zeros_like(dv_acc)

        def block(start, width, masked):
            rows = pl.ds(pl.multiple_of(start, t), width)
            qq = jnp.concatenate([q_ref[rows, :], qb_ref[rows, :]], axis=1)
            dd = jnp.concatenate([do_ref[rows, :], da_ref[rows, :]], axis=1)
            a = lax.dot_general(qq, kk, _DN["nt"], preferred_element_type=F32)
            p = jnp.exp(scale * a)
            if masked:
                r = lax.broadcasted_iota(jnp.int32, (t, t), 0)
                c = lax.broadcasted_iota(jnp.int32, (t, t), 1)
                p = jnp.where(r >= c, p, 0.0)
            dpd = lax.dot_general(dd, vv, _DN["nt"], preferred_element_type=F32)
            ds = (p * dpd).astype(BF16)
            pb = p.astype(BF16)
            dv_acc[...] += lax.dot_general(pb, dd, _DN["tn"], preferred_element_type=F32)
            dk_acc[...] += lax.dot_general(ds, qq, _DN["tn"], preferred_element_type=F32)
            dq_acc[rows, :] += jnp.dot(ds, kk, preferred_element_type=F32)

        block(j * t, t, True)
        n_after = nb - 1 - j

        @pl.when((n_after & 1) != 0)
        def _():
            block((j + 1) * t, t, False)

        first = j + 1 + (n_after & 1)

        def loop_body(ii, carry):
            block((first + 2 * ii) * t, 2 * t, False)
            return carry

        lax.fori_loop(0, n_after // 2, loop_body, 0)

        dkv_ref[...] = jnp.concatenate([dk_acc[:, :HEAD_DIM] * scale, dv_acc[:, :HEAD_DIM]], axis=1).astype(BF16)
        colsum = dk_acc[:, HEAD_DIM + 3:HEAD_DIM + 4]
        krows = pl.ds(pl.multiple_of(j * t, t), t)
        dc_ref[krows, :] += jnp.where(lane == h, -colsum, 0.0)

        @pl.when(j == nb - 1)
        def _():
            dq_ref[...] = (dq_acc[:, :HEAD_DIM] * scale).astype(BF16)
            lane_s = lax.broadcasted_iota(jnp.int32, (S, HEAD_DIM), 1)
            dc_ref[...] += jnp.where(lane_s == h, dq_acc[:, HEAD_DIM:HEAD_DIM + 1], 0.0)

        if ne:
            @pl.when((h == H - 1) & (j == nb - 1))
            def _():
                sends, lands = _exchange_copies(parts, got, ssem, rsem)
                for cp in lands:
                    cp.wait_recv()
                for cp in sends:
                    cp.wait_send()

    full = lambda fn: pl.BlockSpec((S, HEAD_DIM), fn)
    ktile = lambda fn: pl.BlockSpec((t, HEAD_DIM), fn)
    hbm = pl.BlockSpec(memory_space=pl.ANY)
    sems = [pltpu.SemaphoreType.DMA((3 * ne,)), pltpu.SemaphoreType.DMA((3 * ne,))] if ne else []
    outs = _pcall(
        body, name=name, grid=(H, nb),
        in_specs=[full(lambda h, j: (0, h)), full(lambda h, j: (0, h)), full(lambda h, j: (0, h)),
                  full(lambda h, j: (0, h)),
                  ktile(lambda h, j: (j, 2 * h)), ktile(lambda h, j: (j, 2 * h + 1)), ktile(lambda h, j: (j, h)),
                  hbm] + [hbm] * ne,
        out_specs=[full(lambda h, j: (0, h)),
                   pl.BlockSpec((t, 2 * HEAD_DIM), lambda h, j: (j, h)),
                   pl.BlockSpec((S, 128), lambda h, j: (0, 0))] + [hbm] * ne,
        out_shape=[jax.ShapeDtypeStruct((S, 2 * W), BF16), jax.ShapeDtypeStruct((S, 2 * W), BF16),
                   jax.ShapeDtypeStruct((S, 128), F32)]
                  + [jax.ShapeDtypeStruct((p.shape[0], N_CHIPS - 1) + p.shape[2:], p.dtype) for p in exchange],
        scratch_shapes=[pltpu.VMEM((S, 2 * HEAD_DIM), F32), pltpu.VMEM((t, 2 * HEAD_DIM), F32),
                        pltpu.VMEM((t, 2 * HEAD_DIM), F32)] + sems,
        input_output_aliases={7: 0},
        compiler_params=_cparams(("arbitrary", "arbitrary")),
    )(p1, qb, do, da, p2, p2, ka, dp1, *exchange)
    return outs[0], outs[1], outs[2], list(outs[3:])


def fox_decay_bwd(dc, pf, bf, n_heads, name):
    S = dc.shape[0]
    tm = _tile(S, 256, 8)
    nb = S // tm

    def body(dc_ref, pf_ref, bf_ref, dz_ref, db_ref, carry_ref):
        @pl.when(pl.program_id(0) == 0)
        def _():
            carry_ref[...] = jnp.zeros_like(carry_ref)
            db_ref[...] = jnp.zeros_like(db_ref)

        row = lax.broadcasted_iota(jnp.int32, (tm, tm), 0)
        col = lax.broadcasted_iota(jnp.int32, (tm, tm), 1)
        tri = (row <= col).astype(F32)
        dlogf = jnp.dot(tri, dc_ref[...], precision=lax.Precision.HIGHEST, preferred_element_type=F32) + carry_ref[...]
        carry_ref[...] = dlogf[0:1, :]
        z = pf_ref[...] + bf_ref[...]
        lane = lax.broadcasted_iota(jnp.int32, (tm, 128), 1)
        dz = jnp.where(lane < n_heads, dlogf * _sigmoid(-z), 0.0)
        dz_ref[...] = dz.astype(BF16)
        db_ref[...] += jnp.sum(dz, axis=0, keepdims=True)

    rev = pl.BlockSpec((tm, 128), lambda i: (nb - 1 - i, 0))
    vec = pl.BlockSpec((1, 128), lambda i: (0, 0))
    return _pcall(
        body, name=name, grid=(nb,),
        in_specs=[rev, rev, vec], out_specs=[rev, vec],
        out_shape=[jax.ShapeDtypeStruct((S, 128), BF16), jax.ShapeDtypeStruct((1, 128), F32)],
        scratch_shapes=[pltpu.VMEM((1, 128), F32)],
        compiler_params=_cparams(("arbitrary",)),
    )(dc, pf, bf)


def _conv_tile(S):
    return _tile(S, 256, HALO)


def _fill_glu(ubuf, a_ref, b_ref, ah_ref, bh_ref, first, tm):
    uh = ah_ref[...].astype(F32) * _sigmoid(bh_ref[...].astype(F32))
    ubuf[0:HALO, :] = jnp.where(first, 0.0, uh)
    ubuf[HALO:HALO + tm, :] = a_ref[...].astype(F32) * _sigmoid(b_ref[...].astype(F32))


def conv_fwd(proj, dw, dwb, lng, lnb, name):
    S = proj.shape[0]
    C = proj.shape[1] // 3
    tm = _conv_tile(S)
    hb = tm // HALO
    nch = C // 128
    rb = _tile(tm, 128, 8)

    def body(a_ref, b_ref, g_ref, ah_ref, bh_ref, dw_ref, dwb_ref, lng_ref, lnb_ref, u2_ref, y_ref, ubuf, sh):
        i = pl.program_id(0)
        _fill_glu(ubuf, a_ref, b_ref, ah_ref, bh_ref, i == 0, tm)

        def chunk(cc, carry):
            cols = pl.ds(pl.multiple_of(cc * 128, 128), 128)
            for r0 in range(0, tm, rb):
                acc = jnp.broadcast_to(dwb_ref[:, cols], (rb, 128))
                for b in range(8):
                    taps = list(range(b, CONV_K, 8))
                    n = rb + 8 * (len(taps) - 1)
                    sh[0:n, :] = ubuf[pl.ds(HALO - (CONV_K - 1) + b + r0, n), cols]
                    for a, k in enumerate(taps):
                        acc = acc + dw_ref[k:k + 1, cols] * sh[8 * a:8 * a + rb, :]
                u2_ref[pl.ds(r0, rb), cols] = acc
            return carry

        lax.fori_loop(0, nch, chunk, 0)
        x = u2_ref[...]
        mu = jnp.mean(x, axis=-1, keepdims=True)
        xc = x - mu
        var = jnp.mean(xc * xc, axis=-1, keepdims=True)
        ln = xc * lax.rsqrt(var + LN_EPS) * lng_ref[...] + lnb_ref[...]
        gate = g_ref[...].astype(F32)
        y_ref[...] = ((ln * _sigmoid(ln)) * (gate * _sigmoid(gate))).astype(BF16)

    blk = lambda cb: pl.BlockSpec((tm, C), lambda i: (i, cb))
    halo = lambda cb: pl.BlockSpec((HALO, C), lambda i: (jnp.maximum(i * hb - 1, 0), cb))
    vec = pl.BlockSpec((1, C), lambda i: (0, 0))
    return _pcall(
        body, name=name, grid=(S // tm,),
        in_specs=[blk(0), blk(1), blk(2), halo(0), halo(1),
                  pl.BlockSpec((CONV_K, C), lambda i: (0, 0)), vec, vec, vec],
        out_specs=[blk(0), blk(0)],
        out_shape=[jax.ShapeDtypeStruct((S, C), F32), jax.ShapeDtypeStruct((S, C), BF16)],
        scratch_shapes=[pltpu.VMEM((HALO + tm, C), F32), pltpu.VMEM((rb + HALO, 128), F32)],
        compiler_params=_cparams(("parallel",)),
    )(proj, proj, proj, proj, proj, dw, dwb, lng, lnb)


def conv_bwd_norm(dy, proj, u2, lng, lnb, name):
    S, C = dy.shape
    tm = _tile(S, 256, 8)

    def body(dy_ref, g_ref, u2_ref, lng_ref, lnb_ref, du2_ref, dg_ref, sm_ref):
        x = u2_ref[...]
        mu = jnp.mean(x, axis=-1, keepdims=True)
        xc = x - mu
        var = jnp.mean(xc * xc, axis=-1, keepdims=True)
        rs = lax.rsqrt(var + LN_EPS)
        xhat = xc * rs
        gam = lng_ref[...]
        ln = xhat * gam + lnb_ref[...]
        sl = _sigmoid(ln)
        u3 = ln * sl
        gate = g_ref[...].astype(F32)
        sg = _sigmoid(gate)
        dyv = dy_ref[...]
        dgate = dyv * u3 * (sg * (1.0 + gate * (1.0 - sg)))
        dln = (dyv * (gate * sg)) * (sl * (1.0 + ln * (1.0 - sl)))
        dxh = dln * gam
        du2 = rs * (dxh - jnp.mean(dxh, axis=-1, keepdims=True)
                    - xhat * jnp.mean(dxh * xhat, axis=-1, keepdims=True))
        du2_ref[...] = du2
        dg_ref[...] = dgate.astype(BF16)

        @pl.when(pl.program_id(0) == 0)
        def _():
            sm_ref[...] = jnp.zeros_like(sm_ref)

        sm_ref[0:1, :] += jnp.sum(dln * xhat, axis=0, keepdims=True)
        sm_ref[1:2, :] += jnp.sum(dln, axis=0, keepdims=True)
        sm_ref[2:3, :] += jnp.sum(du2, axis=0, keepdims=True)
        sm_ref[3:4, :] += jnp.sum(dgate, axis=0, keepdims=True)

    blk = lambda cb: pl.BlockSpec((tm, C), lambda i: (i, cb))
    vec = pl.BlockSpec((1, C), lambda i: (0, 0))
    return _pcall(
        body, name=name, grid=(S // tm,),
        in_specs=[blk(0), blk(2), blk(0), vec, vec],
        out_specs=[blk(0), blk(2), pl.BlockSpec((8, C), lambda i: (0, 0))],
        out_shape=[jax.ShapeDtypeStruct((S, C), F32), jax.ShapeDtypeStruct((S, 3 * C), BF16),
                   jax.ShapeDtypeStruct((8, C), F32)],
        compiler_params=_cparams(("arbitrary",)),
    )(dy, proj, u2, lng, lnb)


def conv_bwd_taps(du2, proj, dw, dproj, name):
    S, C = du2.shape
    tm = _conv_tile(S)
    hb = tm // HALO
    nb = S // tm
    nch = C // 128
    rb = _tile(tm, 128, 8)

    def body(d_ref, dh_ref, a_ref, b_ref, ah_ref, bh_ref, dw_ref, dp_in, dab_ref, sm_ref, ubuf, dbuf, sh, sh2):
        del dp_in
        i = pl.program_id(0)
        _fill_glu(ubuf, a_ref, b_ref, ah_ref, bh_ref, i == 0, tm)
        dbuf[0:tm, :] = d_ref[...]
        dbuf[tm:tm + HALO, :] = jnp.where(i == nb - 1, 0.0, dh_ref[...])

        @pl.when(i == 0)
        def _():
            sm_ref[...] = jnp.zeros_like(sm_ref)

        def chunk(cc, carry):
            cols = pl.ds(pl.multiple_of(cc * 128, 128), 128)
            cols_b = pl.ds(pl.multiple_of(C + cc * 128, 128), 128)
            for r0 in range(0, tm, rb):
                d0 = dbuf[r0:r0 + rb, cols]
                du = jnp.zeros((rb, 128), F32)
                for b in range(8):
                    offs = list(range(b, CONV_K, 8))
                    n = rb + 8 * (len(offs) - 1)
                    sh[0:n, :] = dbuf[pl.ds(r0 + b, n), cols]
                    for a, o in enumerate(offs):
                        k = CONV_K - 1 - o
                        du = du + dw_ref[k:k + 1, cols] * sh[8 * a:8 * a + rb, :]
                    sh2[0:n, :] = ubuf[pl.ds(HALO - (CONV_K - 1) + b + r0, n), cols]
                    for a, k in enumerate(offs):
                        sm_ref[k:k + 1, cols] += jnp.sum(d0 * sh2[8 * a:8 * a + rb, :], axis=0, keepdims=True)
                rows = pl.ds(r0, rb)
                av = a_ref[rows, cols].astype(F32)
                sb = _sigmoid(b_ref[rows, cols].astype(F32))
                da = du * sb
                db = du * av * sb * (1.0 - sb)
                dab_ref[rows, cols] = da.astype(BF16)
                dab_ref[rows, cols_b] = db.astype(BF16)
                sm_ref[32:33, cols] += jnp.sum(da, axis=0, keepdims=True)
                sm_ref[33:34, cols] += jnp.sum(db, axis=0, keepdims=True)
            return carry

        lax.fori_loop(0, nch, chunk, 0)

    blk = lambda cb: pl.BlockSpec((tm, C), lambda i: (i, cb))
    halo = lambda cb: pl.BlockSpec((HALO, C), lambda i: (jnp.maximum(i * hb - 1, 0), cb))
    nxt = pl.BlockSpec((HALO, C), lambda i: (jnp.minimum((i + 1) * hb, nb * hb - 1), 0))
    return _pcall(
        body, name=name, grid=(nb,),
        in_specs=[blk(0), nxt, blk(0), blk(1), halo(0), halo(1),
                  pl.BlockSpec((CONV_K, C), lambda i: (0, 0)), pl.BlockSpec(memory_space=pl.ANY)],
        out_specs=[pl.BlockSpec((tm, 2 * C), lambda i: (i, 0)), pl.BlockSpec((40, C), lambda i: (0, 0))],
        out_shape=[jax.ShapeDtypeStruct((S, 3 * C), BF16), jax.ShapeDtypeStruct((40, C), F32)],
        scratch_shapes=[pltpu.VMEM((HALO + tm, C), F32), pltpu.VMEM((tm + HALO, C), F32),
                        pltpu.VMEM((rb + HALO, 128), F32), pltpu.VMEM((rb + HALO, 128), F32)],
        input_output_aliases={7: 0},
        compiler_params=_cparams(("arbitrary",)),
    )(du2, du2, proj, proj, proj, proj, dw, dproj)


def _rows_tile(R, Cc, budget=1 << 18):
    cap = max(8, budget // max(Cc, 1))
    if R <= cap:
        return R
    t = (cap // 8) * 8
    while t >= 8:
        if R % t == 0:
            return t
        t -= 8
    return R


def cast_into_slot(w, chip, name, l0, nl):
    _, R, Cc = w.shape
    tr = _rows_tile(R, Cc)

    def body(s_ref, w_ref, o_ref):
        del s_ref
        o_ref[...] = w_ref[...].astype(BF16)

    grid_spec = pltpu.PrefetchScalarGridSpec(
        num_scalar_prefetch=1, grid=(nl, R // tr),
        in_specs=[pl.BlockSpec((None, tr, Cc), lambda l, r, s: (l0 + l, r, 0))],
        out_specs=pl.BlockSpec((None, None, tr, Cc), lambda l, r, s: (l, s[0], r, 0)),
    )
    return _pcall(
        body, name=name, grid_spec=grid_spec, out_shape=jax.ShapeDtypeStruct((nl, N_CHIPS, R, Cc), BF16),
        compiler_params=_cparams(("parallel", "parallel")),
    )(chip, w)


def pair_sum(g, rcv, cidx, name):
    L, K, _, half, Cc = g.shape
    g5 = g
    tr = _rows_tile(half, Cc)

    def body(c_ref, g_ref, r_ref, o_ref):
        del c_ref
        o_ref[...] = (g_ref[...] + r_ref[...]).astype(BF16)

    grid_spec = pltpu.PrefetchScalarGridSpec(
        num_scalar_prefetch=1, grid=(L, K, half // tr),
        in_specs=[pl.BlockSpec((None, None, None, tr, Cc), lambda l, k, r, c: (l, k, c[0], r, 0)),
                  pl.BlockSpec((None, None, tr, Cc), lambda l, k, r, c: (l, k, r, 0))],
        out_specs=pl.BlockSpec((None, None, tr, Cc), lambda l, k, r, c: (l, k, r, 0)),
    )
    return _pcall(
        body, name=name, grid_spec=grid_spec, out_shape=jax.ShapeDtypeStruct((L, K, half, Cc), BF16),
        compiler_params=_cparams(("parallel", "parallel", "parallel")),
    )(cidx, g5, rcv)


def chip_sum(parts, got, sel, name, layers, l0, prev=None):
    Lp, _, R, Cc = parts.shape
    n_got = got.shape[1]
    tr = _rows_tile(R, Cc)

    def body(s_ref, p_ref, g_ref, *rest):
        del s_ref
        o_ref = rest[-1]
        acc = p_ref[...].astype(F32)
        for k in range(n_got):
            acc = acc + g_ref[k].astype(F32)
        o_ref[...] = acc

    in_specs = [pl.BlockSpec((None, None, tr, Cc), lambda l, r, s: (l, s[0], r, 0)),
                pl.BlockSpec((None, n_got, tr, Cc), lambda l, r, s: (l, 0, r, 0))]
    ops = [sel, parts, got]
    aliases = {}
    if prev is not None:
        in_specs.append(pl.BlockSpec(memory_space=pl.ANY))
        ops.append(prev)
        aliases = {3: 0}
    grid_spec = pltpu.PrefetchScalarGridSpec(
        num_scalar_prefetch=1, grid=(Lp, R // tr), in_specs=in_specs,
        out_specs=pl.BlockSpec((None, None, tr, Cc), lambda l, r, s: (l0 + l, s[1], r, 0)),
    )
    return _pcall(
        body, name=name, grid_spec=grid_spec, out_shape=jax.ShapeDtypeStruct((layers, 2, R, Cc), F32),
        input_output_aliases=aliases, compiler_params=_cparams(("parallel", "parallel")),
    )(*ops)


def dev_sum(parts, name):
    K, R, Cc = parts.shape

    def body(p_ref, o_ref):
        acc = p_ref[0]
        for k in range(1, K):
            acc = acc + p_ref[k]
        o_ref[...] = acc

    return _pcall(
        body, name=name, grid=(R // 8,),
        in_specs=[pl.BlockSpec((K, 8, Cc), lambda r: (0, r, 0))],
        out_specs=pl.BlockSpec((8, Cc), lambda r: (r, 0)),
        out_shape=jax.ShapeDtypeStruct((R, Cc), F32), compiler_params=_cparams(("parallel",)),
    )(parts)


def adamw(w, g, m, v, name):
    shape = w.shape
    if w.ndim == 3:
        L, R, Cc = shape
    else:
        L, R, Cc = 1, (1 if w.ndim == 1 else shape[0]), shape[-1]
    view = lambda t: t.reshape((L, R, Cc))
    tr = _rows_tile(R, Cc, budget=1 << 17)
    c1 = 1.0 - ADAM_B1 ** ADAM_STEP
    c2 = 1.0 - ADAM_B2 ** ADAM_STEP

    def body(w_ref, g_ref, m_ref, v_ref, d_ref, nm_ref, nv_ref):
        gg = g_ref[...]
        nm = ADAM_B1 * m_ref[...] + (1.0 - ADAM_B1) * gg
        nv = ADAM_B2 * v_ref[...] + (1.0 - ADAM_B2) * (gg * gg)
        d_ref[...] = -ADAM_LR * ((nm / c1) / (jnp.sqrt(nv / c2) + ADAM_EPS) + ADAM_WD * w_ref[...])
        nm_ref[...] = nm
        nv_ref[...] = nv

    spec = pl.BlockSpec((None, tr, Cc), lambda l, r: (l, r, 0))
    outs = _pcall(
        body, name=name, grid=(L, R // tr), in_specs=[spec] * 4, out_specs=[spec] * 3,
        out_shape=[jax.ShapeDtypeStruct((L, R, Cc), F32)] * 3, compiler_params=_cparams(("parallel", "parallel")),
    )(view(w), view(g), view(m), view(v))
    return tuple(o.reshape(shape) for o in outs)


def _place():
    x, y, c = lax.axis_index("x"), lax.axis_index("y"), lax.axis_index("c")
    other_chips = [(1 - x, y), (x, 1 - y), (1 - x, 1 - y)]
    return x, y, c, other_chips


def _rcopy(src, dst, ssem, rsem, k, to):
    return pltpu.make_async_remote_copy(src_ref=src, dst_ref=dst, send_sem=ssem.at[k], recv_sem=rsem.at[k],
                                        device_id=to, device_id_type=MESH)


def gather_weights(slots, small):
    nt = len(slots)

    def body(*refs):
        outs, small_out = refs[nt + 1:2 * nt + 1], refs[2 * nt + 1]
        ssem, rsem = refs[2 * nt + 2:]
        x, y, c, chips = _place()
        me = 2 * x + y
        sib = (x, y, 1 - c)

        def rows(t, half_of):
            half = outs[t].shape[2] // 2
            return pl.ds(half_of * half, half)

        first, passed = [], []
        for t in range(nt):
            mine = outs[t].at[:, me, rows(t, c), :]
            for j, chip in enumerate(chips):
                first.append(_rcopy(mine, mine, ssem, rsem, 6 * t + j, (*chip, c)))
        for j, chip in enumerate(chips):
            first.append(_rcopy(small_out.at[me], small_out.at[me], ssem, rsem, 6 * nt + j, (*chip, c)))
        for cp in first:
            cp.start()
        for t in range(nt):
            for j, (px, py) in enumerate(chips):
                land = outs[t].at[:, 2 * px + py, rows(t, c), :]
                _rcopy(land, land, ssem, rsem, 6 * t + j, (x, y, c)).wait_recv()
                fw = _rcopy(land, land, ssem, rsem, 6 * t + 3 + j, sib)
                fw.start()
                passed.append(fw)
        for j, (px, py) in enumerate(chips):
            land = small_out.at[2 * px + py]
            _rcopy(land, land, ssem, rsem, 6 * nt + j, (x, y, c)).wait_recv()
        for t in range(nt):
            for j, (px, py) in enumerate(chips):
                land = outs[t].at[:, 2 * px + py, rows(t, 1 - c), :]
                _rcopy(land, land, ssem, rsem, 6 * t + 3 + j, (x, y, c)).wait_recv()
        for cp in first + passed:
            cp.wait_send()

    ops = list(slots) + [small]
    nsem = 6 * nt + 3
    return _pcall(
        body, name="gather_weights", out_shape=[jax.ShapeDtypeStruct(s.shape, s.dtype) for s in ops],
        in_specs=[pl.BlockSpec(memory_space=pl.ANY)] * (nt + 1),
        out_specs=[pl.BlockSpec(memory_space=pl.ANY)] * (nt + 1),
        input_output_aliases={n: n for n in range(nt + 1)},
        scratch_shapes=[pltpu.SemaphoreType.DMA((nsem,)), pltpu.SemaphoreType.DMA((nsem,))],
    )(*ops)


def swap_halves(grads, name):
    nt = len(grads)

    def body(*refs):
        ins, outs = refs[:nt], refs[nt:2 * nt]
        ssem, rsem = refs[2 * nt:]
        x, y, c, _ = _place()
        sib = (x, y, 1 - c)
        cps = [_rcopy(ins[t].at[:, :, 1 - c], outs[t], ssem, rsem, t, sib) for t in range(nt)]
        for cp in cps:
            cp.start()
        for cp in cps:
            cp.wait()

    out_shape = [jax.ShapeDtypeStruct(g.shape[:2] + g.shape[3:], g.dtype) for g in grads]
    return _pcall(
        body, name=name, out_shape=out_shape,
        in_specs=[pl.BlockSpec(memory_space=pl.ANY)] * nt, out_specs=[pl.BlockSpec(memory_space=pl.ANY)] * nt,
        scratch_shapes=[pltpu.SemaphoreType.DMA((nt,)), pltpu.SemaphoreType.DMA((nt,))],
    )(*grads)


def _exchange_copies(ins, outs, ssem, rsem):
    x, y, c, chips = _place()
    sends, lands = [], []
    for t in range(len(ins)):
        for j, (px, py) in enumerate(chips):
            k = 3 * t + j
            sends.append(_rcopy(ins[t].at[:, 2 * px + py], outs[t].at[:, j], ssem, rsem, k, (px, py, c)))
            land = outs[t].at[:, j]
            lands.append(_rcopy(land, land, ssem, rsem, k, (x, y, c)))
    return sends, lands


def _gather_direct_copies(bufs, ssem, rsem):
    x, y, c, chips = _place()
    me = 2 * x + y
    sends, lands = [], []
    for t, buf in enumerate(bufs):
        mine = buf.at[:, me]
        for j, (px, py) in enumerate(chips):
            k = 3 * t + j
            sends.append(_rcopy(mine, mine, ssem, rsem, k, (px, py, c)))
            land = buf.at[:, 2 * px + py]
            lands.append(_rcopy(land, land, ssem, rsem, k, (x, y, c)))
    return sends, lands


def exchange_partials(parts, small):
    nt = len(parts)

    def body(*refs):
        ins = refs[:nt]
        outs, small_out = refs[nt + 1:2 * nt + 1], refs[2 * nt + 1]
        ssem, rsem = refs[2 * nt + 2:]
        x, y, c, chips = _place()
        dev = 4 * x + 2 * y + c
        sends, lands = _exchange_copies(ins, outs, ssem, rsem)
        peers = [(px, py, pc) for pc in (c, 1 - c) for (px, py) in [(x, y)] + chips][1:]
        for j, (px, py, pc) in enumerate(peers):
            k = 3 * nt + j
            sends.append(_rcopy(small_out.at[dev], small_out.at[dev], ssem, rsem, k, (px, py, pc)))
            land = small_out.at[4 * px + 2 * py + pc]
            lands.append(_rcopy(land, land, ssem, rsem, k, (x, y, c)))
        for cp in sends:
            cp.start()
        for cp in lands:
            cp.wait_recv()
        for cp in sends:
            cp.wait_send()

    out_shape = [jax.ShapeDtypeStruct((p.shape[0], N_CHIPS - 1) + p.shape[2:], p.dtype) for p in parts]
    out_shape.append(jax.ShapeDtypeStruct(small.shape, small.dtype))
    nsem = 3 * nt + 7
    return _pcall(
        body, name="exchange_partials", out_shape=out_shape,
        in_specs=[pl.BlockSpec(memory_space=pl.ANY)] * (nt + 1),
        out_specs=[pl.BlockSpec(memory_space=pl.ANY)] * (nt + 1),
        input_output_aliases={nt: nt},
        scratch_shapes=[pltpu.SemaphoreType.DMA((nsem,)), pltpu.SemaphoreType.DMA((nsem,))],
    )(*parts, small)


def join_halves(halves):
    nt = len(halves)

    def body(*refs):
        outs = refs[nt:2 * nt]
        ssem, rsem = refs[2 * nt:]
        x, y, c, _ = _place()
        sib = (x, y, 1 - c)
        sends = [_rcopy(outs[t].at[:, c], outs[t].at[:, c], ssem, rsem, t, sib) for t in range(nt)]
        for cp in sends:
            cp.start()
        for t in range(nt):
            land = outs[t].at[:, 1 - c]
            _rcopy(land, land, ssem, rsem, t, (x, y, c)).wait_recv()
        for cp in sends:
            cp.wait_send()

    return _pcall(
        body, name="join_halves", out_shape=[jax.ShapeDtypeStruct(h.shape, h.dtype) for h in halves],
        in_specs=[pl.BlockSpec(memory_space=pl.ANY)] * nt, out_specs=[pl.BlockSpec(memory_space=pl.ANY)] * nt,
        input_output_aliases={n: n for n in range(nt)},
        scratch_shapes=[pltpu.SemaphoreType.DMA((nt,)), pltpu.SemaphoreType.DMA((nt,))],
    )(*halves)


def _pad_cols(a, n):
    return jnp.pad(a, [(0, 0)] * (a.ndim - 1) + [(0, n - a.shape[-1])])


def kernel(x, norm_g, fox_w_in, fox_b_f, fox_w_out, conv_w_in, conv_b_in, conv_dw, conv_dw_b, conv_ln_g, conv_ln_b, conv_w_out, final_norm_g, loss_target, m_norm_g, m_fox_w_in, m_fox_b_f, m_fox_w_out, m_conv_w_in, m_conv_b_in, m_conv_dw, m_conv_dw_b, m_conv_ln_g, m_conv_ln_b, m_conv_w_out, m_final_norm_g, v_norm_g, v_fox_w_in, v_fox_b_f, v_fox_w_out, v_conv_w_in, v_conv_b_in, v_conv_dw, v_conv_dw_b, v_conv_ln_g, v_conv_ln_b, v_conv_w_out, v_final_norm_g):
    S, D = x.shape[1], x.shape[2]
    H = fox_b_f.shape[1]
    assert D == H * HEAD_DIM, "one head must be one lane tile"
    W = C = D
    NL = fox_w_in.shape[0]
    Dq = D // N_CHIPS
    NA = fox_w_in.shape[2]
    scale = HEAD_DIM ** -0.5
    chip = 2 * lax.axis_index("x") + lax.axis_index("y")
    core = lax.axis_index("c")
    cidx = core.astype(jnp.int32).reshape((1,))
    chip1 = chip.astype(jnp.int32).reshape((1,))
    sel = jnp.stack([chip, core]).astype(jnp.int32)

    small_pack = jnp.concatenate([
        conv_b_in.reshape((NL * 3, Dq)), conv_dw.reshape((NL * CONV_K, Dq)), conv_dw_b, conv_ln_g, conv_ln_b,
        jnp.zeros((PACK_ROWS - NL * (3 + CONV_K + 3), Dq), F32)], axis=0)
    small_slots = lax.dynamic_update_slice(jnp.zeros((N_CHIPS, PACK_ROWS, Dq), F32), small_pack[None], (chip, 0, 0))
    ga0, gb0, gsmall = gather_weights(
        [cast_into_slot(fox_w_in, chip1, "cast_fox_w_in_0", 0, 1),
         cast_into_slot(fox_w_out, chip1, "cast_fox_w_out_0", 0, 1)], small_slots)
    later = [cast_into_slot(fox_w_in, chip1, "cast_fox_w_in_1", 1, NL - 1),
             cast_into_slot(fox_w_out, chip1, "cast_fox_w_out_1", 1, NL - 1),
             cast_into_slot(conv_w_in, chip1, "cast_conv_w_in", 0, NL),
             cast_into_slot(conv_w_out, chip1, "cast_conv_w_out", 0, NL)]

    def fox_weights(ga, gb):
        n = ga.shape[0]
        wfull = jnp.transpose(ga, (0, 2, 1, 3)).reshape((n, D, N_CHIPS * NA))
        wq, wk, wv, wg, wf = (wfull[:, :, 0:W], wfull[:, :, W:2 * W], wfull[:, :, 2 * W:3 * W],
                              wfull[:, :, 3 * W:4 * W], wfull[:, :, 4 * W:])
        w1 = jnp.concatenate([wq, wg], axis=-1).reshape((n, 1, D, 2 * W))
        w2 = jnp.stack([wk.reshape((n, D, H, HEAD_DIM)), wv.reshape((n, D, H, HEAD_DIM))], axis=3)
        return w1, w2.reshape((n, 1, D, 2 * W)), _pad_cols(wf, 128).reshape((n, 1, D, 128)), gb.reshape((n, 1, W, D))

    fox_w = {0: (fox_weights(ga0, gb0), 0)}
    wc = wo_conv = None
    b_in = gsmall[:, 0:3 * NL, :].reshape((N_CHIPS, NL, 3 * Dq)).transpose((1, 0, 2)).reshape((NL, 3 * C))
    dwt = gsmall[:, 3 * NL:3 * NL + CONV_K * NL, :].reshape((N_CHIPS, NL, CONV_K, Dq))
    dwt = dwt.transpose((1, 2, 0, 3)).reshape((NL, CONV_K, C))
    r0 = (3 + CONV_K) * NL
    vecs = gsmall[:, r0:r0 + 3 * NL, :].reshape((N_CHIPS, 3, NL, Dq)).transpose((1, 2, 0, 3)).reshape((3, NL, C))
    dwb, lng, lnb = vecs[0], vecs[1], vecs[2]
    bfp = _pad_cols(fox_b_f, 128)

    h = x.reshape((S, D))
    tgt = loss_target.reshape((S, D))
    saved = []
    n_layers = norm_g.shape[0]
    for i in range(n_layers):
        j = i // 2
        g_i = norm_g[i:i + 1]
        hn = rms_fwd(h, g_i, f"rms_fwd_{i}")
        if i % 2 == 0:
            (w1, w2, wfp, wo_fox), wl = fox_w[j]
            p1 = mm_nn(hn, w1, wl, 1, out_dtype=BF16, name=f"fox_proj_qg_{i}")
            p2 = mm_nn(hn, w2, wl, 1, out_dtype=BF16, name=f"fox_proj_kv_{i}")
            pf = mm_nn(hn, wfp, wl, 1, out_dtype=F32, name=f"fox_proj_f_{i}")
            qa, ka = fox_decay(pf, bfp[j:j + 1], H, scale, f"fox_decay_{i}")
            o, yv, qb, filled = attn_fwd(p1, p2, qa, ka, H, scale, f"attn_fwd_{i}", gather=later if i == 0 else ())
            if i == 0:
                ga1, gb1, wc, gd = filled
                wo_conv = gd.reshape((NL, 1, C, D))
                rest = fox_weights(ga1, gb1)
                for jj in range(1, NL):
                    fox_w[jj] = (rest, jj - 1)
            h_new = mm_nt_res(yv, wo_fox, wl, h, f"fox_out_{i}")
            saved.append((h, hn, p1, p2, pf, ka, o, yv, qb))
        else:
            proj = mm_nn(hn, wc, j, N_CHIPS, out_dtype=BF16, name=f"conv_proj_{i}", bias=b_in[j:j + 1])
            u2, yv = conv_fwd(proj, dwt[j], dwb[j:j + 1], lng[j:j + 1], lnb[j:j + 1], f"conv_fwd_{i}")
            h_new = mm_nt_res(yv, wo_conv, j, h, f"conv_out_{i}")
            saved.append((h, hn, proj, u2, yv))
        h = h_new

    dh, d_gf, loss_part = loss_head(h, final_norm_g.reshape((1, D)), tgt, "loss_head")

    d_norm = [None] * n_layers
    d_fox_b = [None] * NL
    d_conv_small = [None] * NL
    wgrad = {}

    def pair_sums(keys, tag):
        gs = [wgrad[k].reshape(wgrad[k].shape[:2] + (2, wgrad[k].shape[2] // 2, wgrad[k].shape[3])) for k in keys]
        rcv = swap_halves(gs, f"swap_halves_{tag}")
        return {k: pair_sum(g, r, cidx, f"pair_sum_{k[0]}{k[1]}") for k, g, r in zip(keys, gs, rcv)}

    parts, got = {}, {}
    for i in reversed(range(n_layers)):
        j = i // 2
        g_i = norm_g[i:i + 1]
        if i % 2 == 0:
            h_in, hn, p1, p2, pf, ka, o, yv, qb = saved[i]
            (w1, w2, wfp, wo_fox), wl = fox_w[j]
            wgrad[("b", j)] = mm_tn(yv, dh, 1, name=f"fox_out_dw_{i}").reshape((1, N_CHIPS, Dq, D))
            dy = mm_nn_t(dh, wo_fox, wl, f"fox_out_dx_{i}")
            do, dp1, da = attn_bwd_prep(dy, o, p1, H, f"attn_bwd_prep_{i}")
            early = ()
            if i == 0:
                parts.update(pair_sums(list(wgrad), "early"))
                early = list(parts)
            dp1, dp2, dc, arrived = attn_bwd(p1, qb, do, da, p2, ka, dp1, H, scale, f"attn_bwd_{i}",
                                             exchange=[parts[k] for k in early])
            got.update(dict(zip(early, arrived)))
            dz, dbf = fox_decay_bwd(dc, pf, bfp[j:j + 1], H, f"fox_decay_bwd_{i}")
            d_fox_b[j] = dbf
            dw1 = mm_tn(hn, dp1, 1, name=f"fox_dw_qg_{i}")[0, 0]
            dw2 = mm_tn(hn, dp2, 1, name=f"fox_dw_kv_{i}")[0, 0]
            dwf = mm_tn(hn, dz, 1, name=f"fox_dw_f_{i}")[0, 0]
            dw2 = dw2.reshape((D, H, 2, HEAD_DIM))
            d_in = jnp.concatenate([dw1[:, :W], dw2[:, :, 0].reshape((D, W)), dw2[:, :, 1].reshape((D, W)),
                                    dw1[:, W:], dwf[:, :H]], axis=-1)
            wgrad[("a", j)] = d_in.reshape((D, N_CHIPS, NA)).transpose((1, 0, 2))[None]
            dhn = mm_nt(dp1, w1, wl, 1, name=f"fox_dx_qg_{i}")
            dhn = mm_nt(dp2, w2, wl, 1, name=f"fox_dx_kv_{i}", res=dhn)
            dhn = mm_nt(dz, wfp, wl, 1, name=f"fox_dx_f_{i}", res=dhn)
        else:
            h_in, hn, proj, u2, yv = saved[i]
            wgrad[("d", j)] = mm_tn(yv, dh, 1, name=f"conv_out_dw_{i}").reshape((1, N_CHIPS, Dq, D))
            dy = mm_nn_t(dh, wo_conv, j, f"conv_out_dx_{i}")
            du2, dproj, sm1 = conv_bwd_norm(dy, proj, u2, lng[j:j + 1], lnb[j:j + 1], f"conv_bwd_norm_{i}")
            dproj, sm2 = conv_bwd_taps(du2, proj, dwt[j], dproj, f"conv_bwd_taps_{i}")
            d_conv_small[j] = (sm1, sm2)
            wgrad[("c", j)] = mm_tn(hn, dproj, N_CHIPS, name=f"conv_dw_in_{i}")
            dhn = mm_nt(dproj, wc, j, N_CHIPS, name=f"conv_dx_{i}")
        dh, d_norm[i] = rms_bwd(dhn, h_in, g_i, dh, f"rms_bwd_{i}")

    late = [k for k in wgrad if k not in parts]
    parts.update(pair_sums(late, "late"))

    zrow = jnp.zeros((1, D), F32)
    rows = list(d_norm) + [d_gf]
    rows += [_pad_cols(d_fox_b[l][:, :H], D) for l in range(NL)]
    rows += [_pad_cols(loss_part[:, :1], D)]
    for l in range(NL):
        sm1, sm2 = d_conv_small[l]
        rows += [sm2[32:33], sm2[33:34], sm1[3:4]]
    for l in range(NL):
        rows += [d_conv_small[l][1][0:CONV_K]]
    rows += [d_conv_small[l][0][2:3] for l in range(NL)]
    rows += [d_conv_small[l][0][0:1] for l in range(NL)]
    rows += [d_conv_small[l][0][1:2] for l in range(NL)]
    n_rows = sum(r.shape[0] for r in rows)
    rows += [zrow] * (SMALL_ROWS - n_rows)
    small = jnp.concatenate(rows, axis=0)

    dev = 4 * lax.axis_index("x") + 2 * lax.axis_index("y") + core
    small_slots = lax.dynamic_update_slice(jnp.zeros((N_DEV, SMALL_ROWS, D), F32), small[None], (dev, 0, 0))
    arrived = exchange_partials([parts[k] for k in late], small_slots)
    got.update(dict(zip(late, arrived[:-1])))
    tot = dev_sum(arrived[-1], "dev_sum")
    halves = {}
    for kind in "abcd":
        for l in range(NL):
            halves[kind] = chip_sum(parts[(kind, l)], got[(kind, l)], sel, f"chip_sum_{kind}{l}", NL, l,
                                    prev=halves.get(kind))
    full = join_halves([halves[kind] for kind in "abcd"])
    grad_fox_w_in = full[0].reshape(fox_w_in.shape)
    grad_fox_w_out = full[1].reshape(fox_w_out.shape)
    grad_conv_w_in = full[2].reshape(conv_w_in.shape)
    grad_conv_w_out = full[3].reshape(conv_w_out.shape)

    def mine(v):
        return lax.dynamic_slice_in_dim(v, chip * Dq, Dq, axis=v.ndim - 1)

    r = n_layers
    grad_norm_g = tot[0:r]
    grad_final = tot[r]
    grad_fox_b_f = tot[r + 1:r + 1 + NL, :H]
    loss = tot[r + 1 + NL, 0]
    r = r + 2 + NL
    gb_full = tot[r:r + 3 * NL].reshape((NL, 3 * C))
    grad_conv_b_in = lax.dynamic_slice_in_dim(gb_full, chip * 3 * Dq, 3 * Dq, axis=1)
    r += 3 * NL
    grad_conv_dw = mine(tot[r:r + CONV_K * NL].reshape((NL, CONV_K, C)))
    r += CONV_K * NL
    grad_conv_dw_b = mine(tot[r:r + NL])
    grad_conv_ln_g = mine(tot[r + NL:r + 2 * NL])
    grad_conv_ln_b = mine(tot[r + 2 * NL:r + 3 * NL])

    grads = [grad_norm_g, grad_fox_w_in, grad_fox_b_f, grad_fox_w_out, grad_conv_w_in, grad_conv_b_in,
             grad_conv_dw, grad_conv_dw_b, grad_conv_ln_g, grad_conv_ln_b, grad_conv_w_out, grad_final]
    ws = [norm_g, fox_w_in, fox_b_f, fox_w_out, conv_w_in, conv_b_in, conv_dw, conv_dw_b, conv_ln_g, conv_ln_b,
          conv_w_out, final_norm_g]
    ms = [m_norm_g, m_fox_w_in, m_fox_b_f, m_fox_w_out, m_conv_w_in, m_conv_b_in, m_conv_dw, m_conv_dw_b,
          m_conv_ln_g, m_conv_ln_b, m_conv_w_out, m_final_norm_g]
    vs = [v_norm_g, v_fox_w_in, v_fox_b_f, v_fox_w_out, v_conv_w_in, v_conv_b_in, v_conv_dw, v_conv_dw_b,
          v_conv_ln_g, v_conv_ln_b, v_conv_w_out, v_final_norm_g]
    deltas, new_ms, new_vs = [], [], []
    for n, (w_, g_, m_, v_) in enumerate(zip(ws, grads, ms, vs)):
        d_, nm_, nv_ = adamw(w_, g_, m_, v_, f"adamw_{n}")
        deltas.append(d_)
        new_ms.append(nm_)
        new_vs.append(nv_)
    grad_x = dh.reshape(x.shape)
    return (loss, grad_x, *grads, *deltas, *new_ms, *new_vs)


def mm_nt_res(y, wo, lidx, h, name):
    M, K = y.shape
    N = wo.shape[-1]
    tm = _tile(M, 1024)
    tn = _tile(N, 1024)
    grid = (M // tm, N // tn, 1)
    return _matmul(
        y, wo, contract="nn", grid=grid, name=name,
        a_spec=pl.BlockSpec((tm, K), lambda i, j, k: (i, 0)),
        b_spec=pl.BlockSpec((None, None, K, tn), lambda i, j, k: (lidx, 0, 0, j)),
        o_spec=pl.BlockSpec((tm, tn), lambda i, j, k: (i, j)),
        out_shape=jax.ShapeDtypeStruct((M, N), F32), acc_shape=(tm, tn),
        res=h, res_spec=pl.BlockSpec((tm, tn), lambda i, j, k: (i, j)),
    )


def mm_nn_t(dh, wo, lidx, name):
    M, K = dh.shape
    N = wo.shape[-2]
    tm = _tile(M, 512)
    tn = _tile(N, 1024)
    grid = (M // tm, N // tn, 1)
    return _matmul(
        dh, wo, contract="nt", grid=grid, name=name,
        a_spec=pl.BlockSpec((tm, K), lambda i, j, k: (i, 0)),
        b_spec=pl.BlockSpec((None, None, tn, K), lambda i, j, k: (lidx, 0, j, 0)),
        o_spec=pl.BlockSpec((tm, tn), lambda i, j, k: (i, j)),
        out_shape=jax.ShapeDtypeStruct((M, N), F32), acc_shape=(tm, tn),
    )
```

```python
import jax
import jax.numpy as jnp
from jax import lax
from jax.experimental import pallas as pl
from jax.experimental.pallas import tpu as pltpu

F32 = jnp.float32
BF16 = jnp.bfloat16
MESH = pl.DeviceIdType.MESH

RMS_EPS = 1e-6
LN_EPS = 1e-5
CONV_K = 31
HALO = 32
HEAD_DIM = 128
ADAM_LR = 0.001
ADAM_B1 = 0.9
ADAM_B2 = 0.999
ADAM_EPS = 1e-08
ADAM_WD = 0.01
ADAM_STEP = 10
N_CHIPS = 4
N_DEV = 8
VMEM_LIMIT = 56 * 1024 * 1024
NEG_BIG = -1e30
SMALL_ROWS = 88
PACK_ROWS = 80


def _pcall(body, **kw):
    return pl.pallas_call(body, **kw)


def _cparams(sem=None):
    return pltpu.CompilerParams(dimension_semantics=sem, vmem_limit_bytes=VMEM_LIMIT)


def _tile(n, cap, mult=128):
    if n <= cap:
        return n
    t = (cap // mult) * mult
    while t >= mult:
        if n % t == 0:
            return t
        t -= mult
    raise ValueError(f"no tile for {n} under {cap}")


def _sigmoid(x):
    return 1.0 / (1.0 + jnp.exp(-x))


def _split3(x):
    hi = x.astype(BF16).astype(F32)
    r = x - hi
    mid = r.astype(BF16).astype(F32)
    lo = (r - mid).astype(BF16).astype(F32)
    return hi, mid, lo


_DN = {
    "nn": (((1,), (0,)), ((), ())),
    "nt": (((1,), (1,)), ((), ())),
    "tn": (((0,), (0,)), ((), ())),
}


def _matmul(a, b, *, contract, grid, a_spec, b_spec, o_spec, out_shape, acc_shape, name,
            bias=None, bias_spec=None, res=None, res_spec=None, alias_res=False, prev=None):
    nk = grid[2]
    has_bias = bias is not None
    has_res = res is not None
    assert not (alias_res and prev is not None)

    def body(*refs):
        a_ref, b_ref = refs[0], refs[1]
        pos = 2
        bias_ref = res_ref = None
        if has_bias:
            bias_ref = refs[pos]
            pos += 1
        if has_res:
            res_ref = refs[pos]
            pos += 1
        if prev is not None:
            pos += 1
        o_ref = refs[pos]
        acc_ref = refs[pos + 1] if nk > 1 else None
        p = lax.dot_general(a_ref[...].astype(BF16), b_ref[...].astype(BF16), _DN[contract],
                            preferred_element_type=F32)

        def finish(v):
            if has_bias:
                v = v + bias_ref[...]
            if has_res:
                v = res_ref[...] + v
            o_ref[...] = v.astype(o_ref.dtype)

        if nk == 1:
            finish(p)
        else:
            k = pl.program_id(2)

            @pl.when(k == 0)
            def _():
                acc_ref[...] = p

            @pl.when(k > 0)
            def _():
                acc_ref[...] += p

            @pl.when(k == nk - 1)
            def _():
                finish(acc_ref[...])

    ins = [a, b]
    specs = [a_spec, b_spec]
    if has_bias:
        ins.append(bias)
        specs.append(bias_spec)
    if has_res:
        ins.append(res)
        specs.append(res_spec)
    aliases = {len(ins) - 1: 0} if (has_res and alias_res) else {}
    if prev is not None:
        ins.append(prev)
        specs.append(pl.BlockSpec(memory_space=pl.ANY))
        aliases = {len(ins) - 1: 0}
    return _pcall(
        body, name=name, grid=grid, in_specs=specs, out_specs=o_spec, out_shape=out_shape,
        scratch_shapes=[pltpu.VMEM(acc_shape, F32)] if nk > 1 else [],
        input_output_aliases=aliases,
        compiler_params=_cparams(("parallel", "parallel", "arbitrary")),
    )(*ins)


def mm_nn(a, w, lidx, n_slots, *, out_dtype, name, bias=None):
    M, K = a.shape
    Ns = w.shape[-1]
    tm = _tile(M, 1024)
    tn = _tile(Ns, 1024)
    per = Ns // tn
    grid = (M // tm, n_slots * per, 1)
    return _matmul(
        a, w, contract="nn", grid=grid, name=name,
        a_spec=pl.BlockSpec((tm, K), lambda i, j, k: (i, 0)),
        b_spec=pl.BlockSpec((None, None, K, tn), lambda i, j, k: (lidx, j // per, 0, j % per)),
        o_spec=pl.BlockSpec((tm, tn), lambda i, j, k: (i, j)),
        out_shape=jax.ShapeDtypeStruct((M, n_slots * Ns), out_dtype), acc_shape=(tm, tn),
        bias=bias, bias_spec=None if bias is None else pl.BlockSpec((1, tn), lambda i, j, k: (0, j)),
    )


def mm_nt(a, w, lidx, n_slots, *, name, res=None):
    M = a.shape[0]
    N, Ns = w.shape[-2], w.shape[-1]
    tm = _tile(M, 1024)
    tn = _tile(N, 1024)
    tk = _tile(Ns, 2048)
    per = Ns // tk
    grid = (M // tm, N // tn, n_slots * per)
    return _matmul(
        a, w, contract="nt", grid=grid, name=name,
        a_spec=pl.BlockSpec((tm, tk), lambda i, j, k: (i, k)),
        b_spec=pl.BlockSpec((None, None, tn, tk), lambda i, j, k: (lidx, k // per, j, k % per)),
        o_spec=pl.BlockSpec((tm, tn), lambda i, j, k: (i, j)),
        out_shape=jax.ShapeDtypeStruct((M, N), F32), acc_shape=(tm, tn),
        res=res, res_spec=None if res is None else pl.BlockSpec((tm, tn), lambda i, j, k: (i, j)),
        alias_res=res is not None,
    )


def mm_tn(a, b, n_slots, *, name, layers=1, lidx=0, prev=None):
    S, M = a.shape
    Ns = b.shape[1] // n_slots
    tm = _tile(M, 1024)
    tn = _tile(Ns, 1024)
    tk = _tile(S, 2048)
    per = Ns // tn
    grid = (M // tm, n_slots * per, S // tk)
    return _matmul(
        a, b, contract="tn", grid=grid, name=name,
        a_spec=pl.BlockSpec((tk, tm), lambda i, j, k: (k, i)),
        b_spec=pl.BlockSpec((tk, tn), lambda i, j, k: (k, j)),
        o_spec=pl.BlockSpec((None, None, tm, tn), lambda i, j, k: (lidx, j // per, i, j % per)),
        out_shape=jax.ShapeDtypeStruct((layers, n_slots, M, Ns), F32), acc_shape=(tm, tn), prev=prev,
    )


def rms_fwd(h, g, name):
    S, D = h.shape
    tm = _tile(S, 256, 8)

    def body(h_ref, g_ref, o_ref):
        x = h_ref[...]
        r = lax.rsqrt(jnp.mean(x * x, axis=-1, keepdims=True) + RMS_EPS)
        o_ref[...] = (x * r * g_ref[...]).astype(BF16)

    return _pcall(
        body, name=name, grid=(S // tm,),
        in_specs=[pl.BlockSpec((tm, D), lambda i: (i, 0)), pl.BlockSpec((1, D), lambda i: (0, 0))],
        out_specs=pl.BlockSpec((tm, D), lambda i: (i, 0)),
        out_shape=jax.ShapeDtypeStruct((S, D), BF16),
        compiler_params=_cparams(("parallel",)),
    )(h, g)


def _rms_bwd_rows(x, g, dy):
    d = x.shape[-1]
    r = lax.rsqrt(jnp.mean(x * x, axis=-1, keepdims=True) + RMS_EPS)
    gd = dy * g
    dx = r * gd - x * ((r * r * r) * (jnp.sum(x * gd, axis=-1, keepdims=True) / d))
    return dx, dy * x * r


def rms_bwd(dhn, h, g, dres, name):
    S, D = h.shape
    tm = _tile(S, 256, 8)

    def body(dhn_ref, h_ref, g_ref, dres_ref, dh_ref, dg_ref):
        dx, dgr = _rms_bwd_rows(h_ref[...], g_ref[...], dhn_ref[...])
        dh_ref[...] = dres_ref[...] + dx

        @pl.when(pl.program_id(0) == 0)
        def _():
            dg_ref[...] = jnp.zeros_like(dg_ref)

        dg_ref[...] += jnp.sum(dgr, axis=0, keepdims=True)

    row = pl.BlockSpec((tm, D), lambda i: (i, 0))
    vec = pl.BlockSpec((1, D), lambda i: (0, 0))
    return _pcall(
        body, name=name, grid=(S // tm,),
        in_specs=[row, row, vec, row], out_specs=[row, vec],
        out_shape=[jax.ShapeDtypeStruct((S, D), F32), jax.ShapeDtypeStruct((1, D), F32)],
        input_output_aliases={3: 0},
        compiler_params=_cparams(("arbitrary",)),
    )(dhn, h, g, dres)


def loss_head(h, g, target, name):
    S, D = h.shape
    tm = _tile(S, 256, 8)

    def body(h_ref, g_ref, t_ref, dh_ref, dg_ref, loss_ref):
        x = h_ref[...]
        gg = g_ref[...]
        r = lax.rsqrt(jnp.mean(x * x, axis=-1, keepdims=True) + RMS_EPS)
        y = x * r * gg
        e = y - t_ref[...]
        part = 0.5 * jnp.sum(jnp.mean(e * e, axis=-1, keepdims=True), axis=0, keepdims=True)
        dy = e * (1.0 / D)
        dx, dgr = _rms_bwd_rows(x, gg, dy)
        dh_ref[...] = dx

        @pl.when(pl.program_id(0) == 0)
        def _():
            dg_ref[...] = jnp.zeros_like(dg_ref)
            loss_ref[...] = jnp.zeros_like(loss_ref)

        dg_ref[...] += jnp.sum(dgr, axis=0, keepdims=True)
        loss_ref[...] += jnp.broadcast_to(part, loss_ref.shape)

    row = pl.BlockSpec((tm, D), lambda i: (i, 0))
    vec = pl.BlockSpec((1, D), lambda i: (0, 0))
    return _pcall(
        body, name=name, grid=(S // tm,),
        in_specs=[row, vec, row],
        out_specs=[row, vec, pl.BlockSpec((1, 128), lambda i: (0, 0))],
        out_shape=[jax.ShapeDtypeStruct((S, D), F32), jax.ShapeDtypeStruct((1, D), F32),
                   jax.ShapeDtypeStruct((1, 128), F32)],
        compiler_params=_cparams(("arbitrary",)),
    )(h, g, target)


def fox_decay(pf, bf, n_heads, scale, name):
    S = pf.shape[0]
    tm = min(_tile(S, 256, 8), _attn_tile(S))
    inv_scale = 1.0 / scale

    def body(pf_ref, bf_ref, qa_ref, ka_ref, c0_ref, c1_ref, carry_ref):
        @pl.when(pl.program_id(0) == 0)
        def _():
            carry_ref[...] = jnp.zeros_like(carry_ref)

        z = pf_ref[...] + bf_ref[...]
        logf = jnp.minimum(z, 0.0) - jnp.log(1.0 + jnp.exp(-jnp.abs(z)))
        row = lax.broadcasted_iota(jnp.int32, (tm, tm), 0)
        col = lax.broadcasted_iota(jnp.int32, (tm, tm), 1)
        tri = (row >= col).astype(F32)
        c = jnp.dot(tri, logf, precision=lax.Precision.HIGHEST, preferred_element_type=F32) + carry_ref[...]
        carry_ref[...] = c[tm - 1:tm, :]
        c0_ref[...] = c[0:1, :]
        c1_ref[...] = c[tm - 1:tm, :]
        lane = lax.broadcasted_iota(jnp.int32, (tm, HEAD_DIM), 1)
        for hh in range(n_heads):
            hi, mid, lo = _split3(c[:, hh:hh + 1] * inv_scale)
            qa = jnp.where(lane == 0, hi, jnp.where(lane == 1, mid, jnp.where(lane == 2, lo,
                 jnp.where(lane < 6, 1.0, 0.0))))
            ka = jnp.where(lane < 3, 1.0, jnp.where(lane == 3, -hi, jnp.where(lane == 4, -mid,
                 jnp.where(lane == 5, -lo, jnp.where(lane < 9, 1.0, 0.0)))))
            qa_ref[:, hh * HEAD_DIM:(hh + 1) * HEAD_DIM] = qa.astype(BF16)
            ka_ref[:, hh * HEAD_DIM:(hh + 1) * HEAD_DIM] = ka.astype(BF16)

    wide = pl.BlockSpec((tm, n_heads * HEAD_DIM), lambda i: (i, 0))
    edge = pl.BlockSpec((None, 1, 128), lambda i: (i, 0, 0))
    return _pcall(
        body, name=name, grid=(S // tm,),
        in_specs=[pl.BlockSpec((tm, 128), lambda i: (i, 0)), pl.BlockSpec((1, 128), lambda i: (0, 0))],
        out_specs=[wide, wide, edge, edge],
        out_shape=[jax.ShapeDtypeStruct((S, n_heads * HEAD_DIM), BF16)] * 2
                  + [jax.ShapeDtypeStruct((S // tm, 1, 128), F32)] * 2,
        scratch_shapes=[pltpu.VMEM((1, 128), F32)],
        compiler_params=_cparams(("arbitrary",)),
    )(pf, bf)


def attn_norms(p1, p2, n_heads, name):
    S = p1.shape[0]
    W = n_heads * HEAD_DIM
    t = _attn_tile(S)

    def body(q_ref, kv_ref, qn_ref, kn_ref):
        lane = lax.broadcasted_iota(jnp.int32, (1, 128), 1)
        qn = jnp.zeros((1, 128), F32)
        kn = jnp.zeros((1, 128), F32)
        for hh in range(n_heads):
            q = q_ref[:, hh * HEAD_DIM:(hh + 1) * HEAD_DIM].astype(F32)
            k = kv_ref[:, 2 * hh * HEAD_DIM:(2 * hh + 1) * HEAD_DIM].astype(F32)
            q2 = jnp.max(jnp.sum(q * q, axis=1, keepdims=True), axis=0, keepdims=True)
            k2 = jnp.max(jnp.sum(k * k, axis=1, keepdims=True), axis=0, keepdims=True)
            qn = jnp.where(lane == hh, q2, qn)
            kn = jnp.where(lane == hh, k2, kn)
        qn_ref[...] = qn
        kn_ref[...] = kn

    edge = pl.BlockSpec((None, 1, 128), lambda i: (i, 0, 0))
    return _pcall(
        body, name=name, grid=(S // t,),
        in_specs=[pl.BlockSpec((t, W), lambda i: (i, 0)), pl.BlockSpec((t, 2 * W), lambda i: (i, 0))],
        out_specs=[edge, edge], out_shape=[jax.ShapeDtypeStruct((S // t, 1, 128), F32)] * 2,
        compiler_params=_cparams(("parallel",)),
    )(p1, p2)


PRUNE_BELOW = -110.0


def prune_tables(qn, kn, c0, c1, n_heads, scale):
    nb = qn.shape[0]
    r = c0.shape[0] // nb
    qmax = jnp.sqrt(qn[:, 0, :n_heads])
    kmax = jnp.sqrt(kn[:, 0, :n_heads])
    cfirst = c0[::r, 0, :n_heads]
    clast = c1[r - 1::r, 0, :n_heads]
    bound = (scale * qmax[:, None, :] * (kmax[None, :, :] + kmax[:, None, :])
             + cfirst[:, None, :] - clast[None, :, :])
    ii = lax.broadcasted_iota(jnp.int32, (nb, nb, 1), 0)
    jj = lax.broadcasted_iota(jnp.int32, (nb, nb, 1), 1)
    skip = ((bound < PRUNE_BELOW) & (jj < ii)).astype(jnp.int32)
    first_key = jnp.sum(jnp.cumprod(skip, axis=1), axis=1)
    tail = jnp.sum(jnp.cumprod(skip[::-1], axis=0), axis=0)
    return first_key.T.reshape((-1,)), (nb - tail).T.reshape((-1,))


def _attn_tile(S):
    return 512 if S % 512 == 0 and S >= 2048 else 128


def attn_fwd(p1, p2, qa, ka, n_heads, scale, name, gather=(), first_key=None):
    S = p1.shape[0]
    W = n_heads * HEAD_DIM
    t = _attn_tile(S)
    nq = S // t
    ng = len(gather)
    assert not (ng and first_key is not None)

    def body(*refs):
        if first_key is None:
            core(0, *refs)
        else:
            fk = refs[0][pl.program_id(0) * nq + pl.program_id(1)]
            core(jnp.minimum(fk, pl.program_id(1)), *refs[1:])

    def core(js, q_ref, g_ref, qa_ref, k_ref, v_ref, ka_ref, *rest):
        o_ref, y_ref, qb_ref = rest[ng:ng + 3]
        bufs = rest[ng + 3:2 * ng + 3]
        mp_ref, qq_ref, acc_ref = rest[2 * ng + 3:2 * ng + 6]
        i = pl.program_id(1)
        if ng:
            ssem, rsem = rest[2 * ng + 6:]

            @pl.when((pl.program_id(0) == 0) & (i == 0))
            def _():
                for cp in _gather_direct_copies(bufs, ssem, rsem)[0]:
                    cp.start()
        lane = lax.broadcasted_iota(jnp.int32, (t, HEAD_DIM), 1)
        qa = qa_ref[...].astype(F32)

        def tile_with(neg_stat):
            hi, mid, lo = _split3(neg_stat)
            return jnp.where(lane == 6, hi, jnp.where(lane == 7, mid, jnp.where(lane == 8, lo, qa))).astype(BF16)

        def keys(start, width):
            rows = pl.ds(pl.multiple_of(start, t), width)
            return rows, jnp.concatenate([k_ref[rows, :], ka_ref[rows, :]], axis=1)

        def causal():
            r = lax.broadcasted_iota(jnp.int32, (t, t), 0)
            c = lax.broadcasted_iota(jnp.int32, (t, t), 1)
            return r >= c

        def over_keys(block):
            n = i - js

            def wide(jj, carry):
                block((js + 4 * jj) * t, 4 * t, False)
                return carry

            lax.fori_loop(0, n // 4, wide, 0)
            done = js + (n // 4) * 4

            @pl.when((n & 2) != 0)
            def _():
                block(done * t, 2 * t, False)

            @pl.when((n & 1) != 0)
            def _():
                block((done + (n & 2)) * t, t, False)

            block(i * t, t, True)

        qq_ref[:, :HEAD_DIM] = q_ref[...]
        qq_ref[:, HEAD_DIM:] = qa_ref[...]
        mp_ref[...] = jnp.full(mp_ref.shape, NEG_BIG, F32)

        def max_block(start, width, masked):
            _, kk = keys(start, width)
            s = lax.dot_general(qq_ref[...], kk, _DN["nt"], preferred_element_type=F32)
            if masked:
                s = jnp.where(causal(), s, NEG_BIG)
            part = s[:, 0:HEAD_DIM]
            for a in range(1, width // HEAD_DIM):
                part = jnp.maximum(part, s[:, a * HEAD_DIM:(a + 1) * HEAD_DIM])
            mp_ref[...] = jnp.maximum(mp_ref[...], part)

        over_keys(max_block)
        m = jnp.max(mp_ref[...], axis=1, keepdims=True)
        qq_ref[:, HEAD_DIM:] = tile_with(-m)
        acc_ref[...] = jnp.zeros_like(acc_ref)

        def sum_block(start, width, masked):
            rows, kk = keys(start, width)
            a = lax.dot_general(qq_ref[...], kk, _DN["nt"], preferred_element_type=F32)
            p = jnp.exp(scale * a)
            if masked:
                p = jnp.where(causal(), p, 0.0)
            ones0 = jnp.where(lax.broadcasted_iota(jnp.int32, (width, HEAD_DIM), 1) == 0, 1.0, 0.0).astype(BF16)
            vv = jnp.concatenate([v_ref[rows, :], ones0], axis=1)
            acc_ref[...] += jnp.dot(p.astype(BF16), vv, preferred_element_type=F32)

        over_keys(sum_block)

        l = acc_ref[:, HEAD_DIM:HEAD_DIM + 1]
        o = acc_ref[:, :HEAD_DIM] / l
        gate = g_ref[...].astype(F32)
        o_ref[...] = o.astype(BF16)
        y_ref[...] = (o * (gate * _sigmoid(gate))).astype(BF16)
        qb_ref[...] = tile_with(-(m + jnp.log(l) * (1.0 / scale)))

        if ng:
            @pl.when((pl.program_id(0) == n_heads - 1) & (i == nq - 1))
            def _():
                sends, lands = _gather_direct_copies(bufs, ssem, rsem)
                for cp in lands:
                    cp.wait_recv()
                for cp in sends:
                    cp.wait_send()

    H = n_heads
    qtile = lambda off: pl.BlockSpec((t, HEAD_DIM), lambda h, i, *_: (i, off + h))
    full = lambda fn: pl.BlockSpec((S, HEAD_DIM), fn)
    hbm = pl.BlockSpec(memory_space=pl.ANY)
    sems = [pltpu.SemaphoreType.DMA((3 * ng,)), pltpu.SemaphoreType.DMA((3 * ng,))] if ng else []
    in_specs = [qtile(0), qtile(H), qtile(0), full(lambda h, i, *_: (0, 2 * h)), full(lambda h, i, *_: (0, 2 * h + 1)),
                full(lambda h, i, *_: (0, h))] + [hbm] * ng
    out_specs = [qtile(0), qtile(0), qtile(0)] + [hbm] * ng
    scratch = [pltpu.VMEM((t, HEAD_DIM), F32), pltpu.VMEM((t, 2 * HEAD_DIM), BF16),
               pltpu.VMEM((t, 2 * HEAD_DIM), F32)] + sems
    out_shape = [jax.ShapeDtypeStruct((S, W), BF16)] * 3 + [jax.ShapeDtypeStruct(b.shape, b.dtype) for b in gather]
    sem = _cparams(("arbitrary", "arbitrary") if ng else ("parallel", "arbitrary"))
    if first_key is None:
        outs = _pcall(
            body, name=name, grid=(H, nq), in_specs=in_specs, out_specs=out_specs, out_shape=out_shape,
            scratch_shapes=scratch, input_output_aliases={6 + n: 3 + n for n in range(ng)}, compiler_params=sem,
        )(p1, p1, qa, p2, p2, ka, *gather)
    else:
        grid_spec = pltpu.PrefetchScalarGridSpec(num_scalar_prefetch=1, grid=(H, nq), in_specs=in_specs,
                                                 out_specs=out_specs, scratch_shapes=scratch)
        outs = _pcall(body, name=name, grid_spec=grid_spec, out_shape=out_shape, compiler_params=sem,
                      )(first_key, p1, p1, qa, p2, p2, ka)
    return outs[0], outs[1], outs[2], list(outs[3:])


def attn_bwd_prep(dy, o, p1, n_heads, name):
    S, W = dy.shape
    tm = _tile(S, 256, 8)
    H = n_heads

    def body(dy_ref, o_ref, g_ref, do_ref, dg_ref, da_ref):
        lane = lax.broadcasted_iota(jnp.int32, (tm, HEAD_DIM), 1)
        for hh in range(H):
            cs = slice(hh * HEAD_DIM, (hh + 1) * HEAD_DIM)
            g = g_ref[:, cs].astype(F32)
            oo = o_ref[:, cs].astype(F32)
            dyv = dy_ref[:, cs]
            sg = _sigmoid(g)
            do = dyv * (g * sg)
            do_ref[:, cs] = do.astype(BF16)
            dg_ref[:, cs] = (dyv * oo * (sg * (1.0 + g * (1.0 - sg)))).astype(BF16)
            hi, mid, lo = _split3(-jnp.sum(do * oo, axis=1, keepdims=True))
            da = jnp.where(lane == 0, hi, jnp.where(lane == 1, mid, jnp.where(lane == 2, lo, 0.0)))
            da_ref[:, cs] = da.astype(BF16)

    row = lambda blk: pl.BlockSpec((tm, W), lambda i: (i, blk))
    return _pcall(
        body, name=name, grid=(S // tm,),
        in_specs=[row(0), row(0), row(1)],
        out_specs=[row(0), row(1), row(0)],
        out_shape=[jax.ShapeDtypeStruct((S, W), BF16), jax.ShapeDtypeStruct((S, 2 * W), BF16),
                   jax.ShapeDtypeStruct((S, W), BF16)],
        compiler_params=_cparams(("parallel",)),
    )(dy, o, p1)


def attn_bwd(p1, qb, do, da, p2, ka, dp1, n_heads, scale, name, exchange=(), query_end=None):
    S = p1.shape[0]
    W = n_heads * HEAD_DIM
    t = _attn_tile(S)
    nb = S // t
    H = n_heads
    ne = len(exchange)
    assert not (ne and query_end is not None)

    def body(*refs):
        if query_end is None:
            core(nb, *refs)
        else:
            qe = refs[0][pl.program_id(0) * nb + pl.program_id(1)]
            core(jnp.clip(qe, pl.program_id(1) + 1, nb), *refs[1:])

    def core(iend, q_ref, qb_ref, do_ref, da_ref, k_ref, v_ref, ka_ref, dp1_in, *rest):
        del dp1_in
        parts = rest[:ne]
        dq_ref, dkv_ref, dc_ref = rest[ne:ne + 3]
        got = rest[ne + 3:2 * ne + 3]
        dq_acc, dk_acc, dv_acc = rest[2 * ne + 3:2 * ne + 6]
        h = pl.program_id(0)
        j = pl.program_id(1)
        if ne:
            ssem, rsem = rest[2 * ne + 6:]

            @pl.when((h == 0) & (j == 0))
            def _():
                for cp in _exchange_copies(parts, got, ssem, rsem)[0]:
                    cp.start()

        @pl.when(j == 0)
        def _():
            dq_acc[...] = jnp.zeros_like(dq_acc)

        @pl.when((j == 0) & (h == 0))
        def _():
            dc_ref[...] = jnp.zeros_like(dc_ref)

        lane = lax.broadcasted_iota(jnp.int32, (t, HEAD_DIM), 1)
        ones3 = jnp.where(lane < 3, 1.0, 0.0).astype(BF16)
        kk = jnp.concatenate([k_ref[...], ka_ref[...]], axis=1)
        vv = jnp.concatenate([v_ref[...], ones3], axis=1)
        dk_acc[...] = jnp.zeros_like(dk_acc)
        dv_acc[...] = jnp.zeros_like(dv_acc)

        def block(start, width, masked):
            rows = pl.ds(pl.multiple_of(start, t), width)
            qq = jnp.concatenate([q_ref[rows, :], qb_ref[rows, :]], axis=1)
            dd = jnp.concatenate([do_ref[rows, :], da_ref[rows, :]], axis=1)
            a = lax.dot_general(qq, kk, _DN["nt"], preferred_element_type=F32)
            p = jnp.exp(scale * a)
            if masked:
                r = lax.broadcasted_iota(jnp.int32, (t, t), 0)
                c = lax.broadcasted_iota(jnp.int32, (t, t), 1)
                p = jnp.where(r >= c, p, 0.0)
            dpd = lax.dot_general(dd, vv, _DN["nt"], preferred_element_type=F32)
            ds = (p * dpd).astype(BF16)
            pb = p.astype(BF16)
            dv_acc[...] += lax.dot_general(pb, dd, _DN["tn"], preferred_element_type=F32)
            dk_acc[...] += lax.dot_general(ds, qq, _DN["tn"], preferred_element_type=F32)
            dq_acc[rows, :] += jnp.dot(ds, kk, preferred_element_type=F32)

        block(j * t, t, True)
        n_after = iend - 1 - j

        @pl.when((n_after & 1) != 0)
        def _():
            block((j + 1) * t, t, False)

        first = j + 1 + (n_after & 1)

        def loop_body(ii, carry):
            block((first + 2 * ii) * t, 2 * t, False)
            return carry

        lax.fori_loop(0, n_after // 2, loop_body, 0)

        dkv_ref[...] = jnp.concatenate([dk_acc[:, :HEAD_DIM] * scale, dv_acc[:, :HEAD_DIM]], axis=1).astype(BF16)
        colsum = dk_acc[:, HEAD_DIM + 3:HEAD_DIM + 4]
        krows = pl.ds(pl.multiple_of(j * t, t), t)
        dc_ref[krows, :] += jnp.where(lane == h, -colsum, 0.0)

        @pl.when(j == nb - 1)
        def _():
            dq_ref[...] = (dq_acc[:, :HEAD_DIM] * scale).astype(BF16)
            lane_s = lax.broadcasted_iota(jnp.int32, (S, HEAD_DIM), 1)
            dc_ref[...] += jnp.where(lane_s == h, dq_acc[:, HEAD_DIM:HEAD_DIM + 1], 0.0)

        if ne:
            @pl.when((h == H - 1) & (j == nb - 1))
            def _():
                sends, lands = _exchange_copies(parts, got, ssem, rsem)
                for cp in lands:
                    cp.wait_recv()
                for cp in sends:
                    cp.wait_send()

    full = lambda fn: pl.BlockSpec((S, HEAD_DIM), fn)
    ktile = lambda fn: pl.BlockSpec((t, HEAD_DIM), fn)
    hbm = pl.BlockSpec(memory_space=pl.ANY)
    sems = [pltpu.SemaphoreType.DMA((3 * ne,)), pltpu.SemaphoreType.DMA((3 * ne,))] if ne else []
    in_specs = [full(lambda h, j, *_: (0, h)), full(lambda h, j, *_: (0, h)), full(lambda h, j, *_: (0, h)),
                full(lambda h, j, *_: (0, h)),
                ktile(lambda h, j, *_: (j, 2 * h)), ktile(lambda h, j, *_: (j, 2 * h + 1)),
                ktile(lambda h, j, *_: (j, h)), hbm] + [hbm] * ne
    out_specs = [full(lambda h, j, *_: (0, h)),
                 pl.BlockSpec((t, 2 * HEAD_DIM), lambda h, j, *_: (j, h)),
                 pl.BlockSpec((S, 128), lambda h, j, *_: (0, 0))] + [hbm] * ne
    out_shape = ([jax.ShapeDtypeStruct((S, 2 * W), BF16), jax.ShapeDtypeStruct((S, 2 * W), BF16),
                  jax.ShapeDtypeStruct((S, 128), F32)]
                 + [jax.ShapeDtypeStruct((p.shape[0], N_CHIPS - 1) + p.shape[2:], p.dtype) for p in exchange])
    scratch = [pltpu.VMEM((S, 2 * HEAD_DIM), F32), pltpu.VMEM((t, 2 * HEAD_DIM), F32),
               pltpu.VMEM((t, 2 * HEAD_DIM), F32)] + sems
    if query_end is None:
        outs = _pcall(
            body, name=name, grid=(H, nb), in_specs=in_specs, out_specs=out_specs, out_shape=out_shape,
            scratch_shapes=scratch, input_output_aliases={7: 0}, compiler_params=_cparams(("arbitrary", "arbitrary")),
        )(p1, qb, do, da, p2, p2, ka, dp1, *exchange)
    else:
        grid_spec = pltpu.PrefetchScalarGridSpec(num_scalar_prefetch=1, grid=(H, nb), in_specs=in_specs,
                                                 out_specs=out_specs, scratch_shapes=scratch)
        outs = _pcall(body, name=name, grid_spec=grid_spec, out_shape=out_shape, input_output_aliases={8: 0},
                      compiler_params=_cparams(("arbitrary", "arbitrary")),
                      )(query_end, p1, qb, do, da, p2, p2, ka, dp1)
    return outs[0], outs[1], outs[2], list(outs[3:])


def fox_decay_bwd(dc, pf, bf, n_heads, name):
    S = dc.shape[0]
    tm = _tile(S, 256, 8)
    nb = S // tm

    def body(dc_ref, pf_ref, bf_ref, dz_ref, db_ref, carry_ref):
        @pl.when(pl.program_id(0) == 0)
        def _():
            carry_ref[...] = jnp.zeros_like(carry_ref)
            db_ref[...] = jnp.zeros_like(db_ref)

        row = lax.broadcasted_iota(jnp.int32, (tm, tm), 0)
        col = lax.broadcasted_iota(jnp.int32, (tm, tm), 1)
        tri = (row <= col).astype(F32)
        dlogf = jnp.dot(tri, dc_ref[...], precision=lax.Precision.HIGHEST, preferred_element_type=F32) + carry_ref[...]
        carry_ref[...] = dlogf[0:1, :]
        z = pf_ref[...] + bf_ref[...]
        lane = lax.broadcasted_iota(jnp.int32, (tm, 128), 1)
        dz = jnp.where(lane < n_heads, dlogf * _sigmoid(-z), 0.0)
        dz_ref[...] = dz.astype(BF16)
        db_ref[...] += jnp.sum(dz, axis=0, keepdims=True)

    rev = pl.BlockSpec((tm, 128), lambda i: (nb - 1 - i, 0))
    vec = pl.BlockSpec((1, 128), lambda i: (0, 0))
    return _pcall(
        body, name=name, grid=(nb,),
        in_specs=[rev, rev, vec], out_specs=[rev, vec],
        out_shape=[jax.ShapeDtypeStruct((S, 128), BF16), jax.ShapeDtypeStruct((1, 128), F32)],
        scratch_shapes=[pltpu.VMEM((1, 128), F32)],
        compiler_params=_cparams(("arbitrary",)),
    )(dc, pf, bf)


def _conv_tile(S):
    return _tile(S, 256, HALO)


def _fill_glu(ubuf, a_ref, b_ref, ah_ref, bh_ref, first, tm):
    uh = ah_ref[...].astype(F32) * _sigmoid(bh_ref[...].astype(F32))
    ubuf[0:HALO, :] = jnp.where(first, 0.0, uh)
    ubuf[HALO:HALO + tm, :] = a_ref[...].astype(F32) * _sigmoid(b_ref[...].astype(F32))


def conv_fwd(proj, dw, dwb, lng, lnb, name):
    S = proj.shape[0]
    C = proj.shape[1] // 3
    tm = _conv_tile(S)
    hb = tm // HALO
    nch = C // 128
    rb = _tile(tm, 128, 8)

    def body(a_ref, b_ref, g_ref, ah_ref, bh_ref, dw_ref, dwb_ref, lng_ref, lnb_ref, u2_ref, y_ref, ubuf, sh):
        i = pl.program_id(0)
        _fill_glu(ubuf, a_ref, b_ref, ah_ref, bh_ref, i == 0, tm)

        def chunk(cc, carry):
            cols = pl.ds(pl.multiple_of(cc * 128, 128), 128)
            for r0 in range(0, tm, rb):
                acc = jnp.broadcast_to(dwb_ref[:, cols], (rb, 128))
                for b in range(8):
                    taps = list(range(b, CONV_K, 8))
                    n = rb + 8 * (len(taps) - 1)
                    sh[0:n, :] = ubuf[pl.ds(HALO - (CONV_K - 1) + b + r0, n), cols]
                    for a, k in enumerate(taps):
                        acc = acc + dw_ref[k:k + 1, cols] * sh[8 * a:8 * a + rb, :]
                u2_ref[pl.ds(r0, rb), cols] = acc
            return carry

        lax.fori_loop(0, nch, chunk, 0)
        x = u2_ref[...]
        mu = jnp.mean(x, axis=-1, keepdims=True)
        xc = x - mu
        var = jnp.mean(xc * xc, axis=-1, keepdims=True)
        ln = xc * lax.rsqrt(var + LN_EPS) * lng_ref[...] + lnb_ref[...]
        gate = g_ref[...].astype(F32)
        y_ref[...] = ((ln * _sigmoid(ln)) * (gate * _sigmoid(gate))).astype(BF16)

    blk = lambda cb: pl.BlockSpec((tm, C), lambda i: (i, cb))
    halo = lambda cb: pl.BlockSpec((HALO, C), lambda i: (jnp.maximum(i * hb - 1, 0), cb))
    vec = pl.BlockSpec((1, C), lambda i: (0, 0))
    return _pcall(
        body, name=name, grid=(S // tm,),
        in_specs=[blk(0), blk(1), blk(2), halo(0), halo(1),
                  pl.BlockSpec((CONV_K, C), lambda i: (0, 0)), vec, vec, vec],
        out_specs=[blk(0), blk(0)],
        out_shape=[jax.ShapeDtypeStruct((S, C), F32), jax.ShapeDtypeStruct((S, C), BF16)],
        scratch_shapes=[pltpu.VMEM((HALO + tm, C), F32), pltpu.VMEM((rb + HALO, 128), F32)],
        compiler_params=_cparams(("parallel",)),
    )(proj, proj, proj, proj, proj, dw, dwb, lng, lnb)


def conv_bwd_norm(dy, proj, u2, lng, lnb, name):
    S, C = dy.shape
    tm = _tile(S, 256, 8)

    def body(dy_ref, g_ref, u2_ref, lng_ref, lnb_ref, du2_ref, dg_ref, sm_ref):
        x = u2_ref[...]
        mu = jnp.mean(x, axis=-1, keepdims=True)
        xc = x - mu
        var = jnp.mean(xc * xc, axis=-1, keepdims=True)
        rs = lax.rsqrt(var + LN_EPS)
        xhat = xc * rs
        gam = lng_ref[...]
        ln = xhat * gam + lnb_ref[...]
        sl = _sigmoid(ln)
        u3 = ln * sl
        gate = g_ref[...].astype(F32)
        sg = _sigmoid(gate)
        dyv = dy_ref[...]
        dgate = dyv * u3 * (sg * (1.0 + gate * (1.0 - sg)))
        dln = (dyv * (gate * sg)) * (sl * (1.0 + ln * (1.0 - sl)))
        dxh = dln * gam
        du2 = rs * (dxh - jnp.mean(dxh, axis=-1, keepdims=True)
                    - xhat * jnp.mean(dxh * xhat, axis=-1, keepdims=True))
        du2_ref[...] = du2
        dg_ref[...] = dgate.astype(BF16)

        @pl.when(pl.program_id(0) == 0)
        def _():
            sm_ref[...] = jnp.zeros_like(sm_ref)

        sm_ref[0:1, :] += jnp.sum(dln * xhat, axis=0, keepdims=True)
        sm_ref[1:2, :] += jnp.sum(dln, axis=0, keepdims=True)
        sm_ref[2:3, :] += jnp.sum(du2, axis=0, keepdims=True)
        sm_ref[3:4, :] += jnp.sum(dgate, axis=0, keepdims=True)

    blk = lambda cb: pl.BlockSpec((tm, C), lambda i: (i, cb))
    vec = pl.BlockSpec((1, C), lambda i: (0, 0))
    return _pcall(
        body, name=name, grid=(S // tm,),
        in_specs=[blk(0), blk(2), blk(0), vec, vec],
        out_specs=[blk(0), blk(2), pl.BlockSpec((8, C), lambda i: (0, 0))],
        out_shape=[jax.ShapeDtypeStruct((S, C), F32), jax.ShapeDtypeStruct((S, 3 * C), BF16),
                   jax.ShapeDtypeStruct((8, C), F32)],
        compiler_params=_cparams(("arbitrary",)),
    )(dy, proj, u2, lng, lnb)


def conv_bwd_taps(du2, proj, dw, dproj, name):
    S, C = du2.shape
    tm = _conv_tile(S)
    hb = tm // HALO
    nb = S // tm
    nch = C // 128
    rb = _tile(tm, 128, 8)

    def body(d_ref, dh_ref, a_ref, b_ref, ah_ref, bh_ref, dw_ref, dp_in, dab_ref, sm_ref, ubuf, dbuf, sh, sh2):
        del dp_in
        i = pl.program_id(0)
        _fill_glu(ubuf, a_ref, b_ref, ah_ref, bh_ref, i == 0, tm)
        dbuf[0:tm, :] = d_ref[...]
        dbuf[tm:tm + HALO, :] = jnp.where(i == nb - 1, 0.0, dh_ref[...])

        @pl.when(i == 0)
        def _():
            sm_ref[...] = jnp.zeros_like(sm_ref)

        def chunk(cc, carry):
            cols = pl.ds(pl.multiple_of(cc * 128, 128), 128)
            cols_b = pl.ds(pl.multiple_of(C + cc * 128, 128), 128)
            for r0 in range(0, tm, rb):
                d0 = dbuf[r0:r0 + rb, cols]
                du = jnp.zeros((rb, 128), F32)
                for b in range(8):
                    offs = list(range(b, CONV_K, 8))
                    n = rb + 8 * (len(offs) - 1)
                    sh[0:n, :] = dbuf[pl.ds(r0 + b, n), cols]
                    for a, o in enumerate(offs):
                        k = CONV_K - 1 - o
                        du = du + dw_ref[k:k + 1, cols] * sh[8 * a:8 * a + rb, :]
                    sh2[0:n, :] = ubuf[pl.ds(HALO - (CONV_K - 1) + b + r0, n), cols]
                    for a, k in enumerate(offs):
                        sm_ref[k:k + 1, cols] += jnp.sum(d0 * sh2[8 * a:8 * a + rb, :], axis=0, keepdims=True)
                rows = pl.ds(r0, rb)
                av = a_ref[rows, cols].astype(F32)
                sb = _sigmoid(b_ref[rows, cols].astype(F32))
                da = du * sb
                db = du * av * sb * (1.0 - sb)
                dab_ref[rows, cols] = da.astype(BF16)
                dab_ref[rows, cols_b] = db.astype(BF16)
                sm_ref[32:33, cols] += jnp.sum(da, axis=0, keepdims=True)
                sm_ref[33:34, cols] += jnp.sum(db, axis=0, keepdims=True)
            return carry

        lax.fori_loop(0, nch, chunk, 0)

    blk = lambda cb: pl.BlockSpec((tm, C), lambda i: (i, cb))
    halo = lambda cb: pl.BlockSpec((HALO, C), lambda i: (jnp.maximum(i * hb - 1, 0), cb))
    nxt = pl.BlockSpec((HALO, C), lambda i: (jnp.minimum((i + 1) * hb, nb * hb - 1), 0))
    return _pcall(
        body, name=name, grid=(nb,),
        in_specs=[blk(0), nxt, blk(0), blk(1), halo(0), halo(1),
                  pl.BlockSpec((CONV_K, C), lambda i: (0, 0)), pl.BlockSpec(memory_space=pl.ANY)],
        out_specs=[pl.BlockSpec((tm, 2 * C), lambda i: (i, 0)), pl.BlockSpec((40, C), lambda i: (0, 0))],
        out_shape=[jax.ShapeDtypeStruct((S, 3 * C), BF16), jax.ShapeDtypeStruct((40, C), F32)],
        scratch_shapes=[pltpu.VMEM((HALO + tm, C), F32), pltpu.VMEM((tm + HALO, C), F32),
                        pltpu.VMEM((rb + HALO, 128), F32), pltpu.VMEM((rb + HALO, 128), F32)],
        input_output_aliases={7: 0},
        compiler_params=_cparams(("arbitrary",)),
    )(du2, du2, proj, proj, proj, proj, dw, dproj)


def _rows_tile(R, Cc, budget=1 << 18):
    cap = max(8, budget // max(Cc, 1))
    if R <= cap:
        return R
    t = (cap // 8) * 8
    while t >= 8:
        if R % t == 0:
            return t
        t -= 8
    return R


def cast_into_slot(w, chip, name, l0, nl):
    _, R, Cc = w.shape
    tr = _rows_tile(R, Cc)

    def body(s_ref, w_ref, o_ref):
        del s_ref
        o_ref[...] = w_ref[...].astype(BF16)

    grid_spec = pltpu.PrefetchScalarGridSpec(
        num_scalar_prefetch=1, grid=(nl, R // tr),
        in_specs=[pl.BlockSpec((None, tr, Cc), lambda l, r, s: (l0 + l, r, 0))],
        out_specs=pl.BlockSpec((None, None, tr, Cc), lambda l, r, s: (l, s[0], r, 0)),
    )
    return _pcall(
        body, name=name, grid_spec=grid_spec, out_shape=jax.ShapeDtypeStruct((nl, N_CHIPS, R, Cc), BF16),
        compiler_params=_cparams(("parallel", "parallel")),
    )(chip, w)


def pair_sum(g, rcv, cidx, name):
    L, K, _, half, Cc = g.shape
    g5 = g
    tr = _rows_tile(half, Cc)

    def body(c_ref, g_ref, r_ref, o_ref):
        del c_ref
        o_ref[...] = (g_ref[...] + r_ref[...]).astype(BF16)

    grid_spec = pltpu.PrefetchScalarGridSpec(
        num_scalar_prefetch=1, grid=(L, K, half // tr),
        in_specs=[pl.BlockSpec((None, None, None, tr, Cc), lambda l, k, r, c: (l, k, c[0], r, 0)),
                  pl.BlockSpec((None, None, tr, Cc), lambda l, k, r, c: (l, k, r, 0))],
        out_specs=pl.BlockSpec((None, None, tr, Cc), lambda l, k, r, c: (l, k, r, 0)),
    )
    return _pcall(
        body, name=name, grid_spec=grid_spec, out_shape=jax.ShapeDtypeStruct((L, K, half, Cc), BF16),
        compiler_params=_cparams(("parallel", "parallel", "parallel")),
    )(cidx, g5, rcv)


def chip_sum(parts, got, sel, name, layers, l0, prev=None):
    Lp, _, R, Cc = parts.shape
    n_got = got.shape[1]
    tr = _rows_tile(R, Cc)

    def body(s_ref, p_ref, g_ref, *rest):
        del s_ref
        o_ref = rest[-1]
        acc = p_ref[...].astype(F32)
        for k in range(n_got):
            acc = acc + g_ref[k].astype(F32)
        o_ref[...] = acc

    in_specs = [pl.BlockSpec((None, None, tr, Cc), lambda l, r, s: (l, s[0], r, 0)),
                pl.BlockSpec((None, n_got, tr, Cc), lambda l, r, s: (l, 0, r, 0))]
    ops = [sel, parts, got]
    aliases = {}
    if prev is not None:
        in_specs.append(pl.BlockSpec(memory_space=pl.ANY))
        ops.append(prev)
        aliases = {3: 0}
    grid_spec = pltpu.PrefetchScalarGridSpec(
        num_scalar_prefetch=1, grid=(Lp, R // tr), in_specs=in_specs,
        out_specs=pl.BlockSpec((None, None, tr, Cc), lambda l, r, s: (l0 + l, s[1], r, 0)),
    )
    return _pcall(
        body, name=name, grid_spec=grid_spec, out_shape=jax.ShapeDtypeStruct((layers, 2, R, Cc), F32),
        input_output_aliases=aliases, compiler_params=_cparams(("parallel", "parallel")),
    )(*ops)


def dev_sum(parts, name):
    K, R, Cc = parts.shape

    def body(p_ref, o_ref):
        acc = p_ref[0]
        for k in range(1, K):
            acc = acc + p_ref[k]
        o_ref[...] = acc

    return _pcall(
        body, name=name, grid=(R // 8,),
        in_specs=[pl.BlockSpec((K, 8, Cc), lambda r: (0, r, 0))],
        out_specs=pl.BlockSpec((8, Cc), lambda r: (r, 0)),
        out_shape=jax.ShapeDtypeStruct((R, Cc), F32), compiler_params=_cparams(("parallel",)),
    )(parts)


def adamw(w, g, m, v, name):
    shape = w.shape
    if w.ndim == 3:
        L, R, Cc = shape
    else:
        L, R, Cc = 1, (1 if w.ndim == 1 else shape[0]), shape[-1]
    view = lambda t: t.reshape((L, R, Cc))
    tr = _rows_tile(R, Cc, budget=1 << 17)
    c1 = 1.0 - ADAM_B1 ** ADAM_STEP
    c2 = 1.0 - ADAM_B2 ** ADAM_STEP

    def body(w_ref, g_ref, m_ref, v_ref, d_ref, nm_ref, nv_ref):
        gg = g_ref[...]
        nm = ADAM_B1 * m_ref[...] + (1.0 - ADAM_B1) * gg
        nv = ADAM_B2 * v_ref[...] + (1.0 - ADAM_B2) * (gg * gg)
        d_ref[...] = -ADAM_LR * ((nm / c1) / (jnp.sqrt(nv / c2) + ADAM_EPS) + ADAM_WD * w_ref[...])
        nm_ref[...] = nm
        nv_ref[...] = nv

    spec = pl.BlockSpec((None, tr, Cc), lambda l, r: (l, r, 0))
    outs = _pcall(
        body, name=name, grid=(L, R // tr), in_specs=[spec] * 4, out_specs=[spec] * 3,
        out_shape=[jax.ShapeDtypeStruct((L, R, Cc), F32)] * 3, compiler_params=_cparams(("parallel", "parallel")),
    )(view(w), view(g), view(m), view(v))
    return tuple(o.reshape(shape) for o in outs)


def _place():
    x, y, c = lax.axis_index("x"), lax.axis_index("y"), lax.axis_index("c")
    other_chips = [(1 - x, y), (x, 1 - y), (1 - x, 1 - y)]
    return x, y, c, other_chips


def _rcopy(src, dst, ssem, rsem, k, to):
    return pltpu.make_async_remote_copy(src_ref=src, dst_ref=dst, send_sem=ssem.at[k], recv_sem=rsem.at[k],
                                        device_id=to, device_id_type=MESH)


def gather_weights(slots, small):
    nt = len(slots)

    def body(*refs):
        outs, small_out = refs[nt + 1:2 * nt + 1], refs[2 * nt + 1]
        ssem, rsem = refs[2 * nt + 2:]
        x, y, c, chips = _place()
        me = 2 * x + y
        sib = (x, y, 1 - c)

        def rows(t, half_of):
            half = outs[t].shape[2] // 2
            return pl.ds(half_of * half, half)

        first, passed = [], []
        for t in range(nt):
            mine = outs[t].at[:, me, rows(t, c), :]
            for j, chip in enumerate(chips):
                first.append(_rcopy(mine, mine, ssem, rsem, 6 * t + j, (*chip, c)))
        for j, chip in enumerate(chips):
            first.append(_rcopy(small_out.at[me], small_out.at[me], ssem, rsem, 6 * nt + j, (*chip, c)))
        for cp in first:
            cp.start()
        for t in range(nt):
            for j, (px, py) in enumerate(chips):
                land = outs[t].at[:, 2 * px + py, rows(t, c), :]
                _rcopy(land, land, ssem, rsem, 6 * t + j, (x, y, c)).wait_recv()
                fw = _rcopy(land, land, ssem, rsem, 6 * t + 3 + j, sib)
                fw.start()
                passed.append(fw)
        for j, (px, py) in enumerate(chips):
            land = small_out.at[2 * px + py]
            _rcopy(land, land, ssem, rsem, 6 * nt + j, (x, y, c)).wait_recv()
        for t in range(nt):
            for j, (px, py) in enumerate(chips):
                land = outs[t].at[:, 2 * px + py, rows(t, 1 - c), :]
                _rcopy(land, land, ssem, rsem, 6 * t + 3 + j, (x, y, c)).wait_recv()
        for cp in first + passed:
            cp.wait_send()

    ops = list(slots) + [small]
    nsem = 6 * nt + 3
    return _pcall(
        body, name="gather_weights", out_shape=[jax.ShapeDtypeStruct(s.shape, s.dtype) for s in ops],
        in_specs=[pl.BlockSpec(memory_space=pl.ANY)] * (nt + 1),
        out_specs=[pl.BlockSpec(memory_space=pl.ANY)] * (nt + 1),
        input_output_aliases={n: n for n in range(nt + 1)},
        scratch_shapes=[pltpu.SemaphoreType.DMA((nsem,)), pltpu.SemaphoreType.DMA((nsem,))],
    )(*ops)


def swap_halves(grads, name):
    nt = len(grads)

    def body(*refs):
        ins, outs = refs[:nt], refs[nt:2 * nt]
        ssem, rsem = refs[2 * nt:]
        x, y, c, _ = _place()
        sib = (x, y, 1 - c)
        cps = [_rcopy(ins[t].at[:, :, 1 - c], outs[t], ssem, rsem, t, sib) for t in range(nt)]
        for cp in cps:
            cp.start()
        for cp in cps:
            cp.wait()

    out_shape = [jax.ShapeDtypeStruct(g.shape[:2] + g.shape[3:], g.dtype) for g in grads]
    return _pcall(
        body, name=name, out_shape=out_shape,
        in_specs=[pl.BlockSpec(memory_space=pl.ANY)] * nt, out_specs=[pl.BlockSpec(memory_space=pl.ANY)] * nt,
        scratch_shapes=[pltpu.SemaphoreType.DMA((nt,)), pltpu.SemaphoreType.DMA((nt,))],
    )(*grads)


def _exchange_copies(ins, outs, ssem, rsem):
    x, y, c, chips = _place()
    sends, lands = [], []
    for t in range(len(ins)):
        for j, (px, py) in enumerate(chips):
            k = 3 * t + j
            sends.append(_rcopy(ins[t].at[:, 2 * px + py], outs[t].at[:, j], ssem, rsem, k, (px, py, c)))
            land = outs[t].at[:, j]
            lands.append(_rcopy(land, land, ssem, rsem, k, (x, y, c)))
    return sends, lands


def _gather_direct_copies(bufs, ssem, rsem):
    x, y, c, chips = _place()
    me = 2 * x + y
    sends, lands = [], []
    for t, buf in enumerate(bufs):
        mine = buf.at[:, me]
        for j, (px, py) in enumerate(chips):
            k = 3 * t + j
            sends.append(_rcopy(mine, mine, ssem, rsem, k, (px, py, c)))
            land = buf.at[:, 2 * px + py]
            lands.append(_rcopy(land, land, ssem, rsem, k, (x, y, c)))
    return sends, lands


def exchange_partials(parts, small):
    nt = len(parts)

    def body(*refs):
        ins = refs[:nt]
        outs, small_out = refs[nt + 1:2 * nt + 1], refs[2 * nt + 1]
        ssem, rsem = refs[2 * nt + 2:]
        x, y, c, chips = _place()
        dev = 4 * x + 2 * y + c
        sends, lands = _exchange_copies(ins, outs, ssem, rsem)
        peers = [(px, py, pc) for pc in (c, 1 - c) for (px, py) in [(x, y)] + chips][1:]
        for j, (px, py, pc) in enumerate(peers):
            k = 3 * nt + j
            sends.append(_rcopy(small_out.at[dev], small_out.at[dev], ssem, rsem, k, (px, py, pc)))
            land = small_out.at[4 * px + 2 * py + pc]
            lands.append(_rcopy(land, land, ssem, rsem, k, (x, y, c)))
        for cp in sends:
            cp.start()
        for cp in lands:
            cp.wait_recv()
        for cp in sends:
            cp.wait_send()

    out_shape = [jax.ShapeDtypeStruct((p.shape[0], N_CHIPS - 1) + p.shape[2:], p.dtype) for p in parts]
    out_shape.append(jax.ShapeDtypeStruct(small.shape, small.dtype))
    nsem = 3 * nt + 7
    return _pcall(
        body, name="exchange_partials", out_shape=out_shape,
        in_specs=[pl.BlockSpec(memory_space=pl.ANY)] * (nt + 1),
        out_specs=[pl.BlockSpec(memory_space=pl.ANY)] * (nt + 1),
        input_output_aliases={nt: nt},
        scratch_shapes=[pltpu.SemaphoreType.DMA((nsem,)), pltpu.SemaphoreType.DMA((nsem,))],
    )(*parts, small)


def join_halves(halves):
    nt = len(halves)

    def body(*refs):
        outs = refs[nt:2 * nt]
        ssem, rsem = refs[2 * nt:]
        x, y, c, _ = _place()
        sib = (x, y, 1 - c)
        sends = [_rcopy(outs[t].at[:, c], outs[t].at[:, c], ssem, rsem, t, sib) for t in range(nt)]
        for cp in sends:
            cp.start()
        for t in range(nt):
            land = outs[t].at[:, 1 - c]
            _rcopy(land, land, ssem, rsem, t, (x, y, c)).wait_recv()
        for cp in sends:
            cp.wait_send()

    return _pcall(
        body, name="join_halves", out_shape=[jax.ShapeDtypeStruct(h.shape, h.dtype) for h in halves],
        in_specs=[pl.BlockSpec(memory_space=pl.ANY)] * nt, out_specs=[pl.BlockSpec(memory_space=pl.ANY)] * nt,
        input_output_aliases={n: n for n in range(nt)},
        scratch_shapes=[pltpu.SemaphoreType.DMA((nt,)), pltpu.SemaphoreType.DMA((nt,))],
    )(*halves)


def _pad_cols(a, n):
    return jnp.pad(a, [(0, 0)] * (a.ndim - 1) + [(0, n - a.shape[-1])])


def kernel(x, norm_g, fox_w_in, fox_b_f, fox_w_out, conv_w_in, conv_b_in, conv_dw, conv_dw_b, conv_ln_g, conv_ln_b, conv_w_out, final_norm_g, loss_target, m_norm_g, m_fox_w_in, m_fox_b_f, m_fox_w_out, m_conv_w_in, m_conv_b_in, m_conv_dw, m_conv_dw_b, m_conv_ln_g, m_conv_ln_b, m_conv_w_out, m_final_norm_g, v_norm_g, v_fox_w_in, v_fox_b_f, v_fox_w_out, v_conv_w_in, v_conv_b_in, v_conv_dw, v_conv_dw_b, v_conv_ln_g, v_conv_ln_b, v_conv_w_out, v_final_norm_g):
    S, D = x.shape[1], x.shape[2]
    H = fox_b_f.shape[1]
    assert D == H * HEAD_DIM, "one head must be one lane tile"
    W = C = D
    NL = fox_w_in.shape[0]
    Dq = D // N_CHIPS
    NA = fox_w_in.shape[2]
    scale = HEAD_DIM ** -0.5
    chip = 2 * lax.axis_index("x") + lax.axis_index("y")
    core = lax.axis_index("c")
    cidx = core.astype(jnp.int32).reshape((1,))
    chip1 = chip.astype(jnp.int32).reshape((1,))
    sel = jnp.stack([chip, core]).astype(jnp.int32)

    small_pack = jnp.concatenate([
        conv_b_in.reshape((NL * 3, Dq)), conv_dw.reshape((NL * CONV_K, Dq)), conv_dw_b, conv_ln_g, conv_ln_b,
        jnp.zeros((PACK_ROWS - NL * (3 + CONV_K + 3), Dq), F32)], axis=0)
    small_slots = lax.dynamic_update_slice(jnp.zeros((N_CHIPS, PACK_ROWS, Dq), F32), small_pack[None], (chip, 0, 0))
    ga0, gb0, gsmall = gather_weights(
        [cast_into_slot(fox_w_in, chip1, "cast_fox_w_in_0", 0, 1),
         cast_into_slot(fox_w_out, chip1, "cast_fox_w_out_0", 0, 1)], small_slots)
    later = [cast_into_slot(fox_w_in, chip1, "cast_fox_w_in_1", 1, NL - 1),
             cast_into_slot(fox_w_out, chip1, "cast_fox_w_out_1", 1, NL - 1),
             cast_into_slot(conv_w_in, chip1, "cast_conv_w_in", 0, NL),
             cast_into_slot(conv_w_out, chip1, "cast_conv_w_out", 0, NL)]

    def fox_weights(ga, gb):
        n = ga.shape[0]
        wfull = jnp.transpose(ga, (0, 2, 1, 3)).reshape((n, D, N_CHIPS * NA))
        wq, wk, wv, wg, wf = (wfull[:, :, 0:W], wfull[:, :, W:2 * W], wfull[:, :, 2 * W:3 * W],
                              wfull[:, :, 3 * W:4 * W], wfull[:, :, 4 * W:])
        w1 = jnp.concatenate([wq, wg], axis=-1).reshape((n, 1, D, 2 * W))
        w2 = jnp.stack([wk.reshape((n, D, H, HEAD_DIM)), wv.reshape((n, D, H, HEAD_DIM))], axis=3)
        return w1, w2.reshape((n, 1, D, 2 * W)), _pad_cols(wf, 128).reshape((n, 1, D, 128)), gb.reshape((n, 1, W, D))

    fox_w = {0: (fox_weights(ga0, gb0), 0)}
    wc = wo_conv = None
    b_in = gsmall[:, 0:3 * NL, :].reshape((N_CHIPS, NL, 3 * Dq)).transpose((1, 0, 2)).reshape((NL, 3 * C))
    dwt = gsmall[:, 3 * NL:3 * NL + CONV_K * NL, :].reshape((N_CHIPS, NL, CONV_K, Dq))
    dwt = dwt.transpose((1, 2, 0, 3)).reshape((NL, CONV_K, C))
    r0 = (3 + CONV_K) * NL
    vecs = gsmall[:, r0:r0 + 3 * NL, :].reshape((N_CHIPS, 3, NL, Dq)).transpose((1, 2, 0, 3)).reshape((3, NL, C))
    dwb, lng, lnb = vecs[0], vecs[1], vecs[2]
    bfp = _pad_cols(fox_b_f, 128)

    h = x.reshape((S, D))
    tgt = loss_target.reshape((S, D))
    saved = []
    n_layers = norm_g.shape[0]
    for i in range(n_layers):
        j = i // 2
        g_i = norm_g[i:i + 1]
        hn = rms_fwd(h, g_i, f"rms_fwd_{i}")
        if i % 2 == 0:
            (w1, w2, wfp, wo_fox), wl = fox_w[j]
            p1 = mm_nn(hn, w1, wl, 1, out_dtype=BF16, name=f"fox_proj_qg_{i}")
            p2 = mm_nn(hn, w2, wl, 1, out_dtype=BF16, name=f"fox_proj_kv_{i}")
            pf = mm_nn(hn, wfp, wl, 1, out_dtype=F32, name=f"fox_proj_f_{i}")
            qa, ka, c0, c1 = fox_decay(pf, bfp[j:j + 1], H, scale, f"fox_decay_{i}")
            first_key = query_end = None
            if i > 0:
                qn, kn = attn_norms(p1, p2, H, f"attn_norms_{i}")
                first_key, query_end = prune_tables(qn, kn, c0, c1, H, scale)
            o, yv, qb, filled = attn_fwd(p1, p2, qa, ka, H, scale, f"attn_fwd_{i}", gather=later if i == 0 else (),
                                         first_key=first_key)
            if i == 0:
                ga1, gb1, wc, gd = filled
                wo_conv = gd.reshape((NL, 1, C, D))
                rest = fox_weights(ga1, gb1)
                for jj in range(1, NL):
                    fox_w[jj] = (rest, jj - 1)
            h_new = mm_nt_res(yv, wo_fox, wl, h, f"fox_out_{i}")
            saved.append((h, hn, p1, p2, pf, ka, o, yv, qb, query_end))
        else:
            proj = mm_nn(hn, wc, j, N_CHIPS, out_dtype=BF16, name=f"conv_proj_{i}", bias=b_in[j:j + 1])
            u2, yv = conv_fwd(proj, dwt[j], dwb[j:j + 1], lng[j:j + 1], lnb[j:j + 1], f"conv_fwd_{i}")
            h_new = mm_nt_res(yv, wo_conv, j, h, f"conv_out_{i}")
            saved.append((h, hn, proj, u2, yv))
        h = h_new

    dh, d_gf, loss_part = loss_head(h, final_norm_g.reshape((1, D)), tgt, "loss_head")

    d_norm = [None] * n_layers
    d_fox_b = [None] * NL
    d_conv_small = [None] * NL
    wgrad = {}

    def pair_sums(keys, tag):
        gs = [wgrad[k].reshape(wgrad[k].shape[:2] + (2, wgrad[k].shape[2] // 2, wgrad[k].shape[3])) for k in keys]
        rcv = swap_halves(gs, f"swap_halves_{tag}")
        return {k: pair_sum(g, r, cidx, f"pair_sum_{k[0]}{k[1]}") for k, g, r in zip(keys, gs, rcv)}

    parts, got = {}, {}
    for i in reversed(range(n_layers)):
        j = i // 2
        g_i = norm_g[i:i + 1]
        if i % 2 == 0:
            h_in, hn, p1, p2, pf, ka, o, yv, qb, query_end = saved[i]
            (w1, w2, wfp, wo_fox), wl = fox_w[j]
            wgrad[("b", j)] = mm_tn(yv, dh, 1, name=f"fox_out_dw_{i}").reshape((1, N_CHIPS, Dq, D))
            dy = mm_nn_t(dh, wo_fox, wl, f"fox_out_dx_{i}")
            do, dp1, da = attn_bwd_prep(dy, o, p1, H, f"attn_bwd_prep_{i}")
            early = ()
            if i == 0:
                parts.update(pair_sums(list(wgrad), "early"))
                early = list(parts)
            dp1, dp2, dc, arrived = attn_bwd(p1, qb, do, da, p2, ka, dp1, H, scale, f"attn_bwd_{i}",
                                             exchange=[parts[k] for k in early], query_end=query_end)
            got.update(dict(zip(early, arrived)))
            dz, dbf = fox_decay_bwd(dc, pf, bfp[j:j + 1], H, f"fox_decay_bwd_{i}")
            d_fox_b[j] = dbf
            dw1 = mm_tn(hn, dp1, 1, name=f"fox_dw_qg_{i}")[0, 0]
            dw2 = mm_tn(hn, dp2, 1, name=f"fox_dw_kv_{i}")[0, 0]
            dwf = mm_tn(hn, dz, 1, name=f"fox_dw_f_{i}")[0, 0]
            dw2 = dw2.reshape((D, H, 2, HEAD_DIM))
            d_in = jnp.concatenate([dw1[:, :W], dw2[:, :, 0].reshape((D, W)), dw2[:, :, 1].reshape((D, W)),
                                    dw1[:, W:], dwf[:, :H]], axis=-1)
            wgrad[("a", j)] = d_in.reshape((D, N_CHIPS, NA)).transpose((1, 0, 2))[None]
            dhn = mm_nt(dp1, w1, wl, 1, name=f"fox_dx_qg_{i}")
            dhn = mm_nt(dp2, w2, wl, 1, name=f"fox_dx_kv_{i}", res=dhn)
            dhn = mm_nt(dz, wfp, wl, 1, name=f"fox_dx_f_{i}", res=dhn)
        else:
            h_in, hn, proj, u2, yv = saved[i]
            wgrad[("d", j)] = mm_tn(yv, dh, 1, name=f"conv_out_dw_{i}").reshape((1, N_CHIPS, Dq, D))
            dy = mm_nn_t(dh, wo_conv, j, f"conv_out_dx_{i}")
            du2, dproj, sm1 = conv_bwd_norm(dy, proj, u2, lng[j:j + 1], lnb[j:j + 1], f"conv_bwd_norm_{i}")
            dproj, sm2 = conv_bwd_taps(du2, proj, dwt[j], dproj, f"conv_bwd_taps_{i}")
            d_conv_small[j] = (sm1, sm2)
            wgrad[("c", j)] = mm_tn(hn, dproj, N_CHIPS, name=f"conv_dw_in_{i}")
            dhn = mm_nt(dproj, wc, j, N_CHIPS, name=f"conv_dx_{i}")
        dh, d_norm[i] = rms_bwd(dhn, h_in, g_i, dh, f"rms_bwd_{i}")

    late = [k for k in wgrad if k not in parts]
    parts.update(pair_sums(late, "late"))

    zrow = jnp.zeros((1, D), F32)
    rows = list(d_norm) + [d_gf]
    rows += [_pad_cols(d_fox_b[l][:, :H], D) for l in range(NL)]
    rows += [_pad_cols(loss_part[:, :1], D)]
    for l in range(NL):
        sm1, sm2 = d_conv_small[l]
        rows += [sm2[32:33], sm2[33:34], sm1[3:4]]
    for l in range(NL):
        rows += [d_conv_small[l][1][0:CONV_K]]
    rows += [d_conv_small[l][0][2:3] for l in range(NL)]
    rows += [d_conv_small[l][0][0:1] for l in range(NL)]
    rows += [d_conv_small[l][0][1:2] for l in range(NL)]
    n_rows = sum(r.shape[0] for r in rows)
    rows += [zrow] * (SMALL_ROWS - n_rows)
    small = jnp.concatenate(rows, axis=0)

    dev = 4 * lax.axis_index("x") + 2 * lax.axis_index("y") + core
    small_slots = lax.dynamic_update_slice(jnp.zeros((N_DEV, SMALL_ROWS, D), F32), small[None], (dev, 0, 0))
    arrived = exchange_partials([parts[k] for k in late], small_slots)
    got.update(dict(zip(late, arrived[:-1])))
    tot = dev_sum(arrived[-1], "dev_sum")
    halves = {}
    for kind in "abcd":
        for l in range(NL):
            halves[kind] = chip_sum(parts[(kind, l)], got[(kind, l)], sel, f"chip_sum_{kind}{l}", NL, l,
                                    prev=halves.get(kind))
    full = join_halves([halves[kind] for kind in "abcd"])
    grad_fox_w_in = full[0].reshape(fox_w_in.shape)
    grad_fox_w_out = full[1].reshape(fox_w_out.shape)
    grad_conv_w_in = full[2].reshape(conv_w_in.shape)
    grad_conv_w_out = full[3].reshape(conv_w_out.shape)

    def mine(v):
        return lax.dynamic_slice_in_dim(v, chip * Dq, Dq, axis=v.ndim - 1)

    r = n_layers
    grad_norm_g = tot[0:r]
    grad_final = tot[r]
    grad_fox_b_f = tot[r + 1:r + 1 + NL, :H]
    loss = tot[r + 1 + NL, 0]
    r = r + 2 + NL
    gb_full = tot[r:r + 3 * NL].reshape((NL, 3 * C))
    grad_conv_b_in = lax.dynamic_slice_in_dim(gb_full, chip * 3 * Dq, 3 * Dq, axis=1)
    r += 3 * NL
    grad_conv_dw = mine(tot[r:r + CONV_K * NL].reshape((NL, CONV_K, C)))
    r += CONV_K * NL
    grad_conv_dw_b = mine(tot[r:r + NL])
    grad_conv_ln_g = mine(tot[r + NL:r + 2 * NL])
    grad_conv_ln_b = mine(tot[r + 2 * NL:r + 3 * NL])

    grads = [grad_norm_g, grad_fox_w_in, grad_fox_b_f, grad_fox_w_out, grad_conv_w_in, grad_conv_b_in,
             grad_conv_dw, grad_conv_dw_b, grad_conv_ln_g, grad_conv_ln_b, grad_conv_w_out, grad_final]
    ws = [norm_g, fox_w_in, fox_b_f, fox_w_out, conv_w_in, conv_b_in, conv_dw, conv_dw_b, conv_ln_g, conv_ln_b,
          conv_w_out, final_norm_g]
    ms = [m_norm_g, m_fox_w_in, m_fox_b_f, m_fox_w_out, m_conv_w_in, m_conv_b_in, m_conv_dw, m_conv_dw_b,
          m_conv_ln_g, m_conv_ln_b, m_conv_w_out, m_final_norm_g]
    vs = [v_norm_g, v_fox_w_in, v_fox_b_f, v_fox_w_out, v_conv_w_in, v_conv_b_in, v_conv_dw, v_conv_dw_b,
          v_conv_ln_g, v_conv_ln_b, v_conv_w_out, v_final_norm_g]
    deltas, new_ms, new_vs = [], [], []
    for n, (w_, g_, m_, v_) in enumerate(zip(ws, grads, ms, vs)):
        d_, nm_, nv_ = adamw(w_, g_, m_, v_, f"adamw_{n}")
        deltas.append(d_)
        new_ms.append(nm_)
        new_vs.append(nv_)
    grad_x = dh.reshape(x.shape)
    return (loss, grad_x, *grads, *deltas, *new_ms, *new_vs)


def mm_nt_res(y, wo, lidx, h, name):
    M, K = y.shape
    N = wo.shape[-1]
    tm = _tile(M, 1024)
    tn = _tile(N, 1024)
    grid = (M // tm, N // tn, 1)
    return _matmul(
        y, wo, contract="nn", grid=grid, name=name,
        a_spec=pl.BlockSpec((tm, K), lambda i, j, k: (i, 0)),
        b_spec=pl.BlockSpec((None, None, K, tn), lambda i, j, k: (lidx, 0, 0, j)),
        o_spec=pl.BlockSpec((tm, tn), lambda i, j, k: (i, j)),
        out_shape=jax.ShapeDtypeStruct((M, N), F32), acc_shape=(tm, tn),
        res=h, res_spec=pl.BlockSpec((tm, tn), lambda i, j, k: (i, j)),
    )


def mm_nn_t(dh, wo, lidx, name):
    M, K = dh.shape
    N = wo.shape[-2]
    tm = _tile(M, 512)
    tn = _tile(N, 1024)
    grid = (M // tm, N // tn, 1)
    return _matmul(
        dh, wo, contract="nt", grid=grid, name=name,
        a_spec=pl.BlockSpec((tm, K), lambda i, j, k: (i, 0)),
        b_spec=pl.BlockSpec((None, None, tn, K), lambda i, j, k: (lidx, 0, j, 0)),
        o_spec=pl.BlockSpec((tm, tn), lambda i, j, k: (i, j)),
        out_shape=jax.ShapeDtypeStruct((M, N), F32), acc_shape=(tm, tn),
    )
```

```python
import jax
import jax.numpy as jnp
from jax import lax
from jax.experimental import pallas as pl
from jax.experimental.pallas import tpu as pltpu

F32 = jnp.float32
BF16 = jnp.bfloat16
MESH = pl.DeviceIdType.MESH

RMS_EPS = 1e-6
LN_EPS = 1e-5
CONV_K = 31
HALO = 32
HEAD_DIM = 128
ADAM_LR = 0.001
ADAM_B1 = 0.9
ADAM_B2 = 0.999
ADAM_EPS = 1e-08
ADAM_WD = 0.01
ADAM_STEP = 10
N_CHIPS = 4
N_DEV = 8
VMEM_LIMIT = 56 * 1024 * 1024
NEG_BIG = -1e30
SMALL_ROWS = 88
PACK_ROWS = 80


def _pcall(body, **kw):
    return pl.pallas_call(body, **kw)


def _cparams(sem=None):
    return pltpu.CompilerParams(dimension_semantics=sem, vmem_limit_bytes=VMEM_LIMIT)


def _tile(n, cap, mult=128):
    if n <= cap:
        return n
    t = (cap // mult) * mult
    while t >= mult:
        if n % t == 0:
            return t
        t -= mult
    raise ValueError(f"no tile for {n} under {cap}")


def _sigmoid(x):
    return 1.0 / (1.0 + jnp.exp(-x))


def _split3(x):
    hi = x.astype(BF16).astype(F32)
    r = x - hi
    mid = r.astype(BF16).astype(F32)
    lo = (r - mid).astype(BF16).astype(F32)
    return hi, mid, lo


_DN = {
    "nn": (((1,), (0,)), ((), ())),
    "nt": (((1,), (1,)), ((), ())),
    "tn": (((0,), (0,)), ((), ())),
}


def _matmul(a, b, *, contract, grid, a_spec, b_spec, o_spec, out_shape, acc_shape, name,
            bias=None, bias_spec=None, res=None, res_spec=None, alias_res=False, exchange=()):
    nk = grid[2]
    has_bias = bias is not None
    has_res = res is not None
    ne = len(exchange)

    def body(*refs):
        a_ref, b_ref = refs[0], refs[1]
        pos = 2
        bias_ref = res_ref = None
        if has_bias:
            bias_ref = refs[pos]
            pos += 1
        if has_res:
            res_ref = refs[pos]
            pos += 1
        parts = refs[pos:pos + ne]
        pos += ne
        o_ref = refs[pos]
        got = refs[pos + 1:pos + 1 + ne]
        pos += ne
        acc_ref = refs[pos + 1] if nk > 1 else None
        if ne:
            ssem, rsem = refs[-2:]
            ids = [pl.program_id(d) for d in range(3)]

            @pl.when((ids[0] == 0) & (ids[1] == 0) & (ids[2] == 0))
            def _():
                for cp in _exchange_copies(parts, got, ssem, rsem)[0]:
                    cp.start()
        p = lax.dot_general(a_ref[...].astype(BF16), b_ref[...].astype(BF16), _DN[contract],
                            preferred_element_type=F32)

        def finish(v):
            if has_bias:
                v = v + bias_ref[...]
            if has_res:
                v = res_ref[...] + v
            o_ref[...] = v.astype(o_ref.dtype)

        if nk == 1:
            finish(p)
        else:
            k = pl.program_id(2)

            @pl.when(k == 0)
            def _():
                acc_ref[...] = p

            @pl.when(k > 0)
            def _():
                acc_ref[...] += p

            @pl.when(k == nk - 1)
            def _():
                finish(acc_ref[...])

        if ne:
            @pl.when((ids[0] == grid[0] - 1) & (ids[1] == grid[1] - 1) & (ids[2] == grid[2] - 1))
            def _():
                sends, lands = _exchange_copies(parts, got, ssem, rsem)
                for cp in lands:
                    cp.wait_recv()
                for cp in sends:
                    cp.wait_send()

    ins = [a, b]
    specs = [a_spec, b_spec]
    if has_bias:
        ins.append(bias)
        specs.append(bias_spec)
    if has_res:
        ins.append(res)
        specs.append(res_spec)
    aliases = {len(ins) - 1: 0} if (has_res and alias_res) else {}
    hbm = pl.BlockSpec(memory_space=pl.ANY)
    scratch = [pltpu.VMEM(acc_shape, F32)] if nk > 1 else []
    if not ne:
        return _pcall(
            body, name=name, grid=grid, in_specs=specs, out_specs=o_spec, out_shape=out_shape,
            scratch_shapes=scratch, input_output_aliases=aliases,
            compiler_params=_cparams(("parallel", "parallel", "arbitrary")),
        )(*ins)
    outs = _pcall(
        body, name=name, grid=grid, in_specs=specs + [hbm] * ne, out_specs=[o_spec] + [hbm] * ne,
        out_shape=[out_shape] + [jax.ShapeDtypeStruct((p.shape[0], N_CHIPS - 1) + p.shape[2:], p.dtype)
                                 for p in exchange],
        scratch_shapes=scratch + [pltpu.SemaphoreType.DMA((3 * ne,)), pltpu.SemaphoreType.DMA((3 * ne,))],
        input_output_aliases=aliases, compiler_params=_cparams(("arbitrary", "arbitrary", "arbitrary")),
    )(*ins, *exchange)
    return outs[0], list(outs[1:])


def mm_nn(a, w, lidx, n_slots, *, out_dtype, name, bias=None):
    M, K = a.shape
    Ns = w.shape[-1]
    tm = _tile(M, 1024)
    tn = _tile(Ns, 1024)
    per = Ns // tn
    grid = (M // tm, n_slots * per, 1)
    return _matmul(
        a, w, contract="nn", grid=grid, name=name,
        a_spec=pl.BlockSpec((tm, K), lambda i, j, k: (i, 0)),
        b_spec=pl.BlockSpec((None, None, K, tn), lambda i, j, k: (lidx, j // per, 0, j % per)),
        o_spec=pl.BlockSpec((tm, tn), lambda i, j, k: (i, j)),
        out_shape=jax.ShapeDtypeStruct((M, n_slots * Ns), out_dtype), acc_shape=(tm, tn),
        bias=bias, bias_spec=None if bias is None else pl.BlockSpec((1, tn), lambda i, j, k: (0, j)),
    )


def mm_nt(a, w, lidx, n_slots, *, name, res=None):
    M = a.shape[0]
    N, Ns = w.shape[-2], w.shape[-1]
    tm = _tile(M, 1024)
    tn = _tile(N, 1024)
    tk = _tile(Ns, 2048)
    per = Ns // tk
    grid = (M // tm, N // tn, n_slots * per)
    return _matmul(
        a, w, contract="nt", grid=grid, name=name,
        a_spec=pl.BlockSpec((tm, tk), lambda i, j, k: (i, k)),
        b_spec=pl.BlockSpec((None, None, tn, tk), lambda i, j, k: (lidx, k // per, j, k % per)),
        o_spec=pl.BlockSpec((tm, tn), lambda i, j, k: (i, j)),
        out_shape=jax.ShapeDtypeStruct((M, N), F32), acc_shape=(tm, tn),
        res=res, res_spec=None if res is None else pl.BlockSpec((tm, tn), lambda i, j, k: (i, j)),
        alias_res=res is not None,
    )


def mm_tn(a, b, n_slots, *, name, exchange=()):
    S, M = a.shape
    Ns = b.shape[1] // n_slots
    tm = _tile(M, 1024)
    tn = _tile(Ns, 1024)
    tk = _tile(S, 2048)
    per = Ns // tn
    grid = (M // tm, n_slots * per, S // tk)
    return _matmul(
        a, b, contract="tn", grid=grid, name=name,
        a_spec=pl.BlockSpec((tk, tm), lambda i, j, k: (k, i)),
        b_spec=pl.BlockSpec((tk, tn), lambda i, j, k: (k, j)),
        o_spec=pl.BlockSpec((None, None, tm, tn), lambda i, j, k: (0, j // per, i, j % per)),
        out_shape=jax.ShapeDtypeStruct((1, n_slots, M, Ns), F32), acc_shape=(tm, tn), exchange=exchange,
    )


def rms_fwd(h, g, name):
    S, D = h.shape
    tm = _tile(S, 256, 8)

    def body(h_ref, g_ref, o_ref):
        x = h_ref[...]
        r = lax.rsqrt(jnp.mean(x * x, axis=-1, keepdims=True) + RMS_EPS)
        o_ref[...] = (x * r * g_ref[...]).astype(BF16)

    return _pcall(
        body, name=name, grid=(S // tm,),
        in_specs=[pl.BlockSpec((tm, D), lambda i: (i, 0)), pl.BlockSpec((1, D), lambda i: (0, 0))],
        out_specs=pl.BlockSpec((tm, D), lambda i: (i, 0)),
        out_shape=jax.ShapeDtypeStruct((S, D), BF16),
        compiler_params=_cparams(("parallel",)),
    )(h, g)


def _rms_bwd_rows(x, g, dy):
    d = x.shape[-1]
    r = lax.rsqrt(jnp.mean(x * x, axis=-1, keepdims=True) + RMS_EPS)
    gd = dy * g
    dx = r * gd - x * ((r * r * r) * (jnp.sum(x * gd, axis=-1, keepdims=True) / d))
    return dx, dy * x * r


def rms_bwd(dhn, h, g, dres, name):
    S, D = h.shape
    tm = _tile(S, 256, 8)

    def body(dhn_ref, h_ref, g_ref, dres_ref, dh_ref, dg_ref):
        dx, dgr = _rms_bwd_rows(h_ref[...], g_ref[...], dhn_ref[...])
        dh_ref[...] = dres_ref[...] + dx

        @pl.when(pl.program_id(0) == 0)
        def _():
            dg_ref[...] = jnp.zeros_like(dg_ref)

        dg_ref[...] += jnp.sum(dgr, axis=0, keepdims=True)

    row = pl.BlockSpec((tm, D), lambda i: (i, 0))
    vec = pl.BlockSpec((1, D), lambda i: (0, 0))
    return _pcall(
        body, name=name, grid=(S // tm,),
        in_specs=[row, row, vec, row], out_specs=[row, vec],
        out_shape=[jax.ShapeDtypeStruct((S, D), F32), jax.ShapeDtypeStruct((1, D), F32)],
        input_output_aliases={3: 0},
        compiler_params=_cparams(("arbitrary",)),
    )(dhn, h, g, dres)


def loss_head(h, g, target, name):
    S, D = h.shape
    tm = _tile(S, 256, 8)

    def body(h_ref, g_ref, t_ref, dh_ref, dg_ref, loss_ref):
        x = h_ref[...]
        gg = g_ref[...]
        r = lax.rsqrt(jnp.mean(x * x, axis=-1, keepdims=True) + RMS_EPS)
        y = x * r * gg
        e = y - t_ref[...]
        part = 0.5 * jnp.sum(jnp.mean(e * e, axis=-1, keepdims=True), axis=0, keepdims=True)
        dy = e * (1.0 / D)
        dx, dgr = _rms_bwd_rows(x, gg, dy)
        dh_ref[...] = dx

        @pl.when(pl.program_id(0) == 0)
        def _():
            dg_ref[...] = jnp.zeros_like(dg_ref)
            loss_ref[...] = jnp.zeros_like(loss_ref)

        dg_ref[...] += jnp.sum(dgr, axis=0, keepdims=True)
        loss_ref[...] += jnp.broadcast_to(part, loss_ref.shape)

    row = pl.BlockSpec((tm, D), lambda i: (i, 0))
    vec = pl.BlockSpec((1, D), lambda i: (0, 0))
    return _pcall(
        body, name=name, grid=(S // tm,),
        in_specs=[row, vec, row],
        out_specs=[row, vec, pl.BlockSpec((1, 128), lambda i: (0, 0))],
        out_shape=[jax.ShapeDtypeStruct((S, D), F32), jax.ShapeDtypeStruct((1, D), F32),
                   jax.ShapeDtypeStruct((1, 128), F32)],
        compiler_params=_cparams(("arbitrary",)),
    )(h, g, target)


def fox_decay(pf, bf, n_heads, scale, name):
    S = pf.shape[0]
    tm = min(_tile(S, 256, 8), _attn_tile(S))
    inv_scale = 1.0 / scale

    def body(pf_ref, bf_ref, qa_ref, ka_ref, c0_ref, c1_ref, carry_ref):
        @pl.when(pl.program_id(0) == 0)
        def _():
            carry_ref[...] = jnp.zeros_like(carry_ref)

        z = pf_ref[...] + bf_ref[...]
        logf = jnp.minimum(z, 0.0) - jnp.log(1.0 + jnp.exp(-jnp.abs(z)))
        row = lax.broadcasted_iota(jnp.int32, (tm, tm), 0)
        col = lax.broadcasted_iota(jnp.int32, (tm, tm), 1)
        tri = (row >= col).astype(F32)
        c = jnp.dot(tri, logf, precision=lax.Precision.HIGHEST, preferred_element_type=F32) + carry_ref[...]
        carry_ref[...] = c[tm - 1:tm, :]
        c0_ref[...] = c[0:1, :]
        c1_ref[...] = c[tm - 1:tm, :]
        lane = lax.broadcasted_iota(jnp.int32, (tm, HEAD_DIM), 1)
        for hh in range(n_heads):
            hi, mid, lo = _split3(c[:, hh:hh + 1] * inv_scale)
            qa = jnp.where(lane == 0, hi, jnp.where(lane == 1, mid, jnp.where(lane == 2, lo,
                 jnp.where(lane < 6, 1.0, 0.0))))
            ka = jnp.where(lane < 3, 1.0, jnp.where(lane == 3, -hi, jnp.where(lane == 4, -mid,
                 jnp.where(lane == 5, -lo, jnp.where(lane < 9, 1.0, 0.0)))))
            qa_ref[:, hh * HEAD_DIM:(hh + 1) * HEAD_DIM] = qa.astype(BF16)
            ka_ref[:, hh * HEAD_DIM:(hh + 1) * HEAD_DIM] = ka.astype(BF16)

    wide = pl.BlockSpec((tm, n_heads * HEAD_DIM), lambda i: (i, 0))
    edge = pl.BlockSpec((None, 1, 128), lambda i: (i, 0, 0))
    return _pcall(
        body, name=name, grid=(S // tm,),
        in_specs=[pl.BlockSpec((tm, 128), lambda i: (i, 0)), pl.BlockSpec((1, 128), lambda i: (0, 0))],
        out_specs=[wide, wide, edge, edge],
        out_shape=[jax.ShapeDtypeStruct((S, n_heads * HEAD_DIM), BF16)] * 2
                  + [jax.ShapeDtypeStruct((S // tm, 1, 128), F32)] * 2,
        scratch_shapes=[pltpu.VMEM((1, 128), F32)],
        compiler_params=_cparams(("arbitrary",)),
    )(pf, bf)


def attn_norms(p1, p2, n_heads, name):
    S = p1.shape[0]
    W = n_heads * HEAD_DIM
    t = _attn_tile(S)

    def body(q_ref, kv_ref, qn_ref, kn_ref):
        lane = lax.broadcasted_iota(jnp.int32, (1, 128), 1)
        qn = jnp.zeros((1, 128), F32)
        kn = jnp.zeros((1, 128), F32)
        for hh in range(n_heads):
            q = q_ref[:, hh * HEAD_DIM:(hh + 1) * HEAD_DIM].astype(F32)
            k = kv_ref[:, 2 * hh * HEAD_DIM:(2 * hh + 1) * HEAD_DIM].astype(F32)
            q2 = jnp.max(jnp.sum(q * q, axis=1, keepdims=True), axis=0, keepdims=True)
            k2 = jnp.max(jnp.sum(k * k, axis=1, keepdims=True), axis=0, keepdims=True)
            qn = jnp.where(lane == hh, q2, qn)
            kn = jnp.where(lane == hh, k2, kn)
        qn_ref[...] = qn
        kn_ref[...] = kn

    edge = pl.BlockSpec((None, 1, 128), lambda i: (i, 0, 0))
    return _pcall(
        body, name=name, grid=(S // t,),
        in_specs=[pl.BlockSpec((t, W), lambda i: (i, 0)), pl.BlockSpec((t, 2 * W), lambda i: (i, 0))],
        out_specs=[edge, edge], out_shape=[jax.ShapeDtypeStruct((S // t, 1, 128), F32)] * 2,
        compiler_params=_cparams(("parallel",)),
    )(p1, p2)


PRUNE_BELOW = -110.0


def prune_tables(qn, kn, c0, c1, n_heads, scale):
    nb = qn.shape[0]
    r = c0.shape[0] // nb
    qmax = jnp.sqrt(qn[:, 0, :n_heads])
    kmax = jnp.sqrt(kn[:, 0, :n_heads])
    cfirst = c0[::r, 0, :n_heads]
    clast = c1[r - 1::r, 0, :n_heads]
    bound = (scale * qmax[:, None, :] * (kmax[None, :, :] + kmax[:, None, :])
             + cfirst[:, None, :] - clast[None, :, :])
    ii = lax.broadcasted_iota(jnp.int32, (nb, nb, 1), 0)
    jj = lax.broadcasted_iota(jnp.int32, (nb, nb, 1), 1)
    skip = ((bound < PRUNE_BELOW) & (jj < ii)).astype(jnp.int32)
    first_key = jnp.sum(jnp.cumprod(skip, axis=1), axis=1)
    tail = jnp.sum(jnp.cumprod(skip[::-1], axis=0), axis=0)
    return first_key.T.reshape((-1,)), (nb - tail).T.reshape((-1,))


def _attn_tile(S):
    return 512 if S % 512 == 0 and S >= 2048 else 128


def attn_fwd(p1, p2, qa, ka, n_heads, scale, name, gather=(), first_key=None):
    S = p1.shape[0]
    W = n_heads * HEAD_DIM
    t = _attn_tile(S)
    nq = S // t
    ng = len(gather)
    assert not (ng and first_key is not None)

    def body(*refs):
        if first_key is None:
            core(0, *refs)
        else:
            fk = refs[0][pl.program_id(0) * nq + pl.program_id(1)]
            core(jnp.minimum(fk, pl.program_id(1)), *refs[1:])

    def core(js, q_ref, g_ref, qa_ref, k_ref, v_ref, ka_ref, *rest):
        o_ref, y_ref, qb_ref = rest[ng:ng + 3]
        bufs = rest[ng + 3:2 * ng + 3]
        mp_ref, qq_ref, acc_ref = rest[2 * ng + 3:2 * ng + 6]
        i = pl.program_id(1)
        if ng:
            ssem, rsem = rest[2 * ng + 6:]

            @pl.when((pl.program_id(0) == 0) & (i == 0))
            def _():
                for cp in _gather_direct_copies(bufs, ssem, rsem)[0]:
                    cp.start()
        lane = lax.broadcasted_iota(jnp.int32, (t, HEAD_DIM), 1)
        qa = qa_ref[...].astype(F32)

        def tile_with(neg_stat):
            hi, mid, lo = _split3(neg_stat)
            return jnp.where(lane == 6, hi, jnp.where(lane == 7, mid, jnp.where(lane == 8, lo, qa))).astype(BF16)

        def keys(start, width):
            rows = pl.ds(pl.multiple_of(start, t), width)
            return rows, jnp.concatenate([k_ref[rows, :], ka_ref[rows, :]], axis=1)

        def causal():
            r = lax.broadcasted_iota(jnp.int32, (t, t), 0)
            c = lax.broadcasted_iota(jnp.int32, (t, t), 1)
            return r >= c

        def over_keys(block):
            n = i - js

            def wide(jj, carry):
                block((js + 4 * jj) * t, 4 * t, False)
                return carry

            lax.fori_loop(0, n // 4, wide, 0)
            done = js + (n // 4) * 4

            @pl.when((n & 2) != 0)
            def _():
                block(done * t, 2 * t, False)

            @pl.when((n & 1) != 0)
            def _():
                block((done + (n & 2)) * t, t, False)

            block(i * t, t, True)

        qq_ref[:, :HEAD_DIM] = q_ref[...]
        qq_ref[:, HEAD_DIM:] = qa_ref[...]
        mp_ref[...] = jnp.full(mp_ref.shape, NEG_BIG, F32)

        def max_block(start, width, masked):
            _, kk = keys(start, width)
            s = lax.dot_general(qq_ref[...], kk, _DN["nt"], preferred_element_type=F32)
            if masked:
                s = jnp.where(causal(), s, NEG_BIG)
            part = s[:, 0:HEAD_DIM]
            for a in range(1, width // HEAD_DIM):
                part = jnp.maximum(part, s[:, a * HEAD_DIM:(a + 1) * HEAD_DIM])
            mp_ref[...] = jnp.maximum(mp_ref[...], part)

        over_keys(max_block)
        m = jnp.max(mp_ref[...], axis=1, keepdims=True)
        qq_ref[:, HEAD_DIM:] = tile_with(-m)
        acc_ref[...] = jnp.zeros_like(acc_ref)

        def sum_block(start, width, masked):
            rows, kk = keys(start, width)
            a = lax.dot_general(qq_ref[...], kk, _DN["nt"], preferred_element_type=F32)
            p = jnp.exp(scale * a)
            if masked:
                p = jnp.where(causal(), p, 0.0)
            ones0 = jnp.where(lax.broadcasted_iota(jnp.int32, (width, HEAD_DIM), 1) == 0, 1.0, 0.0).astype(BF16)
            vv = jnp.concatenate([v_ref[rows, :], ones0], axis=1)
            acc_ref[...] += jnp.dot(p.astype(BF16), vv, preferred_element_type=F32)

        over_keys(sum_block)

        l = acc_ref[:, HEAD_DIM:HEAD_DIM + 1]
        o = acc_ref[:, :HEAD_DIM] / l
        gate = g_ref[...].astype(F32)
        o_ref[...] = o.astype(BF16)
        y_ref[...] = (o * (gate * _sigmoid(gate))).astype(BF16)
        qb_ref[...] = tile_with(-(m + jnp.log(l) * (1.0 / scale)))

        if ng:
            @pl.when((pl.program_id(0) == n_heads - 1) & (i == nq - 1))
            def _():
                sends, lands = _gather_direct_copies(bufs, ssem, rsem)
                for cp in lands:
                    cp.wait_recv()
                for cp in sends:
                    cp.wait_send()

    H = n_heads
    qtile = lambda off: pl.BlockSpec((t, HEAD_DIM), lambda h, i, *_: (i, off + h))
    full = lambda fn: pl.BlockSpec((S, HEAD_DIM), fn)
    hbm = pl.BlockSpec(memory_space=pl.ANY)
    sems = [pltpu.SemaphoreType.DMA((3 * ng,)), pltpu.SemaphoreType.DMA((3 * ng,))] if ng else []
    in_specs = [qtile(0), qtile(H), qtile(0), full(lambda h, i, *_: (0, 2 * h)), full(lambda h, i, *_: (0, 2 * h + 1)),
                full(lambda h, i, *_: (0, h))] + [hbm] * ng
    out_specs = [qtile(0), qtile(0), qtile(0)] + [hbm] * ng
    scratch = [pltpu.VMEM((t, HEAD_DIM), F32), pltpu.VMEM((t, 2 * HEAD_DIM), BF16),
               pltpu.VMEM((t, 2 * HEAD_DIM), F32)] + sems
    out_shape = [jax.ShapeDtypeStruct((S, W), BF16)] * 3 + [jax.ShapeDtypeStruct(b.shape, b.dtype) for b in gather]
    sem = _cparams(("arbitrary", "arbitrary") if ng else ("parallel", "arbitrary"))
    if first_key is None:
        outs = _pcall(
            body, name=name, grid=(H, nq), in_specs=in_specs, out_specs=out_specs, out_shape=out_shape,
            scratch_shapes=scratch, input_output_aliases={6 + n: 3 + n for n in range(ng)}, compiler_params=sem,
        )(p1, p1, qa, p2, p2, ka, *gather)
    else:
        grid_spec = pltpu.PrefetchScalarGridSpec(num_scalar_prefetch=1, grid=(H, nq), in_specs=in_specs,
                                                 out_specs=out_specs, scratch_shapes=scratch)
        outs = _pcall(body, name=name, grid_spec=grid_spec, out_shape=out_shape, compiler_params=sem,
                      )(first_key, p1, p1, qa, p2, p2, ka)
    return outs[0], outs[1], outs[2], list(outs[3:])


def attn_bwd_prep(dy, o, p1, n_heads, name):
    S, W = dy.shape
    tm = _tile(S, 256, 8)
    H = n_heads

    def body(dy_ref, o_ref, g_ref, do_ref, dg_ref, da_ref):
        lane = lax.broadcasted_iota(jnp.int32, (tm, HEAD_DIM), 1)
        for hh in range(H):
            cs = slice(hh * HEAD_DIM, (hh + 1) * HEAD_DIM)
            g = g_ref[:, cs].astype(F32)
            oo = o_ref[:, cs].astype(F32)
            dyv = dy_ref[:, cs]
            sg = _sigmoid(g)
            do = dyv * (g * sg)
            do_ref[:, cs] = do.astype(BF16)
            dg_ref[:, cs] = (dyv * oo * (sg * (1.0 + g * (1.0 - sg)))).astype(BF16)
            hi, mid, lo = _split3(-jnp.sum(do * oo, axis=1, keepdims=True))
            da = jnp.where(lane == 0, hi, jnp.where(lane == 1, mid, jnp.where(lane == 2, lo, 0.0)))
            da_ref[:, cs] = da.astype(BF16)

    row = lambda blk: pl.BlockSpec((tm, W), lambda i: (i, blk))
    return _pcall(
        body, name=name, grid=(S // tm,),
        in_specs=[row(0), row(0), row(1)],
        out_specs=[row(0), row(1), row(0)],
        out_shape=[jax.ShapeDtypeStruct((S, W), BF16), jax.ShapeDtypeStruct((S, 2 * W), BF16),
                   jax.ShapeDtypeStruct((S, W), BF16)],
        compiler_params=_cparams(("parallel",)),
    )(dy, o, p1)


def attn_bwd(p1, qb, do, da, p2, ka, dp1, n_heads, scale, name, exchange=(), query_end=None):
    S = p1.shape[0]
    W = n_heads * HEAD_DIM
    t = _attn_tile(S)
    nb = S // t
    H = n_heads
    ne = len(exchange)
    assert not (ne and query_end is not None)

    def body(*refs):
        if query_end is None:
            core(nb, *refs)
        else:
            qe = refs[0][pl.program_id(0) * nb + pl.program_id(1)]
            core(jnp.clip(qe, pl.program_id(1) + 1, nb), *refs[1:])

    def core(iend, q_ref, qb_ref, do_ref, da_ref, k_ref, v_ref, ka_ref, dp1_in, *rest):
        del dp1_in
        parts = rest[:ne]
        dq_ref, dkv_ref, dc_ref = rest[ne:ne + 3]
        got = rest[ne + 3:2 * ne + 3]
        dq_acc, dk_acc, dv_acc = rest[2 * ne + 3:2 * ne + 6]
        h = pl.program_id(0)
        j = pl.program_id(1)
        if ne:
            ssem, rsem = rest[2 * ne + 6:]

            @pl.when((h == 0) & (j == 0))
            def _():
                for cp in _exchange_copies(parts, got, ssem, rsem)[0]:
                    cp.start()

        @pl.when(j == 0)
        def _():
            dq_acc[...] = jnp.zeros_like(dq_acc)

        @pl.when((j == 0) & (h == 0))
        def _():
            dc_ref[...] = jnp.zeros_like(dc_ref)

        lane = lax.broadcasted_iota(jnp.int32, (t, HEAD_DIM), 1)
        ones3 = jnp.where(lane < 3, 1.0, 0.0).astype(BF16)
        kk = jnp.concatenate([k_ref[...], ka_ref[...]], axis=1)
        vv = jnp.concatenate([v_ref[...], ones3], axis=1)
        dk_acc[...] = jnp.zeros_like(dk_acc)
        dv_acc[...] = jnp.zeros_like(dv_acc)

        def block(start, width, masked):
            rows = pl.ds(pl.multiple_of(start, t), width)
            qq = jnp.concatenate([q_ref[rows, :], qb_ref[rows, :]], axis=1)
            dd = jnp.concatenate([do_ref[rows, :], da_ref[rows, :]], axis=1)
            a = lax.dot_general(qq, kk, _DN["nt"], preferred_element_type=F32)
            p = jnp.exp(scale * a)
            if masked:
                r = lax.broadcasted_iota(jnp.int32, (t, t), 0)
                c = lax.broadcasted_iota(jnp.int32, (t, t), 1)
                p = jnp.where(r >= c, p, 0.0)
            dpd = lax.dot_general(dd, vv, _DN["nt"], preferred_element_type=F32)
            ds = (p * dpd).astype(BF16)
            pb = p.astype(BF16)
            dv_acc[...] += lax.dot_general(pb, dd, _DN["tn"], preferred_element_type=F32)
            dk_acc[...] += lax.dot_general(ds, qq, _DN["tn"], preferred_element_type=F32)
            dq_acc[rows, :] += jnp.dot(ds, kk, preferred_element_type=F32)

        block(j * t, t, True)
        n_after = iend - 1 - j

        @pl.when((n_after & 1) != 0)
        def _():
            block((j + 1) * t, t, False)

        first = j + 1 + (n_after & 1)

        def loop_body(ii, carry):
            block((first + 2 * ii) * t, 2 * t, False)
            return carry

        lax.fori_loop(0, n_after // 2, loop_body, 0)

        dkv_ref[...] = jnp.concatenate([dk_acc[:, :HEAD_DIM] * scale, dv_acc[:, :HEAD_DIM]], axis=1).astype(BF16)
        colsum = dk_acc[:, HEAD_DIM + 3:HEAD_DIM + 4]
        krows = pl.ds(pl.multiple_of(j * t, t), t)
        dc_ref[krows, :] += jnp.where(lane == h, -colsum, 0.0)

        @pl.when(j == nb - 1)
        def _():
            dq_ref[...] = (dq_acc[:, :HEAD_DIM] * scale).astype(BF16)
            lane_s = lax.broadcasted_iota(jnp.int32, (S, HEAD_DIM), 1)
            dc_ref[...] += jnp.where(lane_s == h, dq_acc[:, HEAD_DIM:HEAD_DIM + 1], 0.0)

        if ne:
            @pl.when((h == H - 1) & (j == nb - 1))
            def _():
                sends, lands = _exchange_copies(parts, got, ssem, rsem)
                for cp in lands:
                    cp.wait_recv()
                for cp in sends:
                    cp.wait_send()

    full = lambda fn: pl.BlockSpec((S, HEAD_DIM), fn)
    ktile = lambda fn: pl.BlockSpec((t, HEAD_DIM), fn)
    hbm = pl.BlockSpec(memory_space=pl.ANY)
    sems = [pltpu.SemaphoreType.DMA((3 * ne,)), pltpu.SemaphoreType.DMA((3 * ne,))] if ne else []
    in_specs = [full(lambda h, j, *_: (0, h)), full(lambda h, j, *_: (0, h)), full(lambda h, j, *_: (0, h)),
                full(lambda h, j, *_: (0, h)),
                ktile(lambda h, j, *_: (j, 2 * h)), ktile(lambda h, j, *_: (j, 2 * h + 1)),
                ktile(lambda h, j, *_: (j, h)), hbm] + [hbm] * ne
    out_specs = [full(lambda h, j, *_: (0, h)),
                 pl.BlockSpec((t, 2 * HEAD_DIM), lambda h, j, *_: (j, h)),
                 pl.BlockSpec((S, 128), lambda h, j, *_: (0, 0))] + [hbm] * ne
    out_shape = ([jax.ShapeDtypeStruct((S, 2 * W), BF16), jax.ShapeDtypeStruct((S, 2 * W), BF16),
                  jax.ShapeDtypeStruct((S, 128), F32)]
                 + [jax.ShapeDtypeStruct((p.shape[0], N_CHIPS - 1) + p.shape[2:], p.dtype) for p in exchange])
    scratch = [pltpu.VMEM((S, 2 * HEAD_DIM), F32), pltpu.VMEM((t, 2 * HEAD_DIM), F32),
               pltpu.VMEM((t, 2 * HEAD_DIM), F32)] + sems
    if query_end is None:
        outs = _pcall(
            body, name=name, grid=(H, nb), in_specs=in_specs, out_specs=out_specs, out_shape=out_shape,
            scratch_shapes=scratch, input_output_aliases={7: 0}, compiler_params=_cparams(("arbitrary", "arbitrary")),
        )(p1, qb, do, da, p2, p2, ka, dp1, *exchange)
    else:
        grid_spec = pltpu.PrefetchScalarGridSpec(num_scalar_prefetch=1, grid=(H, nb), in_specs=in_specs,
                                                 out_specs=out_specs, scratch_shapes=scratch)
        outs = _pcall(body, name=name, grid_spec=grid_spec, out_shape=out_shape, input_output_aliases={8: 0},
                      compiler_params=_cparams(("arbitrary", "arbitrary")),
                      )(query_end, p1, qb, do, da, p2, p2, ka, dp1)
    return outs[0], outs[1], outs[2], list(outs[3:])


def fox_decay_bwd(dc, pf, bf, n_heads, name):
    S = dc.shape[0]
    tm = _tile(S, 256, 8)
    nb = S // tm

    def body(dc_ref, pf_ref, bf_ref, dz_ref, db_ref, carry_ref):
        @pl.when(pl.program_id(0) == 0)
        def _():
            carry_ref[...] = jnp.zeros_like(carry_ref)
            db_ref[...] = jnp.zeros_like(db_ref)

        row = lax.broadcasted_iota(jnp.int32, (tm, tm), 0)
        col = lax.broadcasted_iota(jnp.int32, (tm, tm), 1)
        tri = (row <= col).astype(F32)
        dlogf = jnp.dot(tri, dc_ref[...], precision=lax.Precision.HIGHEST, preferred_element_type=F32) + carry_ref[...]
        carry_ref[...] = dlogf[0:1, :]
        z = pf_ref[...] + bf_ref[...]
        lane = lax.broadcasted_iota(jnp.int32, (tm, 128), 1)
        dz = jnp.where(lane < n_heads, dlogf * _sigmoid(-z), 0.0)
        dz_ref[...] = dz.astype(BF16)
        db_ref[...] += jnp.sum(dz, axis=0, keepdims=True)

    rev = pl.BlockSpec((tm, 128), lambda i: (nb - 1 - i, 0))
    vec = pl.BlockSpec((1, 128), lambda i: (0, 0))
    return _pcall(
        body, name=name, grid=(nb,),
        in_specs=[rev, rev, vec], out_specs=[rev, vec],
        out_shape=[jax.ShapeDtypeStruct((S, 128), BF16), jax.ShapeDtypeStruct((1, 128), F32)],
        scratch_shapes=[pltpu.VMEM((1, 128), F32)],
        compiler_params=_cparams(("arbitrary",)),
    )(dc, pf, bf)


def _conv_tile(S):
    return _tile(S, 256, HALO)


def _fill_glu(ubuf, a_ref, b_ref, ah_ref, bh_ref, first, tm):
    uh = ah_ref[...].astype(F32) * _sigmoid(bh_ref[...].astype(F32))
    ubuf[0:HALO, :] = jnp.where(first, 0.0, uh)
    ubuf[HALO:HALO + tm, :] = a_ref[...].astype(F32) * _sigmoid(b_ref[...].astype(F32))


def conv_fwd(proj, dw, dwb, lng, lnb, name):
    S = proj.shape[0]
    C = proj.shape[1] // 3
    tm = _conv_tile(S)
    hb = tm // HALO
    nch = C // 128
    rb = _tile(tm, 128, 8)

    def body(a_ref, b_ref, g_ref, ah_ref, bh_ref, dw_ref, dwb_ref, lng_ref, lnb_ref, u2_ref, y_ref, ubuf, sh):
        i = pl.program_id(0)
        _fill_glu(ubuf, a_ref, b_ref, ah_ref, bh_ref, i == 0, tm)

        def chunk(cc, carry):
            cols = pl.ds(pl.multiple_of(cc * 128, 128), 128)
            for r0 in range(0, tm, rb):
                acc = jnp.broadcast_to(dwb_ref[:, cols], (rb, 128))
                for b in range(8):
                    taps = list(range(b, CONV_K, 8))
                    n = rb + 8 * (len(taps) - 1)
                    sh[0:n, :] = ubuf[pl.ds(HALO - (CONV_K - 1) + b + r0, n), cols]
                    for a, k in enumerate(taps):
                        acc = acc + dw_ref[k:k + 1, cols] * sh[8 * a:8 * a + rb, :]
                u2_ref[pl.ds(r0, rb), cols] = acc
            return carry

        lax.fori_loop(0, nch, chunk, 0)
        x = u2_ref[...]
        mu = jnp.mean(x, axis=-1, keepdims=True)
        xc = x - mu
        var = jnp.mean(xc * xc, axis=-1, keepdims=True)
        ln = xc * lax.rsqrt(var + LN_EPS) * lng_ref[...] + lnb_ref[...]
        gate = g_ref[...].astype(F32)
        y_ref[...] = ((ln * _sigmoid(ln)) * (gate * _sigmoid(gate))).astype(BF16)

    blk = lambda cb: pl.BlockSpec((tm, C), lambda i: (i, cb))
    halo = lambda cb: pl.BlockSpec((HALO, C), lambda i: (jnp.maximum(i * hb - 1, 0), cb))
    vec = pl.BlockSpec((1, C), lambda i: (0, 0))
    return _pcall(
        body, name=name, grid=(S // tm,),
        in_specs=[blk(0), blk(1), blk(2), halo(0), halo(1),
                  pl.BlockSpec((CONV_K, C), lambda i: (0, 0)), vec, vec, vec],
        out_specs=[blk(0), blk(0)],
        out_shape=[jax.ShapeDtypeStruct((S, C), F32), jax.ShapeDtypeStruct((S, C), BF16)],
        scratch_shapes=[pltpu.VMEM((HALO + tm, C), F32), pltpu.VMEM((rb + HALO, 128), F32)],
        compiler_params=_cparams(("parallel",)),
    )(proj, proj, proj, proj, proj, dw, dwb, lng, lnb)


def conv_bwd_norm(dy, proj, u2, lng, lnb, name):
    S, C = dy.shape
    tm = _tile(S, 256, 8)

    def body(dy_ref, g_ref, u2_ref, lng_ref, lnb_ref, du2_ref, dg_ref, sm_ref):
        x = u2_ref[...]
        mu = jnp.mean(x, axis=-1, keepdims=True)
        xc = x - mu
        var = jnp.mean(xc * xc, axis=-1, keepdims=True)
        rs = lax.rsqrt(var + LN_EPS)
        xhat = xc * rs
        gam = lng_ref[...]
        ln = xhat * gam + lnb_ref[...]
        sl = _sigmoid(ln)
        u3 = ln * sl
        gate = g_ref[...].astype(F32)
        sg = _sigmoid(gate)
        dyv = dy_ref[...]
        dgate = dyv * u3 * (sg * (1.0 + gate * (1.0 - sg)))
        dln = (dyv * (gate * sg)) * (sl * (1.0 + ln * (1.0 - sl)))
        dxh = dln * gam
        du2 = rs * (dxh - jnp.mean(dxh, axis=-1, keepdims=True)
                    - xhat * jnp.mean(dxh * xhat, axis=-1, keepdims=True))
        du2_ref[...] = du2
        dg_ref[...] = dgate.astype(BF16)

        @pl.when(pl.program_id(0) == 0)
        def _():
            sm_ref[...] = jnp.zeros_like(sm_ref)

        sm_ref[0:1, :] += jnp.sum(dln * xhat, axis=0, keepdims=True)
        sm_ref[1:2, :] += jnp.sum(dln, axis=0, keepdims=True)
        sm_ref[2:3, :] += jnp.sum(du2, axis=0, keepdims=True)
        sm_ref[3:4, :] += jnp.sum(dgate, axis=0, keepdims=True)

    blk = lambda cb: pl.BlockSpec((tm, C), lambda i: (i, cb))
    vec = pl.BlockSpec((1, C), lambda i: (0, 0))
    return _pcall(
        body, name=name, grid=(S // tm,),
        in_specs=[blk(0), blk(2), blk(0), vec, vec],
        out_specs=[blk(0), blk(2), pl.BlockSpec((8, C), lambda i: (0, 0))],
        out_shape=[jax.ShapeDtypeStruct((S, C), F32), jax.ShapeDtypeStruct((S, 3 * C), BF16),
                   jax.ShapeDtypeStruct((8, C), F32)],
        compiler_params=_cparams(("arbitrary",)),
    )(dy, proj, u2, lng, lnb)


def conv_bwd_taps(du2, proj, dw, dproj, name, exchange=()):
    S, C = du2.shape
    tm = _conv_tile(S)
    hb = tm // HALO
    nb = S // tm
    nch = C // 128
    rb = _tile(tm, 128, 8)
    ne = len(exchange)

    def body(d_ref, dh_ref, a_ref, b_ref, ah_ref, bh_ref, dw_ref, dp_in, *rest):
        del dp_in
        parts = rest[:ne]
        dab_ref, sm_ref = rest[ne:ne + 2]
        got = rest[ne + 2:2 * ne + 2]
        ubuf, dbuf, sh, sh2 = rest[2 * ne + 2:2 * ne + 6]
        i = pl.program_id(0)
        if ne:
            ssem, rsem = rest[2 * ne + 6:]

            @pl.when(i == 0)
            def _():
                for cp in _exchange_copies(parts, got, ssem, rsem)[0]:
                    cp.start()
        _fill_glu(ubuf, a_ref, b_ref, ah_ref, bh_ref, i == 0, tm)
        dbuf[0:tm, :] = d_ref[...]
        dbuf[tm:tm + HALO, :] = jnp.where(i == nb - 1, 0.0, dh_ref[...])

        @pl.when(i == 0)
        def _():
            sm_ref[...] = jnp.zeros_like(sm_ref)

        def chunk(cc, carry):
            cols = pl.ds(pl.multiple_of(cc * 128, 128), 128)
            cols_b = pl.ds(pl.multiple_of(C + cc * 128, 128), 128)
            for r0 in range(0, tm, rb):
                d0 = dbuf[r0:r0 + rb, cols]
                du = jnp.zeros((rb, 128), F32)
                for b in range(8):
                    offs = list(range(b, CONV_K, 8))
                    n = rb + 8 * (len(offs) - 1)
                    sh[0:n, :] = dbuf[pl.ds(r0 + b, n), cols]
                    for a, o in enumerate(offs):
                        k = CONV_K - 1 - o
                        du = du + dw_ref[k:k + 1, cols] * sh[8 * a:8 * a + rb, :]
                    sh2[0:n, :] = ubuf[pl.ds(HALO - (CONV_K - 1) + b + r0, n), cols]
                    for a, k in enumerate(offs):
                        sm_ref[k:k + 1, cols] += jnp.sum(d0 * sh2[8 * a:8 * a + rb, :], axis=0, keepdims=True)
                rows = pl.ds(r0, rb)
                av = a_ref[rows, cols].astype(F32)
                sb = _sigmoid(b_ref[rows, cols].astype(F32))
                da = du * sb
                db = du * av * sb * (1.0 - sb)
                dab_ref[rows, cols] = da.astype(BF16)
                dab_ref[rows, cols_b] = db.astype(BF16)
                sm_ref[32:33, cols] += jnp.sum(da, axis=0, keepdims=True)
                sm_ref[33:34, cols] += jnp.sum(db, axis=0, keepdims=True)
            return carry

        lax.fori_loop(0, nch, chunk, 0)

        if ne:
            @pl.when(i == nb - 1)
            def _():
                sends, lands = _exchange_copies(parts, got, ssem, rsem)
                for cp in lands:
                    cp.wait_recv()
                for cp in sends:
                    cp.wait_send()

    blk = lambda cb: pl.BlockSpec((tm, C), lambda i: (i, cb))
    halo = lambda cb: pl.BlockSpec((HALO, C), lambda i: (jnp.maximum(i * hb - 1, 0), cb))
    nxt = pl.BlockSpec((HALO, C), lambda i: (jnp.minimum((i + 1) * hb, nb * hb - 1), 0))
    hbm = pl.BlockSpec(memory_space=pl.ANY)
    sems = [pltpu.SemaphoreType.DMA((3 * ne,)), pltpu.SemaphoreType.DMA((3 * ne,))] if ne else []
    outs = _pcall(
        body, name=name, grid=(nb,),
        in_specs=[blk(0), nxt, blk(0), blk(1), halo(0), halo(1),
                  pl.BlockSpec((CONV_K, C), lambda i: (0, 0)), hbm] + [hbm] * ne,
        out_specs=[pl.BlockSpec((tm, 2 * C), lambda i: (i, 0)), pl.BlockSpec((40, C), lambda i: (0, 0))] + [hbm] * ne,
        out_shape=[jax.ShapeDtypeStruct((S, 3 * C), BF16), jax.ShapeDtypeStruct((40, C), F32)]
                  + [jax.ShapeDtypeStruct((p.shape[0], N_CHIPS - 1) + p.shape[2:], p.dtype) for p in exchange],
        scratch_shapes=[pltpu.VMEM((HALO + tm, C), F32), pltpu.VMEM((tm + HALO, C), F32),
                        pltpu.VMEM((rb + HALO, 128), F32), pltpu.VMEM((rb + HALO, 128), F32)] + sems,
        input_output_aliases={7: 0},
        compiler_params=_cparams(("arbitrary",)),
    )(du2, du2, proj, proj, proj, proj, dw, dproj, *exchange)
    return outs[0], outs[1], list(outs[2:])


def _rows_tile(R, Cc, budget=1 << 18):
    cap = max(8, budget // max(Cc, 1))
    if R <= cap:
        return R
    t = (cap // 8) * 8
    while t >= 8:
        if R % t == 0:
            return t
        t -= 8
    return R


def cast_into_slot(w, chip, name, l0, nl):
    _, R, Cc = w.shape
    tr = _rows_tile(R, Cc)

    def body(s_ref, w_ref, o_ref):
        del s_ref
        o_ref[...] = w_ref[...].astype(BF16)

    grid_spec = pltpu.PrefetchScalarGridSpec(
        num_scalar_prefetch=1, grid=(nl, R // tr),
        in_specs=[pl.BlockSpec((None, tr, Cc), lambda l, r, s: (l0 + l, r, 0))],
        out_specs=pl.BlockSpec((None, None, tr, Cc), lambda l, r, s: (l, s[0], r, 0)),
    )
    return _pcall(
        body, name=name, grid_spec=grid_spec, out_shape=jax.ShapeDtypeStruct((nl, N_CHIPS, R, Cc), BF16),
        compiler_params=_cparams(("parallel", "parallel")),
    )(chip, w)


def pair_sum(g, rcv, cidx, name):
    L, K, _, half, Cc = g.shape
    g5 = g
    tr = _rows_tile(half, Cc)

    def body(c_ref, g_ref, r_ref, o_ref):
        del c_ref
        o_ref[...] = (g_ref[...] + r_ref[...]).astype(BF16)

    grid_spec = pltpu.PrefetchScalarGridSpec(
        num_scalar_prefetch=1, grid=(L, K, half // tr),
        in_specs=[pl.BlockSpec((None, None, None, tr, Cc), lambda l, k, r, c: (l, k, c[0], r, 0)),
                  pl.BlockSpec((None, None, tr, Cc), lambda l, k, r, c: (l, k, r, 0))],
        out_specs=pl.BlockSpec((None, None, tr, Cc), lambda l, k, r, c: (l, k, r, 0)),
    )
    return _pcall(
        body, name=name, grid_spec=grid_spec, out_shape=jax.ShapeDtypeStruct((L, K, half, Cc), BF16),
        compiler_params=_cparams(("parallel", "parallel", "parallel")),
    )(cidx, g5, rcv)


def chip_sum(parts, got, sel, name, layers, l0, prev=None):
    Lp, _, R, Cc = parts.shape
    n_got = got.shape[1]
    tr = _rows_tile(R, Cc)

    def body(s_ref, p_ref, g_ref, *rest):
        del s_ref
        o_ref = rest[-1]
        acc = p_ref[...].astype(F32)
        for k in range(n_got):
            acc = acc + g_ref[k].astype(F32)
        o_ref[...] = acc

    in_specs = [pl.BlockSpec((None, None, tr, Cc), lambda l, r, s: (l, s[0], r, 0)),
                pl.BlockSpec((None, n_got, tr, Cc), lambda l, r, s: (l, 0, r, 0))]
    ops = [sel, parts, got]
    aliases = {}
    if prev is not None:
        in_specs.append(pl.BlockSpec(memory_space=pl.ANY))
        ops.append(prev)
        aliases = {3: 0}
    grid_spec = pltpu.PrefetchScalarGridSpec(
        num_scalar_prefetch=1, grid=(Lp, R // tr), in_specs=in_specs,
        out_specs=pl.BlockSpec((None, None, tr, Cc), lambda l, r, s: (l0 + l, s[1], r, 0)),
    )
    return _pcall(
        body, name=name, grid_spec=grid_spec, out_shape=jax.ShapeDtypeStruct((layers, 2, R, Cc), F32),
        input_output_aliases=aliases, compiler_params=_cparams(("parallel", "parallel")),
    )(*ops)


def dev_sum(parts, name):
    K, R, Cc = parts.shape

    def body(p_ref, o_ref):
        acc = p_ref[0]
        for k in range(1, K):
            acc = acc + p_ref[k]
        o_ref[...] = acc

    return _pcall(
        body, name=name, grid=(R // 8,),
        in_specs=[pl.BlockSpec((K, 8, Cc), lambda r: (0, r, 0))],
        out_specs=pl.BlockSpec((8, Cc), lambda r: (r, 0)),
        out_shape=jax.ShapeDtypeStruct((R, Cc), F32), compiler_params=_cparams(("parallel",)),
    )(parts)


def adamw(w, g, m, v, name):
    shape = w.shape
    if w.ndim == 3:
        L, R, Cc = shape
    else:
        L, R, Cc = 1, (1 if w.ndim == 1 else shape[0]), shape[-1]
    view = lambda t: t.reshape((L, R, Cc))
    tr = _rows_tile(R, Cc, budget=1 << 17)
    c1 = 1.0 - ADAM_B1 ** ADAM_STEP
    c2 = 1.0 - ADAM_B2 ** ADAM_STEP

    def body(w_ref, g_ref, m_ref, v_ref, d_ref, nm_ref, nv_ref):
        gg = g_ref[...]
        nm = ADAM_B1 * m_ref[...] + (1.0 - ADAM_B1) * gg
        nv = ADAM_B2 * v_ref[...] + (1.0 - ADAM_B2) * (gg * gg)
        d_ref[...] = -ADAM_LR * ((nm / c1) / (jnp.sqrt(nv / c2) + ADAM_EPS) + ADAM_WD * w_ref[...])
        nm_ref[...] = nm
        nv_ref[...] = nv

    spec = pl.BlockSpec((None, tr, Cc), lambda l, r: (l, r, 0))
    outs = _pcall(
        body, name=name, grid=(L, R // tr), in_specs=[spec] * 4, out_specs=[spec] * 3,
        out_shape=[jax.ShapeDtypeStruct((L, R, Cc), F32)] * 3, compiler_params=_cparams(("parallel", "parallel")),
    )(view(w), view(g), view(m), view(v))
    return tuple(o.reshape(shape) for o in outs)


def _place():
    x, y, c = lax.axis_index("x"), lax.axis_index("y"), lax.axis_index("c")
    other_chips = [(1 - x, y), (x, 1 - y), (1 - x, 1 - y)]
    return x, y, c, other_chips


def _rcopy(src, dst, ssem, rsem, k, to):
    return pltpu.make_async_remote_copy(src_ref=src, dst_ref=dst, send_sem=ssem.at[k], recv_sem=rsem.at[k],
                                        device_id=to, device_id_type=MESH)


def gather_weights(slots, small):
    nt = len(slots)

    def body(*refs):
        outs, small_out = refs[nt + 1:2 * nt + 1], refs[2 * nt + 1]
        ssem, rsem = refs[2 * nt + 2:]
        x, y, c, chips = _place()
        me = 2 * x + y
        sib = (x, y, 1 - c)

        def rows(t, half_of):
            half = outs[t].shape[2] // 2
            return pl.ds(half_of * half, half)

        first, passed = [], []
        for t in range(nt):
            mine = outs[t].at[:, me, rows(t, c), :]
            for j, chip in enumerate(chips):
                first.append(_rcopy(mine, mine, ssem, rsem, 6 * t + j, (*chip, c)))
        for j, chip in enumerate(chips):
            first.append(_rcopy(small_out.at[me], small_out.at[me], ssem, rsem, 6 * nt + j, (*chip, c)))
        for cp in first:
            cp.start()
        for t in range(nt):
            for j, (px, py) in enumerate(chips):
                land = outs[t].at[:, 2 * px + py, rows(t, c), :]
                _rcopy(land, land, ssem, rsem, 6 * t + j, (x, y, c)).wait_recv()
                fw = _rcopy(land, land, ssem, rsem, 6 * t + 3 + j, sib)
                fw.start()
                passed.append(fw)
        for j, (px, py) in enumerate(chips):
            land = small_out.at[2 * px + py]
            _rcopy(land, land, ssem, rsem, 6 * nt + j, (x, y, c)).wait_recv()
        for t in range(nt):
            for j, (px, py) in enumerate(chips):
                land = outs[t].at[:, 2 * px + py, rows(t, 1 - c), :]
                _rcopy(land, land, ssem, rsem, 6 * t + 3 + j, (x, y, c)).wait_recv()
        for cp in first + passed:
            cp.wait_send()

    ops = list(slots) + [small]
    nsem = 6 * nt + 3
    return _pcall(
        body, name="gather_weights", out_shape=[jax.ShapeDtypeStruct(s.shape, s.dtype) for s in ops],
        in_specs=[pl.BlockSpec(memory_space=pl.ANY)] * (nt + 1),
        out_specs=[pl.BlockSpec(memory_space=pl.ANY)] * (nt + 1),
        input_output_aliases={n: n for n in range(nt + 1)},
        scratch_shapes=[pltpu.SemaphoreType.DMA((nsem,)), pltpu.SemaphoreType.DMA((nsem,))],
    )(*ops)


def swap_halves(grads, name):
    nt = len(grads)

    def body(*refs):
        ins, outs = refs[:nt], refs[nt:2 * nt]
        ssem, rsem = refs[2 * nt:]
        x, y, c, _ = _place()
        sib = (x, y, 1 - c)
        cps = [_rcopy(ins[t].at[:, :, 1 - c], outs[t], ssem, rsem, t, sib) for t in range(nt)]
        for cp in cps:
            cp.start()
        for cp in cps:
            cp.wait()

    out_shape = [jax.ShapeDtypeStruct(g.shape[:2] + g.shape[3:], g.dtype) for g in grads]
    return _pcall(
        body, name=name, out_shape=out_shape,
        in_specs=[pl.BlockSpec(memory_space=pl.ANY)] * nt, out_specs=[pl.BlockSpec(memory_space=pl.ANY)] * nt,
        scratch_shapes=[pltpu.SemaphoreType.DMA((nt,)), pltpu.SemaphoreType.DMA((nt,))],
    )(*grads)


def _exchange_copies(ins, outs, ssem, rsem):
    x, y, c, chips = _place()
    sends, lands = [], []
    for t in range(len(ins)):
        for j, (px, py) in enumerate(chips):
            k = 3 * t + j
            sends.append(_rcopy(ins[t].at[:, 2 * px + py], outs[t].at[:, j], ssem, rsem, k, (px, py, c)))
            land = outs[t].at[:, j]
            lands.append(_rcopy(land, land, ssem, rsem, k, (x, y, c)))
    return sends, lands


def _gather_direct_copies(bufs, ssem, rsem):
    x, y, c, chips = _place()
    me = 2 * x + y
    sends, lands = [], []
    for t, buf in enumerate(bufs):
        mine = buf.at[:, me]
        for j, (px, py) in enumerate(chips):
            k = 3 * t + j
            sends.append(_rcopy(mine, mine, ssem, rsem, k, (px, py, c)))
            land = buf.at[:, 2 * px + py]
            lands.append(_rcopy(land, land, ssem, rsem, k, (x, y, c)))
    return sends, lands


def exchange_partials(parts, small):
    nt = len(parts)

    def body(*refs):
        ins = refs[:nt]
        outs, small_out = refs[nt + 1:2 * nt + 1], refs[2 * nt + 1]
        ssem, rsem = refs[2 * nt + 2:]
        x, y, c, chips = _place()
        dev = 4 * x + 2 * y + c
        sends, lands = _exchange_copies(ins, outs, ssem, rsem)
        peers = [(px, py, pc) for pc in (c, 1 - c) for (px, py) in [(x, y)] + chips][1:]
        for j, (px, py, pc) in enumerate(peers):
            k = 3 * nt + j
            sends.append(_rcopy(small_out.at[dev], small_out.at[dev], ssem, rsem, k, (px, py, pc)))
            land = small_out.at[4 * px + 2 * py + pc]
            lands.append(_rcopy(land, land, ssem, rsem, k, (x, y, c)))
        for cp in sends:
            cp.start()
        for cp in lands:
            cp.wait_recv()
        for cp in sends:
            cp.wait_send()

    out_shape = [jax.ShapeDtypeStruct((p.shape[0], N_CHIPS - 1) + p.shape[2:], p.dtype) for p in parts]
    out_shape.append(jax.ShapeDtypeStruct(small.shape, small.dtype))
    nsem = 3 * nt + 7
    return _pcall(
        body, name="exchange_partials", out_shape=out_shape,
        in_specs=[pl.BlockSpec(memory_space=pl.ANY)] * (nt + 1),
        out_specs=[pl.BlockSpec(memory_space=pl.ANY)] * (nt + 1),
        input_output_aliases={nt: nt},
        scratch_shapes=[pltpu.SemaphoreType.DMA((nsem,)), pltpu.SemaphoreType.DMA((nsem,))],
    )(*parts, small)


def join_halves(halves):
    nt = len(halves)

    def body(*refs):
        outs = refs[nt:2 * nt]
        ssem, rsem = refs[2 * nt:]
        x, y, c, _ = _place()
        sib = (x, y, 1 - c)
        sends = [_rcopy(outs[t].at[:, c], outs[t].at[:, c], ssem, rsem, t, sib) for t in range(nt)]
        for cp in sends:
            cp.start()
        for t in range(nt):
            land = outs[t].at[:, 1 - c]
            _rcopy(land, land, ssem, rsem, t, (x, y, c)).wait_recv()
        for cp in sends:
            cp.wait_send()

    return _pcall(
        body, name="join_halves", out_shape=[jax.ShapeDtypeStruct(h.shape, h.dtype) for h in halves],
        in_specs=[pl.BlockSpec(memory_space=pl.ANY)] * nt, out_specs=[pl.BlockSpec(memory_space=pl.ANY)] * nt,
        input_output_aliases={n: n for n in range(nt)},
        scratch_shapes=[pltpu.SemaphoreType.DMA((nt,)), pltpu.SemaphoreType.DMA((nt,))],
    )(*halves)


def _pad_cols(a, n):
    return jnp.pad(a, [(0, 0)] * (a.ndim - 1) + [(0, n - a.shape[-1])])


def kernel(x, norm_g, fox_w_in, fox_b_f, fox_w_out, conv_w_in, conv_b_in, conv_dw, conv_dw_b, conv_ln_g, conv_ln_b, conv_w_out, final_norm_g, loss_target, m_norm_g, m_fox_w_in, m_fox_b_f, m_fox_w_out, m_conv_w_in, m_conv_b_in, m_conv_dw, m_conv_dw_b, m_conv_ln_g, m_conv_ln_b, m_conv_w_out, m_final_norm_g, v_norm_g, v_fox_w_in, v_fox_b_f, v_fox_w_out, v_conv_w_in, v_conv_b_in, v_conv_dw, v_conv_dw_b, v_conv_ln_g, v_conv_ln_b, v_conv_w_out, v_final_norm_g):
    S, D = x.shape[1], x.shape[2]
    H = fox_b_f.shape[1]
    assert D == H * HEAD_DIM, "one head must be one lane tile"
    W = C = D
    NL = fox_w_in.shape[0]
    Dq = D // N_CHIPS
    NA = fox_w_in.shape[2]
    scale = HEAD_DIM ** -0.5
    chip = 2 * lax.axis_index("x") + lax.axis_index("y")
    core = lax.axis_index("c")
    cidx = core.astype(jnp.int32).reshape((1,))
    chip1 = chip.astype(jnp.int32).reshape((1,))
    sel = jnp.stack([chip, core]).astype(jnp.int32)

    small_pack = jnp.concatenate([
        conv_b_in.reshape((NL * 3, Dq)), conv_dw.reshape((NL * CONV_K, Dq)), conv_dw_b, conv_ln_g, conv_ln_b,
        jnp.zeros((PACK_ROWS - NL * (3 + CONV_K + 3), Dq), F32)], axis=0)
    small_slots = lax.dynamic_update_slice(jnp.zeros((N_CHIPS, PACK_ROWS, Dq), F32), small_pack[None], (chip, 0, 0))
    ga0, gb0, gsmall = gather_weights(
        [cast_into_slot(fox_w_in, chip1, "cast_fox_w_in_0", 0, 1),
         cast_into_slot(fox_w_out, chip1, "cast_fox_w_out_0", 0, 1)], small_slots)
    later = [cast_into_slot(fox_w_in, chip1, "cast_fox_w_in_1", 1, NL - 1),
             cast_into_slot(fox_w_out, chip1, "cast_fox_w_out_1", 1, NL - 1),
             cast_into_slot(conv_w_in, chip1, "cast_conv_w_in", 0, NL),
             cast_into_slot(conv_w_out, chip1, "cast_conv_w_out", 0, NL)]

    def fox_weights(ga, gb):
        n = ga.shape[0]
        wfull = jnp.transpose(ga, (0, 2, 1, 3)).reshape((n, D, N_CHIPS * NA))
        wq, wk, wv, wg, wf = (wfull[:, :, 0:W], wfull[:, :, W:2 * W], wfull[:, :, 2 * W:3 * W],
                              wfull[:, :, 3 * W:4 * W], wfull[:, :, 4 * W:])
        w1 = jnp.concatenate([wq, wg], axis=-1).reshape((n, 1, D, 2 * W))
        w2 = jnp.stack([wk.reshape((n, D, H, HEAD_DIM)), wv.reshape((n, D, H, HEAD_DIM))], axis=3)
        return w1, w2.reshape((n, 1, D, 2 * W)), _pad_cols(wf, 128).reshape((n, 1, D, 128)), gb.reshape((n, 1, W, D))

    fox_w = {0: (fox_weights(ga0, gb0), 0)}
    wc = wo_conv = None
    b_in = gsmall[:, 0:3 * NL, :].reshape((N_CHIPS, NL, 3 * Dq)).transpose((1, 0, 2)).reshape((NL, 3 * C))
    dwt = gsmall[:, 3 * NL:3 * NL + CONV_K * NL, :].reshape((N_CHIPS, NL, CONV_K, Dq))
    dwt = dwt.transpose((1, 2, 0, 3)).reshape((NL, CONV_K, C))
    r0 = (3 + CONV_K) * NL
    vecs = gsmall[:, r0:r0 + 3 * NL, :].reshape((N_CHIPS, 3, NL, Dq)).transpose((1, 2, 0, 3)).reshape((3, NL, C))
    dwb, lng, lnb = vecs[0], vecs[1], vecs[2]
    bfp = _pad_cols(fox_b_f, 128)

    h = x.reshape((S, D))
    tgt = loss_target.reshape((S, D))
    saved = []
    n_layers = norm_g.shape[0]
    for i in range(n_layers):
        j = i // 2
        g_i = norm_g[i:i + 1]
        hn = rms_fwd(h, g_i, f"rms_fwd_{i}")
        if i % 2 == 0:
            (w1, w2, wfp, wo_fox), wl = fox_w[j]
            p1 = mm_nn(hn, w1, wl, 1, out_dtype=BF16, name=f"fox_proj_qg_{i}")
            p2 = mm_nn(hn, w2, wl, 1, out_dtype=BF16, name=f"fox_proj_kv_{i}")
            pf = mm_nn(hn, wfp, wl, 1, out_dtype=F32, name=f"fox_proj_f_{i}")
            qa, ka, c0, c1 = fox_decay(pf, bfp[j:j + 1], H, scale, f"fox_decay_{i}")
            qn, kn = attn_norms(p1, p2, H, f"attn_norms_{i}")
            first_key, query_end = prune_tables(qn, kn, c0, c1, H, scale)
            o, yv, qb, filled = attn_fwd(p1, p2, qa, ka, H, scale, f"attn_fwd_{i}", gather=later if i == 0 else (),
                                         first_key=None if i == 0 else first_key)
            if i == 0:
                ga1, gb1, wc, gd = filled
                wo_conv = gd.reshape((NL, 1, C, D))
                rest = fox_weights(ga1, gb1)
                for jj in range(1, NL):
                    fox_w[jj] = (rest, jj - 1)
            h_new = mm_nt_res(yv, wo_fox, wl, h, f"fox_out_{i}")
            saved.append((h, hn, p1, p2, pf, ka, o, yv, qb, query_end))
        else:
            proj = mm_nn(hn, wc, j, N_CHIPS, out_dtype=BF16, name=f"conv_proj_{i}", bias=b_in[j:j + 1])
            u2, yv = conv_fwd(proj, dwt[j], dwb[j:j + 1], lng[j:j + 1], lnb[j:j + 1], f"conv_fwd_{i}")
            h_new = mm_nt_res(yv, wo_conv, j, h, f"conv_out_{i}")
            saved.append((h, hn, proj, u2, yv))
        h = h_new

    dh, d_gf, loss_part = loss_head(h, final_norm_g.reshape((1, D)), tgt, "loss_head")

    d_norm = [None] * n_layers
    d_fox_b = [None] * NL
    d_conv_small = [None] * NL
    wgrad = {}

    def pair_sums(keys, tag):
        gs = [wgrad[k].reshape(wgrad[k].shape[:2] + (2, wgrad[k].shape[2] // 2, wgrad[k].shape[3])) for k in keys]
        rcv = swap_halves(gs, f"swap_halves_{tag}")
        return {k: pair_sum(g, r, cidx, f"pair_sum_{k[0]}{k[1]}") for k, g, r in zip(keys, gs, rcv)}

    parts, got = {}, {}

    def pending(tag):
        keys = [k for k in wgrad if k not in parts]
        parts.update(pair_sums(keys, tag))
        return keys

    for i in reversed(range(n_layers)):
        j = i // 2
        g_i = norm_g[i:i + 1]
        if i % 2 == 0:
            h_in, hn, p1, p2, pf, ka, o, yv, qb, query_end = saved[i]
            (w1, w2, wfp, wo_fox), wl = fox_w[j]
            wgrad[("b", j)] = mm_tn(yv, dh, 1, name=f"fox_out_dw_{i}").reshape((1, N_CHIPS, Dq, D))
            dy = mm_nn_t(dh, wo_fox, wl, f"fox_out_dx_{i}")
            do, dp1, da = attn_bwd_prep(dy, o, p1, H, f"attn_bwd_prep_{i}")
            dp1, dp2, dc, _ = attn_bwd(p1, qb, do, da, p2, ka, dp1, H, scale, f"attn_bwd_{i}", query_end=query_end)
            dz, dbf = fox_decay_bwd(dc, pf, bfp[j:j + 1], H, f"fox_decay_bwd_{i}")
            d_fox_b[j] = dbf
            keys = pending(f"l{i}")
            dw1, arrived = mm_tn(hn, dp1, 1, name=f"fox_dw_qg_{i}", exchange=[parts[k] for k in keys])
            got.update(dict(zip(keys, arrived)))
            dw1 = dw1[0, 0]
            dw2 = mm_tn(hn, dp2, 1, name=f"fox_dw_kv_{i}")[0, 0]
            dwf = mm_tn(hn, dz, 1, name=f"fox_dw_f_{i}")[0, 0]
            dw2 = dw2.reshape((D, H, 2, HEAD_DIM))
            d_in = jnp.concatenate([dw1[:, :W], dw2[:, :, 0].reshape((D, W)), dw2[:, :, 1].reshape((D, W)),
                                    dw1[:, W:], dwf[:, :H]], axis=-1)
            wgrad[("a", j)] = d_in.reshape((D, N_CHIPS, NA)).transpose((1, 0, 2))[None]
            dhn = mm_nt(dp1, w1, wl, 1, name=f"fox_dx_qg_{i}")
            dhn = mm_nt(dp2, w2, wl, 1, name=f"fox_dx_kv_{i}", res=dhn)
            dhn = mm_nt(dz, wfp, wl, 1, name=f"fox_dx_f_{i}", res=dhn)
        else:
            h_in, hn, proj, u2, yv = saved[i]
            wgrad[("d", j)] = mm_tn(yv, dh, 1, name=f"conv_out_dw_{i}").reshape((1, N_CHIPS, Dq, D))
            dy = mm_nn_t(dh, wo_conv, j, f"conv_out_dx_{i}")
            du2, dproj, sm1 = conv_bwd_norm(dy, proj, u2, lng[j:j + 1], lnb[j:j + 1], f"conv_bwd_norm_{i}")
            keys = pending(f"l{i}")
            dproj, sm2, arrived = conv_bwd_taps(du2, proj, dwt[j], dproj, f"conv_bwd_taps_{i}",
                                                exchange=[parts[k] for k in keys])
            got.update(dict(zip(keys, arrived)))
            d_conv_small[j] = (sm1, sm2)
            wgrad[("c", j)] = mm_tn(hn, dproj, N_CHIPS, name=f"conv_dw_in_{i}")
            dhn = mm_nt(dproj, wc, j, N_CHIPS, name=f"conv_dx_{i}")
        dh, d_norm[i] = rms_bwd(dhn, h_in, g_i, dh, f"rms_bwd_{i}")

    late = [k for k in wgrad if k not in parts]
    parts.update(pair_sums(late, "late"))

    zrow = jnp.zeros((1, D), F32)
    rows = list(d_norm) + [d_gf]
    rows += [_pad_cols(d_fox_b[l][:, :H], D) for l in range(NL)]
    rows += [_pad_cols(loss_part[:, :1], D)]
    for l in range(NL):
        sm1, sm2 = d_conv_small[l]
        rows += [sm2[32:33], sm2[33:34], sm1[3:4]]
    for l in range(NL):
        rows += [d_conv_small[l][1][0:CONV_K]]
    rows += [d_conv_small[l][0][2:3] for l in range(NL)]
    rows += [d_conv_small[l][0][0:1] for l in range(NL)]
    rows += [d_conv_small[l][0][1:2] for l in range(NL)]
    n_rows = sum(r.shape[0] for r in rows)
    rows += [zrow] * (SMALL_ROWS - n_rows)
    small = jnp.concatenate(rows, axis=0)

    dev = 4 * lax.axis_index("x") + 2 * lax.axis_index("y") + core
    small_slots = lax.dynamic_update_slice(jnp.zeros((N_DEV, SMALL_ROWS, D), F32), small[None], (dev, 0, 0))
    arrived = exchange_partials([parts[k] for k in late], small_slots)
    got.update(dict(zip(late, arrived[:-1])))
    tot = dev_sum(arrived[-1], "dev_sum")
    halves = {}
    for kind in "abcd":
        for l in range(NL):
            halves[kind] = chip_sum(parts[(kind, l)], got[(kind, l)], sel, f"chip_sum_{kind}{l}", NL, l,
                                    prev=halves.get(kind))
    full = join_halves([halves[kind] for kind in "abcd"])
    grad_fox_w_in = full[0].reshape(fox_w_in.shape)
    grad_fox_w_out = full[1].reshape(fox_w_out.shape)
    grad_conv_w_in = full[2].reshape(conv_w_in.shape)
    grad_conv_w_out = full[3].reshape(conv_w_out.shape)

    def mine(v):
        return lax.dynamic_slice_in_dim(v, chip * Dq, Dq, axis=v.ndim - 1)

    r = n_layers
    grad_norm_g = tot[0:r]
    grad_final = tot[r]
    grad_fox_b_f = tot[r + 1:r + 1 + NL, :H]
    loss = tot[r + 1 + NL, 0]
    r = r + 2 + NL
    gb_full = tot[r:r + 3 * NL].reshape((NL, 3 * C))
    grad_conv_b_in = lax.dynamic_slice_in_dim(gb_full, chip * 3 * Dq, 3 * Dq, axis=1)
    r += 3 * NL
    grad_conv_dw = mine(tot[r:r + CONV_K * NL].reshape((NL, CONV_K, C)))
    r += CONV_K * NL
    grad_conv_dw_b = mine(tot[r:r + NL])
    grad_conv_ln_g = mine(tot[r + NL:r + 2 * NL])
    grad_conv_ln_b = mine(tot[r + 2 * NL:r + 3 * NL])

    grads = [grad_norm_g, grad_fox_w_in, grad_fox_b_f, grad_fox_w_out, grad_conv_w_in, grad_conv_b_in,
             grad_conv_dw, grad_conv_dw_b, grad_conv_ln_g, grad_conv_ln_b, grad_conv_w_out, grad_final]
    ws = [norm_g, fox_w_in, fox_b_f, fox_w_out, conv_w_in, conv_b_in, conv_dw, conv_dw_b, conv_ln_g, conv_ln_b,
          conv_w_out, final_norm_g]
    ms = [m_norm_g, m_fox_w_in, m_fox_b_f, m_fox_w_out, m_conv_w_in, m_conv_b_in, m_conv_dw, m_conv_dw_b,
          m_conv_ln_g, m_conv_ln_b, m_conv_w_out, m_final_norm_g]
    vs = [v_norm_g, v_fox_w_in, v_fox_b_f, v_fox_w_out, v_conv_w_in, v_conv_b_in, v_conv_dw, v_conv_dw_b,
          v_conv_ln_g, v_conv_ln_b, v_conv_w_out, v_final_norm_g]
    deltas, new_ms, new_vs = [], [], []
    for n, (w_, g_, m_, v_) in enumerate(zip(ws, grads, ms, vs)):
        d_, nm_, nv_ = adamw(w_, g_, m_, v_, f"adamw_{n}")
        deltas.append(d_)
        new_ms.append(nm_)
        new_vs.append(nv_)
    grad_x = dh.reshape(x.shape)
    return (loss, grad_x, *grads, *deltas, *new_ms, *new_vs)


def mm_nt_res(y, wo, lidx, h, name):
    M, K = y.shape
    N = wo.shape[-1]
    tm = _tile(M, 1024)
    tn = _tile(N, 1024)
    grid = (M // tm, N // tn, 1)
    return _matmul(
        y, wo, contract="nn", grid=grid, name=name,
        a_spec=pl.BlockSpec((tm, K), lambda i, j, k: (i, 0)),
        b_spec=pl.BlockSpec((None, None, K, tn), lambda i, j, k: (lidx, 0, 0, j)),
        o_spec=pl.BlockSpec((tm, tn), lambda i, j, k: (i, j)),
        out_shape=jax.ShapeDtypeStruct((M, N), F32), acc_shape=(tm, tn),
        res=h, res_spec=pl.BlockSpec((tm, tn), lambda i, j, k: (i, j)),
    )


def mm_nn_t(dh, wo, lidx, name):
    M, K = dh.shape
    N = wo.shape[-2]
    tm = _tile(M, 512)
    tn = _tile(N, 1024)
    grid = (M // tm, N // tn, 1)
    return _matmul(
        dh, wo, contract="nt", grid=grid, name=name,
        a_spec=pl.BlockSpec((tm, K), lambda i, j, k: (i, 0)),
        b_spec=pl.BlockSpec((None, None, tn, K), lambda i, j, k: (lidx, 0, j, 0)),
        o_spec=pl.BlockSpec((tm, tn), lambda i, j, k: (i, j)),
        out_shape=jax.ShapeDtypeStruct((M, N), F32), acc_shape=(tm, tn),
    )
```

```python
import jax
import jax.numpy as jnp
from jax import lax
from jax.experimental import pallas as pl
from jax.experimental.pallas import tpu as pltpu

F32 = jnp.float32
BF16 = jnp.bfloat16
MESH = pl.DeviceIdType.MESH

RMS_EPS = 1e-6
LN_EPS = 1e-5
CONV_K = 31
HALO = 32
HEAD_DIM = 128
ADAM_LR = 0.001
ADAM_B1 = 0.9
ADAM_B2 = 0.999
ADAM_EPS = 1e-08
ADAM_WD = 0.01
ADAM_STEP = 10
N_CHIPS = 4
N_DEV = 8
VMEM_LIMIT = 56 * 1024 * 1024
NEG_BIG = -1e30
SMALL_ROWS = 88
PACK_ROWS = 80


def _pcall(body, **kw):
    return pl.pallas_call(body, **kw)


def _cparams(sem=None):
    return pltpu.CompilerParams(dimension_semantics=sem, vmem_limit_bytes=VMEM_LIMIT)


def _tile(n, cap, mult=128):
    if n <= cap:
        return n
    t = (cap // mult) * mult
    while t >= mult:
        if n % t == 0:
            return t
        t -= mult
    raise ValueError(f"no tile for {n} under {cap}")


def _sigmoid(x):
    return 1.0 / (1.0 + jnp.exp(-x))


def _split3(x):
    hi = x.astype(BF16).astype(F32)
    r = x - hi
    mid = r.astype(BF16).astype(F32)
    lo = (r - mid).astype(BF16).astype(F32)
    return hi, mid, lo


_DN = {
    "nn": (((1,), (0,)), ((), ())),
    "nt": (((1,), (1,)), ((), ())),
    "tn": (((0,), (0,)), ((), ())),
}


def _matmul(a, b, *, contract, grid, a_spec, b_spec, o_spec, out_shape, acc_shape, name,
            bias=None, bias_spec=None, res=None, res_spec=None, alias_res=False, exchange=(), gather=()):
    nk = grid[2]
    has_bias = bias is not None
    has_res = res is not None
    assert not (exchange and gather)
    ne = len(exchange) + len(gather)
    rows = [(r0, rn) for _, r0, rn in gather]

    def _copies(parts, got, ssem, rsem):
        if gather:
            return _gather_direct_copies(got, ssem, rsem, rows)
        return _exchange_copies(parts, got, ssem, rsem)

    def body(*refs):
        a_ref, b_ref = refs[0], refs[1]
        pos = 2
        bias_ref = res_ref = None
        if has_bias:
            bias_ref = refs[pos]
            pos += 1
        if has_res:
            res_ref = refs[pos]
            pos += 1
        parts = refs[pos:pos + ne]
        pos += ne
        o_ref = refs[pos]
        got = refs[pos + 1:pos + 1 + ne]
        pos += ne
        acc_ref = refs[pos + 1] if nk > 1 else None
        if ne:
            ssem, rsem = refs[-2:]
            ids = [pl.program_id(d) for d in range(3)]

            @pl.when((ids[0] == 0) & (ids[1] == 0) & (ids[2] == 0))
            def _():
                for cp in _copies(parts, got, ssem, rsem)[0]:
                    cp.start()
        p = lax.dot_general(a_ref[...].astype(BF16), b_ref[...].astype(BF16), _DN[contract],
                            preferred_element_type=F32)

        def finish(v):
            if has_bias:
                v = v + bias_ref[...]
            if has_res:
                v = res_ref[...] + v
            o_ref[...] = v.astype(o_ref.dtype)

        if nk == 1:
            finish(p)
        else:
            k = pl.program_id(2)

            @pl.when(k == 0)
            def _():
                acc_ref[...] = p

            @pl.when(k > 0)
            def _():
                acc_ref[...] += p

            @pl.when(k == nk - 1)
            def _():
                finish(acc_ref[...])

        if ne:
            @pl.when((ids[0] == grid[0] - 1) & (ids[1] == grid[1] - 1) & (ids[2] == grid[2] - 1))
            def _():
                sends, lands = _copies(parts, got, ssem, rsem)
                for cp in lands:
                    cp.wait_recv()
                for cp in sends:
                    cp.wait_send()

    ins = [a, b]
    specs = [a_spec, b_spec]
    if has_bias:
        ins.append(bias)
        specs.append(bias_spec)
    if has_res:
        ins.append(res)
        specs.append(res_spec)
    aliases = {len(ins) - 1: 0} if (has_res and alias_res) else {}
    hbm = pl.BlockSpec(memory_space=pl.ANY)
    scratch = [pltpu.VMEM(acc_shape, F32)] if nk > 1 else []
    if not ne:
        return _pcall(
            body, name=name, grid=grid, in_specs=specs, out_specs=o_spec, out_shape=out_shape,
            scratch_shapes=scratch, input_output_aliases=aliases,
            compiler_params=_cparams(("parallel", "parallel", "arbitrary")),
        )(*ins)
    if gather:
        extra = [g for g, _, _ in gather]
        extra_out = [jax.ShapeDtypeStruct(g.shape, g.dtype) for g in extra]
        aliases = {**aliases, **{len(ins) + n: 1 + n for n in range(ne)}}
    else:
        extra = list(exchange)
        extra_out = [jax.ShapeDtypeStruct((p.shape[0], N_CHIPS - 1) + p.shape[2:], p.dtype) for p in extra]
    outs = _pcall(
        body, name=name, grid=grid, in_specs=specs + [hbm] * ne, out_specs=[o_spec] + [hbm] * ne,
        out_shape=[out_shape] + extra_out,
        scratch_shapes=scratch + [pltpu.SemaphoreType.DMA((3 * ne,)), pltpu.SemaphoreType.DMA((3 * ne,))],
        input_output_aliases=aliases, compiler_params=_cparams(("arbitrary", "arbitrary", "arbitrary")),
    )(*ins, *extra)
    return outs[0], list(outs[1:])


def mm_nn(a, w, lidx, n_slots, *, out_dtype, name, bias=None, gather=()):
    M, K = a.shape
    Ns = w.shape[-1]
    tm = _tile(M, 1024)
    tn = _tile(Ns, 1024)
    per = Ns // tn
    grid = (M // tm, n_slots * per, 1)
    return _matmul(
        a, w, contract="nn", grid=grid, name=name,
        a_spec=pl.BlockSpec((tm, K), lambda i, j, k: (i, 0)),
        b_spec=pl.BlockSpec((None, None, K, tn), lambda i, j, k: (lidx, j // per, 0, j % per)),
        o_spec=pl.BlockSpec((tm, tn), lambda i, j, k: (i, j)),
        out_shape=jax.ShapeDtypeStruct((M, n_slots * Ns), out_dtype), acc_shape=(tm, tn),
        bias=bias, bias_spec=None if bias is None else pl.BlockSpec((1, tn), lambda i, j, k: (0, j)),
        gather=gather,
    )


def mm_nt(a, w, lidx, n_slots, *, name, res=None):
    M = a.shape[0]
    N, Ns = w.shape[-2], w.shape[-1]
    tm = _tile(M, 1024)
    tn = _tile(N, 1024)
    tk = _tile(Ns, 2048)
    per = Ns // tk
    grid = (M // tm, N // tn, n_slots * per)
    return _matmul(
        a, w, contract="nt", grid=grid, name=name,
        a_spec=pl.BlockSpec((tm, tk), lambda i, j, k: (i, k)),
        b_spec=pl.BlockSpec((None, None, tn, tk), lambda i, j, k: (lidx, k // per, j, k % per)),
        o_spec=pl.BlockSpec((tm, tn), lambda i, j, k: (i, j)),
        out_shape=jax.ShapeDtypeStruct((M, N), F32), acc_shape=(tm, tn),
        res=res, res_spec=None if res is None else pl.BlockSpec((tm, tn), lambda i, j, k: (i, j)),
        alias_res=res is not None,
    )


def mm_tn(a, b, n_slots, *, name, exchange=()):
    S, M = a.shape
    Ns = b.shape[1] // n_slots
    tm = _tile(M, 1024)
    tn = _tile(Ns, 1024)
    tk = _tile(S, 2048)
    per = Ns // tn
    grid = (M // tm, n_slots * per, S // tk)
    return _matmul(
        a, b, contract="tn", grid=grid, name=name,
        a_spec=pl.BlockSpec((tk, tm), lambda i, j, k: (k, i)),
        b_spec=pl.BlockSpec((tk, tn), lambda i, j, k: (k, j)),
        o_spec=pl.BlockSpec((None, None, tm, tn), lambda i, j, k: (0, j // per, i, j % per)),
        out_shape=jax.ShapeDtypeStruct((1, n_slots, M, Ns), F32), acc_shape=(tm, tn), exchange=exchange,
    )


def rms_fwd(h, g, name):
    S, D = h.shape
    tm = _tile(S, 256, 8)

    def body(h_ref, g_ref, o_ref):
        x = h_ref[...]
        r = lax.rsqrt(jnp.mean(x * x, axis=-1, keepdims=True) + RMS_EPS)
        o_ref[...] = (x * r * g_ref[...]).astype(BF16)

    return _pcall(
        body, name=name, grid=(S // tm,),
        in_specs=[pl.BlockSpec((tm, D), lambda i: (i, 0)), pl.BlockSpec((1, D), lambda i: (0, 0))],
        out_specs=pl.BlockSpec((tm, D), lambda i: (i, 0)),
        out_shape=jax.ShapeDtypeStruct((S, D), BF16),
        compiler_params=_cparams(("parallel",)),
    )(h, g)


def _rms_bwd_rows(x, g, dy):
    d = x.shape[-1]
    r = lax.rsqrt(jnp.mean(x * x, axis=-1, keepdims=True) + RMS_EPS)
    gd = dy * g
    dx = r * gd - x * ((r * r * r) * (jnp.sum(x * gd, axis=-1, keepdims=True) / d))
    return dx, dy * x * r


def rms_bwd(dhn, h, g, dres, name):
    S, D = h.shape
    tm = _tile(S, 256, 8)

    def body(dhn_ref, h_ref, g_ref, dres_ref, dh_ref, dg_ref):
        dx, dgr = _rms_bwd_rows(h_ref[...], g_ref[...], dhn_ref[...])
        dh_ref[...] = dres_ref[...] + dx

        @pl.when(pl.program_id(0) == 0)
        def _():
            dg_ref[...] = jnp.zeros_like(dg_ref)

        dg_ref[...] += jnp.sum(dgr, axis=0, keepdims=True)

    row = pl.BlockSpec((tm, D), lambda i: (i, 0))
    vec = pl.BlockSpec((1, D), lambda i: (0, 0))
    return _pcall(
        body, name=name, grid=(S // tm,),
        in_specs=[row, row, vec, row], out_specs=[row, vec],
        out_shape=[jax.ShapeDtypeStruct((S, D), F32), jax.ShapeDtypeStruct((1, D), F32)],
        input_output_aliases={3: 0},
        compiler_params=_cparams(("arbitrary",)),
    )(dhn, h, g, dres)


def loss_head(h, g, target, name):
    S, D = h.shape
    tm = _tile(S, 256, 8)

    def body(h_ref, g_ref, t_ref, dh_ref, dg_ref, loss_ref):
        x = h_ref[...]
        gg = g_ref[...]
        r = lax.rsqrt(jnp.mean(x * x, axis=-1, keepdims=True) + RMS_EPS)
        y = x * r * gg
        e = y - t_ref[...]
        part = 0.5 * jnp.sum(jnp.mean(e * e, axis=-1, keepdims=True), axis=0, keepdims=True)
        dy = e * (1.0 / D)
        dx, dgr = _rms_bwd_rows(x, gg, dy)
        dh_ref[...] = dx

        @pl.when(pl.program_id(0) == 0)
        def _():
            dg_ref[...] = jnp.zeros_like(dg_ref)
            loss_ref[...] = jnp.zeros_like(loss_ref)

        dg_ref[...] += jnp.sum(dgr, axis=0, keepdims=True)
        loss_ref[...] += jnp.broadcast_to(part, loss_ref.shape)

    row = pl.BlockSpec((tm, D), lambda i: (i, 0))
    vec = pl.BlockSpec((1, D), lambda i: (0, 0))
    return _pcall(
        body, name=name, grid=(S // tm,),
        in_specs=[row, vec, row],
        out_specs=[row, vec, pl.BlockSpec((1, 128), lambda i: (0, 0))],
        out_shape=[jax.ShapeDtypeStruct((S, D), F32), jax.ShapeDtypeStruct((1, D), F32),
                   jax.ShapeDtypeStruct((1, 128), F32)],
        compiler_params=_cparams(("arbitrary",)),
    )(h, g, target)


def fox_decay(pf, bf, n_heads, scale, name):
    S = pf.shape[0]
    tm = min(_tile(S, 256, 8), _attn_tile(S))
    inv_scale = 1.0 / scale

    def body(pf_ref, bf_ref, qa_ref, ka_ref, c0_ref, c1_ref, carry_ref):
        @pl.when(pl.program_id(0) == 0)
        def _():
            carry_ref[...] = jnp.zeros_like(carry_ref)

        z = pf_ref[...] + bf_ref[...]
        logf = jnp.minimum(z, 0.0) - jnp.log(1.0 + jnp.exp(-jnp.abs(z)))
        row = lax.broadcasted_iota(jnp.int32, (tm, tm), 0)
        col = lax.broadcasted_iota(jnp.int32, (tm, tm), 1)
        tri = (row >= col).astype(F32)
        c = jnp.dot(tri, logf, precision=lax.Precision.HIGHEST, preferred_element_type=F32) + carry_ref[...]
        carry_ref[...] = c[tm - 1:tm, :]
        c0_ref[...] = c[0:1, :]
        c1_ref[...] = c[tm - 1:tm, :]
        lane = lax.broadcasted_iota(jnp.int32, (tm, HEAD_DIM), 1)
        for hh in range(n_heads):
            hi, mid, lo = _split3(c[:, hh:hh + 1] * inv_scale)
            qa = jnp.where(lane == 0, hi, jnp.where(lane == 1, mid, jnp.where(lane == 2, lo,
                 jnp.where(lane < 6, 1.0, 0.0))))
            ka = jnp.where(lane < 3, 1.0, jnp.where(lane == 3, -hi, jnp.where(lane == 4, -mid,
                 jnp.where(lane == 5, -lo, jnp.where(lane < 9, 1.0, 0.0)))))
            qa_ref[:, hh * HEAD_DIM:(hh + 1) * HEAD_DIM] = qa.astype(BF16)
            ka_ref[:, hh * HEAD_DIM:(hh + 1) * HEAD_DIM] = ka.astype(BF16)

    wide = pl.BlockSpec((tm, n_heads * HEAD_DIM), lambda i: (i, 0))
    edge = pl.BlockSpec((None, 1, 128), lambda i: (i, 0, 0))
    return _pcall(
        body, name=name, grid=(S // tm,),
        in_specs=[pl.BlockSpec((tm, 128), lambda i: (i, 0)), pl.BlockSpec((1, 128), lambda i: (0, 0))],
        out_specs=[wide, wide, edge, edge],
        out_shape=[jax.ShapeDtypeStruct((S, n_heads * HEAD_DIM), BF16)] * 2
                  + [jax.ShapeDtypeStruct((S // tm, 1, 128), F32)] * 2,
        scratch_shapes=[pltpu.VMEM((1, 128), F32)],
        compiler_params=_cparams(("arbitrary",)),
    )(pf, bf)


def attn_norms(p1, p2, n_heads, name):
    S = p1.shape[0]
    W = n_heads * HEAD_DIM
    t = _attn_tile(S)

    def body(q_ref, kv_ref, qn_ref, kn_ref):
        lane = lax.broadcasted_iota(jnp.int32, (1, 128), 1)
        qn = jnp.zeros((1, 128), F32)
        kn = jnp.zeros((1, 128), F32)
        for hh in range(n_heads):
            q = q_ref[:, hh * HEAD_DIM:(hh + 1) * HEAD_DIM].astype(F32)
            k = kv_ref[:, 2 * hh * HEAD_DIM:(2 * hh + 1) * HEAD_DIM].astype(F32)
            q2 = jnp.max(jnp.sum(q * q, axis=1, keepdims=True), axis=0, keepdims=True)
            k2 = jnp.max(jnp.sum(k * k, axis=1, keepdims=True), axis=0, keepdims=True)
            qn = jnp.where(lane == hh, q2, qn)
            kn = jnp.where(lane == hh, k2, kn)
        qn_ref[...] = qn
        kn_ref[...] = kn

    edge = pl.BlockSpec((None, 1, 128), lambda i: (i, 0, 0))
    return _pcall(
        body, name=name, grid=(S // t,),
        in_specs=[pl.BlockSpec((t, W), lambda i: (i, 0)), pl.BlockSpec((t, 2 * W), lambda i: (i, 0))],
        out_specs=[edge, edge], out_shape=[jax.ShapeDtypeStruct((S // t, 1, 128), F32)] * 2,
        compiler_params=_cparams(("parallel",)),
    )(p1, p2)


PRUNE_BELOW = -110.0


def prune_tables(qn, kn, c0, c1, n_heads, scale):
    nb = qn.shape[0]
    r = c0.shape[0] // nb
    qmax = jnp.sqrt(qn[:, 0, :n_heads])
    kmax = jnp.sqrt(kn[:, 0, :n_heads])
    cfirst = c0[::r, 0, :n_heads]
    clast = c1[r - 1::r, 0, :n_heads]
    bound = (scale * qmax[:, None, :] * (kmax[None, :, :] + kmax[:, None, :])
             + cfirst[:, None, :] - clast[None, :, :])
    ii = lax.broadcasted_iota(jnp.int32, (nb, nb, 1), 0)
    jj = lax.broadcasted_iota(jnp.int32, (nb, nb, 1), 1)
    skip = ((bound < PRUNE_BELOW) & (jj < ii)).astype(jnp.int32)
    first_key = jnp.sum(jnp.cumprod(skip, axis=1), axis=1)
    tail = jnp.sum(jnp.cumprod(skip[::-1], axis=0), axis=0)
    return first_key.T.reshape((-1,)), (nb - tail).T.reshape((-1,))


def _attn_tile(S):
    return 512 if S % 512 == 0 and S >= 2048 else 128


def attn_fwd(p1, p2, qa, ka, n_heads, scale, name, gather=(), first_key=None):
    S = p1.shape[0]
    W = n_heads * HEAD_DIM
    t = _attn_tile(S)
    nq = S // t
    ng = len(gather)
    assert not (ng and first_key is not None)

    def body(*refs):
        if first_key is None:
            core(0, *refs)
        else:
            fk = refs[0][pl.program_id(0) * nq + pl.program_id(1)]
            core(jnp.minimum(fk, pl.program_id(1)), *refs[1:])

    def core(js, q_ref, g_ref, qa_ref, k_ref, v_ref, ka_ref, *rest):
        o_ref, y_ref, qb_ref = rest[ng:ng + 3]
        bufs = rest[ng + 3:2 * ng + 3]
        mp_ref, qq_ref, acc_ref = rest[2 * ng + 3:2 * ng + 6]
        i = pl.program_id(1)
        if ng:
            ssem, rsem = rest[2 * ng + 6:]

            @pl.when((pl.program_id(0) == 0) & (i == 0))
            def _():
                for cp in _gather_direct_copies(bufs, ssem, rsem)[0]:
                    cp.start()
        lane = lax.broadcasted_iota(jnp.int32, (t, HEAD_DIM), 1)
        qa = qa_ref[...].astype(F32)

        def tile_with(neg_stat):
            hi, mid, lo = _split3(neg_stat)
            return jnp.where(lane == 6, hi, jnp.where(lane == 7, mid, jnp.where(lane == 8, lo, qa))).astype(BF16)

        def keys(start, width):
            rows = pl.ds(pl.multiple_of(start, t), width)
            return rows, jnp.concatenate([k_ref[rows, :], ka_ref[rows, :]], axis=1)

        def causal():
            r = lax.broadcasted_iota(jnp.int32, (t, t), 0)
            c = lax.broadcasted_iota(jnp.int32, (t, t), 1)
            return r >= c

        def over_keys(block):
            n = i - js

            def wide(jj, carry):
                block((js + 4 * jj) * t, 4 * t, False)
                return carry

            lax.fori_loop(0, n // 4, wide, 0)
            done = js + (n // 4) * 4

            @pl.when((n & 2) != 0)
            def _():
                block(done * t, 2 * t, False)

            @pl.when((n & 1) != 0)
            def _():
                block((done + (n & 2)) * t, t, False)

            block(i * t, t, True)

        qq_ref[:, :HEAD_DIM] = q_ref[...]
        qq_ref[:, HEAD_DIM:] = qa_ref[...]
        mp_ref[...] = jnp.full(mp_ref.shape, NEG_BIG, F32)

        def max_block(start, width, masked):
            _, kk = keys(start, width)
            s = lax.dot_general(qq_ref[...], kk, _DN["nt"], preferred_element_type=F32)
            if masked:
                s = jnp.where(causal(), s, NEG_BIG)
            part = s[:, 0:HEAD_DIM]
            for a in range(1, width // HEAD_DIM):
                part = jnp.maximum(part, s[:, a * HEAD_DIM:(a + 1) * HEAD_DIM])
            mp_ref[...] = jnp.maximum(mp_ref[...], part)

        over_keys(max_block)
        m = jnp.max(mp_ref[...], axis=1, keepdims=True)
        qq_ref[:, HEAD_DIM:] = tile_with(-m)
        acc_ref[...] = jnp.zeros_like(acc_ref)

        def sum_block(start, width, masked):
            rows, kk = keys(start, width)
            a = lax.dot_general(qq_ref[...], kk, _DN["nt"], preferred_element_type=F32)
            p = jnp.exp(scale * a)
            if masked:
                p = jnp.where(causal(), p, 0.0)
            ones0 = jnp.where(lax.broadcasted_iota(jnp.int32, (width, HEAD_DIM), 1) == 0, 1.0, 0.0).astype(BF16)
            vv = jnp.concatenate([v_ref[rows, :], ones0], axis=1)
            acc_ref[...] += jnp.dot(p.astype(BF16), vv, preferred_element_type=F32)

        over_keys(sum_block)

        l = acc_ref[:, HEAD_DIM:HEAD_DIM + 1]
        o = acc_ref[:, :HEAD_DIM] / l
        gate = g_ref[...].astype(F32)
        o_ref[...] = o.astype(BF16)
        y_ref[...] = (o * (gate * _sigmoid(gate))).astype(BF16)
        qb_ref[...] = tile_with(-(m + jnp.log(l) * (1.0 / scale)))

        if ng:
            @pl.when((pl.program_id(0) == n_heads - 1) & (i == nq - 1))
            def _():
                sends, lands = _gather_direct_copies(bufs, ssem, rsem)
                for cp in lands:
                    cp.wait_recv()
                for cp in sends:
                    cp.wait_send()

    H = n_heads
    qtile = lambda off: pl.BlockSpec((t, HEAD_DIM), lambda h, i, *_: (i, off + h))
    full = lambda fn: pl.BlockSpec((S, HEAD_DIM), fn)
    hbm = pl.BlockSpec(memory_space=pl.ANY)
    sems = [pltpu.SemaphoreType.DMA((3 * ng,)), pltpu.SemaphoreType.DMA((3 * ng,))] if ng else []
    in_specs = [qtile(0), qtile(H), qtile(0), full(lambda h, i, *_: (0, 2 * h)), full(lambda h, i, *_: (0, 2 * h + 1)),
                full(lambda h, i, *_: (0, h))] + [hbm] * ng
    out_specs = [qtile(0), qtile(0), qtile(0)] + [hbm] * ng
    scratch = [pltpu.VMEM((t, HEAD_DIM), F32), pltpu.VMEM((t, 2 * HEAD_DIM), BF16),
               pltpu.VMEM((t, 2 * HEAD_DIM), F32)] + sems
    out_shape = [jax.ShapeDtypeStruct((S, W), BF16)] * 3 + [jax.ShapeDtypeStruct(b.shape, b.dtype) for b in gather]
    sem = _cparams(("arbitrary", "arbitrary") if ng else ("parallel", "arbitrary"))
    if first_key is None:
        outs = _pcall(
            body, name=name, grid=(H, nq), in_specs=in_specs, out_specs=out_specs, out_shape=out_shape,
            scratch_shapes=scratch, input_output_aliases={6 + n: 3 + n for n in range(ng)}, compiler_params=sem,
        )(p1, p1, qa, p2, p2, ka, *gather)
    else:
        grid_spec = pltpu.PrefetchScalarGridSpec(num_scalar_prefetch=1, grid=(H, nq), in_specs=in_specs,
                                                 out_specs=out_specs, scratch_shapes=scratch)
        outs = _pcall(body, name=name, grid_spec=grid_spec, out_shape=out_shape, compiler_params=sem,
                      )(first_key, p1, p1, qa, p2, p2, ka)
    return outs[0], outs[1], outs[2], list(outs[3:])


def attn_bwd_prep(dy, o, p1, n_heads, name):
    S, W = dy.shape
    tm = _tile(S, 256, 8)
    H = n_heads

    def body(dy_ref, o_ref, g_ref, do_ref, dg_ref, da_ref):
        lane = lax.broadcasted_iota(jnp.int32, (tm, HEAD_DIM), 1)
        for hh in range(H):
            cs = slice(hh * HEAD_DIM, (hh + 1) * HEAD_DIM)
            g = g_ref[:, cs].astype(F32)
            oo = o_ref[:, cs].astype(F32)
            dyv = dy_ref[:, cs]
            sg = _sigmoid(g)
            do = dyv * (g * sg)
            do_ref[:, cs] = do.astype(BF16)
            dg_ref[:, cs] = (dyv * oo * (sg * (1.0 + g * (1.0 - sg)))).astype(BF16)
            hi, mid, lo = _split3(-jnp.sum(do * oo, axis=1, keepdims=True))
            da = jnp.where(lane == 0, hi, jnp.where(lane == 1, mid, jnp.where(lane == 2, lo, 0.0)))
            da_ref[:, cs] = da.astype(BF16)

    row = lambda blk: pl.BlockSpec((tm, W), lambda i: (i, blk))
    return _pcall(
        body, name=name, grid=(S // tm,),
        in_specs=[row(0), row(0), row(1)],
        out_specs=[row(0), row(1), row(0)],
        out_shape=[jax.ShapeDtypeStruct((S, W), BF16), jax.ShapeDtypeStruct((S, 2 * W), BF16),
                   jax.ShapeDtypeStruct((S, W), BF16)],
        compiler_params=_cparams(("parallel",)),
    )(dy, o, p1)


def attn_bwd(p1, qb, do, da, p2, ka, dp1, n_heads, scale, name, exchange=(), query_end=None):
    S = p1.shape[0]
    W = n_heads * HEAD_DIM
    t = _attn_tile(S)
    nb = S // t
    H = n_heads
    ne = len(exchange)
    assert not (ne and query_end is not None)

    def body(*refs):
        if query_end is None:
            core(nb, *refs)
        else:
            qe = refs[0][pl.program_id(0) * nb + pl.program_id(1)]
            core(jnp.clip(qe, pl.program_id(1) + 1, nb), *refs[1:])

    def core(iend, q_ref, qb_ref, do_ref, da_ref, k_ref, v_ref, ka_ref, dp1_in, *rest):
        del dp1_in
        parts = rest[:ne]
        dq_ref, dkv_ref, dc_ref = rest[ne:ne + 3]
        got = rest[ne + 3:2 * ne + 3]
        dq_acc, dk_acc, dv_acc = rest[2 * ne + 3:2 * ne + 6]
        h = pl.program_id(0)
        j = pl.program_id(1)
        if ne:
            ssem, rsem = rest[2 * ne + 6:]

            @pl.when((h == 0) & (j == 0))
            def _():
                for cp in _exchange_copies(parts, got, ssem, rsem)[0]:
                    cp.start()

        @pl.when(j == 0)
        def _():
            dq_acc[...] = jnp.zeros_like(dq_acc)

        @pl.when((j == 0) & (h == 0))
        def _():
            dc_ref[...] = jnp.zeros_like(dc_ref)

        lane = lax.broadcasted_iota(jnp.int32, (t, HEAD_DIM), 1)
        ones3 = jnp.where(lane < 3, 1.0, 0.0).astype(BF16)
        kk = jnp.concatenate([k_ref[...], ka_ref[...]], axis=1)
        vv = jnp.concatenate([v_ref[...], ones3], axis=1)
        dk_acc[...] = jnp.zeros_like(dk_acc)
        dv_acc[...] = jnp.zeros_like(dv_acc)

        def block(start, width, masked):
            rows = pl.ds(pl.multiple_of(start, t), width)
            qq = jnp.concatenate([q_ref[rows, :], qb_ref[rows, :]], axis=1)
            dd = jnp.concatenate([do_ref[rows, :], da_ref[rows, :]], axis=1)
            a = lax.dot_general(qq, kk, _DN["nt"], preferred_element_type=F32)
            p = jnp.exp(scale * a)
            if masked:
                r = lax.broadcasted_iota(jnp.int32, (t, t), 0)
                c = lax.broadcasted_iota(jnp.int32, (t, t), 1)
                p = jnp.where(r >= c, p, 0.0)
            dpd = lax.dot_general(dd, vv, _DN["nt"], preferred_element_type=F32)
            ds = (p * dpd).astype(BF16)
            pb = p.astype(BF16)
            dv_acc[...] += lax.dot_general(pb, dd, _DN["tn"], preferred_element_type=F32)
            dk_acc[...] += lax.dot_general(ds, qq, _DN["tn"], preferred_element_type=F32)
            dq_acc[rows, :] += jnp.dot(ds, kk, preferred_element_type=F32)

        block(j * t, t, True)
        n_after = iend - 1 - j

        @pl.when((n_after & 1) != 0)
        def _():
            block((j + 1) * t, t, False)

        first = j + 1 + (n_after & 1)

        def loop_body(ii, carry):
            block((first + 2 * ii) * t, 2 * t, False)
            return carry

        lax.fori_loop(0, n_after // 2, loop_body, 0)

        dkv_ref[...] = jnp.concatenate([dk_acc[:, :HEAD_DIM] * scale, dv_acc[:, :HEAD_DIM]], axis=1).astype(BF16)
        colsum = dk_acc[:, HEAD_DIM + 3:HEAD_DIM + 4]
        krows = pl.ds(pl.multiple_of(j * t, t), t)
        dc_ref[krows, :] += jnp.where(lane == h, -colsum, 0.0)

        @pl.when(j == nb - 1)
        def _():
            dq_ref[...] = (dq_acc[:, :HEAD_DIM] * scale).astype(BF16)
            lane_s = lax.broadcasted_iota(jnp.int32, (S, HEAD_DIM), 1)
            dc_ref[...] += jnp.where(lane_s == h, dq_acc[:, HEAD_DIM:HEAD_DIM + 1], 0.0)

        if ne:
            @pl.when((h == H - 1) & (j == nb - 1))
            def _():
                sends, lands = _exchange_copies(parts, got, ssem, rsem)
                for cp in lands:
                    cp.wait_recv()
                for cp in sends:
                    cp.wait_send()

    full = lambda fn: pl.BlockSpec((S, HEAD_DIM), fn)
    ktile = lambda fn: pl.BlockSpec((t, HEAD_DIM), fn)
    hbm = pl.BlockSpec(memory_space=pl.ANY)
    sems = [pltpu.SemaphoreType.DMA((3 * ne,)), pltpu.SemaphoreType.DMA((3 * ne,))] if ne else []
    in_specs = [full(lambda h, j, *_: (0, h)), full(lambda h, j, *_: (0, h)), full(lambda h, j, *_: (0, h)),
                full(lambda h, j, *_: (0, h)),
                ktile(lambda h, j, *_: (j, 2 * h)), ktile(lambda h, j, *_: (j, 2 * h + 1)),
                ktile(lambda h, j, *_: (j, h)), hbm] + [hbm] * ne
    out_specs = [full(lambda h, j, *_: (0, h)),
                 pl.BlockSpec((t, 2 * HEAD_DIM), lambda h, j, *_: (j, h)),
                 pl.BlockSpec((S, 128), lambda h, j, *_: (0, 0))] + [hbm] * ne
    out_shape = ([jax.ShapeDtypeStruct((S, 2 * W), BF16), jax.ShapeDtypeStruct((S, 2 * W), BF16),
                  jax.ShapeDtypeStruct((S, 128), F32)]
                 + [jax.ShapeDtypeStruct((p.shape[0], N_CHIPS - 1) + p.shape[2:], p.dtype) for p in exchange])
    scratch = [pltpu.VMEM((S, 2 * HEAD_DIM), F32), pltpu.VMEM((t, 2 * HEAD_DIM), F32),
               pltpu.VMEM((t, 2 * HEAD_DIM), F32)] + sems
    if query_end is None:
        outs = _pcall(
            body, name=name, grid=(H, nb), in_specs=in_specs, out_specs=out_specs, out_shape=out_shape,
            scratch_shapes=scratch, input_output_aliases={7: 0}, compiler_params=_cparams(("arbitrary", "arbitrary")),
        )(p1, qb, do, da, p2, p2, ka, dp1, *exchange)
    else:
        grid_spec = pltpu.PrefetchScalarGridSpec(num_scalar_prefetch=1, grid=(H, nb), in_specs=in_specs,
                                                 out_specs=out_specs, scratch_shapes=scratch)
        outs = _pcall(body, name=name, grid_spec=grid_spec, out_shape=out_shape, input_output_aliases={8: 0},
                      compiler_params=_cparams(("arbitrary", "arbitrary")),
                      )(query_end, p1, qb, do, da, p2, p2, ka, dp1)
    return outs[0], outs[1], outs[2], list(outs[3:])


def fox_decay_bwd(dc, pf, bf, n_heads, name):
    S = dc.shape[0]
    tm = _tile(S, 256, 8)
    nb = S // tm

    def body(dc_ref, pf_ref, bf_ref, dz_ref, db_ref, carry_ref):
        @pl.when(pl.program_id(0) == 0)
        def _():
            carry_ref[...] = jnp.zeros_like(carry_ref)
            db_ref[...] = jnp.zeros_like(db_ref)

        row = lax.broadcasted_iota(jnp.int32, (tm, tm), 0)
        col = lax.broadcasted_iota(jnp.int32, (tm, tm), 1)
        tri = (row <= col).astype(F32)
        dlogf = jnp.dot(tri, dc_ref[...], precision=lax.Precision.HIGHEST, preferred_element_type=F32) + carry_ref[...]
        carry_ref[...] = dlogf[0:1, :]
        z = pf_ref[...] + bf_ref[...]
        lane = lax.broadcasted_iota(jnp.int32, (tm, 128), 1)
        dz = jnp.where(lane < n_heads, dlogf * _sigmoid(-z), 0.0)
        dz_ref[...] = dz.astype(BF16)
        db_ref[...] += jnp.sum(dz, axis=0, keepdims=True)

    rev = pl.BlockSpec((tm, 128), lambda i: (nb - 1 - i, 0))
    vec = pl.BlockSpec((1, 128), lambda i: (0, 0))
    return _pcall(
        body, name=name, grid=(nb,),
        in_specs=[rev, rev, vec], out_specs=[rev, vec],
        out_shape=[jax.ShapeDtypeStruct((S, 128), BF16), jax.ShapeDtypeStruct((1, 128), F32)],
        scratch_shapes=[pltpu.VMEM((1, 128), F32)],
        compiler_params=_cparams(("arbitrary",)),
    )(dc, pf, bf)


def _conv_tile(S):
    return _tile(S, 256, HALO)


def _fill_glu(ubuf, a_ref, b_ref, ah_ref, bh_ref, first, tm):
    uh = ah_ref[...].astype(F32) * _sigmoid(bh_ref[...].astype(F32))
    ubuf[0:HALO, :] = jnp.where(first, 0.0, uh)
    ubuf[HALO:HALO + tm, :] = a_ref[...].astype(F32) * _sigmoid(b_ref[...].astype(F32))


def conv_fwd(proj, dw, dwb, lng, lnb, name):
    S = proj.shape[0]
    C = proj.shape[1] // 3
    tm = _conv_tile(S)
    hb = tm // HALO
    nch = C // 128
    rb = _tile(tm, 128, 8)

    def body(a_ref, b_ref, g_ref, ah_ref, bh_ref, dw_ref, dwb_ref, lng_ref, lnb_ref, u2_ref, y_ref, ubuf, sh):
        i = pl.program_id(0)
        _fill_glu(ubuf, a_ref, b_ref, ah_ref, bh_ref, i == 0, tm)

        def chunk(cc, carry):
            cols = pl.ds(pl.multiple_of(cc * 128, 128), 128)
            for r0 in range(0, tm, rb):
                acc = jnp.broadcast_to(dwb_ref[:, cols], (rb, 128))
                for b in range(8):
                    taps = list(range(b, CONV_K, 8))
                    n = rb + 8 * (len(taps) - 1)
                    sh[0:n, :] = ubuf[pl.ds(HALO - (CONV_K - 1) + b + r0, n), cols]
                    for a, k in enumerate(taps):
                        acc = acc + dw_ref[k:k + 1, cols] * sh[8 * a:8 * a + rb, :]
                u2_ref[pl.ds(r0, rb), cols] = acc
            return carry

        lax.fori_loop(0, nch, chunk, 0)
        x = u2_ref[...]
        mu = jnp.mean(x, axis=-1, keepdims=True)
        xc = x - mu
        var = jnp.mean(xc * xc, axis=-1, keepdims=True)
        ln = xc * lax.rsqrt(var + LN_EPS) * lng_ref[...] + lnb_ref[...]
        gate = g_ref[...].astype(F32)
        y_ref[...] = ((ln * _sigmoid(ln)) * (gate * _sigmoid(gate))).astype(BF16)

    blk = lambda cb: pl.BlockSpec((tm, C), lambda i: (i, cb))
    halo = lambda cb: pl.BlockSpec((HALO, C), lambda i: (jnp.maximum(i * hb - 1, 0), cb))
    vec = pl.BlockSpec((1, C), lambda i: (0, 0))
    return _pcall(
        body, name=name, grid=(S // tm,),
        in_specs=[blk(0), blk(1), blk(2), halo(0), halo(1),
                  pl.BlockSpec((CONV_K, C), lambda i: (0, 0)), vec, vec, vec],
        out_specs=[blk(0), blk(0)],
        out_shape=[jax.ShapeDtypeStruct((S, C), F32), jax.ShapeDtypeStruct((S, C), BF16)],
        scratch_shapes=[pltpu.VMEM((HALO + tm, C), F32), pltpu.VMEM((rb + HALO, 128), F32)],
        compiler_params=_cparams(("parallel",)),
    )(proj, proj, proj, proj, proj, dw, dwb, lng, lnb)


def conv_bwd_norm(dy, proj, u2, lng, lnb, name):
    S, C = dy.shape
    tm = _tile(S, 256, 8)

    def body(dy_ref, g_ref, u2_ref, lng_ref, lnb_ref, du2_ref, dg_ref, sm_ref):
        x = u2_ref[...]
        mu = jnp.mean(x, axis=-1, keepdims=True)
        xc = x - mu
        var = jnp.mean(xc * xc, axis=-1, keepdims=True)
        rs = lax.rsqrt(var + LN_EPS)
        xhat = xc * rs
        gam = lng_ref[...]
        ln = xhat * gam + lnb_ref[...]
        sl = _sigmoid(ln)
        u3 = ln * sl
        gate = g_ref[...].astype(F32)
        sg = _sigmoid(gate)
        dyv = dy_ref[...]
        dgate = dyv * u3 * (sg * (1.0 + gate * (1.0 - sg)))
        dln = (dyv * (gate * sg)) * (sl * (1.0 + ln * (1.0 - sl)))
        dxh = dln * gam
        du2 = rs * (dxh - jnp.mean(dxh, axis=-1, keepdims=True)
                    - xhat * jnp.mean(dxh * xhat, axis=-1, keepdims=True))
        du2_ref[...] = du2
        dg_ref[...] = dgate.astype(BF16)

        @pl.when(pl.program_id(0) == 0)
        def _():
            sm_ref[...] = jnp.zeros_like(sm_ref)

        sm_ref[0:1, :] += jnp.sum(dln * xhat, axis=0, keepdims=True)
        sm_ref[1:2, :] += jnp.sum(dln, axis=0, keepdims=True)
        sm_ref[2:3, :] += jnp.sum(du2, axis=0, keepdims=True)
        sm_ref[3:4, :] += jnp.sum(dgate, axis=0, keepdims=True)

    blk = lambda cb: pl.BlockSpec((tm, C), lambda i: (i, cb))
    vec = pl.BlockSpec((1, C), lambda i: (0, 0))
    return _pcall(
        body, name=name, grid=(S // tm,),
        in_specs=[blk(0), blk(2), blk(0), vec, vec],
        out_specs=[blk(0), blk(2), pl.BlockSpec((8, C), lambda i: (0, 0))],
        out_shape=[jax.ShapeDtypeStruct((S, C), F32), jax.ShapeDtypeStruct((S, 3 * C), BF16),
                   jax.ShapeDtypeStruct((8, C), F32)],
        compiler_params=_cparams(("arbitrary",)),
    )(dy, proj, u2, lng, lnb)


def conv_bwd_taps(du2, proj, dw, dproj, name, exchange=()):
    S, C = du2.shape
    tm = _conv_tile(S)
    hb = tm // HALO
    nb = S // tm
    nch = C // 128
    rb = _tile(tm, 128, 8)
    ne = len(exchange)

    def body(d_ref, dh_ref, a_ref, b_ref, ah_ref, bh_ref, dw_ref, dp_in, *rest):
        del dp_in
        parts = rest[:ne]
        dab_ref, sm_ref = rest[ne:ne + 2]
        got = rest[ne + 2:2 * ne + 2]
        ubuf, dbuf, sh, sh2 = rest[2 * ne + 2:2 * ne + 6]
        i = pl.program_id(0)
        if ne:
            ssem, rsem = rest[2 * ne + 6:]

            @pl.when(i == 0)
            def _():
                for cp in _exchange_copies(parts, got, ssem, rsem)[0]:
                    cp.start()
        _fill_glu(ubuf, a_ref, b_ref, ah_ref, bh_ref, i == 0, tm)
        dbuf[0:tm, :] = d_ref[...]
        dbuf[tm:tm + HALO, :] = jnp.where(i == nb - 1, 0.0, dh_ref[...])

        @pl.when(i == 0)
        def _():
            sm_ref[...] = jnp.zeros_like(sm_ref)

        def chunk(cc, carry):
            cols = pl.ds(pl.multiple_of(cc * 128, 128), 128)
            cols_b = pl.ds(pl.multiple_of(C + cc * 128, 128), 128)
            for r0 in range(0, tm, rb):
                d0 = dbuf[r0:r0 + rb, cols]
                du = jnp.zeros((rb, 128), F32)
                for b in range(8):
                    offs = list(range(b, CONV_K, 8))
                    n = rb + 8 * (len(offs) - 1)
                    sh[0:n, :] = dbuf[pl.ds(r0 + b, n), cols]
                    for a, o in enumerate(offs):
                        k = CONV_K - 1 - o
                        du = du + dw_ref[k:k + 1, cols] * sh[8 * a:8 * a + rb, :]
                    sh2[0:n, :] = ubuf[pl.ds(HALO - (CONV_K - 1) + b + r0, n), cols]
                    for a, k in enumerate(offs):
                        sm_ref[k:k + 1, cols] += jnp.sum(d0 * sh2[8 * a:8 * a + rb, :], axis=0, keepdims=True)
                rows = pl.ds(r0, rb)
                av = a_ref[rows, cols].astype(F32)
                sb = _sigmoid(b_ref[rows, cols].astype(F32))
                da = du * sb
                db = du * av * sb * (1.0 - sb)
                dab_ref[rows, cols] = da.astype(BF16)
                dab_ref[rows, cols_b] = db.astype(BF16)
                sm_ref[32:33, cols] += jnp.sum(da, axis=0, keepdims=True)
                sm_ref[33:34, cols] += jnp.sum(db, axis=0, keepdims=True)
            return carry

        lax.fori_loop(0, nch, chunk, 0)

        if ne:
            @pl.when(i == nb - 1)
            def _():
                sends, lands = _exchange_copies(parts, got, ssem, rsem)
                for cp in lands:
                    cp.wait_recv()
                for cp in sends:
                    cp.wait_send()

    blk = lambda cb: pl.BlockSpec((tm, C), lambda i: (i, cb))
    halo = lambda cb: pl.BlockSpec((HALO, C), lambda i: (jnp.maximum(i * hb - 1, 0), cb))
    nxt = pl.BlockSpec((HALO, C), lambda i: (jnp.minimum((i + 1) * hb, nb * hb - 1), 0))
    hbm = pl.BlockSpec(memory_space=pl.ANY)
    sems = [pltpu.SemaphoreType.DMA((3 * ne,)), pltpu.SemaphoreType.DMA((3 * ne,))] if ne else []
    outs = _pcall(
        body, name=name, grid=(nb,),
        in_specs=[blk(0), nxt, blk(0), blk(1), halo(0), halo(1),
                  pl.BlockSpec((CONV_K, C), lambda i: (0, 0)), hbm] + [hbm] * ne,
        out_specs=[pl.BlockSpec((tm, 2 * C), lambda i: (i, 0)), pl.BlockSpec((40, C), lambda i: (0, 0))] + [hbm] * ne,
        out_shape=[jax.ShapeDtypeStruct((S, 3 * C), BF16), jax.ShapeDtypeStruct((40, C), F32)]
                  + [jax.ShapeDtypeStruct((p.shape[0], N_CHIPS - 1) + p.shape[2:], p.dtype) for p in exchange],
        scratch_shapes=[pltpu.VMEM((HALO + tm, C), F32), pltpu.VMEM((tm + HALO, C), F32),
                        pltpu.VMEM((rb + HALO, 128), F32), pltpu.VMEM((rb + HALO, 128), F32)] + sems,
        input_output_aliases={7: 0},
        compiler_params=_cparams(("arbitrary",)),
    )(du2, du2, proj, proj, proj, proj, dw, dproj, *exchange)
    return outs[0], outs[1], list(outs[2:])


def _rows_tile(R, Cc, budget=1 << 18):
    cap = max(8, budget // max(Cc, 1))
    if R <= cap:
        return R
    t = (cap // 8) * 8
    while t >= 8:
        if R % t == 0:
            return t
        t -= 8
    return R


def cast_into_slot(w, chip, name, l0, nl):
    _, R, Cc = w.shape
    tr = _rows_tile(R, Cc)

    def body(s_ref, w_ref, o_ref):
        del s_ref
        o_ref[...] = w_ref[...].astype(BF16)

    grid_spec = pltpu.PrefetchScalarGridSpec(
        num_scalar_prefetch=1, grid=(nl, R // tr),
        in_specs=[pl.BlockSpec((None, tr, Cc), lambda l, r, s: (l0 + l, r, 0))],
        out_specs=pl.BlockSpec((None, None, tr, Cc), lambda l, r, s: (l, s[0], r, 0)),
    )
    return _pcall(
        body, name=name, grid_spec=grid_spec, out_shape=jax.ShapeDtypeStruct((nl, N_CHIPS, R, Cc), BF16),
        compiler_params=_cparams(("parallel", "parallel")),
    )(chip, w)


def pair_sum(g, rcv, cidx, name):
    L, K, _, half, Cc = g.shape
    g5 = g
    tr = _rows_tile(half, Cc)

    def body(c_ref, g_ref, r_ref, o_ref):
        del c_ref
        o_ref[...] = (g_ref[...] + r_ref[...]).astype(BF16)

    grid_spec = pltpu.PrefetchScalarGridSpec(
        num_scalar_prefetch=1, grid=(L, K, half // tr),
        in_specs=[pl.BlockSpec((None, None, None, tr, Cc), lambda l, k, r, c: (l, k, c[0], r, 0)),
                  pl.BlockSpec((None, None, tr, Cc), lambda l, k, r, c: (l, k, r, 0))],
        out_specs=pl.BlockSpec((None, None, tr, Cc), lambda l, k, r, c: (l, k, r, 0)),
    )
    return _pcall(
        body, name=name, grid_spec=grid_spec, out_shape=jax.ShapeDtypeStruct((L, K, half, Cc), BF16),
        compiler_params=_cparams(("parallel", "parallel", "parallel")),
    )(cidx, g5, rcv)


def chip_sum(parts, got, sel, name, layers, l0, prev=None):
    Lp, _, R, Cc = parts.shape
    n_got = got.shape[1]
    tr = _rows_tile(R, Cc)

    def body(s_ref, p_ref, g_ref, *rest):
        del s_ref
        o_ref = rest[-1]
        acc = p_ref[...].astype(F32)
        for k in range(n_got):
            acc = acc + g_ref[k].astype(F32)
        o_ref[...] = acc

    in_specs = [pl.BlockSpec((None, None, tr, Cc), lambda l, r, s: (l, s[0], r, 0)),
                pl.BlockSpec((None, n_got, tr, Cc), lambda l, r, s: (l, 0, r, 0))]
    ops = [sel, parts, got]
    aliases = {}
    if prev is not None:
        in_specs.append(pl.BlockSpec(memory_space=pl.ANY))
        ops.append(prev)
        aliases = {3: 0}
    grid_spec = pltpu.PrefetchScalarGridSpec(
        num_scalar_prefetch=1, grid=(Lp, R // tr), in_specs=in_specs,
        out_specs=pl.BlockSpec((None, None, tr, Cc), lambda l, r, s: (l0 + l, s[1], r, 0)),
    )
    return _pcall(
        body, name=name, grid_spec=grid_spec, out_shape=jax.ShapeDtypeStruct((layers, 2, R, Cc), F32),
        input_output_aliases=aliases, compiler_params=_cparams(("parallel", "parallel")),
    )(*ops)


def dev_sum(parts, name):
    K, R, Cc = parts.shape

    def body(p_ref, o_ref):
        acc = p_ref[0]
        for k in range(1, K):
            acc = acc + p_ref[k]
        o_ref[...] = acc

    return _pcall(
        body, name=name, grid=(R // 8,),
        in_specs=[pl.BlockSpec((K, 8, Cc), lambda r: (0, r, 0))],
        out_specs=pl.BlockSpec((8, Cc), lambda r: (r, 0)),
        out_shape=jax.ShapeDtypeStruct((R, Cc), F32), compiler_params=_cparams(("parallel",)),
    )(parts)


def adamw(w, g, m, v, name):
    shape = w.shape
    if w.ndim == 3:
        L, R, Cc = shape
    else:
        L, R, Cc = 1, (1 if w.ndim == 1 else shape[0]), shape[-1]
    view = lambda t: t.reshape((L, R, Cc))
    tr = _rows_tile(R, Cc, budget=1 << 17)
    c1 = 1.0 - ADAM_B1 ** ADAM_STEP
    c2 = 1.0 - ADAM_B2 ** ADAM_STEP

    def body(w_ref, g_ref, m_ref, v_ref, d_ref, nm_ref, nv_ref):
        gg = g_ref[...]
        nm = ADAM_B1 * m_ref[...] + (1.0 - ADAM_B1) * gg
        nv = ADAM_B2 * v_ref[...] + (1.0 - ADAM_B2) * (gg * gg)
        d_ref[...] = -ADAM_LR * ((nm / c1) / (jnp.sqrt(nv / c2) + ADAM_EPS) + ADAM_WD * w_ref[...])
        nm_ref[...] = nm
        nv_ref[...] = nv

    spec = pl.BlockSpec((None, tr, Cc), lambda l, r: (l, r, 0))
    outs = _pcall(
        body, name=name, grid=(L, R // tr), in_specs=[spec] * 4, out_specs=[spec] * 3,
        out_shape=[jax.ShapeDtypeStruct((L, R, Cc), F32)] * 3, compiler_params=_cparams(("parallel", "parallel")),
    )(view(w), view(g), view(m), view(v))
    return tuple(o.reshape(shape) for o in outs)


def _place():
    x, y, c = lax.axis_index("x"), lax.axis_index("y"), lax.axis_index("c")
    other_chips = [(1 - x, y), (x, 1 - y), (1 - x, 1 - y)]
    return x, y, c, other_chips


def _rcopy(src, dst, ssem, rsem, k, to):
    return pltpu.make_async_remote_copy(src_ref=src, dst_ref=dst, send_sem=ssem.at[k], recv_sem=rsem.at[k],
                                        device_id=to, device_id_type=MESH)


def gather_weights(slots, small):
    nt = len(slots)

    def body(*refs):
        outs, small_out = refs[nt + 1:2 * nt + 1], refs[2 * nt + 1]
        ssem, rsem = refs[2 * nt + 2:]
        x, y, c, chips = _place()
        me = 2 * x + y
        sib = (x, y, 1 - c)

        def rows(t, half_of):
            half = outs[t].shape[2] // 2
            return pl.ds(half_of * half, half)

        first, passed = [], []
        for t in range(nt):
            mine = outs[t].at[:, me, rows(t, c), :]
            for j, chip in enumerate(chips):
                first.append(_rcopy(mine, mine, ssem, rsem, 6 * t + j, (*chip, c)))
        for j, chip in enumerate(chips):
            first.append(_rcopy(small_out.at[me], small_out.at[me], ssem, rsem, 6 * nt + j, (*chip, c)))
        for cp in first:
            cp.start()
        for t in range(nt):
            for j, (px, py) in enumerate(chips):
                land = outs[t].at[:, 2 * px + py, rows(t, c), :]
                _rcopy(land, land, ssem, rsem, 6 * t + j, (x, y, c)).wait_recv()
                fw = _rcopy(land, land, ssem, rsem, 6 * t + 3 + j, sib)
                fw.start()
                passed.append(fw)
        for j, (px, py) in enumerate(chips):
            land = small_out.at[2 * px + py]
            _rcopy(land, land, ssem, rsem, 6 * nt + j, (x, y, c)).wait_recv()
        for t in range(nt):
            for j, (px, py) in enumerate(chips):
                land = outs[t].at[:, 2 * px + py, rows(t, 1 - c), :]
                _rcopy(land, land, ssem, rsem, 6 * t + 3 + j, (x, y, c)).wait_recv()
        for cp in first + passed:
            cp.wait_send()

    ops = list(slots) + [small]
    nsem = 6 * nt + 3
    return _pcall(
        body, name="gather_weights", out_shape=[jax.ShapeDtypeStruct(s.shape, s.dtype) for s in ops],
        in_specs=[pl.BlockSpec(memory_space=pl.ANY)] * (nt + 1),
        out_specs=[pl.BlockSpec(memory_space=pl.ANY)] * (nt + 1),
        input_output_aliases={n: n for n in range(nt + 1)},
        scratch_shapes=[pltpu.SemaphoreType.DMA((nsem,)), pltpu.SemaphoreType.DMA((nsem,))],
    )(*ops)


def swap_halves(grads, name):
    nt = len(grads)

    def body(*refs):
        ins, outs = refs[:nt], refs[nt:2 * nt]
        ssem, rsem = refs[2 * nt:]
        x, y, c, _ = _place()
        sib = (x, y, 1 - c)
        cps = [_rcopy(ins[t].at[:, :, 1 - c], outs[t], ssem, rsem, t, sib) for t in range(nt)]
        for cp in cps:
            cp.start()
        for cp in cps:
            cp.wait()

    out_shape = [jax.ShapeDtypeStruct(g.shape[:2] + g.shape[3:], g.dtype) for g in grads]
    return _pcall(
        body, name=name, out_shape=out_shape,
        in_specs=[pl.BlockSpec(memory_space=pl.ANY)] * nt, out_specs=[pl.BlockSpec(memory_space=pl.ANY)] * nt,
        scratch_shapes=[pltpu.SemaphoreType.DMA((nt,)), pltpu.SemaphoreType.DMA((nt,))],
    )(*grads)


def _exchange_copies(ins, outs, ssem, rsem):
    x, y, c, chips = _place()
    sends, lands = [], []
    for t in range(len(ins)):
        for j, (px, py) in enumerate(chips):
            k = 3 * t + j
            sends.append(_rcopy(ins[t].at[:, 2 * px + py], outs[t].at[:, j], ssem, rsem, k, (px, py, c)))
            land = outs[t].at[:, j]
            lands.append(_rcopy(land, land, ssem, rsem, k, (x, y, c)))
    return sends, lands


def _gather_direct_copies(bufs, ssem, rsem, rows=None):
    x, y, c, chips = _place()
    me = 2 * x + y
    sends, lands = [], []
    for t, buf in enumerate(bufs):
        r0, rn = rows[t] if rows else (0, buf.shape[2])
        mine = buf.at[:, me, pl.ds(r0, rn)]
        for j, (px, py) in enumerate(chips):
            k = 3 * t + j
            sends.append(_rcopy(mine, mine, ssem, rsem, k, (px, py, c)))
            land = buf.at[:, 2 * px + py, pl.ds(r0, rn)]
            lands.append(_rcopy(land, land, ssem, rsem, k, (x, y, c)))
    return sends, lands


def exchange_partials(parts, small):
    nt = len(parts)

    def body(*refs):
        ins = refs[:nt]
        outs, small_out = refs[nt + 1:2 * nt + 1], refs[2 * nt + 1]
        ssem, rsem = refs[2 * nt + 2:]
        x, y, c, chips = _place()
        dev = 4 * x + 2 * y + c
        sends, lands = _exchange_copies(ins, outs, ssem, rsem)
        peers = [(px, py, pc) for pc in (c, 1 - c) for (px, py) in [(x, y)] + chips][1:]
        for j, (px, py, pc) in enumerate(peers):
            k = 3 * nt + j
            sends.append(_rcopy(small_out.at[dev], small_out.at[dev], ssem, rsem, k, (px, py, pc)))
            land = small_out.at[4 * px + 2 * py + pc]
            lands.append(_rcopy(land, land, ssem, rsem, k, (x, y, c)))
        for cp in sends:
            cp.start()
        for cp in lands:
            cp.wait_recv()
        for cp in sends:
            cp.wait_send()

    out_shape = [jax.ShapeDtypeStruct((p.shape[0], N_CHIPS - 1) + p.shape[2:], p.dtype) for p in parts]
    out_shape.append(jax.ShapeDtypeStruct(small.shape, small.dtype))
    nsem = 3 * nt + 7
    return _pcall(
        body, name="exchange_partials", out_shape=out_shape,
        in_specs=[pl.BlockSpec(memory_space=pl.ANY)] * (nt + 1),
        out_specs=[pl.BlockSpec(memory_space=pl.ANY)] * (nt + 1),
        input_output_aliases={nt: nt},
        scratch_shapes=[pltpu.SemaphoreType.DMA((nsem,)), pltpu.SemaphoreType.DMA((nsem,))],
    )(*parts, small)


def join_halves(halves):
    nt = len(halves)

    def body(*refs):
        outs = refs[nt:2 * nt]
        ssem, rsem = refs[2 * nt:]
        x, y, c, _ = _place()
        sib = (x, y, 1 - c)
        sends = [_rcopy(outs[t].at[:, c], outs[t].at[:, c], ssem, rsem, t, sib) for t in range(nt)]
        for cp in sends:
            cp.start()
        for t in range(nt):
            land = outs[t].at[:, 1 - c]
            _rcopy(land, land, ssem, rsem, t, (x, y, c)).wait_recv()
        for cp in sends:
            cp.wait_send()

    return _pcall(
        body, name="join_halves", out_shape=[jax.ShapeDtypeStruct(h.shape, h.dtype) for h in halves],
        in_specs=[pl.BlockSpec(memory_space=pl.ANY)] * nt, out_specs=[pl.BlockSpec(memory_space=pl.ANY)] * nt,
        input_output_aliases={n: n for n in range(nt)},
        scratch_shapes=[pltpu.SemaphoreType.DMA((nt,)), pltpu.SemaphoreType.DMA((nt,))],
    )(*halves)


def _pad_cols(a, n):
    return jnp.pad(a, [(0, 0)] * (a.ndim - 1) + [(0, n - a.shape[-1])])


def kernel(x, norm_g, fox_w_in, fox_b_f, fox_w_out, conv_w_in, conv_b_in, conv_dw, conv_dw_b, conv_ln_g, conv_ln_b, conv_w_out, final_norm_g, loss_target, m_norm_g, m_fox_w_in, m_fox_b_f, m_fox_w_out, m_conv_w_in, m_conv_b_in, m_conv_dw, m_conv_dw_b, m_conv_ln_g, m_conv_ln_b, m_conv_w_out, m_final_norm_g, v_norm_g, v_fox_w_in, v_fox_b_f, v_fox_w_out, v_conv_w_in, v_conv_b_in, v_conv_dw, v_conv_dw_b, v_conv_ln_g, v_conv_ln_b, v_conv_w_out, v_final_norm_g):
    S, D = x.shape[1], x.shape[2]
    H = fox_b_f.shape[1]
    assert D == H * HEAD_DIM, "one head must be one lane tile"
    W = C = D
    NL = fox_w_in.shape[0]
    Dq = D // N_CHIPS
    NA = fox_w_in.shape[2]
    scale = HEAD_DIM ** -0.5
    chip = 2 * lax.axis_index("x") + lax.axis_index("y")
    core = lax.axis_index("c")
    cidx = core.astype(jnp.int32).reshape((1,))
    chip1 = chip.astype(jnp.int32).reshape((1,))
    sel = jnp.stack([chip, core]).astype(jnp.int32)

    small_pack = jnp.concatenate([
        conv_b_in.reshape((NL * 3, Dq)), conv_dw.reshape((NL * CONV_K, Dq)), conv_dw_b, conv_ln_g, conv_ln_b,
        jnp.zeros((PACK_ROWS - NL * (3 + CONV_K + 3), Dq), F32)], axis=0)
    small_slots = lax.dynamic_update_slice(jnp.zeros((N_CHIPS, PACK_ROWS, Dq), F32), small_pack[None], (chip, 0, 0))
    ga0, gb0, gsmall = gather_weights(
        [cast_into_slot(fox_w_in, chip1, "cast_fox_w_in_0", 0, 1),
         cast_into_slot(fox_w_out, chip1, "cast_fox_w_out_0", 0, 1)], small_slots)
    fox_raw = {l: (cast_into_slot(fox_w_in, chip1, f"cast_fox_w_in_{l}", l, 1),
                   cast_into_slot(fox_w_out, chip1, f"cast_fox_w_out_{l}", l, 1)) for l in range(1, NL)}
    conv_raw = {l: (cast_into_slot(conv_w_in, chip1, f"cast_conv_w_in_{l}", l, 1),
                    cast_into_slot(conv_w_out, chip1, f"cast_conv_w_out_{l}", l, 1)) for l in range(NL)}

    def fox_weights(ga, gb):
        n = ga.shape[0]
        wfull = jnp.transpose(ga, (0, 2, 1, 3)).reshape((n, D, N_CHIPS * NA))
        wq, wk, wv, wg, wf = (wfull[:, :, 0:W], wfull[:, :, W:2 * W], wfull[:, :, 2 * W:3 * W],
                              wfull[:, :, 3 * W:4 * W], wfull[:, :, 4 * W:])
        w1 = jnp.concatenate([wq, wg], axis=-1).reshape((n, 1, D, 2 * W))
        w2 = jnp.stack([wk.reshape((n, D, H, HEAD_DIM)), wv.reshape((n, D, H, HEAD_DIM))], axis=3)
        return w1, w2.reshape((n, 1, D, 2 * W)), _pad_cols(wf, 128).reshape((n, 1, D, 128)), gb.reshape((n, 1, W, D))

    fox_w = {0: fox_weights(ga0, gb0)}
    conv_w = {}
    b_in = gsmall[:, 0:3 * NL, :].reshape((N_CHIPS, NL, 3 * Dq)).transpose((1, 0, 2)).reshape((NL, 3 * C))
    dwt = gsmall[:, 3 * NL:3 * NL + CONV_K * NL, :].reshape((N_CHIPS, NL, CONV_K, Dq))
    dwt = dwt.transpose((1, 2, 0, 3)).reshape((NL, CONV_K, C))
    r0 = (3 + CONV_K) * NL
    vecs = gsmall[:, r0:r0 + 3 * NL, :].reshape((N_CHIPS, 3, NL, Dq)).transpose((1, 2, 0, 3)).reshape((3, NL, C))
    dwb, lng, lnb = vecs[0], vecs[1], vecs[2]
    bfp = _pad_cols(fox_b_f, 128)

    h = x.reshape((S, D))
    tgt = loss_target.reshape((S, D))
    saved = []
    n_layers = norm_g.shape[0]
    for i in range(n_layers):
        j = i // 2
        g_i = norm_g[i:i + 1]
        hn = rms_fwd(h, g_i, f"rms_fwd_{i}")
        if i % 2 == 0:
            w1, w2, wfp, wo_fox = fox_w[j]
            gc, gd = conv_raw[j]
            p1, (gc,) = mm_nn(hn, w1, 0, 1, out_dtype=BF16, name=f"fox_proj_qg_{i}", gather=[(gc, 0, D // 2)])
            p2, (gc,) = mm_nn(hn, w2, 0, 1, out_dtype=BF16, name=f"fox_proj_kv_{i}", gather=[(gc, D // 2, D // 2)])
            pf = mm_nn(hn, wfp, 0, 1, out_dtype=F32, name=f"fox_proj_f_{i}")
            qa, ka, c0, c1 = fox_decay(pf, bfp[j:j + 1], H, scale, f"fox_decay_{i}")
            qn, kn = attn_norms(p1, p2, H, f"attn_norms_{i}")
            first_key, query_end = prune_tables(qn, kn, c0, c1, H, scale)
            o, yv, qb, _ = attn_fwd(p1, p2, qa, ka, H, scale, f"attn_fwd_{i}", first_key=first_key)
            h_new, (gd,) = mm_nt_res(yv, wo_fox, 0, h, f"fox_out_{i}", gather=[(gd, 0, Dq)])
            conv_w[j] = (gc, gd.reshape((1, 1, C, D)))
            saved.append((h, hn, p1, p2, pf, ka, o, yv, qb, query_end))
        else:
            wc, wo_conv = conv_w[j]
            nxt = fox_raw.get(j + 1)
            proj = mm_nn(hn, wc, 0, N_CHIPS, out_dtype=BF16, name=f"conv_proj_{i}", bias=b_in[j:j + 1],
                         gather=[(nxt[0], 0, D)] if nxt else ())
            u2, yv = conv_fwd(proj[0] if nxt else proj, dwt[j], dwb[j:j + 1], lng[j:j + 1], lnb[j:j + 1],
                              f"conv_fwd_{i}")
            h_new = mm_nt_res(yv, wo_conv, 0, h, f"conv_out_{i}", gather=[(nxt[1], 0, Dq)] if nxt else ())
            if nxt:
                fox_w[j + 1] = fox_weights(proj[1][0], h_new[1][0])
                proj, h_new = proj[0], h_new[0]
            saved.append((h, hn, proj, u2, yv))
        h = h_new

    dh, d_gf, loss_part = loss_head(h, final_norm_g.reshape((1, D)), tgt, "loss_head")

    d_norm = [None] * n_layers
    d_fox_b = [None] * NL
    d_conv_small = [None] * NL
    wgrad = {}

    def pair_sums(keys, tag):
        gs = [wgrad[k].reshape(wgrad[k].shape[:2] + (2, wgrad[k].shape[2] // 2, wgrad[k].shape[3])) for k in keys]
        rcv = swap_halves(gs, f"swap_halves_{tag}")
        return {k: pair_sum(g, r, cidx, f"pair_sum_{k[0]}{k[1]}") for k, g, r in zip(keys, gs, rcv)}

    parts, got = {}, {}

    def pending(tag):
        keys = [k for k in wgrad if k not in parts]
        parts.update(pair_sums(keys, tag))
        return keys

    for i in reversed(range(n_layers)):
        j = i // 2
        g_i = norm_g[i:i + 1]
        if i % 2 == 0:
            h_in, hn, p1, p2, pf, ka, o, yv, qb, query_end = saved[i]
            (w1, w2, wfp, wo_fox), wl = fox_w[j], 0
            wgrad[("b", j)] = mm_tn(yv, dh, 1, name=f"fox_out_dw_{i}").reshape((1, N_CHIPS, Dq, D))
            dy = mm_nn_t(dh, wo_fox, wl, f"fox_out_dx_{i}")
            do, dp1, da = attn_bwd_prep(dy, o, p1, H, f"attn_bwd_prep_{i}")
            dp1, dp2, dc, _ = attn_bwd(p1, qb, do, da, p2, ka, dp1, H, scale, f"attn_bwd_{i}", query_end=query_end)
            dz, dbf = fox_decay_bwd(dc, pf, bfp[j:j + 1], H, f"fox_decay_bwd_{i}")
            d_fox_b[j] = dbf
            keys = pending(f"l{i}")
            dw1, arrived = mm_tn(hn, dp1, 1, name=f"fox_dw_qg_{i}", exchange=[parts[k] for k in keys])
            got.update(dict(zip(keys, arrived)))
            dw1 = dw1[0, 0]
            dw2 = mm_tn(hn, dp2, 1, name=f"fox_dw_kv_{i}")[0, 0]
            dwf = mm_tn(hn, dz, 1, name=f"fox_dw_f_{i}")[0, 0]
            dw2 = dw2.reshape((D, H, 2, HEAD_DIM))
            d_in = jnp.concatenate([dw1[:, :W], dw2[:, :, 0].reshape((D, W)), dw2[:, :, 1].reshape((D, W)),
                                    dw1[:, W:], dwf[:, :H]], axis=-1)
            wgrad[("a", j)] = d_in.reshape((D, N_CHIPS, NA)).transpose((1, 0, 2))[None]
            dhn = mm_nt(dp1, w1, wl, 1, name=f"fox_dx_qg_{i}")
            dhn = mm_nt(dp2, w2, wl, 1, name=f"fox_dx_kv_{i}", res=dhn)
            dhn = mm_nt(dz, wfp, wl, 1, name=f"fox_dx_f_{i}", res=dhn)
        else:
            h_in, hn, proj, u2, yv = saved[i]
            wc, wo_conv = conv_w[j]
            wgrad[("d", j)] = mm_tn(yv, dh, 1, name=f"conv_out_dw_{i}").reshape((1, N_CHIPS, Dq, D))
            dy = mm_nn_t(dh, wo_conv, 0, f"conv_out_dx_{i}")
            du2, dproj, sm1 = conv_bwd_norm(dy, proj, u2, lng[j:j + 1], lnb[j:j + 1], f"conv_bwd_norm_{i}")
            keys = pending(f"l{i}")
            dproj, sm2, arrived = conv_bwd_taps(du2, proj, dwt[j], dproj, f"conv_bwd_taps_{i}",
                                                exchange=[parts[k] for k in keys])
            got.update(dict(zip(keys, arrived)))
            d_conv_small[j] = (sm1, sm2)
            wgrad[("c", j)] = mm_tn(hn, dproj, N_CHIPS, name=f"conv_dw_in_{i}")
            dhn = mm_nt(dproj, wc, 0, N_CHIPS, name=f"conv_dx_{i}")
        dh, d_norm[i] = rms_bwd(dhn, h_in, g_i, dh, f"rms_bwd_{i}")

    late = [k for k in wgrad if k not in parts]
    parts.update(pair_sums(late, "late"))

    zrow = jnp.zeros((1, D), F32)
    rows = list(d_norm) + [d_gf]
    rows += [_pad_cols(d_fox_b[l][:, :H], D) for l in range(NL)]
    rows += [_pad_cols(loss_part[:, :1], D)]
    for l in range(NL):
        sm1, sm2 = d_conv_small[l]
        rows += [sm2[32:33], sm2[33:34], sm1[3:4]]
    for l in range(NL):
        rows += [d_conv_small[l][1][0:CONV_K]]
    rows += [d_conv_small[l][0][2:3] for l in range(NL)]
    rows += [d_conv_small[l][0][0:1] for l in range(NL)]
    rows += [d_conv_small[l][0][1:2] for l in range(NL)]
    n_rows = sum(r.shape[0] for r in rows)
    rows += [zrow] * (SMALL_ROWS - n_rows)
    small = jnp.concatenate(rows, axis=0)

    dev = 4 * lax.axis_index("x") + 2 * lax.axis_index("y") + core
    small_slots = lax.dynamic_update_slice(jnp.zeros((N_DEV, SMALL_ROWS, D), F32), small[None], (dev, 0, 0))
    arrived = exchange_partials([parts[k] for k in late], small_slots)
    got.update(dict(zip(late, arrived[:-1])))
    tot = dev_sum(arrived[-1], "dev_sum")
    halves = {}
    for kind in "abcd":
        for l in range(NL):
            halves[kind] = chip_sum(parts[(kind, l)], got[(kind, l)], sel, f"chip_sum_{kind}{l}", NL, l,
                                    prev=halves.get(kind))
    full = join_halves([halves[kind] for kind in "abcd"])
    grad_fox_w_in = full[0].reshape(fox_w_in.shape)
    grad_fox_w_out = full[1].reshape(fox_w_out.shape)
    grad_conv_w_in = full[2].reshape(conv_w_in.shape)
    grad_conv_w_out = full[3].reshape(conv_w_out.shape)

    def mine(v):
        return lax.dynamic_slice_in_dim(v, chip * Dq, Dq, axis=v.ndim - 1)

    r = n_layers
    grad_norm_g = tot[0:r]
    grad_final = tot[r]
    grad_fox_b_f = tot[r + 1:r + 1 + NL, :H]
    loss = tot[r + 1 + NL, 0]
    r = r + 2 + NL
    gb_full = tot[r:r + 3 * NL].reshape((NL, 3 * C))
    grad_conv_b_in = lax.dynamic_slice_in_dim(gb_full, chip * 3 * Dq, 3 * Dq, axis=1)
    r += 3 * NL
    grad_conv_dw = mine(tot[r:r + CONV_K * NL].reshape((NL, CONV_K, C)))
    r += CONV_K * NL
    grad_conv_dw_b = mine(tot[r:r + NL])
    grad_conv_ln_g = mine(tot[r + NL:r + 2 * NL])
    grad_conv_ln_b = mine(tot[r + 2 * NL:r + 3 * NL])

    grads = [grad_norm_g, grad_fox_w_in, grad_fox_b_f, grad_fox_w_out, grad_conv_w_in, grad_conv_b_in,
             grad_conv_dw, grad_conv_dw_b, grad_conv_ln_g, grad_conv_ln_b, grad_conv_w_out, grad_final]
    ws = [norm_g, fox_w_in, fox_b_f, fox_w_out, conv_w_in, conv_b_in, conv_dw, conv_dw_b, conv_ln_g, conv_ln_b,
          conv_w_out, final_norm_g]
    ms = [m_norm_g, m_fox_w_in, m_fox_b_f, m_fox_w_out, m_conv_w_in, m_conv_b_in, m_conv_dw, m_conv_dw_b,
          m_conv_ln_g, m_conv_ln_b, m_conv_w_out, m_final_norm_g]
    vs = [v_norm_g, v_fox_w_in, v_fox_b_f, v_fox_w_out, v_conv_w_in, v_conv_b_in, v_conv_dw, v_conv_dw_b,
          v_conv_ln_g, v_conv_ln_b, v_conv_w_out, v_final_norm_g]
    deltas, new_ms, new_vs = [], [], []
    for n, (w_, g_, m_, v_) in enumerate(zip(ws, grads, ms, vs)):
        d_, nm_, nv_ = adamw(w_, g_, m_, v_, f"adamw_{n}")
        deltas.append(d_)
        new_ms.append(nm_)
        new_vs.append(nv_)
    grad_x = dh.reshape(x.shape)
    return (loss, grad_x, *grads, *deltas, *new_ms, *new_vs)


def mm_nt_res(y, wo, lidx, h, name, gather=()):
    M, K = y.shape
    N = wo.shape[-1]
    tm = _tile(M, 1024)
    tn = _tile(N, 1024)
    grid = (M // tm, N // tn, 1)
    return _matmul(
        y, wo, contract="nn", grid=grid, name=name,
        a_spec=pl.BlockSpec((tm, K), lambda i, j, k: (i, 0)),
        b_spec=pl.BlockSpec((None, None, K, tn), lambda i, j, k: (lidx, 0, 0, j)),
        o_spec=pl.BlockSpec((tm, tn), lambda i, j, k: (i, j)),
        out_shape=jax.ShapeDtypeStruct((M, N), F32), acc_shape=(tm, tn),
        res=h, res_spec=pl.BlockSpec((tm, tn), lambda i, j, k: (i, j)), gather=gather,
    )


def mm_nn_t(dh, wo, lidx, name):
    M, K = dh.shape
    N = wo.shape[-2]
    tm = _tile(M, 512)
    tn = _tile(N, 1024)
    grid = (M // tm, N // tn, 1)
    return _matmul(
        dh, wo, contract="nt", grid=grid, name=name,
        a_spec=pl.BlockSpec((tm, K), lambda i, j, k: (i, 0)),
        b_spec=pl.BlockSpec((None, None, tn, K), lambda i, j, k: (lidx, 0, j, 0)),
        o_spec=pl.BlockSpec((tm, tn), lambda i, j, k: (i, j)),
        out_shape=jax.ShapeDtypeStruct((M, N), F32), acc_shape=(tm, tn),
    )
```

```python
import jax
import jax.numpy as jnp
from jax import lax
from jax.experimental import pallas as pl
from jax.experimental.pallas import tpu as pltpu

F32 = jnp.float32
BF16 = jnp.bfloat16
MESH = pl.DeviceIdType.MESH

RMS_EPS = 1e-6
LN_EPS = 1e-5
CONV_K = 31
HALO = 32
HEAD_DIM = 128
ADAM_LR = 0.001
ADAM_B1 = 0.9
ADAM_B2 = 0.999
ADAM_EPS = 1e-08
ADAM_WD = 0.01
ADAM_STEP = 10
N_CHIPS = 4
N_DEV = 8
VMEM_LIMIT = 56 * 1024 * 1024
NEG_BIG = -1e30
SMALL_ROWS = 88
PACK_ROWS = 80


def _pcall(body, **kw):
    return pl.pallas_call(body, **kw)


def _cparams(sem=None):
    return pltpu.CompilerParams(dimension_semantics=sem, vmem_limit_bytes=VMEM_LIMIT)


def _tile(n, cap, mult=128):
    if n <= cap:
        return n
    t = (cap // mult) * mult
    while t >= mult:
        if n % t == 0:
            return t
        t -= mult
    raise ValueError(f"no tile for {n} under {cap}")


def _sigmoid(x):
    return 1.0 / (1.0 + jnp.exp(-x))


def _split3(x):
    hi = x.astype(BF16).astype(F32)
    r = x - hi
    mid = r.astype(BF16).astype(F32)
    lo = (r - mid).astype(BF16).astype(F32)
    return hi, mid, lo


_DN = {
    "nn": (((1,), (0,)), ((), ())),
    "nt": (((1,), (1,)), ((), ())),
    "tn": (((0,), (0,)), ((), ())),
}


def _matmul(a, b, *, contract, grid, a_spec, b_spec, o_spec, out_shape, acc_shape, name,
            bias=None, bias_spec=None, res=None, res_spec=None, alias_res=False, exchange=(), gather=()):
    nk = grid[2]
    has_bias = bias is not None
    has_res = res is not None
    assert not (exchange and gather)
    ne = len(exchange) + len(gather)
    rows = [(r0, rn) for _, r0, rn in gather]

    def _copies(parts, got, ssem, rsem):
        if gather:
            return _gather_direct_copies(got, ssem, rsem, rows)
        return _exchange_copies(parts, got, ssem, rsem)

    def body(*refs):
        a_ref, b_ref = refs[0], refs[1]
        pos = 2
        bias_ref = res_ref = None
        if has_bias:
            bias_ref = refs[pos]
            pos += 1
        if has_res:
            res_ref = refs[pos]
            pos += 1
        parts = refs[pos:pos + ne]
        pos += ne
        o_ref = refs[pos]
        got = refs[pos + 1:pos + 1 + ne]
        pos += ne
        acc_ref = refs[pos + 1] if nk > 1 else None
        if ne:
            ssem, rsem = refs[-2:]
            ids = [pl.program_id(d) for d in range(3)]

            @pl.when((ids[0] == 0) & (ids[1] == 0) & (ids[2] == 0))
            def _():
                for cp in _copies(parts, got, ssem, rsem)[0]:
                    cp.start()
        p = lax.dot_general(a_ref[...].astype(BF16), b_ref[...].astype(BF16), _DN[contract],
                            preferred_element_type=F32)

        def finish(v):
            if has_bias:
                v = v + bias_ref[...]
            if has_res:
                v = res_ref[...] + v
            o_ref[...] = v.astype(o_ref.dtype)

        if nk == 1:
            finish(p)
        else:
            k = pl.program_id(2)

            @pl.when(k == 0)
            def _():
                acc_ref[...] = p

            @pl.when(k > 0)
            def _():
                acc_ref[...] += p

            @pl.when(k == nk - 1)
            def _():
                finish(acc_ref[...])

        if ne:
            @pl.when((ids[0] == grid[0] - 1) & (ids[1] == grid[1] - 1) & (ids[2] == grid[2] - 1))
            def _():
                sends, lands = _copies(parts, got, ssem, rsem)
                for cp in lands:
                    cp.wait_recv()
                for cp in sends:
                    cp.wait_send()

    ins = [a, b]
    specs = [a_spec, b_spec]
    if has_bias:
        ins.append(bias)
        specs.append(bias_spec)
    if has_res:
        ins.append(res)
        specs.append(res_spec)
    aliases = {len(ins) - 1: 0} if (has_res and alias_res) else {}
    hbm = pl.BlockSpec(memory_space=pl.ANY)
    scratch = [pltpu.VMEM(acc_shape, F32)] if nk > 1 else []
    if not ne:
        return _pcall(
            body, name=name, grid=grid, in_specs=specs, out_specs=o_spec, out_shape=out_shape,
            scratch_shapes=scratch, input_output_aliases=aliases,
            compiler_params=_cparams(("parallel", "parallel", "arbitrary")),
        )(*ins)
    if gather:
        extra = [g for g, _, _ in gather]
        extra_out = [jax.ShapeDtypeStruct(g.shape, g.dtype) for g in extra]
        aliases = {**aliases, **{len(ins) + n: 1 + n for n in range(ne)}}
    else:
        extra = list(exchange)
        extra_out = [jax.ShapeDtypeStruct((p.shape[0], N_CHIPS - 1) + p.shape[2:], p.dtype) for p in extra]
    outs = _pcall(
        body, name=name, grid=grid, in_specs=specs + [hbm] * ne, out_specs=[o_spec] + [hbm] * ne,
        out_shape=[out_shape] + extra_out,
        scratch_shapes=scratch + [pltpu.SemaphoreType.DMA((3 * ne,)), pltpu.SemaphoreType.DMA((3 * ne,))],
        input_output_aliases=aliases, compiler_params=_cparams(("arbitrary", "arbitrary", "arbitrary")),
    )(*ins, *extra)
    return outs[0], list(outs[1:])


def mm_nn(a, w, lidx, n_slots, *, out_dtype, name, bias=None, gather=()):
    M, K = a.shape
    Ns = w.shape[-1]
    tm = _tile(M, 1024)
    tn = _tile(Ns, 1024)
    per = Ns // tn
    grid = (M // tm, n_slots * per, 1)
    return _matmul(
        a, w, contract="nn", grid=grid, name=name,
        a_spec=pl.BlockSpec((tm, K), lambda i, j, k: (i, 0)),
        b_spec=pl.BlockSpec((None, None, K, tn), lambda i, j, k: (lidx, j // per, 0, j % per)),
        o_spec=pl.BlockSpec((tm, tn), lambda i, j, k: (i, j)),
        out_shape=jax.ShapeDtypeStruct((M, n_slots * Ns), out_dtype), acc_shape=(tm, tn),
        bias=bias, bias_spec=None if bias is None else pl.BlockSpec((1, tn), lambda i, j, k: (0, j)),
        gather=gather,
    )


def mm_nt(a, w, lidx, n_slots, *, name, res=None):
    M = a.shape[0]
    N, Ns = w.shape[-2], w.shape[-1]
    tm = _tile(M, 1024)
    tn = _tile(N, 1024)
    tk = _tile(Ns, 2048)
    per = Ns // tk
    grid = (M // tm, N // tn, n_slots * per)
    return _matmul(
        a, w, contract="nt", grid=grid, name=name,
        a_spec=pl.BlockSpec((tm, tk), lambda i, j, k: (i, k)),
        b_spec=pl.BlockSpec((None, None, tn, tk), lambda i, j, k: (lidx, k // per, j, k % per)),
        o_spec=pl.BlockSpec((tm, tn), lambda i, j, k: (i, j)),
        out_shape=jax.ShapeDtypeStruct((M, N), F32), acc_shape=(tm, tn),
        res=res, res_spec=None if res is None else pl.BlockSpec((tm, tn), lambda i, j, k: (i, j)),
        alias_res=res is not None,
    )


def mm_tn(a, b, n_slots, *, name, exchange=()):
    S, M = a.shape
    Ns = b.shape[1] // n_slots
    tm = _tile(M, 1024)
    tn = _tile(Ns, 1024)
    tk = _tile(S, 2048)
    per = Ns // tn
    grid = (M // tm, n_slots * per, S // tk)
    return _matmul(
        a, b, contract="tn", grid=grid, name=name,
        a_spec=pl.BlockSpec((tk, tm), lambda i, j, k: (k, i)),
        b_spec=pl.BlockSpec((tk, tn), lambda i, j, k: (k, j)),
        o_spec=pl.BlockSpec((None, None, tm, tn), lambda i, j, k: (0, j // per, i, j % per)),
        out_shape=jax.ShapeDtypeStruct((1, n_slots, M, Ns), F32), acc_shape=(tm, tn), exchange=exchange,
    )


def rms_fwd(h, g, name):
    S, D = h.shape
    tm = _tile(S, 256, 8)

    def body(h_ref, g_ref, o_ref):
        x = h_ref[...]
        r = lax.rsqrt(jnp.mean(x * x, axis=-1, keepdims=True) + RMS_EPS)
        o_ref[...] = (x * r * g_ref[...]).astype(BF16)

    return _pcall(
        body, name=name, grid=(S // tm,),
        in_specs=[pl.BlockSpec((tm, D), lambda i: (i, 0)), pl.BlockSpec((1, D), lambda i: (0, 0))],
        out_specs=pl.BlockSpec((tm, D), lambda i: (i, 0)),
        out_shape=jax.ShapeDtypeStruct((S, D), BF16),
        compiler_params=_cparams(("parallel",)),
    )(h, g)


def _rms_bwd_rows(x, g, dy):
    d = x.shape[-1]
    r = lax.rsqrt(jnp.mean(x * x, axis=-1, keepdims=True) + RMS_EPS)
    gd = dy * g
    dx = r * gd - x * ((r * r * r) * (jnp.sum(x * gd, axis=-1, keepdims=True) / d))
    return dx, dy * x * r


def rms_bwd(dhn, h, g, dres, name):
    S, D = h.shape
    tm = _tile(S, 256, 8)

    def body(dhn_ref, h_ref, g_ref, dres_ref, dh_ref, dg_ref):
        dx, dgr = _rms_bwd_rows(h_ref[...], g_ref[...], dhn_ref[...])
        dh_ref[...] = dres_ref[...] + dx

        @pl.when(pl.program_id(0) == 0)
        def _():
            dg_ref[...] = jnp.zeros_like(dg_ref)

        dg_ref[...] += jnp.sum(dgr, axis=0, keepdims=True)

    row = pl.BlockSpec((tm, D), lambda i: (i, 0))
    vec = pl.BlockSpec((1, D), lambda i: (0, 0))
    return _pcall(
        body, name=name, grid=(S // tm,),
        in_specs=[row, row, vec, row], out_specs=[row, vec],
        out_shape=[jax.ShapeDtypeStruct((S, D), F32), jax.ShapeDtypeStruct((1, D), F32)],
        input_output_aliases={3: 0},
        compiler_params=_cparams(("arbitrary",)),
    )(dhn, h, g, dres)


def loss_head(h, g, target, name):
    S, D = h.shape
    tm = _tile(S, 256, 8)

    def body(h_ref, g_ref, t_ref, dh_ref, dg_ref, loss_ref):
        x = h_ref[...]
        gg = g_ref[...]
        r = lax.rsqrt(jnp.mean(x * x, axis=-1, keepdims=True) + RMS_EPS)
        y = x * r * gg
        e = y - t_ref[...]
        part = 0.5 * jnp.sum(jnp.mean(e * e, axis=-1, keepdims=True), axis=0, keepdims=True)
        dy = e * (1.0 / D)
        dx, dgr = _rms_bwd_rows(x, gg, dy)
        dh_ref[...] = dx

        @pl.when(pl.program_id(0) == 0)
        def _():
            dg_ref[...] = jnp.zeros_like(dg_ref)
            loss_ref[...] = jnp.zeros_like(loss_ref)

        dg_ref[...] += jnp.sum(dgr, axis=0, keepdims=True)
        loss_ref[...] += jnp.broadcast_to(part, loss_ref.shape)

    row = pl.BlockSpec((tm, D), lambda i: (i, 0))
    vec = pl.BlockSpec((1, D), lambda i: (0, 0))
    return _pcall(
        body, name=name, grid=(S // tm,),
        in_specs=[row, vec, row],
        out_specs=[row, vec, pl.BlockSpec((1, 128), lambda i: (0, 0))],
        out_shape=[jax.ShapeDtypeStruct((S, D), F32), jax.ShapeDtypeStruct((1, D), F32),
                   jax.ShapeDtypeStruct((1, 128), F32)],
        compiler_params=_cparams(("arbitrary",)),
    )(h, g, target)


def fox_decay(pf, bf, n_heads, scale, name):
    S = pf.shape[0]
    tm = min(_tile(S, 256, 8), _attn_tile(S))
    inv_scale = 1.0 / scale

    def body(pf_ref, bf_ref, qa_ref, ka_ref, c0_ref, c1_ref, carry_ref):
        @pl.when(pl.program_id(0) == 0)
        def _():
            carry_ref[...] = jnp.zeros_like(carry_ref)

        z = pf_ref[...] + bf_ref[...]
        logf = jnp.minimum(z, 0.0) - jnp.log(1.0 + jnp.exp(-jnp.abs(z)))
        row = lax.broadcasted_iota(jnp.int32, (tm, tm), 0)
        col = lax.broadcasted_iota(jnp.int32, (tm, tm), 1)
        tri = (row >= col).astype(F32)
        c = jnp.dot(tri, logf, precision=lax.Precision.HIGHEST, preferred_element_type=F32) + carry_ref[...]
        carry_ref[...] = c[tm - 1:tm, :]
        c0_ref[...] = c[0:1, :]
        c1_ref[...] = c[tm - 1:tm, :]
        lane = lax.broadcasted_iota(jnp.int32, (tm, HEAD_DIM), 1)
        for hh in range(n_heads):
            hi, mid, lo = _split3(c[:, hh:hh + 1] * inv_scale)
            qa = jnp.where(lane == 0, hi, jnp.where(lane == 1, mid, jnp.where(lane == 2, lo,
                 jnp.where(lane < 6, 1.0, 0.0))))
            ka = jnp.where(lane < 3, 1.0, jnp.where(lane == 3, -hi, jnp.where(lane == 4, -mid,
                 jnp.where(lane == 5, -lo, jnp.where(lane < 9, 1.0, 0.0)))))
            qa_ref[:, hh * HEAD_DIM:(hh + 1) * HEAD_DIM] = qa.astype(BF16)
            ka_ref[:, hh * HEAD_DIM:(hh + 1) * HEAD_DIM] = ka.astype(BF16)

    wide = pl.BlockSpec((tm, n_heads * HEAD_DIM), lambda i: (i, 0))
    edge = pl.BlockSpec((None, 1, 128), lambda i: (i, 0, 0))
    return _pcall(
        body, name=name, grid=(S // tm,),
        in_specs=[pl.BlockSpec((tm, 128), lambda i: (i, 0)), pl.BlockSpec((1, 128), lambda i: (0, 0))],
        out_specs=[wide, wide, edge, edge],
        out_shape=[jax.ShapeDtypeStruct((S, n_heads * HEAD_DIM), BF16)] * 2
                  + [jax.ShapeDtypeStruct((S // tm, 1, 128), F32)] * 2,
        scratch_shapes=[pltpu.VMEM((1, 128), F32)],
        compiler_params=_cparams(("arbitrary",)),
    )(pf, bf)


def attn_norms(p1, p2, n_heads, name):
    S = p1.shape[0]
    W = n_heads * HEAD_DIM
    t = _attn_tile(S)

    def body(q_ref, kv_ref, qn_ref, kn_ref):
        lane = lax.broadcasted_iota(jnp.int32, (1, 128), 1)
        qn = jnp.zeros((1, 128), F32)
        kn = jnp.zeros((1, 128), F32)
        for hh in range(n_heads):
            q = q_ref[:, hh * HEAD_DIM:(hh + 1) * HEAD_DIM].astype(F32)
            k = kv_ref[:, 2 * hh * HEAD_DIM:(2 * hh + 1) * HEAD_DIM].astype(F32)
            q2 = jnp.max(jnp.sum(q * q, axis=1, keepdims=True), axis=0, keepdims=True)
            k2 = jnp.max(jnp.sum(k * k, axis=1, keepdims=True), axis=0, keepdims=True)
            qn = jnp.where(lane == hh, q2, qn)
            kn = jnp.where(lane == hh, k2, kn)
        qn_ref[...] = qn
        kn_ref[...] = kn

    edge = pl.BlockSpec((None, 1, 128), lambda i: (i, 0, 0))
    return _pcall(
        body, name=name, grid=(S // t,),
        in_specs=[pl.BlockSpec((t, W), lambda i: (i, 0)), pl.BlockSpec((t, 2 * W), lambda i: (i, 0))],
        out_specs=[edge, edge], out_shape=[jax.ShapeDtypeStruct((S // t, 1, 128), F32)] * 2,
        compiler_params=_cparams(("parallel",)),
    )(p1, p2)


PRUNE_BELOW = -110.0


def prune_tables(qn, kn, c0, c1, n_heads, scale):
    nb = qn.shape[0]
    r = c0.shape[0] // nb
    qmax = jnp.sqrt(qn[:, 0, :n_heads])
    kmax = jnp.sqrt(kn[:, 0, :n_heads])
    cfirst = c0[::r, 0, :n_heads]
    clast = c1[r - 1::r, 0, :n_heads]
    bound = (scale * qmax[:, None, :] * (kmax[None, :, :] + kmax[:, None, :])
             + cfirst[:, None, :] - clast[None, :, :])
    ii = lax.broadcasted_iota(jnp.int32, (nb, nb, 1), 0)
    jj = lax.broadcasted_iota(jnp.int32, (nb, nb, 1), 1)
    skip = ((bound < PRUNE_BELOW) & (jj < ii)).astype(jnp.int32)
    first_key = jnp.sum(jnp.cumprod(skip, axis=1), axis=1)
    tail = jnp.sum(jnp.cumprod(skip[::-1], axis=0), axis=0)
    return first_key.T.reshape((-1,)), (nb - tail).T.reshape((-1,))


def _attn_tile(S):
    return 512 if S % 512 == 0 and S >= 2048 else 128


def attn_fwd(p1, p2, qa, ka, n_heads, scale, name, gather=(), first_key=None):
    S = p1.shape[0]
    W = n_heads * HEAD_DIM
    t = _attn_tile(S)
    nq = S // t
    ng = len(gather)
    assert not (ng and first_key is not None)

    def body(*refs):
        if first_key is None:
            core(0, *refs)
        else:
            fk = refs[0][pl.program_id(0) * nq + pl.program_id(1)]
            core(jnp.minimum(fk, pl.program_id(1)), *refs[1:])

    def core(js, q_ref, g_ref, qa_ref, k_ref, v_ref, ka_ref, *rest):
        o_ref, y_ref, qb_ref = rest[ng:ng + 3]
        bufs = rest[ng + 3:2 * ng + 3]
        mp_ref, qq_ref, acc_ref = rest[2 * ng + 3:2 * ng + 6]
        i = pl.program_id(1)
        if ng:
            ssem, rsem = rest[2 * ng + 6:]

            @pl.when((pl.program_id(0) == 0) & (i == 0))
            def _():
                for cp in _gather_direct_copies(bufs, ssem, rsem)[0]:
                    cp.start()
        lane = lax.broadcasted_iota(jnp.int32, (t, HEAD_DIM), 1)
        qa = qa_ref[...].astype(F32)

        def tile_with(neg_stat):
            hi, mid, lo = _split3(neg_stat)
            return jnp.where(lane == 6, hi, jnp.where(lane == 7, mid, jnp.where(lane == 8, lo, qa))).astype(BF16)

        def keys(start, width):
            rows = pl.ds(pl.multiple_of(start, t), width)
            return rows, jnp.concatenate([k_ref[rows, :], ka_ref[rows, :]], axis=1)

        def causal():
            r = lax.broadcasted_iota(jnp.int32, (t, t), 0)
            c = lax.broadcasted_iota(jnp.int32, (t, t), 1)
            return r >= c

        def over_keys(block):
            n = i - js

            def wide(jj, carry):
                block((js + 4 * jj) * t, 4 * t, False)
                return carry

            lax.fori_loop(0, n // 4, wide, 0)
            done = js + (n // 4) * 4

            @pl.when((n & 2) != 0)
            def _():
                block(done * t, 2 * t, False)

            @pl.when((n & 1) != 0)
            def _():
                block((done + (n & 2)) * t, t, False)

            block(i * t, t, True)

        qq_ref[:, :HEAD_DIM] = q_ref[...]
        qq_ref[:, HEAD_DIM:] = qa_ref[...]
        mp_ref[...] = jnp.full(mp_ref.shape, NEG_BIG, F32)

        def max_block(start, width, masked):
            _, kk = keys(start, width)
            s = lax.dot_general(qq_ref[...], kk, _DN["nt"], preferred_element_type=F32)
            if masked:
                s = jnp.where(causal(), s, NEG_BIG)
            part = s[:, 0:HEAD_DIM]
            for a in range(1, width // HEAD_DIM):
                part = jnp.maximum(part, s[:, a * HEAD_DIM:(a + 1) * HEAD_DIM])
            mp_ref[...] = jnp.maximum(mp_ref[...], part)

        over_keys(max_block)
        m = jnp.max(mp_ref[...], axis=1, keepdims=True)
        qq_ref[:, HEAD_DIM:] = tile_with(-m)
        acc_ref[...] = jnp.zeros_like(acc_ref)

        def sum_block(start, width, masked):
            rows, kk = keys(start, width)
            a = lax.dot_general(qq_ref[...], kk, _DN["nt"], preferred_element_type=F32)
            p = jnp.exp(scale * a)
            if masked:
                p = jnp.where(causal(), p, 0.0)
            ones0 = jnp.where(lax.broadcasted_iota(jnp.int32, (width, HEAD_DIM), 1) == 0, 1.0, 0.0).astype(BF16)
            vv = jnp.concatenate([v_ref[rows, :], ones0], axis=1)
            acc_ref[...] += jnp.dot(p.astype(BF16), vv, preferred_element_type=F32)

        over_keys(sum_block)

        l = acc_ref[:, HEAD_DIM:HEAD_DIM + 1]
        o = acc_ref[:, :HEAD_DIM] / l
        gate = g_ref[...].astype(F32)
        o_ref[...] = o.astype(BF16)
        y_ref[...] = (o * (gate * _sigmoid(gate))).astype(BF16)
        qb_ref[...] = tile_with(-(m + jnp.log(l) * (1.0 / scale)))

        if ng:
            @pl.when((pl.program_id(0) == n_heads - 1) & (i == nq - 1))
            def _():
                sends, lands = _gather_direct_copies(bufs, ssem, rsem)
                for cp in lands:
                    cp.wait_recv()
                for cp in sends:
                    cp.wait_send()

    H = n_heads
    qtile = lambda off: pl.BlockSpec((t, HEAD_DIM), lambda h, i, *_: (i, off + h))
    full = lambda fn: pl.BlockSpec((S, HEAD_DIM), fn)
    hbm = pl.BlockSpec(memory_space=pl.ANY)
    sems = [pltpu.SemaphoreType.DMA((3 * ng,)), pltpu.SemaphoreType.DMA((3 * ng,))] if ng else []
    in_specs = [qtile(0), qtile(H), qtile(0), full(lambda h, i, *_: (0, 2 * h)), full(lambda h, i, *_: (0, 2 * h + 1)),
                full(lambda h, i, *_: (0, h))] + [hbm] * ng
    out_specs = [qtile(0), qtile(0), qtile(0)] + [hbm] * ng
    scratch = [pltpu.VMEM((t, HEAD_DIM), F32), pltpu.VMEM((t, 2 * HEAD_DIM), BF16),
               pltpu.VMEM((t, 2 * HEAD_DIM), F32)] + sems
    out_shape = [jax.ShapeDtypeStruct((S, W), BF16)] * 3 + [jax.ShapeDtypeStruct(b.shape, b.dtype) for b in gather]
    sem = _cparams(("arbitrary", "arbitrary") if ng else ("parallel", "arbitrary"))
    if first_key is None:
        outs = _pcall(
            body, name=name, grid=(H, nq), in_specs=in_specs, out_specs=out_specs, out_shape=out_shape,
            scratch_shapes=scratch, input_output_aliases={6 + n: 3 + n for n in range(ng)}, compiler_params=sem,
        )(p1, p1, qa, p2, p2, ka, *gather)
    else:
        grid_spec = pltpu.PrefetchScalarGridSpec(num_scalar_prefetch=1, grid=(H, nq), in_specs=in_specs,
                                                 out_specs=out_specs, scratch_shapes=scratch)
        outs = _pcall(body, name=name, grid_spec=grid_spec, out_shape=out_shape, compiler_params=sem,
                      )(first_key, p1, p1, qa, p2, p2, ka)
    return outs[0], outs[1], outs[2], list(outs[3:])


def attn_bwd_prep(dy, o, p1, n_heads, name):
    S, W = dy.shape
    tm = _tile(S, 256, 8)
    H = n_heads

    def body(dy_ref, o_ref, g_ref, do_ref, dg_ref, da_ref):
        lane = lax.broadcasted_iota(jnp.int32, (tm, HEAD_DIM), 1)
        for hh in range(H):
            cs = slice(hh * HEAD_DIM, (hh + 1) * HEAD_DIM)
            g = g_ref[:, cs].astype(F32)
            oo = o_ref[:, cs].astype(F32)
            dyv = dy_ref[:, cs]
            sg = _sigmoid(g)
            do = dyv * (g * sg)
            do_ref[:, cs] = do.astype(BF16)
            dg_ref[:, cs] = (dyv * oo * (sg * (1.0 + g * (1.0 - sg)))).astype(BF16)
            hi, mid, lo = _split3(-jnp.sum(do * oo, axis=1, keepdims=True))
            da = jnp.where(lane == 0, hi, jnp.where(lane == 1, mid, jnp.where(lane == 2, lo, 0.0)))
            da_ref[:, cs] = da.astype(BF16)

    row = lambda blk: pl.BlockSpec((tm, W), lambda i: (i, blk))
    return _pcall(
        body, name=name, grid=(S // tm,),
        in_specs=[row(0), row(0), row(1)],
        out_specs=[row(0), row(1), row(0)],
        out_shape=[jax.ShapeDtypeStruct((S, W), BF16), jax.ShapeDtypeStruct((S, 2 * W), BF16),
                   jax.ShapeDtypeStruct((S, W), BF16)],
        compiler_params=_cparams(("parallel",)),
    )(dy, o, p1)


def attn_bwd(p1, qb, do, da, p2, ka, dp1, n_heads, scale, name, exchange=(), query_end=None):
    S = p1.shape[0]
    W = n_heads * HEAD_DIM
    t = _attn_tile(S)
    nb = S // t
    H = n_heads
    ne = len(exchange)
    assert not (ne and query_end is not None)

    def body(*refs):
        if query_end is None:
            core(nb, *refs)
        else:
            qe = refs[0][pl.program_id(0) * nb + pl.program_id(1)]
            core(jnp.clip(qe, pl.program_id(1) + 1, nb), *refs[1:])

    def core(iend, q_ref, qb_ref, do_ref, da_ref, k_ref, v_ref, ka_ref, dp1_in, *rest):
        del dp1_in
        parts = rest[:ne]
        dq_ref, dkv_ref, dc_ref = rest[ne:ne + 3]
        got = rest[ne + 3:2 * ne + 3]
        dq_acc, dk_acc, dv_acc = rest[2 * ne + 3:2 * ne + 6]
        h = pl.program_id(0)
        j = pl.program_id(1)
        if ne:
            ssem, rsem = rest[2 * ne + 6:]

            @pl.when((h == 0) & (j == 0))
            def _():
                for cp in _exchange_copies(parts, got, ssem, rsem)[0]:
                    cp.start()

        @pl.when(j == 0)
        def _():
            dq_acc[...] = jnp.zeros_like(dq_acc)

        @pl.when((j == 0) & (h == 0))
        def _():
            dc_ref[...] = jnp.zeros_like(dc_ref)

        lane = lax.broadcasted_iota(jnp.int32, (t, HEAD_DIM), 1)
        ones3 = jnp.where(lane < 3, 1.0, 0.0).astype(BF16)
        kk = jnp.concatenate([k_ref[...], ka_ref[...]], axis=1)
        vv = jnp.concatenate([v_ref[...], ones3], axis=1)
        dk_acc[...] = jnp.zeros_like(dk_acc)
        dv_acc[...] = jnp.zeros_like(dv_acc)

        def block(start, width, masked):
            rows = pl.ds(pl.multiple_of(start, t), width)
            qq = jnp.concatenate([q_ref[rows, :], qb_ref[rows, :]], axis=1)
            dd = jnp.concatenate([do_ref[rows, :], da_ref[rows, :]], axis=1)
            a = lax.dot_general(qq, kk, _DN["nt"], preferred_element_type=F32)
            p = jnp.exp(scale * a)
            if masked:
                r = lax.broadcasted_iota(jnp.int32, (t, t), 0)
                c = lax.broadcasted_iota(jnp.int32, (t, t), 1)
                p = jnp.where(r >= c, p, 0.0)
            dpd = lax.dot_general(dd, vv, _DN["nt"], preferred_element_type=F32)
            ds = (p * dpd).astype(BF16)
            pb = p.astype(BF16)
            dv_acc[...] += lax.dot_general(pb, dd, _DN["tn"], preferred_element_type=F32)
            dk_acc[...] += lax.dot_general(ds, qq, _DN["tn"], preferred_element_type=F32)
            dq_acc[rows, :] += jnp.dot(ds, kk, preferred_element_type=F32)

        block(j * t, t, True)
        n_after = iend - 1 - j

        @pl.when((n_after & 1) != 0)
        def _():
            block((j + 1) * t, t, False)

        first = j + 1 + (n_after & 1)

        def loop_body(ii, carry):
            block((first + 2 * ii) * t, 2 * t, False)
            return carry

        lax.fori_loop(0, n_after // 2, loop_body, 0)

        dkv_ref[...] = jnp.concatenate([dk_acc[:, :HEAD_DIM] * scale, dv_acc[:, :HEAD_DIM]], axis=1).astype(BF16)
        colsum = dk_acc[:, HEAD_DIM + 3:HEAD_DIM + 4]
        krows = pl.ds(pl.multiple_of(j * t, t), t)
        dc_ref[krows, :] += jnp.where(lane == h, -colsum, 0.0)

        @pl.when(j == nb - 1)
        def _():
            dq_ref[...] = (dq_acc[:, :HEAD_DIM] * scale).astype(BF16)
            lane_s = lax.broadcasted_iota(jnp.int32, (S, HEAD_DIM), 1)
            dc_ref[...] += jnp.where(lane_s == h, dq_acc[:, HEAD_DIM:HEAD_DIM + 1], 0.0)

        if ne:
            @pl.when((h == H - 1) & (j == nb - 1))
            def _():
                sends, lands = _exchange_copies(parts, got, ssem, rsem)
                for cp in lands:
                    cp.wait_recv()
                for cp in sends:
                    cp.wait_send()

    full = lambda fn: pl.BlockSpec((S, HEAD_DIM), fn)
    ktile = lambda fn: pl.BlockSpec((t, HEAD_DIM), fn)
    hbm = pl.BlockSpec(memory_space=pl.ANY)
    sems = [pltpu.SemaphoreType.DMA((3 * ne,)), pltpu.SemaphoreType.DMA((3 * ne,))] if ne else []
    in_specs = [full(lambda h, j, *_: (0, h)), full(lambda h, j, *_: (0, h)), full(lambda h, j, *_: (0, h)),
                full(lambda h, j, *_: (0, h)),
                ktile(lambda h, j, *_: (j, 2 * h)), ktile(lambda h, j, *_: (j, 2 * h + 1)),
                ktile(lambda h, j, *_: (j, h)), hbm] + [hbm] * ne
    out_specs = [full(lambda h, j, *_: (0, h)),
                 pl.BlockSpec((t, 2 * HEAD_DIM), lambda h, j, *_: (j, h)),
                 pl.BlockSpec((S, 128), lambda h, j, *_: (0, 0))] + [hbm] * ne
    out_shape = ([jax.ShapeDtypeStruct((S, 2 * W), BF16), jax.ShapeDtypeStruct((S, 2 * W), BF16),
                  jax.ShapeDtypeStruct((S, 128), F32)]
                 + [jax.ShapeDtypeStruct((p.shape[0], N_CHIPS - 1) + p.shape[2:], p.dtype) for p in exchange])
    scratch = [pltpu.VMEM((S, 2 * HEAD_DIM), F32), pltpu.VMEM((t, 2 * HEAD_DIM), F32),
               pltpu.VMEM((t, 2 * HEAD_DIM), F32)] + sems
    if query_end is None:
        outs = _pcall(
            body, name=name, grid=(H, nb), in_specs=in_specs, out_specs=out_specs, out_shape=out_shape,
            scratch_shapes=scratch, input_output_aliases={7: 0}, compiler_params=_cparams(("arbitrary", "arbitrary")),
        )(p1, qb, do, da, p2, p2, ka, dp1, *exchange)
    else:
        grid_spec = pltpu.PrefetchScalarGridSpec(num_scalar_prefetch=1, grid=(H, nb), in_specs=in_specs,
                                                 out_specs=out_specs, scratch_shapes=scratch)
        outs = _pcall(body, name=name, grid_spec=grid_spec, out_shape=out_shape, input_output_aliases={8: 0},
                      compiler_params=_cparams(("arbitrary", "arbitrary")),
                      )(query_end, p1, qb, do, da, p2, p2, ka, dp1)
    return outs[0], outs[1], outs[2], list(outs[3:])


def fox_decay_bwd(dc, pf, bf, n_heads, name):
    S = dc.shape[0]
    tm = _tile(S, 256, 8)
    nb = S // tm

    def body(dc_ref, pf_ref, bf_ref, dz_ref, db_ref, carry_ref):
        @pl.when(pl.program_id(0) == 0)
        def _():
            carry_ref[...] = jnp.zeros_like(carry_ref)
            db_ref[...] = jnp.zeros_like(db_ref)

        row = lax.broadcasted_iota(jnp.int32, (tm, tm), 0)
        col = lax.broadcasted_iota(jnp.int32, (tm, tm), 1)
        tri = (row <= col).astype(F32)
        dlogf = jnp.dot(tri, dc_ref[...], precision=lax.Precision.HIGHEST, preferred_element_type=F32) + carry_ref[...]
        carry_ref[...] = dlogf[0:1, :]
        z = pf_ref[...] + bf_ref[...]
        lane = lax.broadcasted_iota(jnp.int32, (tm, 128), 1)
        dz = jnp.where(lane < n_heads, dlogf * _sigmoid(-z), 0.0)
        dz_ref[...] = dz.astype(BF16)
        db_ref[...] += jnp.sum(dz, axis=0, keepdims=True)

    rev = pl.BlockSpec((tm, 128), lambda i: (nb - 1 - i, 0))
    vec = pl.BlockSpec((1, 128), lambda i: (0, 0))
    return _pcall(
        body, name=name, grid=(nb,),
        in_specs=[rev, rev, vec], out_specs=[rev, vec],
        out_shape=[jax.ShapeDtypeStruct((S, 128), BF16), jax.ShapeDtypeStruct((1, 128), F32)],
        scratch_shapes=[pltpu.VMEM((1, 128), F32)],
        compiler_params=_cparams(("arbitrary",)),
    )(dc, pf, bf)


def _conv_tile(S):
    return _tile(S, 256, HALO)


def _fill_glu(ubuf, a_ref, b_ref, ah_ref, bh_ref, first, tm):
    uh = ah_ref[...].astype(F32) * _sigmoid(bh_ref[...].astype(F32))
    ubuf[0:HALO, :] = jnp.where(first, 0.0, uh)
    ubuf[HALO:HALO + tm, :] = a_ref[...].astype(F32) * _sigmoid(b_ref[...].astype(F32))


def conv_fwd(proj, dw, dwb, lng, lnb, name, gather=()):
    S = proj.shape[0]
    C = proj.shape[1] // 3
    tm = _conv_tile(S)
    hb = tm // HALO
    nch = C // 128
    rb = _tile(tm, 128, 8)
    ng = len(gather)
    rows = [(r0, rn) for _, r0, rn in gather]

    def body(a_ref, b_ref, g_ref, ah_ref, bh_ref, dw_ref, dwb_ref, lng_ref, lnb_ref, *rest):
        u2_ref, y_ref = rest[ng:ng + 2]
        bufs = rest[ng + 2:2 * ng + 2]
        ubuf, sh = rest[2 * ng + 2:2 * ng + 4]
        i = pl.program_id(0)
        if ng:
            ssem, rsem = rest[2 * ng + 4:]

            @pl.when(i == 0)
            def _():
                for cp in _gather_direct_copies(bufs, ssem, rsem, rows)[0]:
                    cp.start()
        _fill_glu(ubuf, a_ref, b_ref, ah_ref, bh_ref, i == 0, tm)

        def chunk(cc, carry):
            cols = pl.ds(pl.multiple_of(cc * 128, 128), 128)
            for r0 in range(0, tm, rb):
                acc = jnp.broadcast_to(dwb_ref[:, cols], (rb, 128))
                for b in range(8):
                    taps = list(range(b, CONV_K, 8))
                    n = rb + 8 * (len(taps) - 1)
                    sh[0:n, :] = ubuf[pl.ds(HALO - (CONV_K - 1) + b + r0, n), cols]
                    for a, k in enumerate(taps):
                        acc = acc + dw_ref[k:k + 1, cols] * sh[8 * a:8 * a + rb, :]
                u2_ref[pl.ds(r0, rb), cols] = acc
            return carry

        lax.fori_loop(0, nch, chunk, 0)
        x = u2_ref[...]
        mu = jnp.mean(x, axis=-1, keepdims=True)
        xc = x - mu
        var = jnp.mean(xc * xc, axis=-1, keepdims=True)
        ln = xc * lax.rsqrt(var + LN_EPS) * lng_ref[...] + lnb_ref[...]
        gate = g_ref[...].astype(F32)
        y_ref[...] = ((ln * _sigmoid(ln)) * (gate * _sigmoid(gate))).astype(BF16)

        if ng:
            @pl.when(i == S // tm - 1)
            def _():
                sends, lands = _gather_direct_copies(bufs, ssem, rsem, rows)
                for cp in lands:
                    cp.wait_recv()
                for cp in sends:
                    cp.wait_send()

    blk = lambda cb: pl.BlockSpec((tm, C), lambda i: (i, cb))
    halo = lambda cb: pl.BlockSpec((HALO, C), lambda i: (jnp.maximum(i * hb - 1, 0), cb))
    vec = pl.BlockSpec((1, C), lambda i: (0, 0))
    hbm = pl.BlockSpec(memory_space=pl.ANY)
    sems = [pltpu.SemaphoreType.DMA((3 * ng,)), pltpu.SemaphoreType.DMA((3 * ng,))] if ng else []
    outs = _pcall(
        body, name=name, grid=(S // tm,),
        in_specs=[blk(0), blk(1), blk(2), halo(0), halo(1),
                  pl.BlockSpec((CONV_K, C), lambda i: (0, 0)), vec, vec, vec] + [hbm] * ng,
        out_specs=[blk(0), blk(0)] + [hbm] * ng,
        out_shape=[jax.ShapeDtypeStruct((S, C), F32), jax.ShapeDtypeStruct((S, C), BF16)]
                  + [jax.ShapeDtypeStruct(g.shape, g.dtype) for g, _, _ in gather],
        scratch_shapes=[pltpu.VMEM((HALO + tm, C), F32), pltpu.VMEM((rb + HALO, 128), F32)] + sems,
        input_output_aliases={9 + n: 2 + n for n in range(ng)},
        compiler_params=_cparams(("arbitrary",) if ng else ("parallel",)),
    )(proj, proj, proj, proj, proj, dw, dwb, lng, lnb, *[g for g, _, _ in gather])
    return outs[0], outs[1], list(outs[2:])


def conv_bwd_norm(dy, proj, u2, lng, lnb, name):
    S, C = dy.shape
    tm = _tile(S, 256, 8)

    def body(dy_ref, g_ref, u2_ref, lng_ref, lnb_ref, du2_ref, dg_ref, sm_ref):
        x = u2_ref[...]
        mu = jnp.mean(x, axis=-1, keepdims=True)
        xc = x - mu
        var = jnp.mean(xc * xc, axis=-1, keepdims=True)
        rs = lax.rsqrt(var + LN_EPS)
        xhat = xc * rs
        gam = lng_ref[...]
        ln = xhat * gam + lnb_ref[...]
        sl = _sigmoid(ln)
        u3 = ln * sl
        gate = g_ref[...].astype(F32)
        sg = _sigmoid(gate)
        dyv = dy_ref[...]
        dgate = dyv * u3 * (sg * (1.0 + gate * (1.0 - sg)))
        dln = (dyv * (gate * sg)) * (sl * (1.0 + ln * (1.0 - sl)))
        dxh = dln * gam
        du2 = rs * (dxh - jnp.mean(dxh, axis=-1, keepdims=True)
                    - xhat * jnp.mean(dxh * xhat, axis=-1, keepdims=True))
        du2_ref[...] = du2
        dg_ref[...] = dgate.astype(BF16)

        @pl.when(pl.program_id(0) == 0)
        def _():
            sm_ref[...] = jnp.zeros_like(sm_ref)

        sm_ref[0:1, :] += jnp.sum(dln * xhat, axis=0, keepdims=True)
        sm_ref[1:2, :] += jnp.sum(dln, axis=0, keepdims=True)
        sm_ref[2:3, :] += jnp.sum(du2, axis=0, keepdims=True)
        sm_ref[3:4, :] += jnp.sum(dgate, axis=0, keepdims=True)

    blk = lambda cb: pl.BlockSpec((tm, C), lambda i: (i, cb))
    vec = pl.BlockSpec((1, C), lambda i: (0, 0))
    return _pcall(
        body, name=name, grid=(S // tm,),
        in_specs=[blk(0), blk(2), blk(0), vec, vec],
        out_specs=[blk(0), blk(2), pl.BlockSpec((8, C), lambda i: (0, 0))],
        out_shape=[jax.ShapeDtypeStruct((S, C), F32), jax.ShapeDtypeStruct((S, 3 * C), BF16),
                   jax.ShapeDtypeStruct((8, C), F32)],
        compiler_params=_cparams(("arbitrary",)),
    )(dy, proj, u2, lng, lnb)


def conv_bwd_taps(du2, proj, dw, dproj, name, exchange=()):
    S, C = du2.shape
    tm = _conv_tile(S)
    hb = tm // HALO
    nb = S // tm
    nch = C // 128
    rb = _tile(tm, 128, 8)
    ne = len(exchange)

    def body(d_ref, dh_ref, a_ref, b_ref, ah_ref, bh_ref, dw_ref, dp_in, *rest):
        del dp_in
        parts = rest[:ne]
        dab_ref, sm_ref = rest[ne:ne + 2]
        got = rest[ne + 2:2 * ne + 2]
        ubuf, dbuf, sh, sh2 = rest[2 * ne + 2:2 * ne + 6]
        i = pl.program_id(0)
        if ne:
            ssem, rsem = rest[2 * ne + 6:]

            @pl.when(i == 0)
            def _():
                for cp in _exchange_copies(parts, got, ssem, rsem)[0]:
                    cp.start()
        _fill_glu(ubuf, a_ref, b_ref, ah_ref, bh_ref, i == 0, tm)
        dbuf[0:tm, :] = d_ref[...]
        dbuf[tm:tm + HALO, :] = jnp.where(i == nb - 1, 0.0, dh_ref[...])

        @pl.when(i == 0)
        def _():
            sm_ref[...] = jnp.zeros_like(sm_ref)

        def chunk(cc, carry):
            cols = pl.ds(pl.multiple_of(cc * 128, 128), 128)
            cols_b = pl.ds(pl.multiple_of(C + cc * 128, 128), 128)
            for r0 in range(0, tm, rb):
                d0 = dbuf[r0:r0 + rb, cols]
                du = jnp.zeros((rb, 128), F32)
                for b in range(8):
                    offs = list(range(b, CONV_K, 8))
                    n = rb + 8 * (len(offs) - 1)
                    sh[0:n, :] = dbuf[pl.ds(r0 + b, n), cols]
                    for a, o in enumerate(offs):
                        k = CONV_K - 1 - o
                        du = du + dw_ref[k:k + 1, cols] * sh[8 * a:8 * a + rb, :]
                    sh2[0:n, :] = ubuf[pl.ds(HALO - (CONV_K - 1) + b + r0, n), cols]
                    for a, k in enumerate(offs):
                        sm_ref[k:k + 1, cols] += jnp.sum(d0 * sh2[8 * a:8 * a + rb, :], axis=0, keepdims=True)
                rows = pl.ds(r0, rb)
                av = a_ref[rows, cols].astype(F32)
                sb = _sigmoid(b_ref[rows, cols].astype(F32))
                da = du * sb
                db = du * av * sb * (1.0 - sb)
                dab_ref[rows, cols] = da.astype(BF16)
                dab_ref[rows, cols_b] = db.astype(BF16)
                sm_ref[32:33, cols] += jnp.sum(da, axis=0, keepdims=True)
                sm_ref[33:34, cols] += jnp.sum(db, axis=0, keepdims=True)
            return carry

        lax.fori_loop(0, nch, chunk, 0)

        if ne:
            @pl.when(i == nb - 1)
            def _():
                sends, lands = _exchange_copies(parts, got, ssem, rsem)
                for cp in lands:
                    cp.wait_recv()
                for cp in sends:
                    cp.wait_send()

    blk = lambda cb: pl.BlockSpec((tm, C), lambda i: (i, cb))
    halo = lambda cb: pl.BlockSpec((HALO, C), lambda i: (jnp.maximum(i * hb - 1, 0), cb))
    nxt = pl.BlockSpec((HALO, C), lambda i: (jnp.minimum((i + 1) * hb, nb * hb - 1), 0))
    hbm = pl.BlockSpec(memory_space=pl.ANY)
    sems = [pltpu.SemaphoreType.DMA((3 * ne,)), pltpu.SemaphoreType.DMA((3 * ne,))] if ne else []
    outs = _pcall(
        body, name=name, grid=(nb,),
        in_specs=[blk(0), nxt, blk(0), blk(1), halo(0), halo(1),
                  pl.BlockSpec((CONV_K, C), lambda i: (0, 0)), hbm] + [hbm] * ne,
        out_specs=[pl.BlockSpec((tm, 2 * C), lambda i: (i, 0)), pl.BlockSpec((40, C), lambda i: (0, 0))] + [hbm] * ne,
        out_shape=[jax.ShapeDtypeStruct((S, 3 * C), BF16), jax.ShapeDtypeStruct((40, C), F32)]
                  + [jax.ShapeDtypeStruct((p.shape[0], N_CHIPS - 1) + p.shape[2:], p.dtype) for p in exchange],
        scratch_shapes=[pltpu.VMEM((HALO + tm, C), F32), pltpu.VMEM((tm + HALO, C), F32),
                        pltpu.VMEM((rb + HALO, 128), F32), pltpu.VMEM((rb + HALO, 128), F32)] + sems,
        input_output_aliases={7: 0},
        compiler_params=_cparams(("arbitrary",)),
    )(du2, du2, proj, proj, proj, proj, dw, dproj, *exchange)
    return outs[0], outs[1], list(outs[2:])


def _rows_tile(R, Cc, budget=1 << 18):
    cap = max(8, budget // max(Cc, 1))
    if R <= cap:
        return R
    t = (cap // 8) * 8
    while t >= 8:
        if R % t == 0:
            return t
        t -= 8
    return R


def cast_into_slot(w, chip, name, l0, nl):
    _, R, Cc = w.shape
    tr = _rows_tile(R, Cc)

    def body(s_ref, w_ref, o_ref):
        del s_ref
        o_ref[...] = w_ref[...].astype(BF16)

    grid_spec = pltpu.PrefetchScalarGridSpec(
        num_scalar_prefetch=1, grid=(nl, R // tr),
        in_specs=[pl.BlockSpec((None, tr, Cc), lambda l, r, s: (l0 + l, r, 0))],
        out_specs=pl.BlockSpec((None, None, tr, Cc), lambda l, r, s: (l, s[0], r, 0)),
    )
    return _pcall(
        body, name=name, grid_spec=grid_spec, out_shape=jax.ShapeDtypeStruct((nl, N_CHIPS, R, Cc), BF16),
        compiler_params=_cparams(("parallel", "parallel")),
    )(chip, w)


def pair_sum(g, rcv, cidx, name):
    L, K, _, half, Cc = g.shape
    g5 = g
    tr = _rows_tile(half, Cc)

    def body(c_ref, g_ref, r_ref, o_ref):
        del c_ref
        o_ref[...] = (g_ref[...] + r_ref[...]).astype(BF16)

    grid_spec = pltpu.PrefetchScalarGridSpec(
        num_scalar_prefetch=1, grid=(L, K, half // tr),
        in_specs=[pl.BlockSpec((None, None, None, tr, Cc), lambda l, k, r, c: (l, k, c[0], r, 0)),
                  pl.BlockSpec((None, None, tr, Cc), lambda l, k, r, c: (l, k, r, 0))],
        out_specs=pl.BlockSpec((None, None, tr, Cc), lambda l, k, r, c: (l, k, r, 0)),
    )
    return _pcall(
        body, name=name, grid_spec=grid_spec, out_shape=jax.ShapeDtypeStruct((L, K, half, Cc), BF16),
        compiler_params=_cparams(("parallel", "parallel", "parallel")),
    )(cidx, g5, rcv)


def chip_sum(parts, got, sel, name, layers, l0, prev=None):
    Lp, _, R, Cc = parts.shape
    n_got = got.shape[1]
    tr = _rows_tile(R, Cc)

    def body(s_ref, p_ref, g_ref, *rest):
        del s_ref
        o_ref = rest[-1]
        acc = p_ref[...].astype(F32)
        for k in range(n_got):
            acc = acc + g_ref[k].astype(F32)
        o_ref[...] = acc

    in_specs = [pl.BlockSpec((None, None, tr, Cc), lambda l, r, s: (l, s[0], r, 0)),
                pl.BlockSpec((None, n_got, tr, Cc), lambda l, r, s: (l, 0, r, 0))]
    ops = [sel, parts, got]
    aliases = {}
    if prev is not None:
        in_specs.append(pl.BlockSpec(memory_space=pl.ANY))
        ops.append(prev)
        aliases = {3: 0}
    grid_spec = pltpu.PrefetchScalarGridSpec(
        num_scalar_prefetch=1, grid=(Lp, R // tr), in_specs=in_specs,
        out_specs=pl.BlockSpec((None, None, tr, Cc), lambda l, r, s: (l0 + l, s[1], r, 0)),
    )
    return _pcall(
        body, name=name, grid_spec=grid_spec, out_shape=jax.ShapeDtypeStruct((layers, 2, R, Cc), F32),
        input_output_aliases=aliases, compiler_params=_cparams(("parallel", "parallel")),
    )(*ops)


def dev_sum(parts, name):
    K, R, Cc = parts.shape

    def body(p_ref, o_ref):
        acc = p_ref[0]
        for k in range(1, K):
            acc = acc + p_ref[k]
        o_ref[...] = acc

    return _pcall(
        body, name=name, grid=(R // 8,),
        in_specs=[pl.BlockSpec((K, 8, Cc), lambda r: (0, r, 0))],
        out_specs=pl.BlockSpec((8, Cc), lambda r: (r, 0)),
        out_shape=jax.ShapeDtypeStruct((R, Cc), F32), compiler_params=_cparams(("parallel",)),
    )(parts)


def adamw(w, g, m, v, name):
    shape = w.shape
    if w.ndim == 3:
        L, R, Cc = shape
    else:
        L, R, Cc = 1, (1 if w.ndim == 1 else shape[0]), shape[-1]
    view = lambda t: t.reshape((L, R, Cc))
    tr = _rows_tile(R, Cc, budget=1 << 17)
    c1 = 1.0 - ADAM_B1 ** ADAM_STEP
    c2 = 1.0 - ADAM_B2 ** ADAM_STEP

    def body(w_ref, g_ref, m_ref, v_ref, d_ref, nm_ref, nv_ref):
        gg = g_ref[...]
        nm = ADAM_B1 * m_ref[...] + (1.0 - ADAM_B1) * gg
        nv = ADAM_B2 * v_ref[...] + (1.0 - ADAM_B2) * (gg * gg)
        d_ref[...] = -ADAM_LR * ((nm / c1) / (jnp.sqrt(nv / c2) + ADAM_EPS) + ADAM_WD * w_ref[...])
        nm_ref[...] = nm
        nv_ref[...] = nv

    spec = pl.BlockSpec((None, tr, Cc), lambda l, r: (l, r, 0))
    outs = _pcall(
        body, name=name, grid=(L, R // tr), in_specs=[spec] * 4, out_specs=[spec] * 3,
        out_shape=[jax.ShapeDtypeStruct((L, R, Cc), F32)] * 3, compiler_params=_cparams(("parallel", "parallel")),
    )(view(w), view(g), view(m), view(v))
    return tuple(o.reshape(shape) for o in outs)


def _place():
    x, y, c = lax.axis_index("x"), lax.axis_index("y"), lax.axis_index("c")
    other_chips = [(1 - x, y), (x, 1 - y), (1 - x, 1 - y)]
    return x, y, c, other_chips


def _rcopy(src, dst, ssem, rsem, k, to):
    return pltpu.make_async_remote_copy(src_ref=src, dst_ref=dst, send_sem=ssem.at[k], recv_sem=rsem.at[k],
                                        device_id=to, device_id_type=MESH)


def gather_weights(slots, small):
    nt = len(slots)

    def body(*refs):
        outs, small_out = refs[nt + 1:2 * nt + 1], refs[2 * nt + 1]
        ssem, rsem = refs[2 * nt + 2:]
        x, y, c, chips = _place()
        me = 2 * x + y
        sib = (x, y, 1 - c)

        def rows(t, half_of):
            half = outs[t].shape[2] // 2
            return pl.ds(half_of * half, half)

        first, passed = [], []
        for t in range(nt):
            mine = outs[t].at[:, me, rows(t, c), :]
            for j, chip in enumerate(chips):
                first.append(_rcopy(mine, mine, ssem, rsem, 6 * t + j, (*chip, c)))
        for j, chip in enumerate(chips):
            first.append(_rcopy(small_out.at[me], small_out.at[me], ssem, rsem, 6 * nt + j, (*chip, c)))
        for cp in first:
            cp.start()
        for t in range(nt):
            for j, (px, py) in enumerate(chips):
                land = outs[t].at[:, 2 * px + py, rows(t, c), :]
                _rcopy(land, land, ssem, rsem, 6 * t + j, (x, y, c)).wait_recv()
                fw = _rcopy(land, land, ssem, rsem, 6 * t + 3 + j, sib)
                fw.start()
                passed.append(fw)
        for j, (px, py) in enumerate(chips):
            land = small_out.at[2 * px + py]
            _rcopy(land, land, ssem, rsem, 6 * nt + j, (x, y, c)).wait_recv()
        for t in range(nt):
            for j, (px, py) in enumerate(chips):
                land = outs[t].at[:, 2 * px + py, rows(t, 1 - c), :]
                _rcopy(land, land, ssem, rsem, 6 * t + 3 + j, (x, y, c)).wait_recv()
        for cp in first + passed:
            cp.wait_send()

    ops = list(slots) + [small]
    nsem = 6 * nt + 3
    return _pcall(
        body, name="gather_weights", out_shape=[jax.ShapeDtypeStruct(s.shape, s.dtype) for s in ops],
        in_specs=[pl.BlockSpec(memory_space=pl.ANY)] * (nt + 1),
        out_specs=[pl.BlockSpec(memory_space=pl.ANY)] * (nt + 1),
        input_output_aliases={n: n for n in range(nt + 1)},
        scratch_shapes=[pltpu.SemaphoreType.DMA((nsem,)), pltpu.SemaphoreType.DMA((nsem,))],
    )(*ops)


def swap_halves(grads, name):
    nt = len(grads)

    def body(*refs):
        ins, outs = refs[:nt], refs[nt:2 * nt]
        ssem, rsem = refs[2 * nt:]
        x, y, c, _ = _place()
        sib = (x, y, 1 - c)
        cps = [_rcopy(ins[t].at[:, :, 1 - c], outs[t], ssem, rsem, t, sib) for t in range(nt)]
        for cp in cps:
            cp.start()
        for cp in cps:
            cp.wait()

    out_shape = [jax.ShapeDtypeStruct(g.shape[:2] + g.shape[3:], g.dtype) for g in grads]
    return _pcall(
        body, name=name, out_shape=out_shape,
        in_specs=[pl.BlockSpec(memory_space=pl.ANY)] * nt, out_specs=[pl.BlockSpec(memory_space=pl.ANY)] * nt,
        scratch_shapes=[pltpu.SemaphoreType.DMA((nt,)), pltpu.SemaphoreType.DMA((nt,))],
    )(*grads)


def _exchange_copies(ins, outs, ssem, rsem):
    x, y, c, chips = _place()
    sends, lands = [], []
    for t in range(len(ins)):
        for j, (px, py) in enumerate(chips):
            k = 3 * t + j
            sends.append(_rcopy(ins[t].at[:, 2 * px + py], outs[t].at[:, j], ssem, rsem, k, (px, py, c)))
            land = outs[t].at[:, j]
            lands.append(_rcopy(land, land, ssem, rsem, k, (x, y, c)))
    return sends, lands


def _gather_direct_copies(bufs, ssem, rsem, rows=None):
    x, y, c, chips = _place()
    me = 2 * x + y
    sends, lands = [], []
    for t, buf in enumerate(bufs):
        r0, rn = rows[t] if rows else (0, buf.shape[2])
        mine = buf.at[:, me, pl.ds(r0, rn)]
        for j, (px, py) in enumerate(chips):
            k = 3 * t + j
            sends.append(_rcopy(mine, mine, ssem, rsem, k, (px, py, c)))
            land = buf.at[:, 2 * px + py, pl.ds(r0, rn)]
            lands.append(_rcopy(land, land, ssem, rsem, k, (x, y, c)))
    return sends, lands


def exchange_partials(parts, small):
    nt = len(parts)

    def body(*refs):
        ins = refs[:nt]
        outs, small_out = refs[nt + 1:2 * nt + 1], refs[2 * nt + 1]
        ssem, rsem = refs[2 * nt + 2:]
        x, y, c, chips = _place()
        dev = 4 * x + 2 * y + c
        sends, lands = _exchange_copies(ins, outs, ssem, rsem)
        peers = [(px, py, pc) for pc in (c, 1 - c) for (px, py) in [(x, y)] + chips][1:]
        for j, (px, py, pc) in enumerate(peers):
            k = 3 * nt + j
            sends.append(_rcopy(small_out.at[dev], small_out.at[dev], ssem, rsem, k, (px, py, pc)))
            land = small_out.at[4 * px + 2 * py + pc]
            lands.append(_rcopy(land, land, ssem, rsem, k, (x, y, c)))
        for cp in sends:
            cp.start()
        for cp in lands:
            cp.wait_recv()
        for cp in sends:
            cp.wait_send()

    out_shape = [jax.ShapeDtypeStruct((p.shape[0], N_CHIPS - 1) + p.shape[2:], p.dtype) for p in parts]
    out_shape.append(jax.ShapeDtypeStruct(small.shape, small.dtype))
    nsem = 3 * nt + 7
    return _pcall(
        body, name="exchange_partials", out_shape=out_shape,
        in_specs=[pl.BlockSpec(memory_space=pl.ANY)] * (nt + 1),
        out_specs=[pl.BlockSpec(memory_space=pl.ANY)] * (nt + 1),
        input_output_aliases={nt: nt},
        scratch_shapes=[pltpu.SemaphoreType.DMA((nsem,)), pltpu.SemaphoreType.DMA((nsem,))],
    )(*parts, small)


def join_halves(halves):
    nt = len(halves)

    def body(*refs):
        outs = refs[nt:2 * nt]
        ssem, rsem = refs[2 * nt:]
        x, y, c, _ = _place()
        sib = (x, y, 1 - c)
        sends = [_rcopy(outs[t].at[:, c], outs[t].at[:, c], ssem, rsem, t, sib) for t in range(nt)]
        for cp in sends:
            cp.start()
        for t in range(nt):
            land = outs[t].at[:, 1 - c]
            _rcopy(land, land, ssem, rsem, t, (x, y, c)).wait_recv()
        for cp in sends:
            cp.wait_send()

    return _pcall(
        body, name="join_halves", out_shape=[jax.ShapeDtypeStruct(h.shape, h.dtype) for h in halves],
        in_specs=[pl.BlockSpec(memory_space=pl.ANY)] * nt, out_specs=[pl.BlockSpec(memory_space=pl.ANY)] * nt,
        input_output_aliases={n: n for n in range(nt)},
        scratch_shapes=[pltpu.SemaphoreType.DMA((nt,)), pltpu.SemaphoreType.DMA((nt,))],
    )(*halves)


def _pad_cols(a, n):
    return jnp.pad(a, [(0, 0)] * (a.ndim - 1) + [(0, n - a.shape[-1])])


def kernel(x, norm_g, fox_w_in, fox_b_f, fox_w_out, conv_w_in, conv_b_in, conv_dw, conv_dw_b, conv_ln_g, conv_ln_b, conv_w_out, final_norm_g, loss_target, m_norm_g, m_fox_w_in, m_fox_b_f, m_fox_w_out, m_conv_w_in, m_conv_b_in, m_conv_dw, m_conv_dw_b, m_conv_ln_g, m_conv_ln_b, m_conv_w_out, m_final_norm_g, v_norm_g, v_fox_w_in, v_fox_b_f, v_fox_w_out, v_conv_w_in, v_conv_b_in, v_conv_dw, v_conv_dw_b, v_conv_ln_g, v_conv_ln_b, v_conv_w_out, v_final_norm_g):
    S, D = x.shape[1], x.shape[2]
    H = fox_b_f.shape[1]
    assert D == H * HEAD_DIM, "one head must be one lane tile"
    W = C = D
    NL = fox_w_in.shape[0]
    Dq = D // N_CHIPS
    NA = fox_w_in.shape[2]
    scale = HEAD_DIM ** -0.5
    chip = 2 * lax.axis_index("x") + lax.axis_index("y")
    core = lax.axis_index("c")
    cidx = core.astype(jnp.int32).reshape((1,))
    chip1 = chip.astype(jnp.int32).reshape((1,))
    sel = jnp.stack([chip, core]).astype(jnp.int32)

    small_pack = jnp.concatenate([
        conv_b_in.reshape((NL * 3, Dq)), conv_dw.reshape((NL * CONV_K, Dq)), conv_dw_b, conv_ln_g, conv_ln_b,
        jnp.zeros((PACK_ROWS - NL * (3 + CONV_K + 3), Dq), F32)], axis=0)
    small_slots = lax.dynamic_update_slice(jnp.zeros((N_CHIPS, PACK_ROWS, Dq), F32), small_pack[None], (chip, 0, 0))
    ga0, gb0, gsmall = gather_weights(
        [cast_into_slot(fox_w_in, chip1, "cast_fox_w_in_0", 0, 1),
         cast_into_slot(fox_w_out, chip1, "cast_fox_w_out_0", 0, 1)], small_slots)
    fox_raw = {l: (cast_into_slot(fox_w_in, chip1, f"cast_fox_w_in_{l}", l, 1),
                   cast_into_slot(fox_w_out, chip1, f"cast_fox_w_out_{l}", l, 1)) for l in range(1, NL)}
    conv_raw = {l: (cast_into_slot(conv_w_in, chip1, f"cast_conv_w_in_{l}", l, 1),
                    cast_into_slot(conv_w_out, chip1, f"cast_conv_w_out_{l}", l, 1)) for l in range(NL)}

    def fox_weights(ga, gb):
        n = ga.shape[0]
        wfull = jnp.transpose(ga, (0, 2, 1, 3)).reshape((n, D, N_CHIPS * NA))
        wq, wk, wv, wg, wf = (wfull[:, :, 0:W], wfull[:, :, W:2 * W], wfull[:, :, 2 * W:3 * W],
                              wfull[:, :, 3 * W:4 * W], wfull[:, :, 4 * W:])
        w1 = jnp.concatenate([wq, wg], axis=-1).reshape((n, 1, D, 2 * W))
        w2 = jnp.stack([wk.reshape((n, D, H, HEAD_DIM)), wv.reshape((n, D, H, HEAD_DIM))], axis=3)
        return w1, w2.reshape((n, 1, D, 2 * W)), _pad_cols(wf, 128).reshape((n, 1, D, 128)), gb.reshape((n, 1, W, D))

    fox_w = {0: fox_weights(ga0, gb0)}
    conv_w = {}
    b_in = gsmall[:, 0:3 * NL, :].reshape((N_CHIPS, NL, 3 * Dq)).transpose((1, 0, 2)).reshape((NL, 3 * C))
    dwt = gsmall[:, 3 * NL:3 * NL + CONV_K * NL, :].reshape((N_CHIPS, NL, CONV_K, Dq))
    dwt = dwt.transpose((1, 2, 0, 3)).reshape((NL, CONV_K, C))
    r0 = (3 + CONV_K) * NL
    vecs = gsmall[:, r0:r0 + 3 * NL, :].reshape((N_CHIPS, 3, NL, Dq)).transpose((1, 2, 0, 3)).reshape((3, NL, C))
    dwb, lng, lnb = vecs[0], vecs[1], vecs[2]
    bfp = _pad_cols(fox_b_f, 128)

    h = x.reshape((S, D))
    tgt = loss_target.reshape((S, D))
    saved = []
    n_layers = norm_g.shape[0]
    for i in range(n_layers):
        j = i // 2
        g_i = norm_g[i:i + 1]
        hn = rms_fwd(h, g_i, f"rms_fwd_{i}")
        if i % 2 == 0:
            w1, w2, wfp, wo_fox = fox_w[j]
            gc, gd = conv_raw[j]
            p1, (gc,) = mm_nn(hn, w1, 0, 1, out_dtype=BF16, name=f"fox_proj_qg_{i}", gather=[(gc, 0, D // 2)])
            p2, (gc,) = mm_nn(hn, w2, 0, 1, out_dtype=BF16, name=f"fox_proj_kv_{i}", gather=[(gc, D // 2, D // 2)])
            pf = mm_nn(hn, wfp, 0, 1, out_dtype=F32, name=f"fox_proj_f_{i}")
            qa, ka, c0, c1 = fox_decay(pf, bfp[j:j + 1], H, scale, f"fox_decay_{i}")
            qn, kn = attn_norms(p1, p2, H, f"attn_norms_{i}")
            first_key, query_end = prune_tables(qn, kn, c0, c1, H, scale)
            o, yv, qb, _ = attn_fwd(p1, p2, qa, ka, H, scale, f"attn_fwd_{i}", first_key=first_key)
            h_new, (gd,) = mm_nt_res(yv, wo_fox, 0, h, f"fox_out_{i}", gather=[(gd, 0, Dq)])
            conv_w[j] = (gc, gd.reshape((1, 1, C, D)))
            saved.append((h, hn, p1, p2, pf, ka, o, yv, qb, query_end))
        else:
            wc, wo_conv = conv_w[j]
            nxt = fox_raw.get(j + 1)
            proj = mm_nn(hn, wc, 0, N_CHIPS, out_dtype=BF16, name=f"conv_proj_{i}", bias=b_in[j:j + 1],
                         gather=[(nxt[0], 0, D // 2)] if nxt else ())
            if nxt:
                proj, (ga,) = proj
            u2, yv, filled = conv_fwd(proj, dwt[j], dwb[j:j + 1], lng[j:j + 1], lnb[j:j + 1], f"conv_fwd_{i}",
                                      gather=[(ga, D // 2, D // 2)] if nxt else ())
            h_new = mm_nt_res(yv, wo_conv, 0, h, f"conv_out_{i}", gather=[(nxt[1], 0, Dq)] if nxt else ())
            if nxt:
                h_new, (gb,) = h_new
                fox_w[j + 1] = fox_weights(filled[0], gb)
            saved.append((h, hn, proj, u2, yv))
        h = h_new

    dh, d_gf, loss_part = loss_head(h, final_norm_g.reshape((1, D)), tgt, "loss_head")

    d_norm = [None] * n_layers
    d_fox_b = [None] * NL
    d_conv_small = [None] * NL
    wgrad = {}

    def pair_sums(keys, tag):
        gs = [wgrad[k].reshape(wgrad[k].shape[:2] + (2, wgrad[k].shape[2] // 2, wgrad[k].shape[3])) for k in keys]
        rcv = swap_halves(gs, f"swap_halves_{tag}")
        return {k: pair_sum(g, r, cidx, f"pair_sum_{k[0]}{k[1]}") for k, g, r in zip(keys, gs, rcv)}

    parts, got = {}, {}

    def pending(tag):
        keys = [k for k in wgrad if k not in parts]
        parts.update(pair_sums(keys, tag))
        return keys

    for i in reversed(range(n_layers)):
        j = i // 2
        g_i = norm_g[i:i + 1]
        if i % 2 == 0:
            h_in, hn, p1, p2, pf, ka, o, yv, qb, query_end = saved[i]
            (w1, w2, wfp, wo_fox), wl = fox_w[j], 0
            wgrad[("b", j)] = mm_tn(yv, dh, 1, name=f"fox_out_dw_{i}").reshape((1, N_CHIPS, Dq, D))
            dy = mm_nn_t(dh, wo_fox, wl, f"fox_out_dx_{i}")
            do, dp1, da = attn_bwd_prep(dy, o, p1, H, f"attn_bwd_prep_{i}")
            dp1, dp2, dc, _ = attn_bwd(p1, qb, do, da, p2, ka, dp1, H, scale, f"attn_bwd_{i}", query_end=query_end)
            dz, dbf = fox_decay_bwd(dc, pf, bfp[j:j + 1], H, f"fox_decay_bwd_{i}")
            d_fox_b[j] = dbf
            keys = pending(f"l{i}")
            dw1, arrived = mm_tn(hn, dp1, 1, name=f"fox_dw_qg_{i}", exchange=[parts[k] for k in keys])
            got.update(dict(zip(keys, arrived)))
            dw1 = dw1[0, 0]
            dw2 = mm_tn(hn, dp2, 1, name=f"fox_dw_kv_{i}")[0, 0]
            dwf = mm_tn(hn, dz, 1, name=f"fox_dw_f_{i}")[0, 0]
            dw2 = dw2.reshape((D, H, 2, HEAD_DIM))
            d_in = jnp.concatenate([dw1[:, :W], dw2[:, :, 0].reshape((D, W)), dw2[:, :, 1].reshape((D, W)),
                                    dw1[:, W:], dwf[:, :H]], axis=-1)
            wgrad[("a", j)] = d_in.reshape((D, N_CHIPS, NA)).transpose((1, 0, 2))[None]
            dhn = mm_nt(dp1, w1, wl, 1, name=f"fox_dx_qg_{i}")
            dhn = mm_nt(dp2, w2, wl, 1, name=f"fox_dx_kv_{i}", res=dhn)
            dhn = mm_nt(dz, wfp, wl, 1, name=f"fox_dx_f_{i}", res=dhn)
        else:
            h_in, hn, proj, u2, yv = saved[i]
            wc, wo_conv = conv_w[j]
            wgrad[("d", j)] = mm_tn(yv, dh, 1, name=f"conv_out_dw_{i}").reshape((1, N_CHIPS, Dq, D))
            dy = mm_nn_t(dh, wo_conv, 0, f"conv_out_dx_{i}")
            du2, dproj, sm1 = conv_bwd_norm(dy, proj, u2, lng[j:j + 1], lnb[j:j + 1], f"conv_bwd_norm_{i}")
            keys = pending(f"l{i}")
            dproj, sm2, arrived = conv_bwd_taps(du2, proj, dwt[j], dproj, f"conv_bwd_taps_{i}",
                                                exchange=[parts[k] for k in keys])
            got.update(dict(zip(keys, arrived)))
            d_conv_small[j] = (sm1, sm2)
            wgrad[("c", j)] = mm_tn(hn, dproj, N_CHIPS, name=f"conv_dw_in_{i}")
            dhn = mm_nt(dproj, wc, 0, N_CHIPS, name=f"conv_dx_{i}")
        dh, d_norm[i] = rms_bwd(dhn, h_in, g_i, dh, f"rms_bwd_{i}")

    late = [k for k in wgrad if k not in parts]
    parts.update(pair_sums(late, "late"))

    zrow = jnp.zeros((1, D), F32)
    rows = list(d_norm) + [d_gf]
    rows += [_pad_cols(d_fox_b[l][:, :H], D) for l in range(NL)]
    rows += [_pad_cols(loss_part[:, :1], D)]
    for l in range(NL):
        sm1, sm2 = d_conv_small[l]
        rows += [sm2[32:33], sm2[33:34], sm1[3:4]]
    for l in range(NL):
        rows += [d_conv_small[l][1][0:CONV_K]]
    rows += [d_conv_small[l][0][2:3] for l in range(NL)]
    rows += [d_conv_small[l][0][0:1] for l in range(NL)]
    rows += [d_conv_small[l][0][1:2] for l in range(NL)]
    n_rows = sum(r.shape[0] for r in rows)
    rows += [zrow] * (SMALL_ROWS - n_rows)
    small = jnp.concatenate(rows, axis=0)

    dev = 4 * lax.axis_index("x") + 2 * lax.axis_index("y") + core
    small_slots = lax.dynamic_update_slice(jnp.zeros((N_DEV, SMALL_ROWS, D), F32), small[None], (dev, 0, 0))
    arrived = exchange_partials([parts[k] for k in late], small_slots)
    got.update(dict(zip(late, arrived[:-1])))
    tot = dev_sum(arrived[-1], "dev_sum")
    halves = {}
    for kind in "abcd":
        for l in range(NL):
            halves[kind] = chip_sum(parts[(kind, l)], got[(kind, l)], sel, f"chip_sum_{kind}{l}", NL, l,
                                    prev=halves.get(kind))
    full = join_halves([halves[kind] for kind in "abcd"])
    grad_fox_w_in = full[0].reshape(fox_w_in.shape)
    grad_fox_w_out = full[1].reshape(fox_w_out.shape)
    grad_conv_w_in = full[2].reshape(conv_w_in.shape)
    grad_conv_w_out = full[3].reshape(conv_w_out.shape)

    def mine(v):
        return lax.dynamic_slice_in_dim(v, chip * Dq, Dq, axis=v.ndim - 1)

    r = n_layers
    grad_norm_g = tot[0:r]
    grad_final = tot[r]
    grad_fox_b_f = tot[r + 1:r + 1 + NL, :H]
    loss = tot[r + 1 + NL, 0]
    r = r + 2 + NL
    gb_full = tot[r:r + 3 * NL].reshape((NL, 3 * C))
    grad_conv_b_in = lax.dynamic_slice_in_dim(gb_full, chip * 3 * Dq, 3 * Dq, axis=1)
    r += 3 * NL
    grad_conv_dw = mine(tot[r:r + CONV_K * NL].reshape((NL, CONV_K, C)))
    r += CONV_K * NL
    grad_conv_dw_b = mine(tot[r:r + NL])
    grad_conv_ln_g = mine(tot[r + NL:r + 2 * NL])
    grad_conv_ln_b = mine(tot[r + 2 * NL:r + 3 * NL])

    grads = [grad_norm_g, grad_fox_w_in, grad_fox_b_f, grad_fox_w_out, grad_conv_w_in, grad_conv_b_in,
             grad_conv_dw, grad_conv_dw_b, grad_conv_ln_g, grad_conv_ln_b, grad_conv_w_out, grad_final]
    ws = [norm_g, fox_w_in, fox_b_f, fox_w_out, conv_w_in, conv_b_in, conv_dw, conv_dw_b, conv_ln_g, conv_ln_b,
          conv_w_out, final_norm_g]
    ms = [m_norm_g, m_fox_w_in, m_fox_b_f, m_fox_w_out, m_conv_w_in, m_conv_b_in, m_conv_dw, m_conv_dw_b,
          m_conv_ln_g, m_conv_ln_b, m_conv_w_out, m_final_norm_g]
    vs = [v_norm_g, v_fox_w_in, v_fox_b_f, v_fox_w_out, v_conv_w_in, v_conv_b_in, v_conv_dw, v_conv_dw_b,
          v_conv_ln_g, v_conv_ln_b, v_conv_w_out, v_final_norm_g]
    deltas, new_ms, new_vs = [], [], []
    for n, (w_, g_, m_, v_) in enumerate(zip(ws, grads, ms, vs)):
        d_, nm_, nv_ = adamw(w_, g_, m_, v_, f"adamw_{n}")
        deltas.append(d_)
        new_ms.append(nm_)
        new_vs.append(nv_)
    grad_x = dh.reshape(x.shape)
    return (loss, grad_x, *grads, *deltas, *new_ms, *new_vs)


def mm_nt_res(y, wo, lidx, h, name, gather=()):
    M, K = y.shape
    N = wo.shape[-1]
    tm = _tile(M, 1024)
    tn = _tile(N, 1024)
    grid = (M // tm, N // tn, 1)
    return _matmul(
        y, wo, contract="nn", grid=grid, name=name,
        a_spec=pl.BlockSpec((tm, K), lambda i, j, k: (i, 0)),
        b_spec=pl.BlockSpec((None, None, K, tn), lambda i, j, k: (lidx, 0, 0, j)),
        o_spec=pl.BlockSpec((tm, tn), lambda i, j, k: (i, j)),
        out_shape=jax.ShapeDtypeStruct((M, N), F32), acc_shape=(tm, tn),
        res=h, res_spec=pl.BlockSpec((tm, tn), lambda i, j, k: (i, j)), gather=gather,
    )


def mm_nn_t(dh, wo, lidx, name):
    M, K = dh.shape
    N = wo.shape[-2]
    tm = _tile(M, 512)
    tn = _tile(N, 1024)
    grid = (M // tm, N // tn, 1)
    return _matmul(
        dh, wo, contract="nt", grid=grid, name=name,
        a_spec=pl.BlockSpec((tm, K), lambda i, j, k: (i, 0)),
        b_spec=pl.BlockSpec((None, None, tn, K), lambda i, j, k: (lidx, 0, j, 0)),
        o_spec=pl.BlockSpec((tm, tn), lambda i, j, k: (i, j)),
        out_shape=jax.ShapeDtypeStruct((M, N), F32), acc_shape=(tm, tn),
    )
```

```python
import jax
import jax.numpy as jnp
from jax import lax
from jax.experimental import pallas as pl
from jax.experimental.pallas import tpu as pltpu

F32 = jnp.float32
BF16 = jnp.bfloat16
MESH = pl.DeviceIdType.MESH

RMS_EPS = 1e-6
LN_EPS = 1e-5
CONV_K = 31
HALO = 32
HEAD_DIM = 128
ADAM_LR = 0.001
ADAM_B1 = 0.9
ADAM_B2 = 0.999
ADAM_EPS = 1e-08
ADAM_WD = 0.01
ADAM_STEP = 10
N_CHIPS = 4
N_DEV = 8
VMEM_LIMIT = 56 * 1024 * 1024
NEG_BIG = -1e30
SMALL_ROWS = 88
PACK_ROWS = 80


def _pcall(body, **kw):
    return pl.pallas_call(body, **kw)


def _cparams(sem=None):
    return pltpu.CompilerParams(dimension_semantics=sem, vmem_limit_bytes=VMEM_LIMIT)


def _tile(n, cap, mult=128):
    if n <= cap:
        return n
    t = (cap // mult) * mult
    while t >= mult:
        if n % t == 0:
            return t
        t -= mult
    raise ValueError(f"no tile for {n} under {cap}")


def _sigmoid(x):
    return 1.0 / (1.0 + jnp.exp(-x))


def _split3(x):
    hi = x.astype(BF16).astype(F32)
    r = x - hi
    mid = r.astype(BF16).astype(F32)
    lo = (r - mid).astype(BF16).astype(F32)
    return hi, mid, lo


_DN = {
    "nn": (((1,), (0,)), ((), ())),
    "nt": (((1,), (1,)), ((), ())),
    "tn": (((0,), (0,)), ((), ())),
}


def _matmul(a, b, *, contract, grid, a_spec, b_spec, o_spec, out_shape, acc_shape, name,
            bias=None, bias_spec=None, res=None, res_spec=None, alias_res=False, exchange=(), gather=()):
    nk = grid[2]
    has_bias = bias is not None
    has_res = res is not None
    assert not (exchange and gather)
    ne = len(exchange) + len(gather)
    rows = [(r0, rn) for _, r0, rn in gather]

    def _copies(parts, got, ssem, rsem):
        if gather:
            return _gather_direct_copies(got, ssem, rsem, rows)
        return _exchange_copies(parts, got, ssem, rsem)

    def body(*refs):
        a_ref, b_ref = refs[0], refs[1]
        pos = 2
        bias_ref = res_ref = None
        if has_bias:
            bias_ref = refs[pos]
            pos += 1
        if has_res:
            res_ref = refs[pos]
            pos += 1
        parts = refs[pos:pos + ne]
        pos += ne
        o_ref = refs[pos]
        got = refs[pos + 1:pos + 1 + ne]
        pos += ne
        acc_ref = refs[pos + 1] if nk > 1 else None
        if ne:
            ssem, rsem = refs[-2:]
            ids = [pl.program_id(d) for d in range(3)]

            @pl.when((ids[0] == 0) & (ids[1] == 0) & (ids[2] == 0))
            def _():
                for cp in _copies(parts, got, ssem, rsem)[0]:
                    cp.start()
        p = lax.dot_general(a_ref[...].astype(BF16), b_ref[...].astype(BF16), _DN[contract],
                            preferred_element_type=F32)

        def finish(v):
            if has_bias:
                v = v + bias_ref[...]
            if has_res:
                v = res_ref[...] + v
            o_ref[...] = v.astype(o_ref.dtype)

        if nk == 1:
            finish(p)
        else:
            k = pl.program_id(2)

            @pl.when(k == 0)
            def _():
                acc_ref[...] = p

            @pl.when(k > 0)
            def _():
                acc_ref[...] += p

            @pl.when(k == nk - 1)
            def _():
                finish(acc_ref[...])

        if ne:
            @pl.when((ids[0] == grid[0] - 1) & (ids[1] == grid[1] - 1) & (ids[2] == grid[2] - 1))
            def _():
                sends, lands = _copies(parts, got, ssem, rsem)
                for cp in lands:
                    cp.wait_recv()
                for cp in sends:
                    cp.wait_send()

    ins = [a, b]
    specs = [a_spec, b_spec]
    if has_bias:
        ins.append(bias)
        specs.append(bias_spec)
    if has_res:
        ins.append(res)
        specs.append(res_spec)
    aliases = {len(ins) - 1: 0} if (has_res and alias_res) else {}
    hbm = pl.BlockSpec(memory_space=pl.ANY)
    scratch = [pltpu.VMEM(acc_shape, F32)] if nk > 1 else []
    if not ne:
        return _pcall(
            body, name=name, grid=grid, in_specs=specs, out_specs=o_spec, out_shape=out_shape,
            scratch_shapes=scratch, input_output_aliases=aliases,
            compiler_params=_cparams(("parallel", "parallel", "arbitrary")),
        )(*ins)
    if gather:
        extra = [g for g, _, _ in gather]
        extra_out = [jax.ShapeDtypeStruct(g.shape, g.dtype) for g in extra]
        aliases = {**aliases, **{len(ins) + n: 1 + n for n in range(ne)}}
    else:
        extra = list(exchange)
        extra_out = [jax.ShapeDtypeStruct((p.shape[0], N_CHIPS - 1) + p.shape[2:], p.dtype) for p in extra]
    outs = _pcall(
        body, name=name, grid=grid, in_specs=specs + [hbm] * ne, out_specs=[o_spec] + [hbm] * ne,
        out_shape=[out_shape] + extra_out,
        scratch_shapes=scratch + [pltpu.SemaphoreType.DMA((3 * ne,)), pltpu.SemaphoreType.DMA((3 * ne,))],
        input_output_aliases=aliases, compiler_params=_cparams(("arbitrary", "arbitrary", "arbitrary")),
    )(*ins, *extra)
    return outs[0], list(outs[1:])


def mm_nn(a, w, lidx, n_slots, *, out_dtype, name, bias=None, gather=()):
    M, K = a.shape
    Ns = w.shape[-1]
    tm = _tile(M, 1024)
    tn = _tile(Ns, 1024)
    per = Ns // tn
    grid = (M // tm, n_slots * per, 1)
    return _matmul(
        a, w, contract="nn", grid=grid, name=name,
        a_spec=pl.BlockSpec((tm, K), lambda i, j, k: (i, 0)),
        b_spec=pl.BlockSpec((None, None, K, tn), lambda i, j, k: (lidx, j // per, 0, j % per)),
        o_spec=pl.BlockSpec((tm, tn), lambda i, j, k: (i, j)),
        out_shape=jax.ShapeDtypeStruct((M, n_slots * Ns), out_dtype), acc_shape=(tm, tn),
        bias=bias, bias_spec=None if bias is None else pl.BlockSpec((1, tn), lambda i, j, k: (0, j)),
        gather=gather,
    )


def mm_nt(a, w, lidx, n_slots, *, name, res=None):
    M = a.shape[0]
    N, Ns = w.shape[-2], w.shape[-1]
    tm = _tile(M, 1024)
    tn = _tile(N, 1024)
    tk = _tile(Ns, 2048)
    per = Ns // tk
    grid = (M // tm, N // tn, n_slots * per)
    return _matmul(
        a, w, contract="nt", grid=grid, name=name,
        a_spec=pl.BlockSpec((tm, tk), lambda i, j, k: (i, k)),
        b_spec=pl.BlockSpec((None, None, tn, tk), lambda i, j, k: (lidx, k // per, j, k % per)),
        o_spec=pl.BlockSpec((tm, tn), lambda i, j, k: (i, j)),
        out_shape=jax.ShapeDtypeStruct((M, N), F32), acc_shape=(tm, tn),
        res=res, res_spec=None if res is None else pl.BlockSpec((tm, tn), lambda i, j, k: (i, j)),
        alias_res=res is not None,
    )


def mm_tn(a, b, n_slots, *, name, exchange=()):
    S, M = a.shape
    Ns = b.shape[1] // n_slots
    tm = _tile(M, 1024)
    tn = _tile(Ns, 1024)
    tk = _tile(S, 2048)
    per = Ns // tn
    grid = (M // tm, n_slots * per, S // tk)
    return _matmul(
        a, b, contract="tn", grid=grid, name=name,
        a_spec=pl.BlockSpec((tk, tm), lambda i, j, k: (k, i)),
        b_spec=pl.BlockSpec((tk, tn), lambda i, j, k: (k, j)),
        o_spec=pl.BlockSpec((None, None, tm, tn), lambda i, j, k: (0, j // per, i, j % per)),
        out_shape=jax.ShapeDtypeStruct((1, n_slots, M, Ns), F32), acc_shape=(tm, tn), exchange=exchange,
    )


def rms_fwd(h, g, name):
    S, D = h.shape
    tm = _tile(S, 256, 8)

    def body(h_ref, g_ref, o_ref):
        x = h_ref[...]
        r = lax.rsqrt(jnp.mean(x * x, axis=-1, keepdims=True) + RMS_EPS)
        o_ref[...] = (x * r * g_ref[...]).astype(BF16)

    return _pcall(
        body, name=name, grid=(S // tm,),
        in_specs=[pl.BlockSpec((tm, D), lambda i: (i, 0)), pl.BlockSpec((1, D), lambda i: (0, 0))],
        out_specs=pl.BlockSpec((tm, D), lambda i: (i, 0)),
        out_shape=jax.ShapeDtypeStruct((S, D), BF16),
        compiler_params=_cparams(("parallel",)),
    )(h, g)


def _rms_bwd_rows(x, g, dy):
    d = x.shape[-1]
    r = lax.rsqrt(jnp.mean(x * x, axis=-1, keepdims=True) + RMS_EPS)
    gd = dy * g
    dx = r * gd - x * ((r * r * r) * (jnp.sum(x * gd, axis=-1, keepdims=True) / d))
    return dx, dy * x * r


def rms_bwd(dhn, h, g, dres, name):
    S, D = h.shape
    tm = _tile(S, 256, 8)

    def body(dhn_ref, h_ref, g_ref, dres_ref, dh_ref, dg_ref):
        dx, dgr = _rms_bwd_rows(h_ref[...], g_ref[...], dhn_ref[...])
        dh_ref[...] = dres_ref[...] + dx

        @pl.when(pl.program_id(0) == 0)
        def _():
            dg_ref[...] = jnp.zeros_like(dg_ref)

        dg_ref[...] += jnp.sum(dgr, axis=0, keepdims=True)

    row = pl.BlockSpec((tm, D), lambda i: (i, 0))
    vec = pl.BlockSpec((1, D), lambda i: (0, 0))
    return _pcall(
        body, name=name, grid=(S // tm,),
        in_specs=[row, row, vec, row], out_specs=[row, vec],
        out_shape=[jax.ShapeDtypeStruct((S, D), F32), jax.ShapeDtypeStruct((1, D), F32)],
        input_output_aliases={3: 0},
        compiler_params=_cparams(("arbitrary",)),
    )(dhn, h, g, dres)


def loss_head(h, g, target, name):
    S, D = h.shape
    tm = _tile(S, 256, 8)

    def body(h_ref, g_ref, t_ref, dh_ref, dg_ref, loss_ref):
        x = h_ref[...]
        gg = g_ref[...]
        r = lax.rsqrt(jnp.mean(x * x, axis=-1, keepdims=True) + RMS_EPS)
        y = x * r * gg
        e = y - t_ref[...]
        part = 0.5 * jnp.sum(jnp.mean(e * e, axis=-1, keepdims=True), axis=0, keepdims=True)
        dy = e * (1.0 / D)
        dx, dgr = _rms_bwd_rows(x, gg, dy)
        dh_ref[...] = dx

        @pl.when(pl.program_id(0) == 0)
        def _():
            dg_ref[...] = jnp.zeros_like(dg_ref)
            loss_ref[...] = jnp.zeros_like(loss_ref)

        dg_ref[...] += jnp.sum(dgr, axis=0, keepdims=True)
        loss_ref[...] += jnp.broadcast_to(part, loss_ref.shape)

    row = pl.BlockSpec((tm, D), lambda i: (i, 0))
    vec = pl.BlockSpec((1, D), lambda i: (0, 0))
    return _pcall(
        body, name=name, grid=(S // tm,),
        in_specs=[row, vec, row],
        out_specs=[row, vec, pl.BlockSpec((1, 128), lambda i: (0, 0))],
        out_shape=[jax.ShapeDtypeStruct((S, D), F32), jax.ShapeDtypeStruct((1, D), F32),
                   jax.ShapeDtypeStruct((1, 128), F32)],
        compiler_params=_cparams(("arbitrary",)),
    )(h, g, target)


def fox_decay(pf, bf, n_heads, scale, name):
    S = pf.shape[0]
    tm = min(_tile(S, 256, 8), _attn_tile(S))
    inv_scale = 1.0 / scale

    def body(pf_ref, bf_ref, qa_ref, ka_ref, c0_ref, c1_ref, carry_ref):
        @pl.when(pl.program_id(0) == 0)
        def _():
            carry_ref[...] = jnp.zeros_like(carry_ref)

        z = pf_ref[...] + bf_ref[...]
        logf = jnp.minimum(z, 0.0) - jnp.log(1.0 + jnp.exp(-jnp.abs(z)))
        row = lax.broadcasted_iota(jnp.int32, (tm, tm), 0)
        col = lax.broadcasted_iota(jnp.int32, (tm, tm), 1)
        tri = (row >= col).astype(F32)
        c = jnp.dot(tri, logf, precision=lax.Precision.HIGHEST, preferred_element_type=F32) + carry_ref[...]
        carry_ref[...] = c[tm - 1:tm, :]
        c0_ref[...] = c[0:1, :]
        c1_ref[...] = c[tm - 1:tm, :]
        lane = lax.broadcasted_iota(jnp.int32, (tm, HEAD_DIM), 1)
        for hh in range(n_heads):
            hi, mid, lo = _split3(c[:, hh:hh + 1] * inv_scale)
            qa = jnp.where(lane == 0, hi, jnp.where(lane == 1, mid, jnp.where(lane == 2, lo,
                 jnp.where(lane < 6, 1.0, 0.0))))
            ka = jnp.where(lane < 3, 1.0, jnp.where(lane == 3, -hi, jnp.where(lane == 4, -mid,
                 jnp.where(lane == 5, -lo, jnp.where(lane < 9, 1.0, 0.0)))))
            qa_ref[:, hh * HEAD_DIM:(hh + 1) * HEAD_DIM] = qa.astype(BF16)
            ka_ref[:, hh * HEAD_DIM:(hh + 1) * HEAD_DIM] = ka.astype(BF16)

    wide = pl.BlockSpec((tm, n_heads * HEAD_DIM), lambda i: (i, 0))
    edge = pl.BlockSpec((None, 1, 128), lambda i: (i, 0, 0))
    return _pcall(
        body, name=name, grid=(S // tm,),
        in_specs=[pl.BlockSpec((tm, 128), lambda i: (i, 0)), pl.BlockSpec((1, 128), lambda i: (0, 0))],
        out_specs=[wide, wide, edge, edge],
        out_shape=[jax.ShapeDtypeStruct((S, n_heads * HEAD_DIM), BF16)] * 2
                  + [jax.ShapeDtypeStruct((S // tm, 1, 128), F32)] * 2,
        scratch_shapes=[pltpu.VMEM((1, 128), F32)],
        compiler_params=_cparams(("arbitrary",)),
    )(pf, bf)


def attn_norms(p1, p2, n_heads, name):
    S = p1.shape[0]
    W = n_heads * HEAD_DIM
    t = _attn_tile(S)

    def body(q_ref, kv_ref, qn_ref, kn_ref):
        lane = lax.broadcasted_iota(jnp.int32, (1, 128), 1)
        qn = jnp.zeros((1, 128), F32)
        kn = jnp.zeros((1, 128), F32)
        for hh in range(n_heads):
            q = q_ref[:, hh * HEAD_DIM:(hh + 1) * HEAD_DIM].astype(F32)
            k = kv_ref[:, 2 * hh * HEAD_DIM:(2 * hh + 1) * HEAD_DIM].astype(F32)
            q2 = jnp.max(jnp.sum(q * q, axis=1, keepdims=True), axis=0, keepdims=True)
            k2 = jnp.max(jnp.sum(k * k, axis=1, keepdims=True), axis=0, keepdims=True)
            qn = jnp.where(lane == hh, q2, qn)
            kn = jnp.where(lane == hh, k2, kn)
        qn_ref[...] = qn
        kn_ref[...] = kn

    edge = pl.BlockSpec((None, 1, 128), lambda i: (i, 0, 0))
    return _pcall(
        body, name=name, grid=(S // t,),
        in_specs=[pl.BlockSpec((t, W), lambda i: (i, 0)), pl.BlockSpec((t, 2 * W), lambda i: (i, 0))],
        out_specs=[edge, edge], out_shape=[jax.ShapeDtypeStruct((S // t, 1, 128), F32)] * 2,
        compiler_params=_cparams(("parallel",)),
    )(p1, p2)


PRUNE_BELOW = -110.0


def prune_tables(qn, kn, c0, c1, n_heads, scale):
    nb = qn.shape[0]
    r = c0.shape[0] // nb
    qmax = jnp.sqrt(qn[:, 0, :n_heads])
    kmax = jnp.sqrt(kn[:, 0, :n_heads])
    cfirst = c0[::r, 0, :n_heads]
    clast = c1[r - 1::r, 0, :n_heads]
    bound = (scale * qmax[:, None, :] * (kmax[None, :, :] + kmax[:, None, :])
             + cfirst[:, None, :] - clast[None, :, :])
    ii = lax.broadcasted_iota(jnp.int32, (nb, nb, 1), 0)
    jj = lax.broadcasted_iota(jnp.int32, (nb, nb, 1), 1)
    skip = ((bound < PRUNE_BELOW) & (jj < ii)).astype(jnp.int32)
    first_key = jnp.sum(jnp.cumprod(skip, axis=1), axis=1)
    tail = jnp.sum(jnp.cumprod(skip[::-1], axis=0), axis=0)
    return first_key.T.reshape((-1,)), (nb - tail).T.reshape((-1,))


def _attn_tile(S):
    return 512 if S % 512 == 0 and S >= 2048 else 128


def attn_fwd(p1, p2, qa, ka, n_heads, scale, name, gather=(), first_key=None):
    S = p1.shape[0]
    W = n_heads * HEAD_DIM
    t = _attn_tile(S)
    nq = S // t
    ng = len(gather)
    assert not (ng and first_key is not None)

    def body(*refs):
        if first_key is None:
            core(0, *refs)
        else:
            fk = refs[0][pl.program_id(0) * nq + pl.program_id(1)]
            core(jnp.minimum(fk, pl.program_id(1)), *refs[1:])

    def core(js, q_ref, g_ref, qa_ref, k_ref, v_ref, ka_ref, *rest):
        o_ref, y_ref, qb_ref = rest[ng:ng + 3]
        bufs = rest[ng + 3:2 * ng + 3]
        mp_ref, qq_ref, acc_ref = rest[2 * ng + 3:2 * ng + 6]
        i = pl.program_id(1)
        if ng:
            ssem, rsem = rest[2 * ng + 6:]

            @pl.when((pl.program_id(0) == 0) & (i == 0))
            def _():
                for cp in _gather_direct_copies(bufs, ssem, rsem)[0]:
                    cp.start()
        lane = lax.broadcasted_iota(jnp.int32, (t, HEAD_DIM), 1)
        qa = qa_ref[...].astype(F32)

        def tile_with(neg_stat):
            hi, mid, lo = _split3(neg_stat)
            return jnp.where(lane == 6, hi, jnp.where(lane == 7, mid, jnp.where(lane == 8, lo, qa))).astype(BF16)

        def keys(start, width):
            rows = pl.ds(pl.multiple_of(start, t), width)
            return rows, jnp.concatenate([k_ref[rows, :], ka_ref[rows, :]], axis=1)

        def causal():
            r = lax.broadcasted_iota(jnp.int32, (t, t), 0)
            c = lax.broadcasted_iota(jnp.int32, (t, t), 1)
            return r >= c

        def over_keys(block):
            n = i - js

            def wide(jj, carry):
                block((js + 4 * jj) * t, 4 * t, False)
                return carry

            lax.fori_loop(0, n // 4, wide, 0)
            done = js + (n // 4) * 4

            @pl.when((n & 2) != 0)
            def _():
                block(done * t, 2 * t, False)

            @pl.when((n & 1) != 0)
            def _():
                block((done + (n & 2)) * t, t, False)

            block(i * t, t, True)

        qq_ref[:, :HEAD_DIM] = q_ref[...]
        qq_ref[:, HEAD_DIM:] = qa_ref[...]
        mp_ref[...] = jnp.full(mp_ref.shape, NEG_BIG, F32)

        def max_block(start, width, masked):
            _, kk = keys(start, width)
            s = lax.dot_general(qq_ref[...], kk, _DN["nt"], preferred_element_type=F32)
            if masked:
                s = jnp.where(causal(), s, NEG_BIG)
            part = s[:, 0:HEAD_DIM]
            for a in range(1, width // HEAD_DIM):
                part = jnp.maximum(part, s[:, a * HEAD_DIM:(a + 1) * HEAD_DIM])
            mp_ref[...] = jnp.maximum(mp_ref[...], part)

        over_keys(max_block)
        m = jnp.max(mp_ref[...], axis=1, keepdims=True)
        qq_ref[:, HEAD_DIM:] = tile_with(-m)
        acc_ref[...] = jnp.zeros_like(acc_ref)

        def sum_block(start, width, masked):
            rows, kk = keys(start, width)
            a = lax.dot_general(qq_ref[...], kk, _DN["nt"], preferred_element_type=F32)
            p = jnp.exp(scale * a)
            if masked:
                p = jnp.where(causal(), p, 0.0)
            ones0 = jnp.where(lax.broadcasted_iota(jnp.int32, (width, HEAD_DIM), 1) == 0, 1.0, 0.0).astype(BF16)
            vv = jnp.concatenate([v_ref[rows, :], ones0], axis=1)
            acc_ref[...] += jnp.dot(p.astype(BF16), vv, preferred_element_type=F32)

        over_keys(sum_block)

        l = acc_ref[:, HEAD_DIM:HEAD_DIM + 1]
        o = acc_ref[:, :HEAD_DIM] / l
        gate = g_ref[...].astype(F32)
        o_ref[...] = o.astype(BF16)
        y_ref[...] = (o * (gate * _sigmoid(gate))).astype(BF16)
        qb_ref[...] = tile_with(-(m + jnp.log(l) * (1.0 / scale)))

        if ng:
            @pl.when((pl.program_id(0) == n_heads - 1) & (i == nq - 1))
            def _():
                sends, lands = _gather_direct_copies(bufs, ssem, rsem)
                for cp in lands:
                    cp.wait_recv()
                for cp in sends:
                    cp.wait_send()

    H = n_heads
    qtile = lambda off: pl.BlockSpec((t, HEAD_DIM), lambda h, i, *_: (i, off + h))
    full = lambda fn: pl.BlockSpec((S, HEAD_DIM), fn)
    hbm = pl.BlockSpec(memory_space=pl.ANY)
    sems = [pltpu.SemaphoreType.DMA((3 * ng,)), pltpu.SemaphoreType.DMA((3 * ng,))] if ng else []
    in_specs = [qtile(0), qtile(H), qtile(0), full(lambda h, i, *_: (0, 2 * h)), full(lambda h, i, *_: (0, 2 * h + 1)),
                full(lambda h, i, *_: (0, h))] + [hbm] * ng
    out_specs = [qtile(0), qtile(0), qtile(0)] + [hbm] * ng
    scratch = [pltpu.VMEM((t, HEAD_DIM), F32), pltpu.VMEM((t, 2 * HEAD_DIM), BF16),
               pltpu.VMEM((t, 2 * HEAD_DIM), F32)] + sems
    out_shape = [jax.ShapeDtypeStruct((S, W), BF16)] * 3 + [jax.ShapeDtypeStruct(b.shape, b.dtype) for b in gather]
    sem = _cparams(("arbitrary", "arbitrary") if ng else ("parallel", "arbitrary"))
    if first_key is None:
        outs = _pcall(
            body, name=name, grid=(H, nq), in_specs=in_specs, out_specs=out_specs, out_shape=out_shape,
            scratch_shapes=scratch, input_output_aliases={6 + n: 3 + n for n in range(ng)}, compiler_params=sem,
        )(p1, p1, qa, p2, p2, ka, *gather)
    else:
        grid_spec = pltpu.PrefetchScalarGridSpec(num_scalar_prefetch=1, grid=(H, nq), in_specs=in_specs,
                                                 out_specs=out_specs, scratch_shapes=scratch)
        outs = _pcall(body, name=name, grid_spec=grid_spec, out_shape=out_shape, compiler_params=sem,
                      )(first_key, p1, p1, qa, p2, p2, ka)
    return outs[0], outs[1], outs[2], list(outs[3:])


def attn_bwd_prep(dy, o, p1, n_heads, name):
    S, W = dy.shape
    tm = _tile(S, 256, 8)
    H = n_heads

    def body(dy_ref, o_ref, g_ref, do_ref, dg_ref, da_ref):
        lane = lax.broadcasted_iota(jnp.int32, (tm, HEAD_DIM), 1)
        for hh in range(H):
            cs = slice(hh * HEAD_DIM, (hh + 1) * HEAD_DIM)
            g = g_ref[:, cs].astype(F32)
            oo = o_ref[:, cs].astype(F32)
            dyv = dy_ref[:, cs]
            sg = _sigmoid(g)
            do = dyv * (g * sg)
            do_ref[:, cs] = do.astype(BF16)
            dg_ref[:, cs] = (dyv * oo * (sg * (1.0 + g * (1.0 - sg)))).astype(BF16)
            hi, mid, lo = _split3(-jnp.sum(do * oo, axis=1, keepdims=True))
            da = jnp.where(lane == 0, hi, jnp.where(lane == 1, mid, jnp.where(lane == 2, lo, 0.0)))
            da_ref[:, cs] = da.astype(BF16)

    row = lambda blk: pl.BlockSpec((tm, W), lambda i: (i, blk))
    return _pcall(
        body, name=name, grid=(S // tm,),
        in_specs=[row(0), row(0), row(1)],
        out_specs=[row(0), row(1), row(0)],
        out_shape=[jax.ShapeDtypeStruct((S, W), BF16), jax.ShapeDtypeStruct((S, 2 * W), BF16),
                   jax.ShapeDtypeStruct((S, W), BF16)],
        compiler_params=_cparams(("parallel",)),
    )(dy, o, p1)


def attn_bwd(p1, qb, do, da, p2, ka, dp1, n_heads, scale, name, exchange=(), query_end=None):
    S = p1.shape[0]
    W = n_heads * HEAD_DIM
    t = _attn_tile(S)
    nb = S // t
    H = n_heads
    ne = len(exchange)
    assert not (ne and query_end is not None)

    def body(*refs):
        if query_end is None:
            core(nb, *refs)
        else:
            qe = refs[0][pl.program_id(0) * nb + pl.program_id(1)]
            core(jnp.clip(qe, pl.program_id(1) + 1, nb), *refs[1:])

    def core(iend, q_ref, qb_ref, do_ref, da_ref, k_ref, v_ref, ka_ref, dp1_in, *rest):
        del dp1_in
        parts = rest[:ne]
        dq_ref, dkv_ref, dc_ref = rest[ne:ne + 3]
        got = rest[ne + 3:2 * ne + 3]
        dq_acc, dk_acc, dv_acc = rest[2 * ne + 3:2 * ne + 6]
        h = pl.program_id(0)
        j = pl.program_id(1)
        if ne:
            ssem, rsem = rest[2 * ne + 6:]

            @pl.when((h == 0) & (j == 0))
            def _():
                for cp in _exchange_copies(parts, got, ssem, rsem)[0]:
                    cp.start()

        @pl.when(j == 0)
        def _():
            dq_acc[...] = jnp.zeros_like(dq_acc)

        @pl.when((j == 0) & (h == 0))
        def _():
            dc_ref[...] = jnp.zeros_like(dc_ref)

        lane = lax.broadcasted_iota(jnp.int32, (t, HEAD_DIM), 1)
        ones3 = jnp.where(lane < 3, 1.0, 0.0).astype(BF16)
        kk = jnp.concatenate([k_ref[...], ka_ref[...]], axis=1)
        vv = jnp.concatenate([v_ref[...], ones3], axis=1)
        dk_acc[...] = jnp.zeros_like(dk_acc)
        dv_acc[...] = jnp.zeros_like(dv_acc)

        def block(start, width, masked):
            rows = pl.ds(pl.multiple_of(start, t), width)
            qq = jnp.concatenate([q_ref[rows, :], qb_ref[rows, :]], axis=1)
            dd = jnp.concatenate([do_ref[rows, :], da_ref[rows, :]], axis=1)
            a = lax.dot_general(qq, kk, _DN["nt"], preferred_element_type=F32)
            p = jnp.exp(scale * a)
            if masked:
                r = lax.broadcasted_iota(jnp.int32, (t, t), 0)
                c = lax.broadcasted_iota(jnp.int32, (t, t), 1)
                p = jnp.where(r >= c, p, 0.0)
            dpd = lax.dot_general(dd, vv, _DN["nt"], preferred_element_type=F32)
            ds = (p * dpd).astype(BF16)
            pb = p.astype(BF16)
            dv_acc[...] += lax.dot_general(pb, dd, _DN["tn"], preferred_element_type=F32)
            dk_acc[...] += lax.dot_general(ds, qq, _DN["tn"], preferred_element_type=F32)
            dq_acc[rows, :] += jnp.dot(ds, kk, preferred_element_type=F32)

        block(j * t, t, True)
        n_after = iend - 1 - j

        @pl.when((n_after & 1) != 0)
        def _():
            block((j + 1) * t, t, False)

        first = j + 1 + (n_after & 1)

        def loop_body(ii, carry):
            block((first + 2 * ii) * t, 2 * t, False)
            return carry

        lax.fori_loop(0, n_after // 2, loop_body, 0)

        dkv_ref[...] = jnp.concatenate([dk_acc[:, :HEAD_DIM] * scale, dv_acc[:, :HEAD_DIM]], axis=1).astype(BF16)
        colsum = dk_acc[:, HEAD_DIM + 3:HEAD_DIM + 4]
        krows = pl.ds(pl.multiple_of(j * t, t), t)
        dc_ref[krows, :] += jnp.where(lane == h, -colsum, 0.0)

        @pl.when(j == nb - 1)
        def _():
            dq_ref[...] = (dq_acc[:, :HEAD_DIM] * scale).astype(BF16)
            lane_s = lax.broadcasted_iota(jnp.int32, (S, HEAD_DIM), 1)
            dc_ref[...] += jnp.where(lane_s == h, dq_acc[:, HEAD_DIM:HEAD_DIM + 1], 0.0)

        if ne:
            @pl.when((h == H - 1) & (j == nb - 1))
            def _():
                sends, lands = _exchange_copies(parts, got, ssem, rsem)
                for cp in lands:
                    cp.wait_recv()
                for cp in sends:
                    cp.wait_send()

    full = lambda fn: pl.BlockSpec((S, HEAD_DIM), fn)
    ktile = lambda fn: pl.BlockSpec((t, HEAD_DIM), fn)
    hbm = pl.BlockSpec(memory_space=pl.ANY)
    sems = [pltpu.SemaphoreType.DMA((3 * ne,)), pltpu.SemaphoreType.DMA((3 * ne,))] if ne else []
    in_specs = [full(lambda h, j, *_: (0, h)), full(lambda h, j, *_: (0, h)), full(lambda h, j, *_: (0, h)),
                full(lambda h, j, *_: (0, h)),
                ktile(lambda h, j, *_: (j, 2 * h)), ktile(lambda h, j, *_: (j, 2 * h + 1)),
                ktile(lambda h, j, *_: (j, h)), hbm] + [hbm] * ne
    out_specs = [full(lambda h, j, *_: (0, h)),
                 pl.BlockSpec((t, 2 * HEAD_DIM), lambda h, j, *_: (j, h)),
                 pl.BlockSpec((S, 128), lambda h, j, *_: (0, 0))] + [hbm] * ne
    out_shape = ([jax.ShapeDtypeStruct((S, 2 * W), BF16), jax.ShapeDtypeStruct((S, 2 * W), BF16),
                  jax.ShapeDtypeStruct((S, 128), F32)]
                 + [jax.ShapeDtypeStruct((p.shape[0], N_CHIPS - 1) + p.shape[2:], p.dtype) for p in exchange])
    scratch = [pltpu.VMEM((S, 2 * HEAD_DIM), F32), pltpu.VMEM((t, 2 * HEAD_DIM), F32),
               pltpu.VMEM((t, 2 * HEAD_DIM), F32)] + sems
    if query_end is None:
        outs = _pcall(
            body, name=name, grid=(H, nb), in_specs=in_specs, out_specs=out_specs, out_shape=out_shape,
            scratch_shapes=scratch, input_output_aliases={7: 0}, compiler_params=_cparams(("arbitrary", "arbitrary")),
        )(p1, qb, do, da, p2, p2, ka, dp1, *exchange)
    else:
        grid_spec = pltpu.PrefetchScalarGridSpec(num_scalar_prefetch=1, grid=(H, nb), in_specs=in_specs,
                                                 out_specs=out_specs, scratch_shapes=scratch)
        outs = _pcall(body, name=name, grid_spec=grid_spec, out_shape=out_shape, input_output_aliases={8: 0},
                      compiler_params=_cparams(("arbitrary", "arbitrary")),
                      )(query_end, p1, qb, do, da, p2, p2, ka, dp1)
    return outs[0], outs[1], outs[2], list(outs[3:])


def fox_decay_bwd(dc, pf, bf, n_heads, name):
    S = dc.shape[0]
    tm = _tile(S, 256, 8)
    nb = S // tm

    def body(dc_ref, pf_ref, bf_ref, dz_ref, db_ref, carry_ref):
        @pl.when(pl.program_id(0) == 0)
        def _():
            carry_ref[...] = jnp.zeros_like(carry_ref)
            db_ref[...] = jnp.zeros_like(db_ref)

        row = lax.broadcasted_iota(jnp.int32, (tm, tm), 0)
        col = lax.broadcasted_iota(jnp.int32, (tm, tm), 1)
        tri = (row <= col).astype(F32)
        dlogf = jnp.dot(tri, dc_ref[...], precision=lax.Precision.HIGHEST, preferred_element_type=F32) + carry_ref[...]
        carry_ref[...] = dlogf[0:1, :]
        z = pf_ref[...] + bf_ref[...]
        lane = lax.broadcasted_iota(jnp.int32, (tm, 128), 1)
        dz = jnp.where(lane < n_heads, dlogf * _sigmoid(-z), 0.0)
        dz_ref[...] = dz.astype(BF16)
        db_ref[...] += jnp.sum(dz, axis=0, keepdims=True)

    rev = pl.BlockSpec((tm, 128), lambda i: (nb - 1 - i, 0))
    vec = pl.BlockSpec((1, 128), lambda i: (0, 0))
    return _pcall(
        body, name=name, grid=(nb,),
        in_specs=[rev, rev, vec], out_specs=[rev, vec],
        out_shape=[jax.ShapeDtypeStruct((S, 128), BF16), jax.ShapeDtypeStruct((1, 128), F32)],
        scratch_shapes=[pltpu.VMEM((1, 128), F32)],
        compiler_params=_cparams(("arbitrary",)),
    )(dc, pf, bf)


def _conv_tile(S):
    return _tile(S, 256, HALO)


def _fill_glu(ubuf, a_ref, b_ref, ah_ref, bh_ref, first, tm):
    uh = ah_ref[...].astype(F32) * _sigmoid(bh_ref[...].astype(F32))
    ubuf[0:HALO, :] = jnp.where(first, 0.0, uh)
    ubuf[HALO:HALO + tm, :] = a_ref[...].astype(F32) * _sigmoid(b_ref[...].astype(F32))


def conv_fwd(proj, dw, dwb, lng, lnb, name, gather=()):
    S = proj.shape[0]
    C = proj.shape[1] // 3
    tm = _conv_tile(S)
    hb = tm // HALO
    nch = C // 128
    rb = _tile(tm, 128, 8)
    ng = len(gather)
    rows = [(r0, rn) for _, r0, rn in gather]

    def body(a_ref, b_ref, g_ref, ah_ref, bh_ref, dw_ref, dwb_ref, lng_ref, lnb_ref, *rest):
        u2_ref, y_ref = rest[ng:ng + 2]
        bufs = rest[ng + 2:2 * ng + 2]
        ubuf, sh = rest[2 * ng + 2:2 * ng + 4]
        i = pl.program_id(0)
        if ng:
            ssem, rsem = rest[2 * ng + 4:]

            @pl.when(i == 0)
            def _():
                for cp in _gather_direct_copies(bufs, ssem, rsem, rows)[0]:
                    cp.start()
        _fill_glu(ubuf, a_ref, b_ref, ah_ref, bh_ref, i == 0, tm)

        def chunk(cc, carry):
            cols = pl.ds(pl.multiple_of(cc * 128, 128), 128)
            for r0 in range(0, tm, rb):
                acc = jnp.broadcast_to(dwb_ref[:, cols], (rb, 128))
                for b in range(8):
                    taps = list(range(b, CONV_K, 8))
                    n = rb + 8 * (len(taps) - 1)
                    sh[0:n, :] = ubuf[pl.ds(HALO - (CONV_K - 1) + b + r0, n), cols]
                    for a, k in enumerate(taps):
                        acc = acc + dw_ref[k:k + 1, cols] * sh[8 * a:8 * a + rb, :]
                u2_ref[pl.ds(r0, rb), cols] = acc
            return carry

        lax.fori_loop(0, nch, chunk, 0)
        x = u2_ref[...]
        mu = jnp.mean(x, axis=-1, keepdims=True)
        xc = x - mu
        var = jnp.mean(xc * xc, axis=-1, keepdims=True)
        ln = xc * lax.rsqrt(var + LN_EPS) * lng_ref[...] + lnb_ref[...]
        gate = g_ref[...].astype(F32)
        y_ref[...] = ((ln * _sigmoid(ln)) * (gate * _sigmoid(gate))).astype(BF16)

        if ng:
            @pl.when(i == S // tm - 1)
            def _():
                sends, lands = _gather_direct_copies(bufs, ssem, rsem, rows)
                for cp in lands:
                    cp.wait_recv()
                for cp in sends:
                    cp.wait_send()

    blk = lambda cb: pl.BlockSpec((tm, C), lambda i: (i, cb))
    halo = lambda cb: pl.BlockSpec((HALO, C), lambda i: (jnp.maximum(i * hb - 1, 0), cb))
    vec = pl.BlockSpec((1, C), lambda i: (0, 0))
    hbm = pl.BlockSpec(memory_space=pl.ANY)
    sems = [pltpu.SemaphoreType.DMA((3 * ng,)), pltpu.SemaphoreType.DMA((3 * ng,))] if ng else []
    outs = _pcall(
        body, name=name, grid=(S // tm,),
        in_specs=[blk(0), blk(1), blk(2), halo(0), halo(1),
                  pl.BlockSpec((CONV_K, C), lambda i: (0, 0)), vec, vec, vec] + [hbm] * ng,
        out_specs=[blk(0), blk(0)] + [hbm] * ng,
        out_shape=[jax.ShapeDtypeStruct((S, C), F32), jax.ShapeDtypeStruct((S, C), BF16)]
                  + [jax.ShapeDtypeStruct(g.shape, g.dtype) for g, _, _ in gather],
        scratch_shapes=[pltpu.VMEM((HALO + tm, C), F32), pltpu.VMEM((rb + HALO, 128), F32)] + sems,
        input_output_aliases={9 + n: 2 + n for n in range(ng)},
        compiler_params=_cparams(("arbitrary",) if ng else ("parallel",)),
    )(proj, proj, proj, proj, proj, dw, dwb, lng, lnb, *[g for g, _, _ in gather])
    return outs[0], outs[1], list(outs[2:])


def conv_bwd_norm(dy, proj, u2, lng, lnb, name):
    S, C = dy.shape
    tm = _tile(S, 256, 8)

    def body(dy_ref, g_ref, u2_ref, lng_ref, lnb_ref, du2_ref, dg_ref, sm_ref):
        x = u2_ref[...]
        mu = jnp.mean(x, axis=-1, keepdims=True)
        xc = x - mu
        var = jnp.mean(xc * xc, axis=-1, keepdims=True)
        rs = lax.rsqrt(var + LN_EPS)
        xhat = xc * rs
        gam = lng_ref[...]
        ln = xhat * gam + lnb_ref[...]
        sl = _sigmoid(ln)
        u3 = ln * sl
        gate = g_ref[...].astype(F32)
        sg = _sigmoid(gate)
        dyv = dy_ref[...]
        dgate = dyv * u3 * (sg * (1.0 + gate * (1.0 - sg)))
        dln = (dyv * (gate * sg)) * (sl * (1.0 + ln * (1.0 - sl)))
        dxh = dln * gam
        du2 = rs * (dxh - jnp.mean(dxh, axis=-1, keepdims=True)
                    - xhat * jnp.mean(dxh * xhat, axis=-1, keepdims=True))
        du2_ref[...] = du2
        dg_ref[...] = dgate.astype(BF16)

        @pl.when(pl.program_id(0) == 0)
        def _():
            sm_ref[...] = jnp.zeros_like(sm_ref)

        sm_ref[0:1, :] += jnp.sum(dln * xhat, axis=0, keepdims=True)
        sm_ref[1:2, :] += jnp.sum(dln, axis=0, keepdims=True)
        sm_ref[2:3, :] += jnp.sum(du2, axis=0, keepdims=True)
        sm_ref[3:4, :] += jnp.sum(dgate, axis=0, keepdims=True)

    blk = lambda cb: pl.BlockSpec((tm, C), lambda i: (i, cb))
    vec = pl.BlockSpec((1, C), lambda i: (0, 0))
    return _pcall(
        body, name=name, grid=(S // tm,),
        in_specs=[blk(0), blk(2), blk(0), vec, vec],
        out_specs=[blk(0), blk(2), pl.BlockSpec((8, C), lambda i: (0, 0))],
        out_shape=[jax.ShapeDtypeStruct((S, C), F32), jax.ShapeDtypeStruct((S, 3 * C), BF16),
                   jax.ShapeDtypeStruct((8, C), F32)],
        compiler_params=_cparams(("arbitrary",)),
    )(dy, proj, u2, lng, lnb)


def conv_bwd_taps(du2, proj, dw, dproj, name, exchange=()):
    S, C = du2.shape
    tm = _conv_tile(S)
    hb = tm // HALO
    nb = S // tm
    nch = C // 128
    rb = _tile(tm, 128, 8)
    ne = len(exchange)

    def body(d_ref, dh_ref, a_ref, b_ref, ah_ref, bh_ref, dw_ref, dp_in, *rest):
        del dp_in
        parts = rest[:ne]
        dab_ref, sm_ref = rest[ne:ne + 2]
        got = rest[ne + 2:2 * ne + 2]
        ubuf, dbuf, sh, sh2 = rest[2 * ne + 2:2 * ne + 6]
        i = pl.program_id(0)
        if ne:
            ssem, rsem = rest[2 * ne + 6:]

            @pl.when(i == 0)
            def _():
                for cp in _exchange_copies(parts, got, ssem, rsem)[0]:
                    cp.start()
        _fill_glu(ubuf, a_ref, b_ref, ah_ref, bh_ref, i == 0, tm)
        dbuf[0:tm, :] = d_ref[...]
        dbuf[tm:tm + HALO, :] = jnp.where(i == nb - 1, 0.0, dh_ref[...])

        @pl.when(i == 0)
        def _():
            sm_ref[...] = jnp.zeros_like(sm_ref)

        def chunk(cc, carry):
            cols = pl.ds(pl.multiple_of(cc * 128, 128), 128)
            cols_b = pl.ds(pl.multiple_of(C + cc * 128, 128), 128)
            for r0 in range(0, tm, rb):
                d0 = dbuf[r0:r0 + rb, cols]
                du = jnp.zeros((rb, 128), F32)
                for b in range(8):
                    offs = list(range(b, CONV_K, 8))
                    n = rb + 8 * (len(offs) - 1)
                    sh[0:n, :] = dbuf[pl.ds(r0 + b, n), cols]
                    for a, o in enumerate(offs):
                        k = CONV_K - 1 - o
                        du = du + dw_ref[k:k + 1, cols] * sh[8 * a:8 * a + rb, :]
                    sh2[0:n, :] = ubuf[pl.ds(HALO - (CONV_K - 1) + b + r0, n), cols]
                    for a, k in enumerate(offs):
                        sm_ref[k:k + 1, cols] += jnp.sum(d0 * sh2[8 * a:8 * a + rb, :], axis=0, keepdims=True)
                rows = pl.ds(r0, rb)
                av = a_ref[rows, cols].astype(F32)
                sb = _sigmoid(b_ref[rows, cols].astype(F32))
                da = du * sb
                db = du * av * sb * (1.0 - sb)
                dab_ref[rows, cols] = da.astype(BF16)
                dab_ref[rows, cols_b] = db.astype(BF16)
                sm_ref[32:33, cols] += jnp.sum(da, axis=0, keepdims=True)
                sm_ref[33:34, cols] += jnp.sum(db, axis=0, keepdims=True)
            return carry

        lax.fori_loop(0, nch, chunk, 0)

        if ne:
            @pl.when(i == nb - 1)
            def _():
                sends, lands = _exchange_copies(parts, got, ssem, rsem)
                for cp in lands:
                    cp.wait_recv()
                for cp in sends:
                    cp.wait_send()

    blk = lambda cb: pl.BlockSpec((tm, C), lambda i: (i, cb))
    halo = lambda cb: pl.BlockSpec((HALO, C), lambda i: (jnp.maximum(i * hb - 1, 0), cb))
    nxt = pl.BlockSpec((HALO, C), lambda i: (jnp.minimum((i + 1) * hb, nb * hb - 1), 0))
    hbm = pl.BlockSpec(memory_space=pl.ANY)
    sems = [pltpu.SemaphoreType.DMA((3 * ne,)), pltpu.SemaphoreType.DMA((3 * ne,))] if ne else []
    outs = _pcall(
        body, name=name, grid=(nb,),
        in_specs=[blk(0), nxt, blk(0), blk(1), halo(0), halo(1),
                  pl.BlockSpec((CONV_K, C), lambda i: (0, 0)), hbm] + [hbm] * ne,
        out_specs=[pl.BlockSpec((tm, 2 * C), lambda i: (i, 0)), pl.BlockSpec((40, C), lambda i: (0, 0))] + [hbm] * ne,
        out_shape=[jax.ShapeDtypeStruct((S, 3 * C), BF16), jax.ShapeDtypeStruct((40, C), F32)]
                  + [jax.ShapeDtypeStruct((p.shape[0], N_CHIPS - 1) + p.shape[2:], p.dtype) for p in exchange],
        scratch_shapes=[pltpu.VMEM((HALO + tm, C), F32), pltpu.VMEM((tm + HALO, C), F32),
                        pltpu.VMEM((rb + HALO, 128), F32), pltpu.VMEM((rb + HALO, 128), F32)] + sems,
        input_output_aliases={7: 0},
        compiler_params=_cparams(("arbitrary",)),
    )(du2, du2, proj, proj, proj, proj, dw, dproj, *exchange)
    return outs[0], outs[1], list(outs[2:])


def _rows_tile(R, Cc, budget=1 << 18):
    cap = max(8, budget // max(Cc, 1))
    if R <= cap:
        return R
    t = (cap // 8) * 8
    while t >= 8:
        if R % t == 0:
            return t
        t -= 8
    return R


def cast_into_slot(w, chip, name, l0, nl):
    _, R, Cc = w.shape
    tr = _rows_tile(R, Cc)

    def body(s_ref, w_ref, o_ref):
        del s_ref
        o_ref[...] = w_ref[...].astype(BF16)

    grid_spec = pltpu.PrefetchScalarGridSpec(
        num_scalar_prefetch=1, grid=(nl, R // tr),
        in_specs=[pl.BlockSpec((None, tr, Cc), lambda l, r, s: (l0 + l, r, 0))],
        out_specs=pl.BlockSpec((None, None, tr, Cc), lambda l, r, s: (l, s[0], r, 0)),
    )
    return _pcall(
        body, name=name, grid_spec=grid_spec, out_shape=jax.ShapeDtypeStruct((nl, N_CHIPS, R, Cc), BF16),
        compiler_params=_cparams(("parallel", "parallel")),
    )(chip, w)


def pair_sum(g, rcv, cidx, name):
    L, K, _, half, Cc = g.shape
    g5 = g
    tr = _rows_tile(half, Cc)

    def body(c_ref, g_ref, r_ref, o_ref):
        del c_ref
        o_ref[...] = (g_ref[...] + r_ref[...]).astype(BF16)

    grid_spec = pltpu.PrefetchScalarGridSpec(
        num_scalar_prefetch=1, grid=(L, K, half // tr),
        in_specs=[pl.BlockSpec((None, None, None, tr, Cc), lambda l, k, r, c: (l, k, c[0], r, 0)),
                  pl.BlockSpec((None, None, tr, Cc), lambda l, k, r, c: (l, k, r, 0))],
        out_specs=pl.BlockSpec((None, None, tr, Cc), lambda l, k, r, c: (l, k, r, 0)),
    )
    return _pcall(
        body, name=name, grid_spec=grid_spec, out_shape=jax.ShapeDtypeStruct((L, K, half, Cc), BF16),
        compiler_params=_cparams(("parallel", "parallel", "parallel")),
    )(cidx, g5, rcv)


def chip_sum(parts, got, sel, name, layers, l0, prev=None):
    Lp, _, R, Cc = parts.shape
    n_got = got.shape[1]
    tr = _rows_tile(R, Cc)

    def body(s_ref, p_ref, g_ref, *rest):
        del s_ref
        o_ref = rest[-1]
        acc = p_ref[...].astype(F32)
        for k in range(n_got):
            acc = acc + g_ref[k].astype(F32)
        o_ref[...] = acc

    in_specs = [pl.BlockSpec((None, None, tr, Cc), lambda l, r, s: (l, s[0], r, 0)),
                pl.BlockSpec((None, n_got, tr, Cc), lambda l, r, s: (l, 0, r, 0))]
    ops = [sel, parts, got]
    aliases = {}
    if prev is not None:
        in_specs.append(pl.BlockSpec(memory_space=pl.ANY))
        ops.append(prev)
        aliases = {3: 0}
    grid_spec = pltpu.PrefetchScalarGridSpec(
        num_scalar_prefetch=1, grid=(Lp, R // tr), in_specs=in_specs,
        out_specs=pl.BlockSpec((None, None, tr, Cc), lambda l, r, s: (l0 + l, s[1], r, 0)),
    )
    return _pcall(
        body, name=name, grid_spec=grid_spec, out_shape=jax.ShapeDtypeStruct((layers, 2, R, Cc), F32),
        input_output_aliases=aliases, compiler_params=_cparams(("parallel", "parallel")),
    )(*ops)


def dev_sum(parts, name):
    K, R, Cc = parts.shape

    def body(p_ref, o_ref):
        acc = p_ref[0]
        for k in range(1, K):
            acc = acc + p_ref[k]
        o_ref[...] = acc

    return _pcall(
        body, name=name, grid=(R // 8,),
        in_specs=[pl.BlockSpec((K, 8, Cc), lambda r: (0, r, 0))],
        out_specs=pl.BlockSpec((8, Cc), lambda r: (r, 0)),
        out_shape=jax.ShapeDtypeStruct((R, Cc), F32), compiler_params=_cparams(("parallel",)),
    )(parts)


def adamw(w, g, m, v, name):
    shape = w.shape
    if w.ndim == 3:
        L, R, Cc = shape
    else:
        L, R, Cc = 1, (1 if w.ndim == 1 else shape[0]), shape[-1]
    view = lambda t: t.reshape((L, R, Cc))
    tr = _rows_tile(R, Cc, budget=1 << 17)
    c1 = 1.0 - ADAM_B1 ** ADAM_STEP
    c2 = 1.0 - ADAM_B2 ** ADAM_STEP

    def body(w_ref, g_ref, m_ref, v_ref, d_ref, nm_ref, nv_ref):
        gg = g_ref[...]
        nm = ADAM_B1 * m_ref[...] + (1.0 - ADAM_B1) * gg
        nv = ADAM_B2 * v_ref[...] + (1.0 - ADAM_B2) * (gg * gg)
        d_ref[...] = -ADAM_LR * ((nm / c1) / (jnp.sqrt(nv / c2) + ADAM_EPS) + ADAM_WD * w_ref[...])
        nm_ref[...] = nm
        nv_ref[...] = nv

    if w.ndim == 3 and Cc % 128 != 0:
        ta = max(t for t in range(1, 17) if Cc % t == 0)
        tview = lambda t: jnp.transpose(t, (2, 0, 1))
        spec = pl.BlockSpec((ta, L, R), lambda a: (a, 0, 0))
        outs = _pcall(
            body, name=name, grid=(Cc // ta,), in_specs=[spec] * 4, out_specs=[spec] * 3,
            out_shape=[jax.ShapeDtypeStruct((Cc, L, R), F32)] * 3, compiler_params=_cparams(("parallel",)),
        )(tview(w), tview(g), tview(m), tview(v))
        return tuple(jnp.transpose(o, (1, 2, 0)) for o in outs)
    spec = pl.BlockSpec((None, tr, Cc), lambda l, r: (l, r, 0))
    outs = _pcall(
        body, name=name, grid=(L, R // tr), in_specs=[spec] * 4, out_specs=[spec] * 3,
        out_shape=[jax.ShapeDtypeStruct((L, R, Cc), F32)] * 3, compiler_params=_cparams(("parallel", "parallel")),
    )(view(w), view(g), view(m), view(v))
    return tuple(o.reshape(shape) for o in outs)


def _place():
    x, y, c = lax.axis_index("x"), lax.axis_index("y"), lax.axis_index("c")
    other_chips = [(1 - x, y), (x, 1 - y), (1 - x, 1 - y)]
    return x, y, c, other_chips


def _rcopy(src, dst, ssem, rsem, k, to):
    return pltpu.make_async_remote_copy(src_ref=src, dst_ref=dst, send_sem=ssem.at[k], recv_sem=rsem.at[k],
                                        device_id=to, device_id_type=MESH)


def gather_weights(slots, small):
    nt = len(slots)

    def body(*refs):
        outs, small_out = refs[nt + 1:2 * nt + 1], refs[2 * nt + 1]
        ssem, rsem = refs[2 * nt + 2:]
        x, y, c, chips = _place()
        me = 2 * x + y
        sib = (x, y, 1 - c)

        def rows(t, half_of):
            half = outs[t].shape[2] // 2
            return pl.ds(half_of * half, half)

        first, passed = [], []
        for t in range(nt):
            mine = outs[t].at[:, me, rows(t, c), :]
            for j, chip in enumerate(chips):
                first.append(_rcopy(mine, mine, ssem, rsem, 6 * t + j, (*chip, c)))
        for j, chip in enumerate(chips):
            first.append(_rcopy(small_out.at[me], small_out.at[me], ssem, rsem, 6 * nt + j, (*chip, c)))
        for cp in first:
            cp.start()
        for t in range(nt):
            for j, (px, py) in enumerate(chips):
                land = outs[t].at[:, 2 * px + py, rows(t, c), :]
                _rcopy(land, land, ssem, rsem, 6 * t + j, (x, y, c)).wait_recv()
                fw = _rcopy(land, land, ssem, rsem, 6 * t + 3 + j, sib)
                fw.start()
                passed.append(fw)
        for j, (px, py) in enumerate(chips):
            land = small_out.at[2 * px + py]
            _rcopy(land, land, ssem, rsem, 6 * nt + j, (x, y, c)).wait_recv()
        for t in range(nt):
            for j, (px, py) in enumerate(chips):
                land = outs[t].at[:, 2 * px + py, rows(t, 1 - c), :]
                _rcopy(land, land, ssem, rsem, 6 * t + 3 + j, (x, y, c)).wait_recv()
        for cp in first + passed:
            cp.wait_send()

    ops = list(slots) + [small]
    nsem = 6 * nt + 3
    return _pcall(
        body, name="gather_weights", out_shape=[jax.ShapeDtypeStruct(s.shape, s.dtype) for s in ops],
        in_specs=[pl.BlockSpec(memory_space=pl.ANY)] * (nt + 1),
        out_specs=[pl.BlockSpec(memory_space=pl.ANY)] * (nt + 1),
        input_output_aliases={n: n for n in range(nt + 1)},
        scratch_shapes=[pltpu.SemaphoreType.DMA((nsem,)), pltpu.SemaphoreType.DMA((nsem,))],
    )(*ops)


def swap_halves(grads, name):
    nt = len(grads)

    def body(*refs):
        ins, outs = refs[:nt], refs[nt:2 * nt]
        ssem, rsem = refs[2 * nt:]
        x, y, c, _ = _place()
        sib = (x, y, 1 - c)
        cps = [_rcopy(ins[t].at[:, :, 1 - c], outs[t], ssem, rsem, t, sib) for t in range(nt)]
        for cp in cps:
            cp.start()
        for cp in cps:
            cp.wait()

    out_shape = [jax.ShapeDtypeStruct(g.shape[:2] + g.shape[3:], g.dtype) for g in grads]
    return _pcall(
        body, name=name, out_shape=out_shape,
        in_specs=[pl.BlockSpec(memory_space=pl.ANY)] * nt, out_specs=[pl.BlockSpec(memory_space=pl.ANY)] * nt,
        scratch_shapes=[pltpu.SemaphoreType.DMA((nt,)), pltpu.SemaphoreType.DMA((nt,))],
    )(*grads)


def _exchange_copies(ins, outs, ssem, rsem):
    x, y, c, chips = _place()
    sends, lands = [], []
    for t in range(len(ins)):
        for j, (px, py) in enumerate(chips):
            k = 3 * t + j
            sends.append(_rcopy(ins[t].at[:, 2 * px + py], outs[t].at[:, j], ssem, rsem, k, (px, py, c)))
            land = outs[t].at[:, j]
            lands.append(_rcopy(land, land, ssem, rsem, k, (x, y, c)))
    return sends, lands


def _gather_direct_copies(bufs, ssem, rsem, rows=None):
    x, y, c, chips = _place()
    me = 2 * x + y
    sends, lands = [], []
    for t, buf in enumerate(bufs):
        r0, rn = rows[t] if rows else (0, buf.shape[2])
        mine = buf.at[:, me, pl.ds(r0, rn)]
        for j, (px, py) in enumerate(chips):
            k = 3 * t + j
            sends.append(_rcopy(mine, mine, ssem, rsem, k, (px, py, c)))
            land = buf.at[:, 2 * px + py, pl.ds(r0, rn)]
            lands.append(_rcopy(land, land, ssem, rsem, k, (x, y, c)))
    return sends, lands


def exchange_partials(parts, small):
    nt = len(parts)

    def body(*refs):
        ins = refs[:nt]
        outs, small_out = refs[nt + 1:2 * nt + 1], refs[2 * nt + 1]
        ssem, rsem = refs[2 * nt + 2:]
        x, y, c, chips = _place()
        dev = 4 * x + 2 * y + c
        sends, lands = _exchange_copies(ins, outs, ssem, rsem)
        peers = [(px, py, pc) for pc in (c, 1 - c) for (px, py) in [(x, y)] + chips][1:]
        for j, (px, py, pc) in enumerate(peers):
            k = 3 * nt + j
            sends.append(_rcopy(small_out.at[dev], small_out.at[dev], ssem, rsem, k, (px, py, pc)))
            land = small_out.at[4 * px + 2 * py + pc]
            lands.append(_rcopy(land, land, ssem, rsem, k, (x, y, c)))
        for cp in sends:
            cp.start()
        for cp in lands:
            cp.wait_recv()
        for cp in sends:
            cp.wait_send()

    out_shape = [jax.ShapeDtypeStruct((p.shape[0], N_CHIPS - 1) + p.shape[2:], p.dtype) for p in parts]
    out_shape.append(jax.ShapeDtypeStruct(small.shape, small.dtype))
    nsem = 3 * nt + 7
    return _pcall(
        body, name="exchange_partials", out_shape=out_shape,
        in_specs=[pl.BlockSpec(memory_space=pl.ANY)] * (nt + 1),
        out_specs=[pl.BlockSpec(memory_space=pl.ANY)] * (nt + 1),
        input_output_aliases={nt: nt},
        scratch_shapes=[pltpu.SemaphoreType.DMA((nsem,)), pltpu.SemaphoreType.DMA((nsem,))],
    )(*parts, small)


def join_halves(halves):
    nt = len(halves)

    def body(*refs):
        outs = refs[nt:2 * nt]
        ssem, rsem = refs[2 * nt:]
        x, y, c, _ = _place()
        sib = (x, y, 1 - c)
        sends = [_rcopy(outs[t].at[:, c], outs[t].at[:, c], ssem, rsem, t, sib) for t in range(nt)]
        for cp in sends:
            cp.start()
        for t in range(nt):
            land = outs[t].at[:, 1 - c]
            _rcopy(land, land, ssem, rsem, t, (x, y, c)).wait_recv()
        for cp in sends:
            cp.wait_send()

    return _pcall(
        body, name="join_halves", out_shape=[jax.ShapeDtypeStruct(h.shape, h.dtype) for h in halves],
        in_specs=[pl.BlockSpec(memory_space=pl.ANY)] * nt, out_specs=[pl.BlockSpec(memory_space=pl.ANY)] * nt,
        input_output_aliases={n: n for n in range(nt)},
        scratch_shapes=[pltpu.SemaphoreType.DMA((nt,)), pltpu.SemaphoreType.DMA((nt,))],
    )(*halves)


def _pad_cols(a, n):
    return jnp.pad(a, [(0, 0)] * (a.ndim - 1) + [(0, n - a.shape[-1])])


def kernel(x, norm_g, fox_w_in, fox_b_f, fox_w_out, conv_w_in, conv_b_in, conv_dw, conv_dw_b, conv_ln_g, conv_ln_b, conv_w_out, final_norm_g, loss_target, m_norm_g, m_fox_w_in, m_fox_b_f, m_fox_w_out, m_conv_w_in, m_conv_b_in, m_conv_dw, m_conv_dw_b, m_conv_ln_g, m_conv_ln_b, m_conv_w_out, m_final_norm_g, v_norm_g, v_fox_w_in, v_fox_b_f, v_fox_w_out, v_conv_w_in, v_conv_b_in, v_conv_dw, v_conv_dw_b, v_conv_ln_g, v_conv_ln_b, v_conv_w_out, v_final_norm_g):
    S, D = x.shape[1], x.shape[2]
    H = fox_b_f.shape[1]
    assert D == H * HEAD_DIM, "one head must be one lane tile"
    W = C = D
    NL = fox_w_in.shape[0]
    Dq = D // N_CHIPS
    NA = fox_w_in.shape[2]
    scale = HEAD_DIM ** -0.5
    chip = 2 * lax.axis_index("x") + lax.axis_index("y")
    core = lax.axis_index("c")
    cidx = core.astype(jnp.int32).reshape((1,))
    chip1 = chip.astype(jnp.int32).reshape((1,))
    sel = jnp.stack([chip, core]).astype(jnp.int32)

    small_pack = jnp.concatenate([
        conv_b_in.reshape((NL * 3, Dq)), conv_dw.reshape((NL * CONV_K, Dq)), conv_dw_b, conv_ln_g, conv_ln_b,
        jnp.zeros((PACK_ROWS - NL * (3 + CONV_K + 3), Dq), F32)], axis=0)
    small_slots = lax.dynamic_update_slice(jnp.zeros((N_CHIPS, PACK_ROWS, Dq), F32), small_pack[None], (chip, 0, 0))
    ga0, gb0, gsmall = gather_weights(
        [cast_into_slot(fox_w_in, chip1, "cast_fox_w_in_0", 0, 1),
         cast_into_slot(fox_w_out, chip1, "cast_fox_w_out_0", 0, 1)], small_slots)
    fox_raw = {l: (cast_into_slot(fox_w_in, chip1, f"cast_fox_w_in_{l}", l, 1),
                   cast_into_slot(fox_w_out, chip1, f"cast_fox_w_out_{l}", l, 1)) for l in range(1, NL)}
    conv_raw = {l: (cast_into_slot(conv_w_in, chip1, f"cast_conv_w_in_{l}", l, 1),
                    cast_into_slot(conv_w_out, chip1, f"cast_conv_w_out_{l}", l, 1)) for l in range(NL)}

    def fox_weights(ga, gb):
        n = ga.shape[0]
        wfull = jnp.transpose(ga, (0, 2, 1, 3)).reshape((n, D, N_CHIPS * NA))
        wq, wk, wv, wg, wf = (wfull[:, :, 0:W], wfull[:, :, W:2 * W], wfull[:, :, 2 * W:3 * W],
                              wfull[:, :, 3 * W:4 * W], wfull[:, :, 4 * W:])
        w1 = jnp.concatenate([wq, wg], axis=-1).reshape((n, 1, D, 2 * W))
        w2 = jnp.stack([wk.reshape((n, D, H, HEAD_DIM)), wv.reshape((n, D, H, HEAD_DIM))], axis=3)
        return w1, w2.reshape((n, 1, D, 2 * W)), _pad_cols(wf, 128).reshape((n, 1, D, 128)), gb.reshape((n, 1, W, D))

    fox_w = {0: fox_weights(ga0, gb0)}
    conv_w = {}
    b_in = gsmall[:, 0:3 * NL, :].reshape((N_CHIPS, NL, 3 * Dq)).transpose((1, 0, 2)).reshape((NL, 3 * C))
    dwt = gsmall[:, 3 * NL:3 * NL + CONV_K * NL, :].reshape((N_CHIPS, NL, CONV_K, Dq))
    dwt = dwt.transpose((1, 2, 0, 3)).reshape((NL, CONV_K, C))
    r0 = (3 + CONV_K) * NL
    vecs = gsmall[:, r0:r0 + 3 * NL, :].reshape((N_CHIPS, 3, NL, Dq)).transpose((1, 2, 0, 3)).reshape((3, NL, C))
    dwb, lng, lnb = vecs[0], vecs[1], vecs[2]
    bfp = _pad_cols(fox_b_f, 128)

    h = x.reshape((S, D))
    tgt = loss_target.reshape((S, D))
    saved = []
    n_layers = norm_g.shape[0]
    for i in range(n_layers):
        j = i // 2
        g_i = norm_g[i:i + 1]
        hn = rms_fwd(h, g_i, f"rms_fwd_{i}")
        if i % 2 == 0:
            w1, w2, wfp, wo_fox = fox_w[j]
            gc, gd = conv_raw[j]
            p1, (gc,) = mm_nn(hn, w1, 0, 1, out_dtype=BF16, name=f"fox_proj_qg_{i}", gather=[(gc, 0, D // 2)])
            p2, (gc,) = mm_nn(hn, w2, 0, 1, out_dtype=BF16, name=f"fox_proj_kv_{i}", gather=[(gc, D // 2, D // 2)])
            pf = mm_nn(hn, wfp, 0, 1, out_dtype=F32, name=f"fox_proj_f_{i}")
            qa, ka, c0, c1 = fox_decay(pf, bfp[j:j + 1], H, scale, f"fox_decay_{i}")
            qn, kn = attn_norms(p1, p2, H, f"attn_norms_{i}")
            first_key, query_end = prune_tables(qn, kn, c0, c1, H, scale)
            o, yv, qb, _ = attn_fwd(p1, p2, qa, ka, H, scale, f"attn_fwd_{i}", first_key=first_key)
            h_new, (gd,) = mm_nt_res(yv, wo_fox, 0, h, f"fox_out_{i}", gather=[(gd, 0, Dq)])
            conv_w[j] = (gc, gd.reshape((1, 1, C, D)))
            saved.append((h, hn, p1, p2, pf, ka, o, yv, qb, query_end))
        else:
            wc, wo_conv = conv_w[j]
            nxt = fox_raw.get(j + 1)
            proj = mm_nn(hn, wc, 0, N_CHIPS, out_dtype=BF16, name=f"conv_proj_{i}", bias=b_in[j:j + 1],
                         gather=[(nxt[0], 0, D // 2)] if nxt else ())
            if nxt:
                proj, (ga,) = proj
            u2, yv, filled = conv_fwd(proj, dwt[j], dwb[j:j + 1], lng[j:j + 1], lnb[j:j + 1], f"conv_fwd_{i}",
                                      gather=[(ga, D // 2, D // 2)] if nxt else ())
            h_new = mm_nt_res(yv, wo_conv, 0, h, f"conv_out_{i}", gather=[(nxt[1], 0, Dq)] if nxt else ())
            if nxt:
                h_new, (gb,) = h_new
                fox_w[j + 1] = fox_weights(filled[0], gb)
            saved.append((h, hn, proj, u2, yv))
        h = h_new

    dh, d_gf, loss_part = loss_head(h, final_norm_g.reshape((1, D)), tgt, "loss_head")

    d_norm = [None] * n_layers
    d_fox_b = [None] * NL
    d_conv_small = [None] * NL
    wgrad = {}

    def pair_sums(keys, tag):
        gs = [wgrad[k].reshape(wgrad[k].shape[:2] + (2, wgrad[k].shape[2] // 2, wgrad[k].shape[3])) for k in keys]
        rcv = swap_halves(gs, f"swap_halves_{tag}")
        return {k: pair_sum(g, r, cidx, f"pair_sum_{k[0]}{k[1]}") for k, g, r in zip(keys, gs, rcv)}

    parts, got = {}, {}

    def pending(tag):
        keys = [k for k in wgrad if k not in parts]
        parts.update(pair_sums(keys, tag))
        return keys

    for i in reversed(range(n_layers)):
        j = i // 2
        g_i = norm_g[i:i + 1]
        if i % 2 == 0:
            h_in, hn, p1, p2, pf, ka, o, yv, qb, query_end = saved[i]
            (w1, w2, wfp, wo_fox), wl = fox_w[j], 0
            wgrad[("b", j)] = mm_tn(yv, dh, 1, name=f"fox_out_dw_{i}").reshape((1, N_CHIPS, Dq, D))
            dy = mm_nn_t(dh, wo_fox, wl, f"fox_out_dx_{i}")
            do, dp1, da = attn_bwd_prep(dy, o, p1, H, f"attn_bwd_prep_{i}")
            dp1, dp2, dc, _ = attn_bwd(p1, qb, do, da, p2, ka, dp1, H, scale, f"attn_bwd_{i}", query_end=query_end)
            dz, dbf = fox_decay_bwd(dc, pf, bfp[j:j + 1], H, f"fox_decay_bwd_{i}")
            d_fox_b[j] = dbf
            keys = pending(f"l{i}")
            dw1, arrived = mm_tn(hn, dp1, 1, name=f"fox_dw_qg_{i}", exchange=[parts[k] for k in keys])
            got.update(dict(zip(keys, arrived)))
            dw1 = dw1[0, 0]
            dw2 = mm_tn(hn, dp2, 1, name=f"fox_dw_kv_{i}")[0, 0]
            dwf = mm_tn(hn, dz, 1, name=f"fox_dw_f_{i}")[0, 0]
            dw2 = dw2.reshape((D, H, 2, HEAD_DIM))
            d_in = jnp.concatenate([dw1[:, :W], dw2[:, :, 0].reshape((D, W)), dw2[:, :, 1].reshape((D, W)),
                                    dw1[:, W:], dwf[:, :H]], axis=-1)
            wgrad[("a", j)] = d_in.reshape((D, N_CHIPS, NA)).transpose((1, 0, 2))[None]
            dhn = mm_nt(dp1, w1, wl, 1, name=f"fox_dx_qg_{i}")
            dhn = mm_nt(dp2, w2, wl, 1, name=f"fox_dx_kv_{i}", res=dhn)
            dhn = mm_nt(dz, wfp, wl, 1, name=f"fox_dx_f_{i}", res=dhn)
        else:
            h_in, hn, proj, u2, yv = saved[i]
            wc, wo_conv = conv_w[j]
            wgrad[("d", j)] = mm_tn(yv, dh, 1, name=f"conv_out_dw_{i}").reshape((1, N_CHIPS, Dq, D))
            dy = mm_nn_t(dh, wo_conv, 0, f"conv_out_dx_{i}")
            du2, dproj, sm1 = conv_bwd_norm(dy, proj, u2, lng[j:j + 1], lnb[j:j + 1], f"conv_bwd_norm_{i}")
            keys = pending(f"l{i}")
            dproj, sm2, arrived = conv_bwd_taps(du2, proj, dwt[j], dproj, f"conv_bwd_taps_{i}",
                                                exchange=[parts[k] for k in keys])
            got.update(dict(zip(keys, arrived)))
            d_conv_small[j] = (sm1, sm2)
            wgrad[("c", j)] = mm_tn(hn, dproj, N_CHIPS, name=f"conv_dw_in_{i}")
            dhn = mm_nt(dproj, wc, 0, N_CHIPS, name=f"conv_dx_{i}")
        dh, d_norm[i] = rms_bwd(dhn, h_in, g_i, dh, f"rms_bwd_{i}")

    late = [k for k in wgrad if k not in parts]
    parts.update(pair_sums(late, "late"))

    zrow = jnp.zeros((1, D), F32)
    rows = list(d_norm) + [d_gf]
    rows += [_pad_cols(d_fox_b[l][:, :H], D) for l in range(NL)]
    rows += [_pad_cols(loss_part[:, :1], D)]
    for l in range(NL):
        sm1, sm2 = d_conv_small[l]
        rows += [sm2[32:33], sm2[33:34], sm1[3:4]]
    for l in range(NL):
        rows += [d_conv_small[l][1][0:CONV_K]]
    rows += [d_conv_small[l][0][2:3] for l in range(NL)]
    rows += [d_conv_small[l][0][0:1] for l in range(NL)]
    rows += [d_conv_small[l][0][1:2] for l in range(NL)]
    n_rows = sum(r.shape[0] for r in rows)
    rows += [zrow] * (SMALL_ROWS - n_rows)
    small = jnp.concatenate(rows, axis=0)

    dev = 4 * lax.axis_index("x") + 2 * lax.axis_index("y") + core
    small_slots = lax.dynamic_update_slice(jnp.zeros((N_DEV, SMALL_ROWS, D), F32), small[None], (dev, 0, 0))
    arrived = exchange_partials([parts[k] for k in late], small_slots)
    got.update(dict(zip(late, arrived[:-1])))
    tot = dev_sum(arrived[-1], "dev_sum")
    halves = {}
    for kind in "abcd":
        for l in range(NL):
            halves[kind] = chip_sum(parts[(kind, l)], got[(kind, l)], sel, f"chip_sum_{kind}{l}", NL, l,
                                    prev=halves.get(kind))
    full = join_halves([halves[kind] for kind in "abcd"])
    grad_fox_w_in = full[0].reshape(fox_w_in.shape)
    grad_fox_w_out = full[1].reshape(fox_w_out.shape)
    grad_conv_w_in = full[2].reshape(conv_w_in.shape)
    grad_conv_w_out = full[3].reshape(conv_w_out.shape)

    def mine(v):
        return lax.dynamic_slice_in_dim(v, chip * Dq, Dq, axis=v.ndim - 1)

    r = n_layers
    grad_norm_g = tot[0:r]
    grad_final = tot[r]
    grad_fox_b_f = tot[r + 1:r + 1 + NL, :H]
    loss = tot[r + 1 + NL, 0]
    r = r + 2 + NL
    gb_full = tot[r:r + 3 * NL].reshape((NL, 3 * C))
    grad_conv_b_in = lax.dynamic_slice_in_dim(gb_full, chip * 3 * Dq, 3 * Dq, axis=1)
    r += 3 * NL
    grad_conv_dw = mine(tot[r:r + CONV_K * NL].reshape((NL, CONV_K, C)))
    r += CONV_K * NL
    grad_conv_dw_b = mine(tot[r:r + NL])
    grad_conv_ln_g = mine(tot[r + NL:r + 2 * NL])
    grad_conv_ln_b = mine(tot[r + 2 * NL:r + 3 * NL])

    grads = [grad_norm_g, grad_fox_w_in, grad_fox_b_f, grad_fox_w_out, grad_conv_w_in, grad_conv_b_in,
             grad_conv_dw, grad_conv_dw_b, grad_conv_ln_g, grad_conv_ln_b, grad_conv_w_out, grad_final]
    ws = [norm_g, fox_w_in, fox_b_f, fox_w_out, conv_w_in, conv_b_in, conv_dw, conv_dw_b, conv_ln_g, conv_ln_b,
          conv_w_out, final_norm_g]
    ms = [m_norm_g, m_fox_w_in, m_fox_b_f, m_fox_w_out, m_conv_w_in, m_conv_b_in, m_conv_dw, m_conv_dw_b,
          m_conv_ln_g, m_conv_ln_b, m_conv_w_out, m_final_norm_g]
    vs = [v_norm_g, v_fox_w_in, v_fox_b_f, v_fox_w_out, v_conv_w_in, v_conv_b_in, v_conv_dw, v_conv_dw_b,
          v_conv_ln_g, v_conv_ln_b, v_conv_w_out, v_final_norm_g]
    deltas, new_ms, new_vs = [], [], []
    for n, (w_, g_, m_, v_) in enumerate(zip(ws, grads, ms, vs)):
        d_, nm_, nv_ = adamw(w_, g_, m_, v_, f"adamw_{n}")
        deltas.append(d_)
        new_ms.append(nm_)
        new_vs.append(nv_)
    grad_x = dh.reshape(x.shape)
    return (loss, grad_x, *grads, *deltas, *new_ms, *new_vs)


def mm_nt_res(y, wo, lidx, h, name, gather=()):
    M, K = y.shape
    N = wo.shape[-1]
    tm = _tile(M, 1024)
    tn = _tile(N, 1024)
    grid = (M // tm, N // tn, 1)
    return _matmul(
        y, wo, contract="nn", grid=grid, name=name,
        a_spec=pl.BlockSpec((tm, K), lambda i, j, k: (i, 0)),
        b_spec=pl.BlockSpec((None, None, K, tn), lambda i, j, k: (lidx, 0, 0, j)),
        o_spec=pl.BlockSpec((tm, tn), lambda i, j, k: (i, j)),
        out_shape=jax.ShapeDtypeStruct((M, N), F32), acc_shape=(tm, tn),
        res=h, res_spec=pl.BlockSpec((tm, tn), lambda i, j, k: (i, j)), gather=gather,
    )


def mm_nn_t(dh, wo, lidx, name):
    M, K = dh.shape
    N = wo.shape[-2]
    tm = _tile(M, 512)
    tn = _tile(N, 1024)
    grid = (M // tm, N // tn, 1)
    return _matmul(
        dh, wo, contract="nt", grid=grid, name=name,
        a_spec=pl.BlockSpec((tm, K), lambda i, j, k: (i, 0)),
        b_spec=pl.BlockSpec((None, None, tn, K), lambda i, j, k: (lidx, 0, j, 0)),
        o_spec=pl.BlockSpec((tm, tn), lambda i, j, k: (i, j)),
        out_shape=jax.ShapeDtypeStruct((M, N), F32), acc_shape=(tm, tn),
    )
```

```python
import jax
import jax.numpy as jnp
from jax import lax
from jax.experimental import pallas as pl
from jax.experimental.pallas import tpu as pltpu

F32 = jnp.float32
BF16 = jnp.bfloat16
MESH = pl.DeviceIdType.MESH

RMS_EPS = 1e-6
LN_EPS = 1e-5
CONV_K = 31
HALO = 32
HEAD_DIM = 128
ADAM_LR = 0.001
ADAM_B1 = 0.9
ADAM_B2 = 0.999
ADAM_EPS = 1e-08
ADAM_WD = 0.01
ADAM_STEP = 10
N_CHIPS = 4
N_DEV = 8
VMEM_LIMIT = 56 * 1024 * 1024
NEG_BIG = -1e30
SMALL_ROWS = 88
PACK_ROWS = 80


def _pcall(body, **kw):
    return pl.pallas_call(body, **kw)


def _cparams(sem=None):
    return pltpu.CompilerParams(dimension_semantics=sem, vmem_limit_bytes=VMEM_LIMIT)


def _tile(n, cap, mult=128):
    if n <= cap:
        return n
    t = (cap // mult) * mult
    while t >= mult:
        if n % t == 0:
            return t
        t -= mult
    raise ValueError(f"no tile for {n} under {cap}")


def _sigmoid(x):
    return 1.0 / (1.0 + jnp.exp(-x))


def _split3(x):
    hi = x.astype(BF16).astype(F32)
    r = x - hi
    mid = r.astype(BF16).astype(F32)
    lo = (r - mid).astype(BF16).astype(F32)
    return hi, mid, lo


_DN = {
    "nn": (((1,), (0,)), ((), ())),
    "nt": (((1,), (1,)), ((), ())),
    "tn": (((0,), (0,)), ((), ())),
}


def _matmul(a, b, *, contract, grid, a_spec, b_spec, o_spec, out_shape, acc_shape, name,
            bias=None, bias_spec=None, res=None, res_spec=None, alias_res=False, exchange=(), gather=()):
    nk = grid[2]
    has_bias = bias is not None
    has_res = res is not None
    assert not (exchange and gather)
    ne = len(exchange) + len(gather)
    rows = [(r0, rn) for _, r0, rn in gather]

    def _copies(parts, got, ssem, rsem):
        if gather:
            return _gather_direct_copies(got, ssem, rsem, rows)
        return _exchange_copies(parts, got, ssem, rsem)

    def body(*refs):
        a_ref, b_ref = refs[0], refs[1]
        pos = 2
        bias_ref = res_ref = None
        if has_bias:
            bias_ref = refs[pos]
            pos += 1
        if has_res:
            res_ref = refs[pos]
            pos += 1
        parts = refs[pos:pos + ne]
        pos += ne
        o_ref = refs[pos]
        got = refs[pos + 1:pos + 1 + ne]
        pos += ne
        acc_ref = refs[pos + 1] if nk > 1 else None
        if ne:
            ssem, rsem = refs[-2:]
            ids = [pl.program_id(d) for d in range(3)]

            @pl.when((ids[0] == 0) & (ids[1] == 0) & (ids[2] == 0))
            def _():
                for cp in _copies(parts, got, ssem, rsem)[0]:
                    cp.start()
        p = lax.dot_general(a_ref[...].astype(BF16), b_ref[...].astype(BF16), _DN[contract],
                            preferred_element_type=F32)

        def finish(v):
            if has_bias:
                v = v + bias_ref[...]
            if has_res:
                v = res_ref[...] + v
            o_ref[...] = v.astype(o_ref.dtype)

        if nk == 1:
            finish(p)
        else:
            k = pl.program_id(2)

            @pl.when(k == 0)
            def _():
                acc_ref[...] = p

            @pl.when(k > 0)
            def _():
                acc_ref[...] += p

            @pl.when(k == nk - 1)
            def _():
                finish(acc_ref[...])

        if ne:
            @pl.when((ids[0] == grid[0] - 1) & (ids[1] == grid[1] - 1) & (ids[2] == grid[2] - 1))
            def _():
                sends, lands = _copies(parts, got, ssem, rsem)
                for cp in lands:
                    cp.wait_recv()
                for cp in sends:
                    cp.wait_send()

    ins = [a, b]
    specs = [a_spec, b_spec]
    if has_bias:
        ins.append(bias)
        specs.append(bias_spec)
    if has_res:
        ins.append(res)
        specs.append(res_spec)
    aliases = {len(ins) - 1: 0} if (has_res and alias_res) else {}
    hbm = pl.BlockSpec(memory_space=pl.ANY)
    scratch = [pltpu.VMEM(acc_shape, F32)] if nk > 1 else []
    if not ne:
        return _pcall(
            body, name=name, grid=grid, in_specs=specs, out_specs=o_spec, out_shape=out_shape,
            scratch_shapes=scratch, input_output_aliases=aliases,
            compiler_params=_cparams(("parallel", "parallel", "arbitrary")),
        )(*ins)
    if gather:
        extra = [g for g, _, _ in gather]
        extra_out = [jax.ShapeDtypeStruct(g.shape, g.dtype) for g in extra]
        aliases = {**aliases, **{len(ins) + n: 1 + n for n in range(ne)}}
    else:
        extra = list(exchange)
        extra_out = [jax.ShapeDtypeStruct((p.shape[0], N_CHIPS - 1) + p.shape[2:], p.dtype) for p in extra]
    outs = _pcall(
        body, name=name, grid=grid, in_specs=specs + [hbm] * ne, out_specs=[o_spec] + [hbm] * ne,
        out_shape=[out_shape] + extra_out,
        scratch_shapes=scratch + [pltpu.SemaphoreType.DMA((3 * ne,)), pltpu.SemaphoreType.DMA((3 * ne,))],
        input_output_aliases=aliases, compiler_params=_cparams(("arbitrary", "arbitrary", "arbitrary")),
    )(*ins, *extra)
    return outs[0], list(outs[1:])


def mm_nn(a, w, lidx, n_slots, *, out_dtype, name, bias=None, gather=()):
    M, K = a.shape
    Ns = w.shape[-1]
    tm = _tile(M, 1024)
    tn = _tile(Ns, 1024)
    per = Ns // tn
    grid = (M // tm, n_slots * per, 1)
    return _matmul(
        a, w, contract="nn", grid=grid, name=name,
        a_spec=pl.BlockSpec((tm, K), lambda i, j, k: (i, 0)),
        b_spec=pl.BlockSpec((None, None, K, tn), lambda i, j, k: (lidx, j // per, 0, j % per)),
        o_spec=pl.BlockSpec((tm, tn), lambda i, j, k: (i, j)),
        out_shape=jax.ShapeDtypeStruct((M, n_slots * Ns), out_dtype), acc_shape=(tm, tn),
        bias=bias, bias_spec=None if bias is None else pl.BlockSpec((1, tn), lambda i, j, k: (0, j)),
        gather=gather,
    )


def mm_nt(a, w, lidx, n_slots, *, name, res=None, exchange=()):
    M = a.shape[0]
    N, Ns = w.shape[-2], w.shape[-1]
    tm = _tile(M, 1024)
    tn = _tile(N, 1024)
    tk = _tile(Ns, 2048)
    per = Ns // tk
    grid = (M // tm, N // tn, n_slots * per)
    return _matmul(
        a, w, contract="nt", grid=grid, name=name,
        a_spec=pl.BlockSpec((tm, tk), lambda i, j, k: (i, k)),
        b_spec=pl.BlockSpec((None, None, tn, tk), lambda i, j, k: (lidx, k // per, j, k % per)),
        o_spec=pl.BlockSpec((tm, tn), lambda i, j, k: (i, j)),
        out_shape=jax.ShapeDtypeStruct((M, N), F32), acc_shape=(tm, tn),
        res=res, res_spec=None if res is None else pl.BlockSpec((tm, tn), lambda i, j, k: (i, j)),
        alias_res=res is not None, exchange=exchange,
    )


def mm_tn(a, b, n_slots, *, name, exchange=()):
    S, M = a.shape
    Ns = b.shape[1] // n_slots
    tm = _tile(M, 1024)
    tn = _tile(Ns, 1024)
    tk = _tile(S, 2048)
    per = Ns // tn
    grid = (M // tm, n_slots * per, S // tk)
    return _matmul(
        a, b, contract="tn", grid=grid, name=name,
        a_spec=pl.BlockSpec((tk, tm), lambda i, j, k: (k, i)),
        b_spec=pl.BlockSpec((tk, tn), lambda i, j, k: (k, j)),
        o_spec=pl.BlockSpec((None, None, tm, tn), lambda i, j, k: (0, j // per, i, j % per)),
        out_shape=jax.ShapeDtypeStruct((1, n_slots, M, Ns), F32), acc_shape=(tm, tn), exchange=exchange,
    )


def rms_fwd(h, g, name):
    S, D = h.shape
    tm = _tile(S, 256, 8)

    def body(h_ref, g_ref, o_ref):
        x = h_ref[...]
        r = lax.rsqrt(jnp.mean(x * x, axis=-1, keepdims=True) + RMS_EPS)
        o_ref[...] = (x * r * g_ref[...]).astype(BF16)

    return _pcall(
        body, name=name, grid=(S // tm,),
        in_specs=[pl.BlockSpec((tm, D), lambda i: (i, 0)), pl.BlockSpec((1, D), lambda i: (0, 0))],
        out_specs=pl.BlockSpec((tm, D), lambda i: (i, 0)),
        out_shape=jax.ShapeDtypeStruct((S, D), BF16),
        compiler_params=_cparams(("parallel",)),
    )(h, g)


def _rms_bwd_rows(x, g, dy):
    d = x.shape[-1]
    r = lax.rsqrt(jnp.mean(x * x, axis=-1, keepdims=True) + RMS_EPS)
    gd = dy * g
    dx = r * gd - x * ((r * r * r) * (jnp.sum(x * gd, axis=-1, keepdims=True) / d))
    return dx, dy * x * r


def rms_bwd(dhn, h, g, dres, name):
    S, D = h.shape
    tm = _tile(S, 256, 8)

    def body(dhn_ref, h_ref, g_ref, dres_ref, dh_ref, dg_ref):
        dx, dgr = _rms_bwd_rows(h_ref[...], g_ref[...], dhn_ref[...])
        dh_ref[...] = dres_ref[...] + dx

        @pl.when(pl.program_id(0) == 0)
        def _():
            dg_ref[...] = jnp.zeros_like(dg_ref)

        dg_ref[...] += jnp.sum(dgr, axis=0, keepdims=True)

    row = pl.BlockSpec((tm, D), lambda i: (i, 0))
    vec = pl.BlockSpec((1, D), lambda i: (0, 0))
    return _pcall(
        body, name=name, grid=(S // tm,),
        in_specs=[row, row, vec, row], out_specs=[row, vec],
        out_shape=[jax.ShapeDtypeStruct((S, D), F32), jax.ShapeDtypeStruct((1, D), F32)],
        input_output_aliases={3: 0},
        compiler_params=_cparams(("arbitrary",)),
    )(dhn, h, g, dres)


def loss_head(h, g, target, name):
    S, D = h.shape
    tm = _tile(S, 256, 8)

    def body(h_ref, g_ref, t_ref, dh_ref, dg_ref, loss_ref):
        x = h_ref[...]
        gg = g_ref[...]
        r = lax.rsqrt(jnp.mean(x * x, axis=-1, keepdims=True) + RMS_EPS)
        y = x * r * gg
        e = y - t_ref[...]
        part = 0.5 * jnp.sum(jnp.mean(e * e, axis=-1, keepdims=True), axis=0, keepdims=True)
        dy = e * (1.0 / D)
        dx, dgr = _rms_bwd_rows(x, gg, dy)
        dh_ref[...] = dx

        @pl.when(pl.program_id(0) == 0)
        def _():
            dg_ref[...] = jnp.zeros_like(dg_ref)
            loss_ref[...] = jnp.zeros_like(loss_ref)

        dg_ref[...] += jnp.sum(dgr, axis=0, keepdims=True)
        loss_ref[...] += jnp.broadcast_to(part, loss_ref.shape)

    row = pl.BlockSpec((tm, D), lambda i: (i, 0))
    vec = pl.BlockSpec((1, D), lambda i: (0, 0))
    return _pcall(
        body, name=name, grid=(S // tm,),
        in_specs=[row, vec, row],
        out_specs=[row, vec, pl.BlockSpec((1, 128), lambda i: (0, 0))],
        out_shape=[jax.ShapeDtypeStruct((S, D), F32), jax.ShapeDtypeStruct((1, D), F32),
                   jax.ShapeDtypeStruct((1, 128), F32)],
        compiler_params=_cparams(("arbitrary",)),
    )(h, g, target)


def fox_decay(pf, bf, n_heads, scale, name):
    S = pf.shape[0]
    tm = min(_tile(S, 256, 8), _attn_tile(S))
    inv_scale = 1.0 / scale

    def body(pf_ref, bf_ref, qa_ref, ka_ref, c0_ref, c1_ref, carry_ref):
        @pl.when(pl.program_id(0) == 0)
        def _():
            carry_ref[...] = jnp.zeros_like(carry_ref)

        z = pf_ref[...] + bf_ref[...]
        logf = jnp.minimum(z, 0.0) - jnp.log(1.0 + jnp.exp(-jnp.abs(z)))
        row = lax.broadcasted_iota(jnp.int32, (tm, tm), 0)
        col = lax.broadcasted_iota(jnp.int32, (tm, tm), 1)
        tri = (row >= col).astype(F32)
        c = jnp.dot(tri, logf, precision=lax.Precision.HIGHEST, preferred_element_type=F32) + carry_ref[...]
        carry_ref[...] = c[tm - 1:tm, :]
        c0_ref[...] = c[0:1, :]
        c1_ref[...] = c[tm - 1:tm, :]
        lane = lax.broadcasted_iota(jnp.int32, (tm, HEAD_DIM), 1)
        for hh in range(n_heads):
            hi, mid, lo = _split3(c[:, hh:hh + 1] * inv_scale)
            qa = jnp.where(lane == 0, hi, jnp.where(lane == 1, mid, jnp.where(lane == 2, lo,
                 jnp.where(lane < 6, 1.0, 0.0))))
            ka = jnp.where(lane < 3, 1.0, jnp.where(lane == 3, -hi, jnp.where(lane == 4, -mid,
                 jnp.where(lane == 5, -lo, jnp.where(lane < 9, 1.0, 0.0)))))
            qa_ref[:, hh * HEAD_DIM:(hh + 1) * HEAD_DIM] = qa.astype(BF16)
            ka_ref[:, hh * HEAD_DIM:(hh + 1) * HEAD_DIM] = ka.astype(BF16)

    wide = pl.BlockSpec((tm, n_heads * HEAD_DIM), lambda i: (i, 0))
    edge = pl.BlockSpec((None, 1, 128), lambda i: (i, 0, 0))
    return _pcall(
        body, name=name, grid=(S // tm,),
        in_specs=[pl.BlockSpec((tm, 128), lambda i: (i, 0)), pl.BlockSpec((1, 128), lambda i: (0, 0))],
        out_specs=[wide, wide, edge, edge],
        out_shape=[jax.ShapeDtypeStruct((S, n_heads * HEAD_DIM), BF16)] * 2
                  + [jax.ShapeDtypeStruct((S // tm, 1, 128), F32)] * 2,
        scratch_shapes=[pltpu.VMEM((1, 128), F32)],
        compiler_params=_cparams(("arbitrary",)),
    )(pf, bf)


def attn_norms(p1, p2, n_heads, name):
    S = p1.shape[0]
    W = n_heads * HEAD_DIM
    t = _attn_tile(S)

    def body(q_ref, kv_ref, qn_ref, kn_ref):
        lane = lax.broadcasted_iota(jnp.int32, (1, 128), 1)
        qn = jnp.zeros((1, 128), F32)
        kn = jnp.zeros((1, 128), F32)
        for hh in range(n_heads):
            q = q_ref[:, hh * HEAD_DIM:(hh + 1) * HEAD_DIM].astype(F32)
            k = kv_ref[:, 2 * hh * HEAD_DIM:(2 * hh + 1) * HEAD_DIM].astype(F32)
            q2 = jnp.max(jnp.sum(q * q, axis=1, keepdims=True), axis=0, keepdims=True)
            k2 = jnp.max(jnp.sum(k * k, axis=1, keepdims=True), axis=0, keepdims=True)
            qn = jnp.where(lane == hh, q2, qn)
            kn = jnp.where(lane == hh, k2, kn)
        qn_ref[...] = qn
        kn_ref[...] = kn

    edge = pl.BlockSpec((None, 1, 128), lambda i: (i, 0, 0))
    return _pcall(
        body, name=name, grid=(S // t,),
        in_specs=[pl.BlockSpec((t, W), lambda i: (i, 0)), pl.BlockSpec((t, 2 * W), lambda i: (i, 0))],
        out_specs=[edge, edge], out_shape=[jax.ShapeDtypeStruct((S // t, 1, 128), F32)] * 2,
        compiler_params=_cparams(("parallel",)),
    )(p1, p2)


PRUNE_BELOW = -110.0


def prune_tables(qn, kn, c0, c1, n_heads, scale):
    nb = qn.shape[0]
    r = c0.shape[0] // nb
    qmax = jnp.sqrt(qn[:, 0, :n_heads])
    kmax = jnp.sqrt(kn[:, 0, :n_heads])
    cfirst = c0[::r, 0, :n_heads]
    clast = c1[r - 1::r, 0, :n_heads]
    bound = (scale * qmax[:, None, :] * (kmax[None, :, :] + kmax[:, None, :])
             + cfirst[:, None, :] - clast[None, :, :])
    ii = lax.broadcasted_iota(jnp.int32, (nb, nb, 1), 0)
    jj = lax.broadcasted_iota(jnp.int32, (nb, nb, 1), 1)
    skip = ((bound < PRUNE_BELOW) & (jj < ii)).astype(jnp.int32)
    first_key = jnp.sum(jnp.cumprod(skip, axis=1), axis=1)
    tail = jnp.sum(jnp.cumprod(skip[::-1], axis=0), axis=0)
    return first_key.T.reshape((-1,)), (nb - tail).T.reshape((-1,))


def _attn_tile(S):
    return 512 if S % 512 == 0 and S >= 2048 else 128


def attn_fwd(p1, p2, qa, ka, n_heads, scale, name, gather=(), first_key=None):
    S = p1.shape[0]
    W = n_heads * HEAD_DIM
    t = _attn_tile(S)
    nq = S // t
    ng = len(gather)
    assert not (ng and first_key is not None)

    def body(*refs):
        if first_key is None:
            core(0, *refs)
        else:
            fk = refs[0][pl.program_id(0) * nq + pl.program_id(1)]
            core(jnp.minimum(fk, pl.program_id(1)), *refs[1:])

    def core(js, q_ref, g_ref, qa_ref, k_ref, v_ref, ka_ref, *rest):
        o_ref, y_ref, qb_ref = rest[ng:ng + 3]
        bufs = rest[ng + 3:2 * ng + 3]
        mp_ref, qq_ref, acc_ref = rest[2 * ng + 3:2 * ng + 6]
        i = pl.program_id(1)
        if ng:
            ssem, rsem = rest[2 * ng + 6:]

            @pl.when((pl.program_id(0) == 0) & (i == 0))
            def _():
                for cp in _gather_direct_copies(bufs, ssem, rsem)[0]:
                    cp.start()
        lane = lax.broadcasted_iota(jnp.int32, (t, HEAD_DIM), 1)
        qa = qa_ref[...].astype(F32)

        def tile_with(neg_stat):
            hi, mid, lo = _split3(neg_stat)
            return jnp.where(lane == 6, hi, jnp.where(lane == 7, mid, jnp.where(lane == 8, lo, qa))).astype(BF16)

        def keys(start, width):
            rows = pl.ds(pl.multiple_of(start, t), width)
            return rows, jnp.concatenate([k_ref[rows, :], ka_ref[rows, :]], axis=1)

        def causal():
            r = lax.broadcasted_iota(jnp.int32, (t, t), 0)
            c = lax.broadcasted_iota(jnp.int32, (t, t), 1)
            return r >= c

        def over_keys(block):
            n = i - js

            def wide(jj, carry):
                block((js + 4 * jj) * t, 4 * t, False)
                return carry

            lax.fori_loop(0, n // 4, wide, 0)
            done = js + (n // 4) * 4

            @pl.when((n & 2) != 0)
            def _():
                block(done * t, 2 * t, False)

            @pl.when((n & 1) != 0)
            def _():
                block((done + (n & 2)) * t, t, False)

            block(i * t, t, True)

        qq_ref[:, :HEAD_DIM] = q_ref[...]
        qq_ref[:, HEAD_DIM:] = qa_ref[...]
        mp_ref[...] = jnp.full(mp_ref.shape, NEG_BIG, F32)

        def max_block(start, width, masked):
            _, kk = keys(start, width)
            s = lax.dot_general(qq_ref[...], kk, _DN["nt"], preferred_element_type=F32)
            if masked:
                s = jnp.where(causal(), s, NEG_BIG)
            part = s[:, 0:HEAD_DIM]
            for a in range(1, width // HEAD_DIM):
                part = jnp.maximum(part, s[:, a * HEAD_DIM:(a + 1) * HEAD_DIM])
            mp_ref[...] = jnp.maximum(mp_ref[...], part)

        over_keys(max_block)
        m = jnp.max(mp_ref[...], axis=1, keepdims=True)
        qq_ref[:, HEAD_DIM:] = tile_with(-m)
        acc_ref[...] = jnp.zeros_like(acc_ref)

        def sum_block(start, width, masked):
            rows, kk = keys(start, width)
            a = lax.dot_general(qq_ref[...], kk, _DN["nt"], preferred_element_type=F32)
            p = jnp.exp(scale * a)
            if masked:
                p = jnp.where(causal(), p, 0.0)
            ones0 = jnp.where(lax.broadcasted_iota(jnp.int32, (width, HEAD_DIM), 1) == 0, 1.0, 0.0).astype(BF16)
            vv = jnp.concatenate([v_ref[rows, :], ones0], axis=1)
            acc_ref[...] += jnp.dot(p.astype(BF16), vv, preferred_element_type=F32)

        over_keys(sum_block)

        l = acc_ref[:, HEAD_DIM:HEAD_DIM + 1]
        o = acc_ref[:, :HEAD_DIM] / l
        gate = g_ref[...].astype(F32)
        o_ref[...] = o.astype(BF16)
        y_ref[...] = (o * (gate * _sigmoid(gate))).astype(BF16)
        qb_ref[...] = tile_with(-(m + jnp.log(l) * (1.0 / scale)))

        if ng:
            @pl.when((pl.program_id(0) == n_heads - 1) & (i == nq - 1))
            def _():
                sends, lands = _gather_direct_copies(bufs, ssem, rsem)
                for cp in lands:
                    cp.wait_recv()
                for cp in sends:
                    cp.wait_send()

    H = n_heads
    qtile = lambda off: pl.BlockSpec((t, HEAD_DIM), lambda h, i, *_: (i, off + h))
    full = lambda fn: pl.BlockSpec((S, HEAD_DIM), fn)
    hbm = pl.BlockSpec(memory_space=pl.ANY)
    sems = [pltpu.SemaphoreType.DMA((3 * ng,)), pltpu.SemaphoreType.DMA((3 * ng,))] if ng else []
    in_specs = [qtile(0), qtile(H), qtile(0), full(lambda h, i, *_: (0, 2 * h)), full(lambda h, i, *_: (0, 2 * h + 1)),
                full(lambda h, i, *_: (0, h))] + [hbm] * ng
    out_specs = [qtile(0), qtile(0), qtile(0)] + [hbm] * ng
    scratch = [pltpu.VMEM((t, HEAD_DIM), F32), pltpu.VMEM((t, 2 * HEAD_DIM), BF16),
               pltpu.VMEM((t, 2 * HEAD_DIM), F32)] + sems
    out_shape = [jax.ShapeDtypeStruct((S, W), BF16)] * 3 + [jax.ShapeDtypeStruct(b.shape, b.dtype) for b in gather]
    sem = _cparams(("arbitrary", "arbitrary") if ng else ("parallel", "arbitrary"))
    if first_key is None:
        outs = _pcall(
            body, name=name, grid=(H, nq), in_specs=in_specs, out_specs=out_specs, out_shape=out_shape,
            scratch_shapes=scratch, input_output_aliases={6 + n: 3 + n for n in range(ng)}, compiler_params=sem,
        )(p1, p1, qa, p2, p2, ka, *gather)
    else:
        grid_spec = pltpu.PrefetchScalarGridSpec(num_scalar_prefetch=1, grid=(H, nq), in_specs=in_specs,
                                                 out_specs=out_specs, scratch_shapes=scratch)
        outs = _pcall(body, name=name, grid_spec=grid_spec, out_shape=out_shape, compiler_params=sem,
                      )(first_key, p1, p1, qa, p2, p2, ka)
    return outs[0], outs[1], outs[2], list(outs[3:])


def attn_bwd_prep(dy, o, p1, n_heads, name):
    S, W = dy.shape
    tm = _tile(S, 256, 8)
    H = n_heads

    def body(dy_ref, o_ref, g_ref, do_ref, dg_ref, da_ref):
        lane = lax.broadcasted_iota(jnp.int32, (tm, HEAD_DIM), 1)
        for hh in range(H):
            cs = slice(hh * HEAD_DIM, (hh + 1) * HEAD_DIM)
            g = g_ref[:, cs].astype(F32)
            oo = o_ref[:, cs].astype(F32)
            dyv = dy_ref[:, cs]
            sg = _sigmoid(g)
            do = dyv * (g * sg)
            do_ref[:, cs] = do.astype(BF16)
            dg_ref[:, cs] = (dyv * oo * (sg * (1.0 + g * (1.0 - sg)))).astype(BF16)
            hi, mid, lo = _split3(-jnp.sum(do * oo, axis=1, keepdims=True))
            da = jnp.where(lane == 0, hi, jnp.where(lane == 1, mid, jnp.where(lane == 2, lo, 0.0)))
            da_ref[:, cs] = da.astype(BF16)

    row = lambda blk: pl.BlockSpec((tm, W), lambda i: (i, blk))
    return _pcall(
        body, name=name, grid=(S // tm,),
        in_specs=[row(0), row(0), row(1)],
        out_specs=[row(0), row(1), row(0)],
        out_shape=[jax.ShapeDtypeStruct((S, W), BF16), jax.ShapeDtypeStruct((S, 2 * W), BF16),
                   jax.ShapeDtypeStruct((S, W), BF16)],
        compiler_params=_cparams(("parallel",)),
    )(dy, o, p1)


def attn_bwd(p1, qb, do, da, p2, ka, dp1, n_heads, scale, name, exchange=(), query_end=None):
    S = p1.shape[0]
    W = n_heads * HEAD_DIM
    t = _attn_tile(S)
    nb = S // t
    H = n_heads
    ne = len(exchange)
    assert not (ne and query_end is not None)

    def body(*refs):
        if query_end is None:
            core(nb, *refs)
        else:
            qe = refs[0][pl.program_id(0) * nb + pl.program_id(1)]
            core(jnp.clip(qe, pl.program_id(1) + 1, nb), *refs[1:])

    def core(iend, q_ref, qb_ref, do_ref, da_ref, k_ref, v_ref, ka_ref, dp1_in, *rest):
        del dp1_in
        parts = rest[:ne]
        dq_ref, dkv_ref, dc_ref = rest[ne:ne + 3]
        got = rest[ne + 3:2 * ne + 3]
        dq_acc, dk_acc, dv_acc = rest[2 * ne + 3:2 * ne + 6]
        h = pl.program_id(0)
        j = pl.program_id(1)
        if ne:
            ssem, rsem = rest[2 * ne + 6:]

            @pl.when((h == 0) & (j == 0))
            def _():
                for cp in _exchange_copies(parts, got, ssem, rsem)[0]:
                    cp.start()

        @pl.when(j == 0)
        def _():
            dq_acc[...] = jnp.zeros_like(dq_acc)

        @pl.when((j == 0) & (h == 0))
        def _():
            dc_ref[...] = jnp.zeros_like(dc_ref)

        lane = lax.broadcasted_iota(jnp.int32, (t, HEAD_DIM), 1)
        ones3 = jnp.where(lane < 3, 1.0, 0.0).astype(BF16)
        kk = jnp.concatenate([k_ref[...], ka_ref[...]], axis=1)
        vv = jnp.concatenate([v_ref[...], ones3], axis=1)
        dk_acc[...] = jnp.zeros_like(dk_acc)
        dv_acc[...] = jnp.zeros_like(dv_acc)

        def block(start, width, masked):
            rows = pl.ds(pl.multiple_of(start, t), width)
            qq = jnp.concatenate([q_ref[rows, :], qb_ref[rows, :]], axis=1)
            dd = jnp.concatenate([do_ref[rows, :], da_ref[rows, :]], axis=1)
            a = lax.dot_general(qq, kk, _DN["nt"], preferred_element_type=F32)
            p = jnp.exp(scale * a)
            if masked:
                r = lax.broadcasted_iota(jnp.int32, (t, t), 0)
                c = lax.broadcasted_iota(jnp.int32, (t, t), 1)
                p = jnp.where(r >= c, p, 0.0)
            dpd = lax.dot_general(dd, vv, _DN["nt"], preferred_element_type=F32)
            ds = (p * dpd).astype(BF16)
            pb = p.astype(BF16)
            dv_acc[...] += lax.dot_general(pb, dd, _DN["tn"], preferred_element_type=F32)
            dk_acc[...] += lax.dot_general(ds, qq, _DN["tn"], preferred_element_type=F32)
            dq_acc[rows, :] += jnp.dot(ds, kk, preferred_element_type=F32)

        block(j * t, t, True)
        n_after = iend - 1 - j

        @pl.when((n_after & 1) != 0)
        def _():
            block((j + 1) * t, t, False)

        first = j + 1 + (n_after & 1)

        def loop_body(ii, carry):
            block((first + 2 * ii) * t, 2 * t, False)
            return carry

        lax.fori_loop(0, n_after // 2, loop_body, 0)

        dkv_ref[...] = jnp.concatenate([dk_acc[:, :HEAD_DIM] * scale, dv_acc[:, :HEAD_DIM]], axis=1).astype(BF16)
        colsum = dk_acc[:, HEAD_DIM + 3:HEAD_DIM + 4]
        krows = pl.ds(pl.multiple_of(j * t, t), t)
        dc_ref[krows, :] += jnp.where(lane == h, -colsum, 0.0)

        @pl.when(j == nb - 1)
        def _():
            dq_ref[...] = (dq_acc[:, :HEAD_DIM] * scale).astype(BF16)
            lane_s = lax.broadcasted_iota(jnp.int32, (S, HEAD_DIM), 1)
            dc_ref[...] += jnp.where(lane_s == h, dq_acc[:, HEAD_DIM:HEAD_DIM + 1], 0.0)

        if ne:
            @pl.when((h == H - 1) & (j == nb - 1))
            def _():
                sends, lands = _exchange_copies(parts, got, ssem, rsem)
                for cp in lands:
                    cp.wait_recv()
                for cp in sends:
                    cp.wait_send()

    full = lambda fn: pl.BlockSpec((S, HEAD_DIM), fn)
    ktile = lambda fn: pl.BlockSpec((t, HEAD_DIM), fn)
    hbm = pl.BlockSpec(memory_space=pl.ANY)
    sems = [pltpu.SemaphoreType.DMA((3 * ne,)), pltpu.SemaphoreType.DMA((3 * ne,))] if ne else []
    in_specs = [full(lambda h, j, *_: (0, h)), full(lambda h, j, *_: (0, h)), full(lambda h, j, *_: (0, h)),
                full(lambda h, j, *_: (0, h)),
                ktile(lambda h, j, *_: (j, 2 * h)), ktile(lambda h, j, *_: (j, 2 * h + 1)),
                ktile(lambda h, j, *_: (j, h)), hbm] + [hbm] * ne
    out_specs = [full(lambda h, j, *_: (0, h)),
                 pl.BlockSpec((t, 2 * HEAD_DIM), lambda h, j, *_: (j, h)),
                 pl.BlockSpec((S, 128), lambda h, j, *_: (0, 0))] + [hbm] * ne
    out_shape = ([jax.ShapeDtypeStruct((S, 2 * W), BF16), jax.ShapeDtypeStruct((S, 2 * W), BF16),
                  jax.ShapeDtypeStruct((S, 128), F32)]
                 + [jax.ShapeDtypeStruct((p.shape[0], N_CHIPS - 1) + p.shape[2:], p.dtype) for p in exchange])
    scratch = [pltpu.VMEM((S, 2 * HEAD_DIM), F32), pltpu.VMEM((t, 2 * HEAD_DIM), F32),
               pltpu.VMEM((t, 2 * HEAD_DIM), F32)] + sems
    if query_end is None:
        outs = _pcall(
            body, name=name, grid=(H, nb), in_specs=in_specs, out_specs=out_specs, out_shape=out_shape,
            scratch_shapes=scratch, input_output_aliases={7: 0}, compiler_params=_cparams(("arbitrary", "arbitrary")),
        )(p1, qb, do, da, p2, p2, ka, dp1, *exchange)
    else:
        grid_spec = pltpu.PrefetchScalarGridSpec(num_scalar_prefetch=1, grid=(H, nb), in_specs=in_specs,
                                                 out_specs=out_specs, scratch_shapes=scratch)
        outs = _pcall(body, name=name, grid_spec=grid_spec, out_shape=out_shape, input_output_aliases={8: 0},
                      compiler_params=_cparams(("arbitrary", "arbitrary")),
                      )(query_end, p1, qb, do, da, p2, p2, ka, dp1)
    return outs[0], outs[1], outs[2], list(outs[3:])


def fox_decay_bwd(dc, pf, bf, n_heads, name):
    S = dc.shape[0]
    tm = _tile(S, 256, 8)
    nb = S // tm

    def body(dc_ref, pf_ref, bf_ref, dz_ref, db_ref, carry_ref):
        @pl.when(pl.program_id(0) == 0)
        def _():
            carry_ref[...] = jnp.zeros_like(carry_ref)
            db_ref[...] = jnp.zeros_like(db_ref)

        row = lax.broadcasted_iota(jnp.int32, (tm, tm), 0)
        col = lax.broadcasted_iota(jnp.int32, (tm, tm), 1)
        tri = (row <= col).astype(F32)
        dlogf = jnp.dot(tri, dc_ref[...], precision=lax.Precision.HIGHEST, preferred_element_type=F32) + carry_ref[...]
        carry_ref[...] = dlogf[0:1, :]
        z = pf_ref[...] + bf_ref[...]
        lane = lax.broadcasted_iota(jnp.int32, (tm, 128), 1)
        dz = jnp.where(lane < n_heads, dlogf * _sigmoid(-z), 0.0)
        dz_ref[...] = dz.astype(BF16)
        db_ref[...] += jnp.sum(dz, axis=0, keepdims=True)

    rev = pl.BlockSpec((tm, 128), lambda i: (nb - 1 - i, 0))
    vec = pl.BlockSpec((1, 128), lambda i: (0, 0))
    return _pcall(
        body, name=name, grid=(nb,),
        in_specs=[rev, rev, vec], out_specs=[rev, vec],
        out_shape=[jax.ShapeDtypeStruct((S, 128), BF16), jax.ShapeDtypeStruct((1, 128), F32)],
        scratch_shapes=[pltpu.VMEM((1, 128), F32)],
        compiler_params=_cparams(("arbitrary",)),
    )(dc, pf, bf)


def _conv_tile(S):
    return _tile(S, 256, HALO)


def _fill_glu(ubuf, a_ref, b_ref, ah_ref, bh_ref, first, tm):
    uh = ah_ref[...].astype(F32) * _sigmoid(bh_ref[...].astype(F32))
    ubuf[0:HALO, :] = jnp.where(first, 0.0, uh)
    ubuf[HALO:HALO + tm, :] = a_ref[...].astype(F32) * _sigmoid(b_ref[...].astype(F32))


def conv_fwd(proj, dw, dwb, lng, lnb, name, gather=()):
    S = proj.shape[0]
    C = proj.shape[1] // 3
    tm = _conv_tile(S)
    hb = tm // HALO
    nch = C // 128
    rb = _tile(tm, 128, 8)
    ng = len(gather)
    rows = [(r0, rn) for _, r0, rn in gather]

    def body(a_ref, b_ref, g_ref, ah_ref, bh_ref, dw_ref, dwb_ref, lng_ref, lnb_ref, *rest):
        u2_ref, y_ref = rest[ng:ng + 2]
        bufs = rest[ng + 2:2 * ng + 2]
        ubuf, sh = rest[2 * ng + 2:2 * ng + 4]
        i = pl.program_id(0)
        if ng:
            ssem, rsem = rest[2 * ng + 4:]

            @pl.when(i == 0)
            def _():
                for cp in _gather_direct_copies(bufs, ssem, rsem, rows)[0]:
                    cp.start()
        _fill_glu(ubuf, a_ref, b_ref, ah_ref, bh_ref, i == 0, tm)

        def chunk(cc, carry):
            cols = pl.ds(pl.multiple_of(cc * 128, 128), 128)
            for r0 in range(0, tm, rb):
                acc = jnp.broadcast_to(dwb_ref[:, cols], (rb, 128))
                for b in range(8):
                    taps = list(range(b, CONV_K, 8))
                    n = rb + 8 * (len(taps) - 1)
                    sh[0:n, :] = ubuf[pl.ds(HALO - (CONV_K - 1) + b + r0, n), cols]
                    for a, k in enumerate(taps):
                        acc = acc + dw_ref[k:k + 1, cols] * sh[8 * a:8 * a + rb, :]
                u2_ref[pl.ds(r0, rb), cols] = acc
            return carry

        lax.fori_loop(0, nch, chunk, 0)
        x = u2_ref[...]
        mu = jnp.mean(x, axis=-1, keepdims=True)
        xc = x - mu
        var = jnp.mean(xc * xc, axis=-1, keepdims=True)
        ln = xc * lax.rsqrt(var + LN_EPS) * lng_ref[...] + lnb_ref[...]
        gate = g_ref[...].astype(F32)
        y_ref[...] = ((ln * _sigmoid(ln)) * (gate * _sigmoid(gate))).astype(BF16)

        if ng:
            @pl.when(i == S // tm - 1)
            def _():
                sends, lands = _gather_direct_copies(bufs, ssem, rsem, rows)
                for cp in lands:
                    cp.wait_recv()
                for cp in sends:
                    cp.wait_send()

    blk = lambda cb: pl.BlockSpec((tm, C), lambda i: (i, cb))
    halo = lambda cb: pl.BlockSpec((HALO, C), lambda i: (jnp.maximum(i * hb - 1, 0), cb))
    vec = pl.BlockSpec((1, C), lambda i: (0, 0))
    hbm = pl.BlockSpec(memory_space=pl.ANY)
    sems = [pltpu.SemaphoreType.DMA((3 * ng,)), pltpu.SemaphoreType.DMA((3 * ng,))] if ng else []
    outs = _pcall(
        body, name=name, grid=(S // tm,),
        in_specs=[blk(0), blk(1), blk(2), halo(0), halo(1),
                  pl.BlockSpec((CONV_K, C), lambda i: (0, 0)), vec, vec, vec] + [hbm] * ng,
        out_specs=[blk(0), blk(0)] + [hbm] * ng,
        out_shape=[jax.ShapeDtypeStruct((S, C), F32), jax.ShapeDtypeStruct((S, C), BF16)]
                  + [jax.ShapeDtypeStruct(g.shape, g.dtype) for g, _, _ in gather],
        scratch_shapes=[pltpu.VMEM((HALO + tm, C), F32), pltpu.VMEM((rb + HALO, 128), F32)] + sems,
        input_output_aliases={9 + n: 2 + n for n in range(ng)},
        compiler_params=_cparams(("arbitrary",) if ng else ("parallel",)),
    )(proj, proj, proj, proj, proj, dw, dwb, lng, lnb, *[g for g, _, _ in gather])
    return outs[0], outs[1], list(outs[2:])


def conv_bwd_norm(dy, proj, u2, lng, lnb, name):
    S, C = dy.shape
    tm = _tile(S, 256, 8)

    def body(dy_ref, g_ref, u2_ref, lng_ref, lnb_ref, du2_ref, dg_ref, sm_ref):
        x = u2_ref[...]
        mu = jnp.mean(x, axis=-1, keepdims=True)
        xc = x - mu
        var = jnp.mean(xc * xc, axis=-1, keepdims=True)
        rs = lax.rsqrt(var + LN_EPS)
        xhat = xc * rs
        gam = lng_ref[...]
        ln = xhat * gam + lnb_ref[...]
        sl = _sigmoid(ln)
        u3 = ln * sl
        gate = g_ref[...].astype(F32)
        sg = _sigmoid(gate)
        dyv = dy_ref[...]
        dgate = dyv * u3 * (sg * (1.0 + gate * (1.0 - sg)))
        dln = (dyv * (gate * sg)) * (sl * (1.0 + ln * (1.0 - sl)))
        dxh = dln * gam
        du2 = rs * (dxh - jnp.mean(dxh, axis=-1, keepdims=True)
                    - xhat * jnp.mean(dxh * xhat, axis=-1, keepdims=True))
        du2_ref[...] = du2
        dg_ref[...] = dgate.astype(BF16)

        @pl.when(pl.program_id(0) == 0)
        def _():
            sm_ref[...] = jnp.zeros_like(sm_ref)

        sm_ref[0:1, :] += jnp.sum(dln * xhat, axis=0, keepdims=True)
        sm_ref[1:2, :] += jnp.sum(dln, axis=0, keepdims=True)
        sm_ref[2:3, :] += jnp.sum(du2, axis=0, keepdims=True)
        sm_ref[3:4, :] += jnp.sum(dgate, axis=0, keepdims=True)

    blk = lambda cb: pl.BlockSpec((tm, C), lambda i: (i, cb))
    vec = pl.BlockSpec((1, C), lambda i: (0, 0))
    return _pcall(
        body, name=name, grid=(S // tm,),
        in_specs=[blk(0), blk(2), blk(0), vec, vec],
        out_specs=[blk(0), blk(2), pl.BlockSpec((8, C), lambda i: (0, 0))],
        out_shape=[jax.ShapeDtypeStruct((S, C), F32), jax.ShapeDtypeStruct((S, 3 * C), BF16),
                   jax.ShapeDtypeStruct((8, C), F32)],
        compiler_params=_cparams(("arbitrary",)),
    )(dy, proj, u2, lng, lnb)


def conv_bwd_taps(du2, proj, dw, dproj, name, exchange=()):
    S, C = du2.shape
    tm = _conv_tile(S)
    hb = tm // HALO
    nb = S // tm
    nch = C // 128
    rb = _tile(tm, 128, 8)
    ne = len(exchange)

    def body(d_ref, dh_ref, a_ref, b_ref, ah_ref, bh_ref, dw_ref, dp_in, *rest):
        del dp_in
        parts = rest[:ne]
        dab_ref, sm_ref = rest[ne:ne + 2]
        got = rest[ne + 2:2 * ne + 2]
        ubuf, dbuf, sh, sh2 = rest[2 * ne + 2:2 * ne + 6]
        i = pl.program_id(0)
        if ne:
            ssem, rsem = rest[2 * ne + 6:]

            @pl.when(i == 0)
            def _():
                for cp in _exchange_copies(parts, got, ssem, rsem)[0]:
                    cp.start()
        _fill_glu(ubuf, a_ref, b_ref, ah_ref, bh_ref, i == 0, tm)
        dbuf[0:tm, :] = d_ref[...]
        dbuf[tm:tm + HALO, :] = jnp.where(i == nb - 1, 0.0, dh_ref[...])

        @pl.when(i == 0)
        def _():
            sm_ref[...] = jnp.zeros_like(sm_ref)

        def chunk(cc, carry):
            cols = pl.ds(pl.multiple_of(cc * 128, 128), 128)
            cols_b = pl.ds(pl.multiple_of(C + cc * 128, 128), 128)
            for r0 in range(0, tm, rb):
                d0 = dbuf[r0:r0 + rb, cols]
                du = jnp.zeros((rb, 128), F32)
                for b in range(8):
                    offs = list(range(b, CONV_K, 8))
                    n = rb + 8 * (len(offs) - 1)
                    sh[0:n, :] = dbuf[pl.ds(r0 + b, n), cols]
                    for a, o in enumerate(offs):
                        k = CONV_K - 1 - o
                        du = du + dw_ref[k:k + 1, cols] * sh[8 * a:8 * a + rb, :]
                    sh2[0:n, :] = ubuf[pl.ds(HALO - (CONV_K - 1) + b + r0, n), cols]
                    for a, k in enumerate(offs):
                        sm_ref[k:k + 1, cols] += jnp.sum(d0 * sh2[8 * a:8 * a + rb, :], axis=0, keepdims=True)
                rows = pl.ds(r0, rb)
                av = a_ref[rows, cols].astype(F32)
                sb = _sigmoid(b_ref[rows, cols].astype(F32))
                da = du * sb
                db = du * av * sb * (1.0 - sb)
                dab_ref[rows, cols] = da.astype(BF16)
                dab_ref[rows, cols_b] = db.astype(BF16)
                sm_ref[32:33, cols] += jnp.sum(da, axis=0, keepdims=True)
                sm_ref[33:34, cols] += jnp.sum(db, axis=0, keepdims=True)
            return carry

        lax.fori_loop(0, nch, chunk, 0)

        if ne:
            @pl.when(i == nb - 1)
            def _():
                sends, lands = _exchange_copies(parts, got, ssem, rsem)
                for cp in lands:
                    cp.wait_recv()
                for cp in sends:
                    cp.wait_send()

    blk = lambda cb: pl.BlockSpec((tm, C), lambda i: (i, cb))
    halo = lambda cb: pl.BlockSpec((HALO, C), lambda i: (jnp.maximum(i * hb - 1, 0), cb))
    nxt = pl.BlockSpec((HALO, C), lambda i: (jnp.minimum((i + 1) * hb, nb * hb - 1), 0))
    hbm = pl.BlockSpec(memory_space=pl.ANY)
    sems = [pltpu.SemaphoreType.DMA((3 * ne,)), pltpu.SemaphoreType.DMA((3 * ne,))] if ne else []
    outs = _pcall(
        body, name=name, grid=(nb,),
        in_specs=[blk(0), nxt, blk(0), blk(1), halo(0), halo(1),
                  pl.BlockSpec((CONV_K, C), lambda i: (0, 0)), hbm] + [hbm] * ne,
        out_specs=[pl.BlockSpec((tm, 2 * C), lambda i: (i, 0)), pl.BlockSpec((40, C), lambda i: (0, 0))] + [hbm] * ne,
        out_shape=[jax.ShapeDtypeStruct((S, 3 * C), BF16), jax.ShapeDtypeStruct((40, C), F32)]
                  + [jax.ShapeDtypeStruct((p.shape[0], N_CHIPS - 1) + p.shape[2:], p.dtype) for p in exchange],
        scratch_shapes=[pltpu.VMEM((HALO + tm, C), F32), pltpu.VMEM((tm + HALO, C), F32),
                        pltpu.VMEM((rb + HALO, 128), F32), pltpu.VMEM((rb + HALO, 128), F32)] + sems,
        input_output_aliases={7: 0},
        compiler_params=_cparams(("arbitrary",)),
    )(du2, du2, proj, proj, proj, proj, dw, dproj, *exchange)
    return outs[0], outs[1], list(outs[2:])


def _rows_tile(R, Cc, budget=1 << 18):
    cap = max(8, budget // max(Cc, 1))
    if R <= cap:
        return R
    t = (cap // 8) * 8
    while t >= 8:
        if R % t == 0:
            return t
        t -= 8
    return R


def cast_into_slot(w, chip, name, l0, nl):
    _, R, Cc = w.shape
    tr = _rows_tile(R, Cc)

    def body(s_ref, w_ref, o_ref):
        del s_ref
        o_ref[...] = w_ref[...].astype(BF16)

    grid_spec = pltpu.PrefetchScalarGridSpec(
        num_scalar_prefetch=1, grid=(nl, R // tr),
        in_specs=[pl.BlockSpec((None, tr, Cc), lambda l, r, s: (l0 + l, r, 0))],
        out_specs=pl.BlockSpec((None, None, tr, Cc), lambda l, r, s: (l, s[0], r, 0)),
    )
    return _pcall(
        body, name=name, grid_spec=grid_spec, out_shape=jax.ShapeDtypeStruct((nl, N_CHIPS, R, Cc), BF16),
        compiler_params=_cparams(("parallel", "parallel")),
    )(chip, w)


def pair_sum(g, rcv, cidx, name):
    L, K, _, half, Cc = g.shape
    g5 = g
    tr = _rows_tile(half, Cc)

    def body(c_ref, g_ref, r_ref, o_ref):
        del c_ref
        o_ref[...] = (g_ref[...] + r_ref[...]).astype(BF16)

    grid_spec = pltpu.PrefetchScalarGridSpec(
        num_scalar_prefetch=1, grid=(L, K, half // tr),
        in_specs=[pl.BlockSpec((None, None, None, tr, Cc), lambda l, k, r, c: (l, k, c[0], r, 0)),
                  pl.BlockSpec((None, None, tr, Cc), lambda l, k, r, c: (l, k, r, 0))],
        out_specs=pl.BlockSpec((None, None, tr, Cc), lambda l, k, r, c: (l, k, r, 0)),
    )
    return _pcall(
        body, name=name, grid_spec=grid_spec, out_shape=jax.ShapeDtypeStruct((L, K, half, Cc), BF16),
        compiler_params=_cparams(("parallel", "parallel", "parallel")),
    )(cidx, g5, rcv)


def chip_sum(parts, got, sel, name, layers, l0, prev=None):
    Lp, _, R, Cc = parts.shape
    n_got = got.shape[1]
    tr = _rows_tile(R, Cc)

    def body(s_ref, p_ref, g_ref, *rest):
        del s_ref
        o_ref = rest[-1]
        acc = p_ref[...].astype(F32)
        for k in range(n_got):
            acc = acc + g_ref[k].astype(F32)
        o_ref[...] = acc

    in_specs = [pl.BlockSpec((None, None, tr, Cc), lambda l, r, s: (l, s[0], r, 0)),
                pl.BlockSpec((None, n_got, tr, Cc), lambda l, r, s: (l, 0, r, 0))]
    ops = [sel, parts, got]
    aliases = {}
    if prev is not None:
        in_specs.append(pl.BlockSpec(memory_space=pl.ANY))
        ops.append(prev)
        aliases = {3: 0}
    grid_spec = pltpu.PrefetchScalarGridSpec(
        num_scalar_prefetch=1, grid=(Lp, R // tr), in_specs=in_specs,
        out_specs=pl.BlockSpec((None, None, tr, Cc), lambda l, r, s: (l0 + l, s[1], r, 0)),
    )
    return _pcall(
        body, name=name, grid_spec=grid_spec, out_shape=jax.ShapeDtypeStruct((layers, 2, R, Cc), F32),
        input_output_aliases=aliases, compiler_params=_cparams(("parallel", "parallel")),
    )(*ops)


def dev_sum(parts, name):
    K, R, Cc = parts.shape

    def body(p_ref, o_ref):
        acc = p_ref[0]
        for k in range(1, K):
            acc = acc + p_ref[k]
        o_ref[...] = acc

    return _pcall(
        body, name=name, grid=(R // 8,),
        in_specs=[pl.BlockSpec((K, 8, Cc), lambda r: (0, r, 0))],
        out_specs=pl.BlockSpec((8, Cc), lambda r: (r, 0)),
        out_shape=jax.ShapeDtypeStruct((R, Cc), F32), compiler_params=_cparams(("parallel",)),
    )(parts)


def adamw(w, g, m, v, name):
    shape = w.shape
    if w.ndim == 3:
        L, R, Cc = shape
    else:
        L, R, Cc = 1, (1 if w.ndim == 1 else shape[0]), shape[-1]
    view = lambda t: t.reshape((L, R, Cc))
    tr = _rows_tile(R, Cc, budget=1 << 17)
    c1 = 1.0 - ADAM_B1 ** ADAM_STEP
    c2 = 1.0 - ADAM_B2 ** ADAM_STEP

    def body(w_ref, g_ref, m_ref, v_ref, d_ref, nm_ref, nv_ref):
        gg = g_ref[...]
        nm = ADAM_B1 * m_ref[...] + (1.0 - ADAM_B1) * gg
        nv = ADAM_B2 * v_ref[...] + (1.0 - ADAM_B2) * (gg * gg)
        d_ref[...] = -ADAM_LR * ((nm / c1) / (jnp.sqrt(nv / c2) + ADAM_EPS) + ADAM_WD * w_ref[...])
        nm_ref[...] = nm
        nv_ref[...] = nv

    if w.ndim == 3 and Cc % 128 != 0:
        ta = max(t for t in range(1, 17) if Cc % t == 0)
        tview = lambda t: jnp.transpose(t, (2, 0, 1))
        spec = pl.BlockSpec((ta, L, R), lambda a: (a, 0, 0))
        outs = _pcall(
            body, name=name, grid=(Cc // ta,), in_specs=[spec] * 4, out_specs=[spec] * 3,
            out_shape=[jax.ShapeDtypeStruct((Cc, L, R), F32)] * 3, compiler_params=_cparams(("parallel",)),
        )(tview(w), tview(g), tview(m), tview(v))
        return tuple(jnp.transpose(o, (1, 2, 0)) for o in outs)
    spec = pl.BlockSpec((None, tr, Cc), lambda l, r: (l, r, 0))
    outs = _pcall(
        body, name=name, grid=(L, R // tr), in_specs=[spec] * 4, out_specs=[spec] * 3,
        out_shape=[jax.ShapeDtypeStruct((L, R, Cc), F32)] * 3, compiler_params=_cparams(("parallel", "parallel")),
    )(view(w), view(g), view(m), view(v))
    return tuple(o.reshape(shape) for o in outs)


def _place():
    x, y, c = lax.axis_index("x"), lax.axis_index("y"), lax.axis_index("c")
    other_chips = [(1 - x, y), (x, 1 - y), (1 - x, 1 - y)]
    return x, y, c, other_chips


def _rcopy(src, dst, ssem, rsem, k, to):
    return pltpu.make_async_remote_copy(src_ref=src, dst_ref=dst, send_sem=ssem.at[k], recv_sem=rsem.at[k],
                                        device_id=to, device_id_type=MESH)


def gather_weights(slots, small):
    nt = len(slots)

    def body(*refs):
        outs, small_out = refs[nt + 1:2 * nt + 1], refs[2 * nt + 1]
        ssem, rsem = refs[2 * nt + 2:]
        x, y, c, chips = _place()
        me = 2 * x + y
        sib = (x, y, 1 - c)

        def rows(t, half_of):
            half = outs[t].shape[2] // 2
            return pl.ds(half_of * half, half)

        first, passed = [], []
        for t in range(nt):
            mine = outs[t].at[:, me, rows(t, c), :]
            for j, chip in enumerate(chips):
                first.append(_rcopy(mine, mine, ssem, rsem, 6 * t + j, (*chip, c)))
        for j, chip in enumerate(chips):
            first.append(_rcopy(small_out.at[me], small_out.at[me], ssem, rsem, 6 * nt + j, (*chip, c)))
        for cp in first:
            cp.start()
        for t in range(nt):
            for j, (px, py) in enumerate(chips):
                land = outs[t].at[:, 2 * px + py, rows(t, c), :]
                _rcopy(land, land, ssem, rsem, 6 * t + j, (x, y, c)).wait_recv()
                fw = _rcopy(land, land, ssem, rsem, 6 * t + 3 + j, sib)
                fw.start()
                passed.append(fw)
        for j, (px, py) in enumerate(chips):
            land = small_out.at[2 * px + py]
            _rcopy(land, land, ssem, rsem, 6 * nt + j, (x, y, c)).wait_recv()
        for t in range(nt):
            for j, (px, py) in enumerate(chips):
                land = outs[t].at[:, 2 * px + py, rows(t, 1 - c), :]
                _rcopy(land, land, ssem, rsem, 6 * t + 3 + j, (x, y, c)).wait_recv()
        for cp in first + passed:
            cp.wait_send()

    ops = list(slots) + [small]
    nsem = 6 * nt + 3
    return _pcall(
        body, name="gather_weights", out_shape=[jax.ShapeDtypeStruct(s.shape, s.dtype) for s in ops],
        in_specs=[pl.BlockSpec(memory_space=pl.ANY)] * (nt + 1),
        out_specs=[pl.BlockSpec(memory_space=pl.ANY)] * (nt + 1),
        input_output_aliases={n: n for n in range(nt + 1)},
        scratch_shapes=[pltpu.SemaphoreType.DMA((nsem,)), pltpu.SemaphoreType.DMA((nsem,))],
    )(*ops)


def swap_halves(grads, name):
    nt = len(grads)

    def body(*refs):
        ins, outs = refs[:nt], refs[nt:2 * nt]
        ssem, rsem = refs[2 * nt:]
        x, y, c, _ = _place()
        sib = (x, y, 1 - c)
        cps = [_rcopy(ins[t].at[:, :, 1 - c], outs[t], ssem, rsem, t, sib) for t in range(nt)]
        for cp in cps:
            cp.start()
        for cp in cps:
            cp.wait()

    out_shape = [jax.ShapeDtypeStruct(g.shape[:2] + g.shape[3:], g.dtype) for g in grads]
    return _pcall(
        body, name=name, out_shape=out_shape,
        in_specs=[pl.BlockSpec(memory_space=pl.ANY)] * nt, out_specs=[pl.BlockSpec(memory_space=pl.ANY)] * nt,
        scratch_shapes=[pltpu.SemaphoreType.DMA((nt,)), pltpu.SemaphoreType.DMA((nt,))],
    )(*grads)


def _exchange_copies(ins, outs, ssem, rsem):
    x, y, c, chips = _place()
    sends, lands = [], []
    for t in range(len(ins)):
        for j, (px, py) in enumerate(chips):
            k = 3 * t + j
            sends.append(_rcopy(ins[t].at[:, 2 * px + py], outs[t].at[:, j], ssem, rsem, k, (px, py, c)))
            land = outs[t].at[:, j]
            lands.append(_rcopy(land, land, ssem, rsem, k, (x, y, c)))
    return sends, lands


def _gather_direct_copies(bufs, ssem, rsem, rows=None):
    x, y, c, chips = _place()
    me = 2 * x + y
    sends, lands = [], []
    for t, buf in enumerate(bufs):
        r0, rn = rows[t] if rows else (0, buf.shape[2])
        mine = buf.at[:, me, pl.ds(r0, rn)]
        for j, (px, py) in enumerate(chips):
            k = 3 * t + j
            sends.append(_rcopy(mine, mine, ssem, rsem, k, (px, py, c)))
            land = buf.at[:, 2 * px + py, pl.ds(r0, rn)]
            lands.append(_rcopy(land, land, ssem, rsem, k, (x, y, c)))
    return sends, lands


def exchange_partials(parts, small):
    nt = len(parts)

    def body(*refs):
        ins = refs[:nt]
        outs, small_out = refs[nt + 1:2 * nt + 1], refs[2 * nt + 1]
        ssem, rsem = refs[2 * nt + 2:]
        x, y, c, chips = _place()
        dev = 4 * x + 2 * y + c
        sends, lands = _exchange_copies(ins, outs, ssem, rsem)
        peers = [(px, py, pc) for pc in (c, 1 - c) for (px, py) in [(x, y)] + chips][1:]
        for j, (px, py, pc) in enumerate(peers):
            k = 3 * nt + j
            sends.append(_rcopy(small_out.at[dev], small_out.at[dev], ssem, rsem, k, (px, py, pc)))
            land = small_out.at[4 * px + 2 * py + pc]
            lands.append(_rcopy(land, land, ssem, rsem, k, (x, y, c)))
        for cp in sends:
            cp.start()
        for cp in lands:
            cp.wait_recv()
        for cp in sends:
            cp.wait_send()

    out_shape = [jax.ShapeDtypeStruct((p.shape[0], N_CHIPS - 1) + p.shape[2:], p.dtype) for p in parts]
    out_shape.append(jax.ShapeDtypeStruct(small.shape, small.dtype))
    nsem = 3 * nt + 7
    return _pcall(
        body, name="exchange_partials", out_shape=out_shape,
        in_specs=[pl.BlockSpec(memory_space=pl.ANY)] * (nt + 1),
        out_specs=[pl.BlockSpec(memory_space=pl.ANY)] * (nt + 1),
        input_output_aliases={nt: nt},
        scratch_shapes=[pltpu.SemaphoreType.DMA((nsem,)), pltpu.SemaphoreType.DMA((nsem,))],
    )(*parts, small)


def join_halves(halves):
    nt = len(halves)

    def body(*refs):
        outs = refs[nt:2 * nt]
        ssem, rsem = refs[2 * nt:]
        x, y, c, _ = _place()
        sib = (x, y, 1 - c)
        sends = [_rcopy(outs[t].at[:, c], outs[t].at[:, c], ssem, rsem, t, sib) for t in range(nt)]
        for cp in sends:
            cp.start()
        for t in range(nt):
            land = outs[t].at[:, 1 - c]
            _rcopy(land, land, ssem, rsem, t, (x, y, c)).wait_recv()
        for cp in sends:
            cp.wait_send()

    return _pcall(
        body, name="join_halves", out_shape=[jax.ShapeDtypeStruct(h.shape, h.dtype) for h in halves],
        in_specs=[pl.BlockSpec(memory_space=pl.ANY)] * nt, out_specs=[pl.BlockSpec(memory_space=pl.ANY)] * nt,
        input_output_aliases={n: n for n in range(nt)},
        scratch_shapes=[pltpu.SemaphoreType.DMA((nt,)), pltpu.SemaphoreType.DMA((nt,))],
    )(*halves)


def _pad_cols(a, n):
    return jnp.pad(a, [(0, 0)] * (a.ndim - 1) + [(0, n - a.shape[-1])])


def kernel(x, norm_g, fox_w_in, fox_b_f, fox_w_out, conv_w_in, conv_b_in, conv_dw, conv_dw_b, conv_ln_g, conv_ln_b, conv_w_out, final_norm_g, loss_target, m_norm_g, m_fox_w_in, m_fox_b_f, m_fox_w_out, m_conv_w_in, m_conv_b_in, m_conv_dw, m_conv_dw_b, m_conv_ln_g, m_conv_ln_b, m_conv_w_out, m_final_norm_g, v_norm_g, v_fox_w_in, v_fox_b_f, v_fox_w_out, v_conv_w_in, v_conv_b_in, v_conv_dw, v_conv_dw_b, v_conv_ln_g, v_conv_ln_b, v_conv_w_out, v_final_norm_g):
    S, D = x.shape[1], x.shape[2]
    H = fox_b_f.shape[1]
    assert D == H * HEAD_DIM, "one head must be one lane tile"
    W = C = D
    NL = fox_w_in.shape[0]
    Dq = D // N_CHIPS
    NA = fox_w_in.shape[2]
    scale = HEAD_DIM ** -0.5
    chip = 2 * lax.axis_index("x") + lax.axis_index("y")
    core = lax.axis_index("c")
    cidx = core.astype(jnp.int32).reshape((1,))
    chip1 = chip.astype(jnp.int32).reshape((1,))
    sel = jnp.stack([chip, core]).astype(jnp.int32)

    small_pack = jnp.concatenate([
        conv_b_in.reshape((NL * 3, Dq)), conv_dw.reshape((NL * CONV_K, Dq)), conv_dw_b, conv_ln_g, conv_ln_b,
        jnp.zeros((PACK_ROWS - NL * (3 + CONV_K + 3), Dq), F32)], axis=0)
    small_slots = lax.dynamic_update_slice(jnp.zeros((N_CHIPS, PACK_ROWS, Dq), F32), small_pack[None], (chip, 0, 0))
    ga0, gb0, gsmall = gather_weights(
        [cast_into_slot(fox_w_in, chip1, "cast_fox_w_in_0", 0, 1),
         cast_into_slot(fox_w_out, chip1, "cast_fox_w_out_0", 0, 1)], small_slots)
    fox_raw = {l: (cast_into_slot(fox_w_in, chip1, f"cast_fox_w_in_{l}", l, 1),
                   cast_into_slot(fox_w_out, chip1, f"cast_fox_w_out_{l}", l, 1)) for l in range(1, NL)}
    conv_raw = {l: (cast_into_slot(conv_w_in, chip1, f"cast_conv_w_in_{l}", l, 1),
                    cast_into_slot(conv_w_out, chip1, f"cast_conv_w_out_{l}", l, 1)) for l in range(NL)}

    def fox_weights(ga, gb):
        n = ga.shape[0]
        wfull = jnp.transpose(ga, (0, 2, 1, 3)).reshape((n, D, N_CHIPS * NA))
        wq, wk, wv, wg, wf = (wfull[:, :, 0:W], wfull[:, :, W:2 * W], wfull[:, :, 2 * W:3 * W],
                              wfull[:, :, 3 * W:4 * W], wfull[:, :, 4 * W:])
        w1 = jnp.concatenate([wq, wg], axis=-1).reshape((n, 1, D, 2 * W))
        w2 = jnp.stack([wk.reshape((n, D, H, HEAD_DIM)), wv.reshape((n, D, H, HEAD_DIM))], axis=3)
        return w1, w2.reshape((n, 1, D, 2 * W)), _pad_cols(wf, 128).reshape((n, 1, D, 128)), gb.reshape((n, 1, W, D))

    fox_w = {0: fox_weights(ga0, gb0)}
    conv_w = {}
    b_in = gsmall[:, 0:3 * NL, :].reshape((N_CHIPS, NL, 3 * Dq)).transpose((1, 0, 2)).reshape((NL, 3 * C))
    dwt = gsmall[:, 3 * NL:3 * NL + CONV_K * NL, :].reshape((N_CHIPS, NL, CONV_K, Dq))
    dwt = dwt.transpose((1, 2, 0, 3)).reshape((NL, CONV_K, C))
    r0 = (3 + CONV_K) * NL
    vecs = gsmall[:, r0:r0 + 3 * NL, :].reshape((N_CHIPS, 3, NL, Dq)).transpose((1, 2, 0, 3)).reshape((3, NL, C))
    dwb, lng, lnb = vecs[0], vecs[1], vecs[2]
    bfp = _pad_cols(fox_b_f, 128)

    h = x.reshape((S, D))
    tgt = loss_target.reshape((S, D))
    saved = []
    n_layers = norm_g.shape[0]
    for i in range(n_layers):
        j = i // 2
        g_i = norm_g[i:i + 1]
        hn = rms_fwd(h, g_i, f"rms_fwd_{i}")
        if i % 2 == 0:
            w1, w2, wfp, wo_fox = fox_w[j]
            gc, gd = conv_raw[j]
            p1, (gc,) = mm_nn(hn, w1, 0, 1, out_dtype=BF16, name=f"fox_proj_qg_{i}", gather=[(gc, 0, D // 2)])
            p2, (gc,) = mm_nn(hn, w2, 0, 1, out_dtype=BF16, name=f"fox_proj_kv_{i}", gather=[(gc, D // 2, D // 2)])
            pf = mm_nn(hn, wfp, 0, 1, out_dtype=F32, name=f"fox_proj_f_{i}")
            qa, ka, c0, c1 = fox_decay(pf, bfp[j:j + 1], H, scale, f"fox_decay_{i}")
            qn, kn = attn_norms(p1, p2, H, f"attn_norms_{i}")
            first_key, query_end = prune_tables(qn, kn, c0, c1, H, scale)
            o, yv, qb, _ = attn_fwd(p1, p2, qa, ka, H, scale, f"attn_fwd_{i}", first_key=first_key)
            h_new, (gd,) = mm_nt_res(yv, wo_fox, 0, h, f"fox_out_{i}", gather=[(gd, 0, Dq)])
            conv_w[j] = (gc, gd.reshape((1, 1, C, D)))
            saved.append((h, hn, p1, p2, pf, ka, o, yv, qb, query_end))
        else:
            wc, wo_conv = conv_w[j]
            nxt = fox_raw.get(j + 1)
            proj = mm_nn(hn, wc, 0, N_CHIPS, out_dtype=BF16, name=f"conv_proj_{i}", bias=b_in[j:j + 1],
                         gather=[(nxt[0], 0, D // 2)] if nxt else ())
            if nxt:
                proj, (ga,) = proj
            u2, yv, filled = conv_fwd(proj, dwt[j], dwb[j:j + 1], lng[j:j + 1], lnb[j:j + 1], f"conv_fwd_{i}",
                                      gather=[(ga, D // 2, D // 2)] if nxt else ())
            h_new = mm_nt_res(yv, wo_conv, 0, h, f"conv_out_{i}", gather=[(nxt[1], 0, Dq)] if nxt else ())
            if nxt:
                h_new, (gb,) = h_new
                fox_w[j + 1] = fox_weights(filled[0], gb)
            saved.append((h, hn, proj, u2, yv))
        h = h_new

    dh, d_gf, loss_part = loss_head(h, final_norm_g.reshape((1, D)), tgt, "loss_head")

    d_norm = [None] * n_layers
    d_fox_b = [None] * NL
    d_conv_small = [None] * NL
    wgrad = {}

    def pair_sums(keys, tag):
        gs = [wgrad[k].reshape(wgrad[k].shape[:2] + (2, wgrad[k].shape[2] // 2, wgrad[k].shape[3])) for k in keys]
        rcv = swap_halves(gs, f"swap_halves_{tag}")
        return {k: pair_sum(g, r, cidx, f"pair_sum_{k[0]}{k[1]}") for k, g, r in zip(keys, gs, rcv)}

    parts, got = {}, {}

    def pending(tag):
        keys = [k for k in wgrad if k not in parts]
        parts.update(pair_sums(keys, tag))
        return keys

    for i in reversed(range(n_layers)):
        j = i // 2
        g_i = norm_g[i:i + 1]
        if i % 2 == 0:
            h_in, hn, p1, p2, pf, ka, o, yv, qb, query_end = saved[i]
            (w1, w2, wfp, wo_fox), wl = fox_w[j], 0
            wgrad[("b", j)] = mm_tn(yv, dh, 1, name=f"fox_out_dw_{i}").reshape((1, N_CHIPS, Dq, D))
            dy = mm_nn_t(dh, wo_fox, wl, f"fox_out_dx_{i}")
            do, dp1, da = attn_bwd_prep(dy, o, p1, H, f"attn_bwd_prep_{i}")
            dp1, dp2, dc, _ = attn_bwd(p1, qb, do, da, p2, ka, dp1, H, scale, f"attn_bwd_{i}", query_end=query_end)
            dz, dbf = fox_decay_bwd(dc, pf, bfp[j:j + 1], H, f"fox_decay_bwd_{i}")
            d_fox_b[j] = dbf
            keys = pending(f"l{i}")
            dw1, arrived = mm_tn(hn, dp1, 1, name=f"fox_dw_qg_{i}", exchange=[parts[k] for k in keys])
            got.update(dict(zip(keys, arrived)))
            dw1 = dw1[0, 0]
            dw2 = mm_tn(hn, dp2, 1, name=f"fox_dw_kv_{i}")[0, 0]
            dwf = mm_tn(hn, dz, 1, name=f"fox_dw_f_{i}")[0, 0]
            dw2 = dw2.reshape((D, H, 2, HEAD_DIM))
            d_in = jnp.concatenate([dw1[:, :W], dw2[:, :, 0].reshape((D, W)), dw2[:, :, 1].reshape((D, W)),
                                    dw1[:, W:], dwf[:, :H]], axis=-1)
            wgrad[("a", j)] = d_in.reshape((D, N_CHIPS, NA)).transpose((1, 0, 2))[None]
            keys = pending(f"l{i}b")
            dhn, arrived = mm_nt(dp1, w1, wl, 1, name=f"fox_dx_qg_{i}", exchange=[parts[k] for k in keys])
            got.update(dict(zip(keys, arrived)))
            dhn = mm_nt(dp2, w2, wl, 1, name=f"fox_dx_kv_{i}", res=dhn)
            dhn = mm_nt(dz, wfp, wl, 1, name=f"fox_dx_f_{i}", res=dhn)
        else:
            h_in, hn, proj, u2, yv = saved[i]
            wc, wo_conv = conv_w[j]
            wgrad[("d", j)] = mm_tn(yv, dh, 1, name=f"conv_out_dw_{i}").reshape((1, N_CHIPS, Dq, D))
            dy = mm_nn_t(dh, wo_conv, 0, f"conv_out_dx_{i}")
            du2, dproj, sm1 = conv_bwd_norm(dy, proj, u2, lng[j:j + 1], lnb[j:j + 1], f"conv_bwd_norm_{i}")
            keys = pending(f"l{i}")
            dproj, sm2, arrived = conv_bwd_taps(du2, proj, dwt[j], dproj, f"conv_bwd_taps_{i}",
                                                exchange=[parts[k] for k in keys])
            got.update(dict(zip(keys, arrived)))
            d_conv_small[j] = (sm1, sm2)
            wgrad[("c", j)] = mm_tn(hn, dproj, N_CHIPS, name=f"conv_dw_in_{i}")
            dhn = mm_nt(dproj, wc, 0, N_CHIPS, name=f"conv_dx_{i}")
        dh, d_norm[i] = rms_bwd(dhn, h_in, g_i, dh, f"rms_bwd_{i}")

    late = [k for k in wgrad if k not in parts]
    if late:
        parts.update(pair_sums(late, "late"))

    zrow = jnp.zeros((1, D), F32)
    rows = list(d_norm) + [d_gf]
    rows += [_pad_cols(d_fox_b[l][:, :H], D) for l in range(NL)]
    rows += [_pad_cols(loss_part[:, :1], D)]
    for l in range(NL):
        sm1, sm2 = d_conv_small[l]
        rows += [sm2[32:33], sm2[33:34], sm1[3:4]]
    for l in range(NL):
        rows += [d_conv_small[l][1][0:CONV_K]]
    rows += [d_conv_small[l][0][2:3] for l in range(NL)]
    rows += [d_conv_small[l][0][0:1] for l in range(NL)]
    rows += [d_conv_small[l][0][1:2] for l in range(NL)]
    n_rows = sum(r.shape[0] for r in rows)
    rows += [zrow] * (SMALL_ROWS - n_rows)
    small = jnp.concatenate(rows, axis=0)

    dev = 4 * lax.axis_index("x") + 2 * lax.axis_index("y") + core
    small_slots = lax.dynamic_update_slice(jnp.zeros((N_DEV, SMALL_ROWS, D), F32), small[None], (dev, 0, 0))
    arrived = exchange_partials([parts[k] for k in late], small_slots)
    got.update(dict(zip(late, arrived[:-1])))
    tot = dev_sum(arrived[-1], "dev_sum")
    halves = {}
    for kind in "abcd":
        for l in range(NL):
            halves[kind] = chip_sum(parts[(kind, l)], got[(kind, l)], sel, f"chip_sum_{kind}{l}", NL, l,
                                    prev=halves.get(kind))
    full = join_halves([halves[kind] for kind in "abcd"])
    grad_fox_w_in = full[0].reshape(fox_w_in.shape)
    grad_fox_w_out = full[1].reshape(fox_w_out.shape)
    grad_conv_w_in = full[2].reshape(conv_w_in.shape)
    grad_conv_w_out = full[3].reshape(conv_w_out.shape)

    def mine(v):
        return lax.dynamic_slice_in_dim(v, chip * Dq, Dq, axis=v.ndim - 1)

    r = n_layers
    grad_norm_g = tot[0:r]
    grad_final = tot[r]
    grad_fox_b_f = tot[r + 1:r + 1 + NL, :H]
    loss = tot[r + 1 + NL, 0]
    r = r + 2 + NL
    gb_full = tot[r:r + 3 * NL].reshape((NL, 3 * C))
    grad_conv_b_in = lax.dynamic_slice_in_dim(gb_full, chip * 3 * Dq, 3 * Dq, axis=1)
    r += 3 * NL
    grad_conv_dw = mine(tot[r:r + CONV_K * NL].reshape((NL, CONV_K, C)))
    r += CONV_K * NL
    grad_conv_dw_b = mine(tot[r:r + NL])
    grad_conv_ln_g = mine(tot[r + NL:r + 2 * NL])
    grad_conv_ln_b = mine(tot[r + 2 * NL:r + 3 * NL])

    grads = [grad_norm_g, grad_fox_w_in, grad_fox_b_f, grad_fox_w_out, grad_conv_w_in, grad_conv_b_in,
             grad_conv_dw, grad_conv_dw_b, grad_conv_ln_g, grad_conv_ln_b, grad_conv_w_out, grad_final]
    ws = [norm_g, fox_w_in, fox_b_f, fox_w_out, conv_w_in, conv_b_in, conv_dw, conv_dw_b, conv_ln_g, conv_ln_b,
          conv_w_out, final_norm_g]
    ms = [m_norm_g, m_fox_w_in, m_fox_b_f, m_fox_w_out, m_conv_w_in, m_conv_b_in, m_conv_dw, m_conv_dw_b,
          m_conv_ln_g, m_conv_ln_b, m_conv_w_out, m_final_norm_g]
    vs = [v_norm_g, v_fox_w_in, v_fox_b_f, v_fox_w_out, v_conv_w_in, v_conv_b_in, v_conv_dw, v_conv_dw_b,
          v_conv_ln_g, v_conv_ln_b, v_conv_w_out, v_final_norm_g]
    deltas, new_ms, new_vs = [], [], []
    for n, (w_, g_, m_, v_) in enumerate(zip(ws, grads, ms, vs)):
        d_, nm_, nv_ = adamw(w_, g_, m_, v_, f"adamw_{n}")
        deltas.append(d_)
        new_ms.append(nm_)
        new_vs.append(nv_)
    grad_x = dh.reshape(x.shape)
    return (loss, grad_x, *grads, *deltas, *new_ms, *new_vs)


def mm_nt_res(y, wo, lidx, h, name, gather=()):
    M, K = y.shape
    N = wo.shape[-1]
    tm = _tile(M, 1024)
    tn = _tile(N, 1024)
    grid = (M // tm, N // tn, 1)
    return _matmul(
        y, wo, contract="nn", grid=grid, name=name,
        a_spec=pl.BlockSpec((tm, K), lambda i, j, k: (i, 0)),
        b_spec=pl.BlockSpec((None, None, K, tn), lambda i, j, k: (lidx, 0, 0, j)),
        o_spec=pl.BlockSpec((tm, tn), lambda i, j, k: (i, j)),
        out_shape=jax.ShapeDtypeStruct((M, N), F32), acc_shape=(tm, tn),
        res=h, res_spec=pl.BlockSpec((tm, tn), lambda i, j, k: (i, j)), gather=gather,
    )


def mm_nn_t(dh, wo, lidx, name):
    M, K = dh.shape
    N = wo.shape[-2]
    tm = _tile(M, 512)
    tn = _tile(N, 1024)
    grid = (M // tm, N // tn, 1)
    return _matmul(
        dh, wo, contract="nt", grid=grid, name=name,
        a_spec=pl.BlockSpec((tm, K), lambda i, j, k: (i, 0)),
        b_spec=pl.BlockSpec((None, None, tn, K), lambda i, j, k: (lidx, 0, j, 0)),
        o_spec=pl.BlockSpec((tm, tn), lambda i, j, k: (i, j)),
        out_shape=jax.ShapeDtypeStruct((M, N), F32), acc_shape=(tm, tn),
    )
```
